```python
import math
import jax
import jax.numpy as jnp
from jax import lax
import numpy as np

D_MODEL = 1024
BATCH = 4
SEQ = 8192
DEPTH = 2

Q_BLOCK = 128
RMS_EPS = 1e-6
N_BRANCHES = 4

DA_HEADS = 4
DA_QK_DIM = 64
DA_V_DIM = 2 * DA_QK_DIM
ROPE_THETA = 500000.0
ROPE_DIM = DA_QK_DIM // 4
DA_LAMBDA_INIT_BASE = 0.8
DA_LAMBDA_INIT_SCALE = 0.6
DA_LAMBDA_INIT_RATE = 0.3

SB_HEADS = 4
SB_HEAD_DIM = 64

HG_HEADS = 4
HG_KEY_DIM = 64
HG_VAL_DIM = 64
HG_CHUNK = 64

S5_GROUPS = 16
S5_GROUP_CH = 16
S5_STATE = 64
S5_WIDTH = S5_GROUPS * S5_GROUP_CH
S5_STEP_MIN = 1e-3
S5_STEP_MAX = 1e-1
S5_EIG_CLIP = -1e-4

N_GROUPS = 4
EXPERTS_PER_GROUP = 4
N_EXPERTS = N_GROUPS * EXPERTS_PER_GROUP
TOP_K = 2
D_EXPERT = 384

DA_QK_COLS = DA_HEADS * 2 * DA_QK_DIM
DA_V_COLS = DA_HEADS * DA_V_DIM
SB_COLS = SB_HEADS * SB_HEAD_DIM
HG_K_COLS = HG_HEADS * HG_KEY_DIM
HG_V_COLS = HG_HEADS * HG_VAL_DIM
GATE_COLS = N_BRANCHES * D_MODEL
SPLIT_SIZES = (DA_QK_COLS, DA_QK_COLS, DA_V_COLS, SB_COLS, SB_COLS, SB_COLS,
               HG_K_COLS, HG_V_COLS, HG_K_COLS, HG_V_COLS, S5_WIDTH, GATE_COLS)
IN_COLS = sum(SPLIT_SIZES)
SPLIT_POINTS = tuple(int(v) for v in np.cumsum(SPLIT_SIZES)[:-1])

kernel_name = 'hybrid_gated_diffattn_stickbreak_hgrn2_s5_hmoe'

F32 = jnp.float32


def rms_norm(t, gain):
    tf = t.astype(F32)
    y = tf * lax.rsqrt(jnp.mean(tf * tf, axis=-1, keepdims=True) + RMS_EPS)
    return (y * gain.astype(F32)).astype(t.dtype)


def partial_rope(t, positions):
    half = ROPE_DIM // 2
    inv_freq = jnp.exp(-math.log(ROPE_THETA) * jnp.arange(half, dtype=F32) * (2.0 / ROPE_DIM))
    ang = positions.astype(F32)[:, :, None] * inv_freq
    cos = jnp.cos(ang)[:, :, None, :]
    sin = jnp.sin(ang)[:, :, None, :]
    tf = t.astype(F32)
    t1 = tf[..., :half]
    t2 = tf[..., half:ROPE_DIM]
    out = jnp.concatenate([t1 * cos - t2 * sin, t2 * cos + t1 * sin, tf[..., ROPE_DIM:]], axis=-1)
    return out.astype(t.dtype)


def differential_attention(q, k, v, positions, q_gain, k_gain, lq1, lk1, lq2, lk2,
                           subln_gain, lambda_init):
    bsz, seq, _ = q.shape
    q = partial_rope(rms_norm(q.reshape(bsz, seq, 2 * DA_HEADS, DA_QK_DIM), q_gain), positions)
    k = partial_rope(rms_norm(k.reshape(bsz, seq, 2 * DA_HEADS, DA_QK_DIM), k_gain), positions)
    q = q.reshape(bsz, seq, DA_HEADS, 2, DA_QK_DIM).transpose(3, 0, 2, 1, 4)
    k = k.reshape(bsz, seq, DA_HEADS, 2, DA_QK_DIM).transpose(3, 0, 2, 1, 4)
    v = v.reshape(bsz, seq, DA_HEADS, DA_V_DIM).transpose(0, 2, 1, 3).astype(F32)
    lam = (jnp.exp(jnp.sum(lq1.astype(F32) * lk1.astype(F32)))
           - jnp.exp(jnp.sum(lq2.astype(F32) * lk2.astype(F32))) + lambda_init)
    n_blk = seq // Q_BLOCK
    q_blocks = q.reshape(2, bsz, DA_HEADS, n_blk, Q_BLOCK, DA_QK_DIM).transpose(3, 0, 1, 2, 4, 5)
    scale = DA_QK_DIM ** -0.5
    key_pos = jnp.arange(seq)

    def block(args):
        q_blk, blk = args
        s = jnp.einsum('cbhqd,cbhkd->cbhqk', q_blk, k).astype(F32) * scale
        q_pos = blk * Q_BLOCK + jnp.arange(Q_BLOCK)
        causal = key_pos[None, :] <= q_pos[:, None]
        p = jax.nn.softmax(jnp.where(causal, s, -jnp.inf), axis=-1)
        w = p[0] - lam * p[1]
        return jnp.einsum('bhqk,bhkd->bhqd', w, v)

    o = lax.map(block, (q_blocks, jnp.arange(n_blk)))
    o = o.transpose(1, 0, 3, 2, 4).reshape(bsz, seq, DA_HEADS, DA_V_DIM)
    o = rms_norm(o, subln_gain) * (1.0 - lambda_init)
    return o.reshape(bsz, seq, DA_V_COLS)


def stick_breaking_attention(q, k, v):
    bsz, seq, _ = q.shape
    q = q.reshape(bsz, seq, SB_HEADS, SB_HEAD_DIM).transpose(0, 2, 1, 3)
    k = k.reshape(bsz, seq, SB_HEADS, SB_HEAD_DIM).transpose(0, 2, 1, 3)
    v = v.reshape(bsz, seq, SB_HEADS, SB_HEAD_DIM).transpose(0, 2, 1, 3).astype(F32)
    n_blk = seq // Q_BLOCK
    q_blocks = q.reshape(bsz, SB_HEADS, n_blk, Q_BLOCK, SB_HEAD_DIM).transpose(2, 0, 1, 3, 4)
    scale = SB_HEAD_DIM ** -0.5
    key_pos = jnp.arange(seq)

    def block(args):
        q_blk, blk = args
        z = jnp.einsum('bhqd,bhkd->bhqk', q_blk, k).astype(F32) * scale
        q_pos = blk * Q_BLOCK + jnp.arange(Q_BLOCK)
        strict = key_pos[None, :] < q_pos[:, None]
        log_keep = jnp.where(strict, jax.nn.log_sigmoid(-z), 0.0)
        log_later = lax.cumsum(log_keep, axis=3, reverse=True) - log_keep
        w = jnp.where(strict, jnp.exp(jax.nn.log_sigmoid(z) + log_later), 0.0)
        return jnp.einsum('bhqk,bhkd->bhqd', w, v)

    o = lax.map(block, (q_blocks, jnp.arange(n_blk)))
    return o.transpose(1, 0, 3, 2, 4).reshape(bsz, seq, SB_COLS)


def hgrn2(f_logit, i_val, q, g, lower_bound, out_gain):
    bsz, seq, _ = f_logit.shape

    def heads(t, d):
        return t.astype(F32).reshape(bsz, seq, HG_HEADS, d).transpose(0, 2, 1, 3)

    z = heads(f_logit, HG_KEY_DIM)
    v = heads(i_val, HG_VAL_DIM)
    qh = heads(q, HG_KEY_DIM)
    lb = lower_bound.astype(F32).reshape(1, HG_HEADS, 1, HG_KEY_DIM)
    log_f = jnp.logaddexp(jnp.log(lb), jnp.log1p(-lb) + jax.nn.log_sigmoid(z))
    key = (1.0 - lb) * jax.nn.sigmoid(-z)
    n_chunk = seq // HG_CHUNK

    def chunks(t):
        return t.reshape(bsz, HG_HEADS, n_chunk, HG_CHUNK, t.shape[-1]).transpose(2, 0, 1, 3, 4)

    causal = jnp.tril(jnp.ones((HG_CHUNK, HG_CHUNK), dtype=bool))

    def step(state, inp):
        qc, kc, vc, gc = inp
        b = jnp.cumsum(gc, axis=2)
        decay = jnp.exp(jnp.where(causal[:, :, None], b[:, :, :, None, :] - b[:, :, None, :, :], -jnp.inf))
        scores = jnp.einsum('bhtk,bhsk,bhtsk->bhts', qc, kc, decay)
        o = (jnp.einsum('bhts,bhsv->bhtv', scores, vc)
             + jnp.einsum('bhtk,bhkv->bhtv', qc * jnp.exp(b), state))
        b_end = b[:, :, -1:, :]
        state = (jnp.exp(b_end[:, :, 0, :, None]) * state
                 + jnp.einsum('bhsk,bhsv->bhkv', kc * jnp.exp(b_end - b), vc))
        return state, o

    s0 = jnp.zeros((bsz, HG_HEADS, HG_KEY_DIM, HG_VAL_DIM), F32)
    _, o = lax.scan(step, s0, (chunks(qh), chunks(key), chunks(v), chunks(log_f)))
    o = o.transpose(1, 0, 3, 2, 4).reshape(bsz, seq, HG_HEADS, HG_VAL_DIM)
    gate = jax.nn.silu(g.astype(F32)).reshape(bsz, seq, HG_HEADS, HG_VAL_DIM)
    return (rms_norm(o, out_gain) * gate).reshape(bsz, seq, HG_V_COLS)


def s5_layer(u, lam_re, lam_im, log_step, b_re, b_im, c_re, c_im, d_skip, glu_w, glu_b):
    bsz, seq, _ = u.shape
    uf = u.astype(F32)
    ug = uf.reshape(bsz, seq, S5_GROUPS, S5_GROUP_CH)
    lam_re = jnp.minimum(lam_re.astype(F32), S5_EIG_CLIP)
    lam_im = lam_im.astype(F32)
    step = jnp.exp(log_step.astype(F32))[:, None]
    mag = jnp.exp(lam_re * step)
    phase = lam_im * step
    a_re = mag * jnp.cos(phase)
    a_im = mag * jnp.sin(phase)
    denom = lam_re * lam_re + lam_im * lam_im
    num_re = a_re - 1.0
    gam_re = (num_re * lam_re + a_im * lam_im) / denom
    gam_im = (a_im * lam_re - num_re * lam_im) / denom
    b_re = b_re.astype(F32)
    b_im = b_im.astype(F32)
    bb_re = gam_re[..., None] * b_re - gam_im[..., None] * b_im
    bb_im = gam_re[..., None] * b_im + gam_im[..., None] * b_re
    bu_re = jnp.einsum('gnc,blgc->blgn', bb_re, ug)
    bu_im = jnp.einsum('gnc,blgc->blgn', bb_im, ug)
    a_re_t = jnp.broadcast_to(a_re, (1, seq, S5_GROUPS, S5_STATE))
    a_im_t = jnp.broadcast_to(a_im, (1, seq, S5_GROUPS, S5_STATE))

    def combine(e1, e2):
        a1r, a1i, b1r, b1i = e1
        a2r, a2i, b2r, b2i = e2
        return (a2r * a1r - a2i * a1i,
                a2r * a1i + a2i * a1r,
                a2r * b1r - a2i * b1i + b2r,
                a2r * b1i + a2i * b1r + b2i)

    _, _, x_re, x_im = lax.associative_scan(combine, (a_re_t, a_im_t, bu_re, bu_im), axis=1)
    y = (jnp.einsum('gcn,blgn->blgc', c_re.astype(F32), x_re)
         - jnp.einsum('gcn,blgn->blgc', c_im.astype(F32), x_im))
    y = y.reshape(bsz, seq, S5_WIDTH) + d_skip.astype(F32) * uf
    y = jax.nn.gelu(y)
    return y * jax.nn.sigmoid(y @ glu_w.astype(F32) + glu_b.astype(F32))


def hierarchical_moe(h, wg, bg, we, be, w_gate, w_up, w_down):
    bsz, seq, dm = h.shape
    t = h.reshape(-1, dm)
    g_prob = jax.nn.softmax((t @ wg).astype(F32) + bg.astype(F32), axis=-1)
    g_val, g_idx = lax.top_k(g_prob, 1)
    e_logits = ((t @ we).astype(F32) + be.astype(F32)).reshape(-1, N_GROUPS, EXPERTS_PER_GROUP)
    e_sel = jnp.einsum('ng,nge->ne', jax.nn.one_hot(g_idx[:, 0], N_GROUPS, dtype=F32), e_logits)
    e_prob = jax.nn.softmax(e_sel, axis=-1)
    e_val, e_idx = lax.top_k(e_prob, TOP_K)
    e_val = e_val / jnp.sum(e_val, axis=-1, keepdims=True)
    weights = g_val * e_val
    expert_id = g_idx * EXPERTS_PER_GROUP + e_idx
    combine_w = jnp.sum(jax.nn.one_hot(expert_id, N_EXPERTS, dtype=F32) * weights[..., None], axis=1)
    combine_w = combine_w.astype(t.dtype)
    y = jnp.zeros_like(t)
    for e in range(N_EXPERTS):
        hid = jax.nn.silu(t @ w_gate[e]) * (t @ w_up[e])
        y = y + combine_w[:, e:e + 1] * (hid @ w_down[e])
    return y.reshape(bsz, seq, dm)


def setup_inputs(seed: int = 0) -> dict:
    key = jax.random.key(seed)
    ks = jax.random.split(key, 40)

    def nrm(i, shape, scale):
        return jax.random.normal(ks[i], shape, F32) * scale

    def gain(i, shape):
        return 1.0 + nrm(i, shape, 0.02)

    offsets = jax.random.randint(ks[1], (BATCH, 1), 0, 4096, dtype=jnp.int32)
    positions = offsets + jnp.arange(SEQ, dtype=jnp.int32)[None, :]
    gs = (DEPTH, S5_GROUPS, S5_STATE)
    return {
        'x': nrm(0, (BATCH, SEQ, D_MODEL), 1.0),
        'positions': positions,
        'norm_mix': gain(2, (DEPTH, D_MODEL)),
        'w_in': nrm(3, (DEPTH, D_MODEL, IN_COLS), D_MODEL ** -0.5),
        'da_q_gain': gain(4, (DEPTH, DA_QK_DIM)),
        'da_k_gain': gain(5, (DEPTH, DA_QK_DIM)),
        'da_lambda_q1': nrm(6, (DEPTH, DA_QK_DIM), 0.1),
        'da_lambda_k1': nrm(7, (DEPTH, DA_QK_DIM), 0.1),
        'da_lambda_q2': nrm(8, (DEPTH, DA_QK_DIM), 0.1),
        'da_lambda_k2': nrm(9, (DEPTH, DA_QK_DIM), 0.1),
        'da_subln_gain': gain(10, (DEPTH, DA_V_DIM)),
        'hg_lower_bounds': nrm(11, (DEPTH, HG_K_COLS), 0.1),
        'hg_out_gain': gain(12, (DEPTH, HG_VAL_DIM)),
        's5_lambda_re': -0.5 + nrm(13, gs, 0.01),
        's5_lambda_im': math.pi * jnp.arange(S5_STATE, dtype=F32) + nrm(14, gs, 0.01),
        's5_log_step': jax.random.uniform(ks[15], (DEPTH, S5_GROUPS), F32,
                                          math.log(S5_STEP_MIN), math.log(S5_STEP_MAX)),
        's5_b_re': nrm(16, (DEPTH, S5_GROUPS, S5_STATE, S5_GROUP_CH), (2 * S5_GROUP_CH) ** -0.5),
        's5_b_im': nrm(17, (DEPTH, S5_GROUPS, S5_STATE, S5_GROUP_CH), (2 * S5_GROUP_CH) ** -0.5),
        's5_c_re': nrm(18, (DEPTH, S5_GROUPS, S5_GROUP_CH, S5_STATE), (2 * S5_STATE) ** -0.5),
        's5_c_im': nrm(19, (DEPTH, S5_GROUPS, S5_GROUP_CH, S5_STATE), (2 * S5_STATE) ** -0.5),
        's5_d': nrm(20, (DEPTH, S5_WIDTH), 1.0),
        's5_glu_w': nrm(21, (DEPTH, S5_WIDTH, S5_WIDTH), S5_WIDTH ** -0.5),
        's5_glu_b': nrm(22, (DEPTH, S5_WIDTH), 0.01),
        'w_branch_attn': nrm(23, (DEPTH, DA_V_COLS, D_MODEL), DA_V_COLS ** -0.5),
        'w_branch_sb': nrm(24, (DEPTH, SB_COLS, D_MODEL), SB_COLS ** -0.5),
        'w_branch_hgrn': nrm(25, (DEPTH, HG_V_COLS, D_MODEL), HG_V_COLS ** -0.5),
        'w_branch_s5': nrm(26, (DEPTH, S5_WIDTH, D_MODEL), S5_WIDTH ** -0.5),
        'w_out': nrm(27, (DEPTH, D_MODEL, D_MODEL), D_MODEL ** -0.5),
        'norm_ffn': gain(28, (DEPTH, D_MODEL)),
        'router_group_w': nrm(29, (DEPTH, D_MODEL, N_GROUPS), D_MODEL ** -0.5),
        'router_group_b': nrm(30, (DEPTH, N_GROUPS), 0.01),
        'router_expert_w': nrm(31, (DEPTH, D_MODEL, N_EXPERTS), D_MODEL ** -0.5),
        'router_expert_b': nrm(32, (DEPTH, N_EXPERTS), 0.01),
        'expert_w_gate': nrm(33, (DEPTH, N_EXPERTS, D_MODEL, D_EXPERT), D_MODEL ** -0.5),
        'expert_w_up': nrm(34, (DEPTH, N_EXPERTS, D_MODEL, D_EXPERT), D_MODEL ** -0.5),
        'expert_w_down': nrm(35, (DEPTH, N_EXPERTS, D_EXPERT, D_MODEL), D_EXPERT ** -0.5),
    }


def reference(x, positions, norm_mix, w_in, da_q_gain, da_k_gain, da_lambda_q1, da_lambda_k1,
              da_lambda_q2, da_lambda_k2, da_subln_gain, hg_lower_bounds, hg_out_gain,
              s5_lambda_re, s5_lambda_im, s5_log_step, s5_b_re, s5_b_im, s5_c_re, s5_c_im,
              s5_d, s5_glu_w, s5_glu_b, w_branch_attn, w_branch_sb, w_branch_hgrn, w_branch_s5,
              w_out, norm_ffn, router_group_w, router_group_b, router_expert_w, router_expert_b,
              expert_w_gate, expert_w_up, expert_w_down):
    bsz, seq, _ = x.shape
    lb_all = jnp.cumsum(jax.nn.softmax(hg_lower_bounds.astype(F32), axis=0), axis=0)
    lb_all = lb_all - lb_all[0:1]
    for l in range(DEPTH):
        h = rms_norm(x, norm_mix[l])
        proj = h @ w_in[l]
        (da_q, da_k, da_v, sb_q, sb_k, sb_v, hg_f, hg_i, hg_q, hg_g, s5_u,
         gate_logits) = jnp.split(proj, SPLIT_POINTS, axis=-1)
        lambda_init = DA_LAMBDA_INIT_BASE - DA_LAMBDA_INIT_SCALE * math.exp(-DA_LAMBDA_INIT_RATE * l)
        y_a = differential_attention(da_q, da_k, da_v, positions, da_q_gain[l], da_k_gain[l],
                                     da_lambda_q1[l], da_lambda_k1[l], da_lambda_q2[l],
                                     da_lambda_k2[l], da_subln_gain[l], lambda_init)
        y_b = stick_breaking_attention(sb_q, sb_k, sb_v)
        y_c = hgrn2(hg_f, hg_i, hg_q, hg_g, lb_all[l], hg_out_gain[l])
        y_d = s5_layer(s5_u, s5_lambda_re[l], s5_lambda_im[l], s5_log_step[l], s5_b_re[l],
                       s5_b_im[l], s5_c_re[l], s5_c_im[l], s5_d[l], s5_glu_w[l], s5_glu_b[l])
        gates = jax.nn.sigmoid(gate_logits.astype(F32)).reshape(bsz, seq, N_BRANCHES, D_MODEL)
        gates = gates.astype(x.dtype)
        merged = (gates[:, :, 0] * (y_a.astype(x.dtype) @ w_branch_attn[l])
                  + gates[:, :, 1] * (y_b.astype(x.dtype) @ w_branch_sb[l])
                  + gates[:, :, 2] * (y_c.astype(x.dtype) @ w_branch_hgrn[l])
                  + gates[:, :, 3] * (y_d.astype(x.dtype) @ w_branch_s5[l]))
        x = x + merged @ w_out[l]
        x = x + hierarchical_moe(rms_norm(x, norm_ffn[l]), router_group_w[l], router_group_b[l],
                                 router_expert_w[l], router_expert_b[l], expert_w_gate[l],
                                 expert_w_up[l], expert_w_down[l])
    return x
```

```python
import functools
import math

import jax
import jax.numpy as jnp
import numpy as np
from jax import lax
from jax.experimental import pallas as pl
from jax.experimental.pallas import tpu as pltpu

F32 = jnp.float32
BF16 = jnp.bfloat16

RMS_EPS = 1e-6
LANES = 128

DA_HEADS = 4
DA_QK_DIM = 64
ROPE_THETA = 500000.0
ROPE_DIM = DA_QK_DIM // 4
DA_LAMBDA_INIT_BASE = 0.8
DA_LAMBDA_INIT_SCALE = 0.6
DA_LAMBDA_INIT_RATE = 0.3

SB_HEADS = 4
SB_HEAD_DIM = 64

HG_HEADS = 4
HG_DIM = 64
HG_COLS = HG_HEADS * HG_DIM
HG_SUB = 16

S5_GROUPS = 16
S5_GROUP_CH = 16
S5_STATE = 64
S5_WIDTH = S5_GROUPS * S5_GROUP_CH
S5_NSTATE = S5_GROUPS * S5_STATE
S5_EIG_CLIP = -1e-4

N_GROUPS = 4
EXPERTS_PER_GROUP = 4
N_EXPERTS = N_GROUPS * EXPERTS_PER_GROUP

REF_GATE_START = 3584
COL_GATE = 0
COL_DA_Q = 4096
COL_DA_K = 4608
COL_DA_V = 5120
COL_SB_Q = 5632
COL_SB_K = 5888
COL_SB_V = 6144
COL_HG_F = 6400
COL_HG_I = 6656
COL_HG_Q = 6912
COL_HG_G = 7168
COL_S5_U = 7424
IN_COLS = 7680

VMEM_LIMIT = 48 * 1024 * 1024

_NT = (((1,), (1,)), ((), ()))
_TN = (((0,), (0,)), ((), ()))


def _cparams(sem):
    return pltpu.CompilerParams(dimension_semantics=sem, vmem_limit_bytes=VMEM_LIMIT)


def _dot(a, b):
    return jnp.dot(a, b, preferred_element_type=F32)


def _norm_proj_kernel(x_ref, g_ref, w_ref, o_ref, h_ref):
    @pl.when(pl.program_id(1) == 0)
    def _():
        x = x_ref[...]
        ms = jnp.mean(x * x, axis=-1, keepdims=True)
        h_ref[...] = (x * lax.rsqrt(ms + RMS_EPS) * g_ref[...]).astype(BF16)

    o_ref[...] = _dot(h_ref[...], w_ref[...]).astype(o_ref.dtype)


def _norm_proj(x2, gain, w, tm, tn):
    n, d = x2.shape
    cols = w.shape[1]
    return pl.pallas_call(
        _norm_proj_kernel,
        grid=(n // tm, cols // tn),
        in_specs=[pl.BlockSpec((tm, d), lambda i, j: (i, 0)),
                  pl.BlockSpec((1, d), lambda i, j: (0, 0)),
                  pl.BlockSpec((d, tn), lambda i, j: (0, j))],
        out_specs=pl.BlockSpec((tm, tn), lambda i, j: (i, j)),
        out_shape=jax.ShapeDtypeStruct((n, cols), BF16),
        scratch_shapes=[pltpu.VMEM((tm, d), BF16)],
        compiler_params=_cparams(("parallel", "arbitrary")),
        name="norm_proj",
    )(x2, gain, w)


def _rope_kernel(pos_ref, invf_ref, sa_ref, sb_ref, cos_ref, sina_ref, sinb_ref):
    ang = pos_ref[...].astype(F32) * invf_ref[...]
    c = jnp.cos(ang)
    s = jnp.sin(ang)
    cos_ref[...] = c
    sina_ref[...] = s * sa_ref[...]
    sinb_ref[...] = s * sb_ref[...]


def _rope_tables(pos_col, tm):
    n = pos_col.shape[0]
    half = ROPE_DIM // 2
    inv_freq = jnp.exp(-math.log(ROPE_THETA) * jnp.arange(half, dtype=F32) * (2.0 / ROPE_DIM))
    lane = np.arange(LANES) % DA_QK_DIM
    invf = jnp.where(lane < ROPE_DIM, inv_freq[lane % half], 0.0).astype(F32)[None, :]
    sgn_a = jnp.asarray(np.where(lane < half, -1.0, 0.0), F32)[None, :]
    sgn_b = jnp.asarray(np.where((lane >= half) & (lane < ROPE_DIM), 1.0, 0.0), F32)[None, :]
    row = pl.BlockSpec((1, LANES), lambda i: (0, 0))
    tab = pl.BlockSpec((tm, LANES), lambda i: (i, 0))
    shp = jax.ShapeDtypeStruct((n, LANES), F32)
    return pl.pallas_call(
        _rope_kernel,
        grid=(n // tm,),
        in_specs=[pl.BlockSpec((tm, 1), lambda i: (i, 0)), row, row, row],
        out_specs=[tab, tab, tab],
        out_shape=[shp, shp, shp],
        compiler_params=_cparams(("parallel",)),
        name="rope_tables",
    )(pos_col, invf, sgn_a, sgn_b)


def _qk_prep_kernel(x_ref, gain_ref, cos_ref, sina_ref, sinb_ref, bd_ref, o_ref):
    c = cos_ref[...]
    sa = sina_ref[...]
    sb = sinb_ref[...]
    bd = bd_ref[...]
    n_tiles = x_ref.shape[1] // LANES
    for j in range(n_tiles):
        sl = slice(j * LANES, (j + 1) * LANES)
        t = x_ref[:, sl].astype(F32)
        ss = _dot((t * t).astype(BF16), bd)
        y = t * lax.rsqrt(ss * (1.0 / DA_QK_DIM) + RMS_EPS) * gain_ref[:, sl]
        y = (y * c + pltpu.roll(y, LANES - ROPE_DIM // 2, 1) * sa
             + pltpu.roll(y, ROPE_DIM // 2, 1) * sb)
        if j < n_tiles // 2:
            y = y * (DA_QK_DIM ** -0.5)
        o_ref[:, sl] = y.astype(BF16)


def _qk_prep(proj, gain_row, cos_t, sina_t, sinb_t, tm):
    n = proj.shape[0]
    w = 2 * DA_HEADS * 2 * DA_QK_DIM
    seg = np.arange(LANES) // DA_QK_DIM
    bd = jnp.asarray(seg[:, None] == seg[None, :], BF16)
    tab = pl.BlockSpec((tm, LANES), lambda i: (i, 0))
    return pl.pallas_call(
        _qk_prep_kernel,
        grid=(n // tm,),
        in_specs=[pl.BlockSpec((tm, w), lambda i: (i, COL_DA_Q // w)),
                  pl.BlockSpec((1, w), lambda i: (0, 0)),
                  tab, tab, tab,
                  pl.BlockSpec((LANES, LANES), lambda i: (0, 0))],
        out_specs=pl.BlockSpec((tm, w), lambda i: (i, 0)),
        out_shape=jax.ShapeDtypeStruct((n, w), BF16),
        compiler_params=_cparams(("parallel",)),
        name="qk_prep",
    )(proj, gain_row, cos_t, sina_t, sinb_t, bd)


def _diff_attn_kernel(q_ref, k_ref, v_ref, lq1_ref, lk1_ref, lq2_ref, lk2_ref, sg_ref, o_ref,
                      m_ref, l_ref, acc_ref, *, tq, lambda_init):
    qi = pl.program_id(2)
    q = q_ref[...]
    lane = lax.broadcasted_iota(jnp.int32, q.shape, 1)
    zero = jnp.zeros_like(q)
    qs = (jnp.where(lane < DA_QK_DIM, q, zero), jnp.where(lane >= DA_QK_DIM, q, zero))
    m_ref[...] = jnp.full(m_ref.shape, -jnp.inf, F32)
    l_ref[...] = jnp.zeros(l_ref.shape, F32)
    acc_ref[...] = jnp.zeros(acc_ref.shape, F32)

    def tile(ki, masked):
        start = pl.multiple_of(ki * tq, tq)
        kb = k_ref[pl.ds(start, tq), :]
        vb = v_ref[pl.ds(start, tq), :]
        for c in range(2):
            s = lax.dot_general(qs[c], kb, _NT, preferred_element_type=F32)
            if masked:
                row = lax.broadcasted_iota(jnp.int32, s.shape, 0)
                col = lax.broadcasted_iota(jnp.int32, s.shape, 1)
                s = jnp.where(col <= row, s, -jnp.inf)
            m_prev = m_ref[c]
            m_new = jnp.maximum(m_prev, jnp.max(s, axis=-1, keepdims=True))
            p = jnp.exp(s - m_new)
            alpha = jnp.exp(m_prev - m_new)
            l_ref[c] = alpha * l_ref[c] + jnp.sum(p, axis=-1, keepdims=True)
            acc_ref[c] = alpha * acc_ref[c] + _dot(p.astype(BF16), vb)
            m_ref[c] = m_new

    def body(ki, carry):
        tile(ki, False)
        return carry

    lax.fori_loop(0, qi, body, 0)
    tile(qi, True)

    lam = (jnp.exp(jnp.sum(lq1_ref[...] * lk1_ref[...], axis=-1, keepdims=True))
           - jnp.exp(jnp.sum(lq2_ref[...] * lk2_ref[...], axis=-1, keepdims=True)) + lambda_init)
    o = acc_ref[0] / l_ref[0] - lam * (acc_ref[1] / l_ref[1])
    ms = jnp.mean(o * o, axis=-1, keepdims=True)
    o = o * lax.rsqrt(ms + RMS_EPS) * sg_ref[...] * (1.0 - lambda_init)
    o_ref[...] = o.astype(o_ref.dtype)


def _diff_attn(qk, proj, lq1, lk1, lq2, lk2, subln, bsz, seq, tq, lambda_init):
    n = qk.shape[0]
    nq = seq // tq
    kcol = (DA_HEADS * 2 * DA_QK_DIM) // LANES
    vcol = COL_DA_V // LANES
    vec = pl.BlockSpec((1, DA_QK_DIM), lambda b, h, i: (0, 0))
    return pl.pallas_call(
        functools.partial(_diff_attn_kernel, tq=tq, lambda_init=lambda_init),
        grid=(bsz, DA_HEADS, nq),
        in_specs=[pl.BlockSpec((tq, LANES), lambda b, h, i: (b * nq + i, h)),
                  pl.BlockSpec((seq, LANES), lambda b, h, i: (b, kcol + h)),
                  pl.BlockSpec((seq, LANES), lambda b, h, i: (b, vcol + h)),
                  vec, vec, vec, vec,
                  pl.BlockSpec((1, LANES), lambda b, h, i: (0, 0))],
        out_specs=pl.BlockSpec((tq, LANES), lambda b, h, i: (b * nq + i, h)),
        out_shape=jax.ShapeDtypeStruct((n, DA_HEADS * LANES), BF16),
        scratch_shapes=[pltpu.VMEM((2, tq, 1), F32), pltpu.VMEM((2, tq, 1), F32),
                        pltpu.VMEM((2, tq, LANES), F32)],
        compiler_params=_cparams(("parallel", "parallel", "arbitrary")),
        name="diff_attn",
    )(qk, qk, proj, lq1, lk1, lq2, lk2, subln)


def _stick_break_kernel(q_ref, k_ref, v_ref, o_ref, r_ref, acc_ref, *, tq):
    qi = pl.program_id(2)
    q = q_ref[...] * (SB_HEAD_DIM ** -0.5)
    lane = lax.broadcasted_iota(jnp.int32, q.shape, 1)
    zero = jnp.zeros_like(q)
    qs = (jnp.where(lane < SB_HEAD_DIM, q, zero), jnp.where(lane >= SB_HEAD_DIM, q, zero))
    row = lax.broadcasted_iota(jnp.int32, (tq, tq), 0)
    col = lax.broadcasted_iota(jnp.int32, (tq, tq), 1)
    later = jnp.where(row > col, 1.0, 0.0).astype(BF16)
    strict = col < row
    r_ref[...] = jnp.zeros(r_ref.shape, F32)
    acc_ref[...] = jnp.zeros(acc_ref.shape, F32)

    def tile(ki, masked):
        start = pl.multiple_of(ki * tq, tq)
        kb = k_ref[pl.ds(start, tq), :]
        vb = v_ref[pl.ds(start, tq), :]
        ws = []
        for c in range(2):
            z = lax.dot_general(qs[c], kb, _NT, preferred_element_type=F32)
            sp = jnp.maximum(z, 0.0) + jnp.log(1.0 + jnp.exp(-jnp.abs(z)))
            log_keep = -sp
            log_beta = z - sp
            if masked:
                log_keep = jnp.where(strict, log_keep, 0.0)
            cum = _dot(log_keep.astype(BF16), later)
            w = jnp.exp(log_beta + cum + r_ref[c])
            if masked:
                w = jnp.where(strict, w, 0.0)
            ws.append(w.astype(BF16))
            r_ref[c] = r_ref[c] + cum[:, 0:1] + log_keep[:, 0:1]
        lane_v = lax.broadcasted_iota(jnp.int32, vb.shape, 1)
        zv = jnp.zeros_like(vb)
        v2 = jnp.concatenate([jnp.where(lane_v < SB_HEAD_DIM, vb, zv),
                              jnp.where(lane_v >= SB_HEAD_DIM, vb, zv)], axis=0)
        acc_ref[...] += _dot(jnp.concatenate(ws, axis=1), v2)

    tile(qi, True)

    def body(j, carry):
        tile(qi - 1 - j, False)
        return carry

    lax.fori_loop(0, qi, body, 0)
    o_ref[...] = acc_ref[...].astype(o_ref.dtype)


def _stick_break(proj, bsz, seq, tq):
    n = proj.shape[0]
    nq = seq // tq
    pairs = (SB_HEADS * SB_HEAD_DIM) // LANES
    qc, kc, vc = COL_SB_Q // LANES, COL_SB_K // LANES, COL_SB_V // LANES
    return pl.pallas_call(
        functools.partial(_stick_break_kernel, tq=tq),
        grid=(bsz, pairs, nq),
        in_specs=[pl.BlockSpec((tq, LANES), lambda b, p, i: (b * nq + i, qc + p)),
                  pl.BlockSpec((seq, LANES), lambda b, p, i: (b, kc + p)),
                  pl.BlockSpec((seq, LANES), lambda b, p, i: (b, vc + p))],
        out_specs=pl.BlockSpec((tq, LANES), lambda b, p, i: (b * nq + i, p)),
        out_shape=jax.ShapeDtypeStruct((n, pairs * LANES), BF16),
        scratch_shapes=[pltpu.VMEM((2, tq, 1), F32), pltpu.VMEM((tq, LANES), F32)],
        compiler_params=_cparams(("parallel", "parallel", "arbitrary")),
        name="stick_break",
    )(proj, proj, proj)


def _split_bf16(x):
    hi = x.astype(BF16)
    lo = (x - hi.astype(F32)).astype(BF16)
    return hi, lo


def _hgrn_kernel(f_ref, i_ref, q_ref, g_ref, lb_ref, gain_ref, o_ref,
                 st_ref, b_ref, e_ref, qq_ref, kk_ref, vv_ref, oo_ref, gm_ref, *, tb):
    @pl.when(pl.program_id(1) == 0)
    def _():
        st_ref[...] = jnp.zeros(st_ref.shape, F32)

    nsub = tb // HG_SUB
    z = f_ref[...].astype(F32)
    lb = lb_ref[...]
    sp = jnp.maximum(z, 0.0) + jnp.log(1.0 + jnp.exp(-jnp.abs(z)))
    log_sig = z - sp
    a = jnp.log(lb)
    c = jnp.log(1.0 - lb) + log_sig
    mx = jnp.maximum(a, c)
    log_f = mx + jnp.log(jnp.exp(a - mx) + jnp.exp(c - mx))
    key = (1.0 - lb) * jax.nn.sigmoid(-z)

    row = lax.broadcasted_iota(jnp.int32, (tb, tb), 0)
    col = lax.broadcasted_iota(jnp.int32, (tb, tb), 1)
    sub_shift = HG_SUB.bit_length() - 1
    dim_shift = HG_DIM.bit_length() - 1
    same = (row >> sub_shift) == (col >> sub_shift)
    tri = jnp.where(same & (col <= row), 1.0, 0.0).astype(BF16)
    blk = jnp.where(same, 1.0, 0.0).astype(BF16)
    hi, lo = _split_bf16(log_f)
    b = _dot(tri, hi) + _dot(tri, lo)
    e = _dot(blk, hi) + _dot(blk, lo)
    b_ref[...] = b
    e_ref[...] = e
    qq_ref[...] = q_ref[...].astype(F32)
    kk_ref[...] = key
    vv_ref[...] = i_ref[...].astype(F32)

    seg_r = lax.broadcasted_iota(jnp.int32, (HG_COLS, HG_COLS), 0) >> dim_shift
    seg_c = lax.broadcasted_iota(jnp.int32, (HG_COLS, HG_COLS), 1) >> dim_shift
    head_mask = seg_r == seg_c
    ones_bd = jnp.where(head_mask, 1.0, 0.0).astype(BF16)
    sel = jnp.where(lax.broadcasted_iota(jnp.int32, (HG_SUB, HG_SUB * HG_SUB), 0)
                    == (lax.broadcasted_iota(jnp.int32, (HG_SUB, HG_SUB * HG_SUB), 1) >> sub_shift),
                    1.0, 0.0).astype(BF16)
    srow = lax.broadcasted_iota(jnp.int32, (HG_SUB, HG_COLS), 0)

    def chunk(ci, carry):
        r0 = pl.multiple_of(ci * HG_SUB, HG_SUB)
        bi = b_ref[pl.ds(r0, HG_SUB), :]
        ei = e_ref[pl.ds(r0, HG_SUB), :]
        qi = qq_ref[pl.ds(r0, HG_SUB), :]
        ki = kk_ref[pl.ds(r0, HG_SUB), :]
        vi = vv_ref[pl.ds(r0, HG_SUB), :]
        for t in range(HG_SUB):
            d = jnp.exp(jnp.minimum(bi[t:t + 1, :] - bi, 0.0))
            g = jnp.where(srow <= t, qi[t:t + 1, :] * ki * d, 0.0)
            gm_ref[t * HG_SUB:(t + 1) * HG_SUB, :] = g.astype(BF16)
        sc = _dot(gm_ref[...], ones_bd)
        vt = jnp.concatenate([vi] * HG_SUB, axis=0)
        o_intra = _dot(sel, (sc * vt).astype(BF16))
        st = st_ref[...]
        o_inter = lax.dot_general((qi * jnp.exp(bi)).astype(BF16), st.astype(BF16), _NT,
                                  preferred_element_type=F32)
        oo_ref[pl.ds(r0, HG_SUB), :] = o_intra + o_inter
        ke = ki * jnp.exp(ei - bi)
        upd = lax.dot_general(vi.astype(BF16), ke.astype(BF16), _TN, preferred_element_type=F32)
        st_ref[...] = st * jnp.exp(ei[0:1, :]) + jnp.where(head_mask, upd, 0.0)
        return carry

    lax.fori_loop(0, nsub, chunk, 0)

    o = oo_ref[...]
    ms = _dot((o * o).astype(BF16), ones_bd) * (1.0 / HG_DIM)
    gate = g_ref[...].astype(F32)
    gate = gate * jax.nn.sigmoid(gate)
    o_ref[...] = (o * lax.rsqrt(ms + RMS_EPS) * gain_ref[...] * gate).astype(o_ref.dtype)


def _hgrn(proj, lb_row, gain_row, bsz, seq, tb):
    n = proj.shape[0]
    nb = seq // tb
    w = HG_COLS
    cf, ci, cq, cg = COL_HG_F // w, COL_HG_I // w, COL_HG_Q // w, COL_HG_G // w

    def col(cc):
        return pl.BlockSpec((tb, w), lambda b, i: (b * nb + i, cc))

    rowspec = pl.BlockSpec((1, w), lambda b, i: (0, 0))
    big = pltpu.VMEM((tb, w), F32)
    return pl.pallas_call(
        functools.partial(_hgrn_kernel, tb=tb),
        grid=(bsz, nb),
        in_specs=[col(cf), col(ci), col(cq), col(cg), rowspec, rowspec],
        out_specs=pl.BlockSpec((tb, w), lambda b, i: (b * nb + i, 0)),
        out_shape=jax.ShapeDtypeStruct((n, w), BF16),
        scratch_shapes=[pltpu.VMEM((w, w), F32), big, big, big, big, big, big,
                        pltpu.VMEM((HG_SUB * HG_SUB, w), BF16)],
        compiler_params=_cparams(("parallel", "arbitrary")),
        name="hgrn2",
    )(proj, proj, proj, proj, lb_row, gain_row)


def _s5_kernel(u_ref, bblk_ref, lev_re_ref, lev_im_ref, pw_re_ref, pw_im_ref, cblk_ref, d_ref,
               gw_ref, gb_ref, o_ref, cr_ref, ci_ref, *, tb):
    @pl.when(pl.program_id(1) == 0)
    def _():
        cr_ref[...] = jnp.zeros(cr_ref.shape, F32)
        ci_ref[...] = jnp.zeros(ci_ref.shape, F32)

    u = u_ref[...]
    bu = _dot(u, bblk_ref[...])
    xr = bu[:, :S5_NSTATE]
    xi = bu[:, S5_NSTATE:]
    row = lax.broadcasted_iota(jnp.int32, xr.shape, 0)
    nlev = lev_re_ref.shape[0]
    for j in range(nlev):
        d = 1 << j
        ar = lev_re_ref[j:j + 1, :]
        ai = lev_im_ref[j:j + 1, :]
        keep = row >= d
        sr = jnp.where(keep, pltpu.roll(xr, d, 0), 0.0)
        si = jnp.where(keep, pltpu.roll(xi, d, 0), 0.0)
        xr, xi = xr + ar * sr - ai * si, xi + ar * si + ai * sr
    cr = cr_ref[...]
    ci = ci_ref[...]
    pr = pw_re_ref[...]
    pi = pw_im_ref[...]
    xr, xi = xr + pr * cr - pi * ci, xi + pr * ci + pi * cr
    cr_ref[...] = xr[tb - 1:tb, :]
    ci_ref[...] = xi[tb - 1:tb, :]
    y = (_dot(xr.astype(BF16), cblk_ref[:S5_NSTATE, :]) + _dot(xi.astype(BF16), cblk_ref[S5_NSTATE:, :])
         + d_ref[...] * u.astype(F32))
    y = jax.nn.gelu(y)
    zg = _dot(y.astype(BF16), gw_ref[...]) + gb_ref[...]
    o_ref[...] = (y * jax.nn.sigmoid(zg)).astype(o_ref.dtype)


def _s5_params(lam_re, lam_im, log_step, b_re, b_im, c_re, c_im, tb):
    lam_re = jnp.minimum(lam_re.astype(F32), S5_EIG_CLIP)
    lam_im = lam_im.astype(F32)
    step = jnp.exp(log_step.astype(F32))[:, None]
    mag = jnp.exp(lam_re * step)
    phase = lam_im * step
    a_re = mag * jnp.cos(phase)
    a_im = mag * jnp.sin(phase)
    denom = lam_re * lam_re + lam_im * lam_im
    num_re = a_re - 1.0
    gam_re = (num_re * lam_re + a_im * lam_im) / denom
    gam_im = (a_im * lam_re - num_re * lam_im) / denom
    b_re = b_re.astype(F32)
    b_im = b_im.astype(F32)
    bb_re = gam_re[..., None] * b_re - gam_im[..., None] * b_im
    bb_im = gam_re[..., None] * b_im + gam_im[..., None] * b_re
    eye = jnp.eye(S5_GROUPS, dtype=F32)

    def in_blk(bb):
        return jnp.einsum('gnc,gh->gchn', bb, eye).reshape(S5_WIDTH, S5_NSTATE)

    def out_blk(cc):
        return jnp.einsum('gcn,gh->gnhc', cc.astype(F32), eye).reshape(S5_NSTATE, S5_WIDTH)

    bblk = jnp.concatenate([in_blk(bb_re), in_blk(bb_im)], axis=1).astype(BF16)
    cblk = jnp.concatenate([out_blk(c_re), -out_blk(c_im)], axis=0).astype(BF16)
    ar = a_re.reshape(1, S5_NSTATE)
    ai = a_im.reshape(1, S5_NSTATE)
    nlev = int(math.log2(tb))
    lev_re, lev_im = [ar], [ai]
    pw_re, pw_im = ar, ai
    for _ in range(nlev):
        sr, si = lev_re[-1], lev_im[-1]
        pw_re, pw_im = (jnp.concatenate([pw_re, pw_re * sr - pw_im * si], axis=0),
                        jnp.concatenate([pw_im, pw_re * si + pw_im * sr], axis=0))
        lev_re.append(sr * sr - si * si)
        lev_im.append(2.0 * sr * si)
    lev_re = jnp.concatenate(lev_re[:nlev], axis=0)
    lev_im = jnp.concatenate(lev_im[:nlev], axis=0)
    return bblk, cblk, lev_re, lev_im, pw_re, pw_im


def _s5(proj, params, d_row, glu_w, glu_b, bsz, seq, tb):
    n = proj.shape[0]
    nb = seq // tb
    bblk, cblk, lev_re, lev_im, pw_re, pw_im = params
    nlev = lev_re.shape[0]
    ucol = COL_S5_U // S5_WIDTH

    def full(shape):
        return pl.BlockSpec(shape, lambda b, i: (0,) * len(shape))

    return pl.pallas_call(
        functools.partial(_s5_kernel, tb=tb),
        grid=(bsz, nb),
        in_specs=[pl.BlockSpec((tb, S5_WIDTH), lambda b, i: (b * nb + i, ucol)),
                  full((S5_WIDTH, 2 * S5_NSTATE)),
                  full((nlev, S5_NSTATE)), full((nlev, S5_NSTATE)),
                  full((tb, S5_NSTATE)), full((tb, S5_NSTATE)),
                  full((2 * S5_NSTATE, S5_WIDTH)),
                  full((1, S5_WIDTH)), full((S5_WIDTH, S5_WIDTH)), full((1, S5_WIDTH))],
        out_specs=pl.BlockSpec((tb, S5_WIDTH), lambda b, i: (b * nb + i, 0)),
        out_shape=jax.ShapeDtypeStruct((n, S5_WIDTH), BF16),
        scratch_shapes=[pltpu.VMEM((1, S5_NSTATE), F32), pltpu.VMEM((1, S5_NSTATE), F32)],
        compiler_params=_cparams(("parallel", "arbitrary")),
        name="s5",
    )(proj, bblk, lev_re, lev_im, pw_re, pw_im, cblk, d_row, glu_w, glu_b)


def _merge_kernel(x_ref, ya_ref, yb_ref, yc_ref, yd_ref, gate_ref, wa_ref, wb_ref, wc_ref, wd_ref,
                  wo_ref, gn_ref, wrh_ref, wrl_ref, br_ref, xo_ref, h_ref, lg_ref):
    d = x_ref.shape[1]
    merged = None
    for i, (y_ref, w_ref) in enumerate(((ya_ref, wa_ref), (yb_ref, wb_ref),
                                        (yc_ref, wc_ref), (yd_ref, wd_ref))):
        gate = jax.nn.sigmoid(gate_ref[:, i * d:(i + 1) * d].astype(F32))
        term = gate * _dot(y_ref[...], w_ref[...])
        merged = term if merged is None else merged + term
    xn = x_ref[...] + _dot(merged.astype(BF16), wo_ref[...])
    xo_ref[...] = xn
    ms = jnp.mean(xn * xn, axis=-1, keepdims=True)
    h = xn * lax.rsqrt(ms + RMS_EPS) * gn_ref[...]
    hi, lo = _split_bf16(h)
    h_ref[...] = hi
    wrh = wrh_ref[...]
    lg_ref[...] = _dot(hi, wrh) + _dot(lo, wrh) + _dot(hi, wrl_ref[...]) + br_ref[...]


def _merge(x2, ya, yb, yc, yd, proj, wa, wb, wc, wd, wo, gn_row, wr_hi, wr_lo, br_row, tm):
    n, d = x2.shape
    assert COL_GATE % (4 * d) == 0
    gcol = COL_GATE // (4 * d)

    def rows(width, cc=0):
        return pl.BlockSpec((tm, width), lambda i: (i, cc))

    def full(arr):
        return pl.BlockSpec(arr.shape, lambda i: (0, 0))

    return pl.pallas_call(
        _merge_kernel,
        grid=(n // tm,),
        in_specs=[rows(d), rows(ya.shape[1]), rows(yb.shape[1]), rows(yc.shape[1]), rows(yd.shape[1]),
                  rows(4 * d, gcol),
                  full(wa), full(wb), full(wc), full(wd), full(wo), full(gn_row),
                  full(wr_hi), full(wr_lo), full(br_row)],
        out_specs=[rows(d), rows(d), rows(LANES)],
        out_shape=[jax.ShapeDtypeStruct((n, d), F32), jax.ShapeDtypeStruct((n, d), BF16),
                   jax.ShapeDtypeStruct((n, LANES), F32)],
        compiler_params=_cparams(("parallel",)),
        name="merge",
    )(x2, ya, yb, yc, yd, proj, wa, wb, wc, wd, wo, gn_row, wr_hi, wr_lo, br_row)


def _first_index(mask, lane):
    return jnp.min(jnp.where(mask, lane, float(LANES)), axis=-1, keepdims=True)


def _moe_kernel(h_ref, x_ref, lg_ref, wg_ref, wu_ref, wd_ref, o_ref, cw_ref, acc_ref):
    e = pl.program_id(1)
    h = h_ref[...]

    @pl.when(e == 0)
    def _():
        logits = lg_ref[...]
        lane = lax.broadcasted_iota(jnp.int32, logits.shape, 1).astype(F32)
        neg = -jnp.inf
        gl = jnp.where(lane < N_GROUPS, logits, neg)
        gmax = jnp.max(gl, axis=-1, keepdims=True)
        gsum = jnp.sum(jnp.exp(gl - gmax), axis=-1, keepdims=True)
        g_val = 1.0 / gsum
        g_idx = _first_index(gl == gmax, lane)
        lo = N_GROUPS + EXPERTS_PER_GROUP * g_idx
        el = jnp.where((lane >= lo) & (lane < lo + EXPERTS_PER_GROUP), logits, neg)
        emax = jnp.max(el, axis=-1, keepdims=True)
        esum = jnp.sum(jnp.exp(el - emax), axis=-1, keepdims=True)
        i1 = _first_index(el == emax, lane)
        el2 = jnp.where(lane == i1, neg, el)
        e2max = jnp.max(el2, axis=-1, keepdims=True)
        i2 = _first_index(el2 == e2max, lane)
        p1 = 1.0 / esum
        p2 = jnp.exp(e2max - emax) / esum
        tot = p1 + p2
        cw_ref[...] = (jnp.where(lane == i1, g_val * (p1 / tot), 0.0)
                       + jnp.where(lane == i2, g_val * (p2 / tot), 0.0))
        acc_ref[...] = jnp.zeros(acc_ref.shape, F32)

    a = _dot(h, wg_ref[0])
    hid = (a * jax.nn.sigmoid(a)) * _dot(h, wu_ref[0])
    lane = lax.broadcasted_iota(jnp.int32, cw_ref.shape, 1)
    cw = jnp.sum(jnp.where(lane == e + N_GROUPS, cw_ref[...], 0.0), axis=-1, keepdims=True)
    acc_ref[...] += cw * _dot(hid.astype(BF16), wd_ref[0])

    @pl.when(e == pl.num_programs(1) - 1)
    def _():
        o_ref[...] = x_ref[...] + acc_ref[...]


def _moe(h, x2, logits, wg, wu, wd, tm):
    n, d = x2.shape
    ne, _, de = wg.shape
    return pl.pallas_call(
        _moe_kernel,
        grid=(n // tm, ne),
        in_specs=[pl.BlockSpec((tm, d), lambda i, e: (i, 0)),
                  pl.BlockSpec((tm, d), lambda i, e: (i, 0)),
                  pl.BlockSpec((tm, LANES), lambda i, e: (i, 0)),
                  pl.BlockSpec((1, d, de), lambda i, e: (e, 0, 0)),
                  pl.BlockSpec((1, d, de), lambda i, e: (e, 0, 0)),
                  pl.BlockSpec((1, de, d), lambda i, e: (e, 0, 0))],
        out_specs=pl.BlockSpec((tm, d), lambda i, e: (i, 0)),
        out_shape=jax.ShapeDtypeStruct((n, d), F32),
        scratch_shapes=[pltpu.VMEM((tm, LANES), F32), pltpu.VMEM((tm, d), F32)],
        compiler_params=_cparams(("parallel", "arbitrary")),
        name="moe",
    )(h, x2, logits, wg, wu, wd)


def _pick(n, pref):
    t = min(n, pref)
    while n % t:
        t //= 2
    return t


def kernel(x, positions, norm_mix, w_in, da_q_gain, da_k_gain, da_lambda_q1, da_lambda_k1,
           da_lambda_q2, da_lambda_k2, da_subln_gain, hg_lower_bounds, hg_out_gain,
           s5_lambda_re, s5_lambda_im, s5_log_step, s5_b_re, s5_b_im, s5_c_re, s5_c_im,
           s5_d, s5_glu_w, s5_glu_b, w_branch_attn, w_branch_sb, w_branch_hgrn, w_branch_s5,
           w_out, norm_ffn, router_group_w, router_group_b, router_expert_w, router_expert_b,
           expert_w_gate, expert_w_up, expert_w_down):
    bsz, seq, d = x.shape
    depth = w_in.shape[0]
    n = bsz * seq
    assert w_in.shape[2] == IN_COLS and seq % 128 == 0
    tm = _pick(n, 1024)
    tq = _pick(seq, 256)
    tb = _pick(seq, 256)

    x2 = x.reshape(n, d).astype(F32)
    cos_t, sina_t, sinb_t = _rope_tables(positions.reshape(n, 1).astype(jnp.int32), tm)

    lb_all = jnp.cumsum(jax.nn.softmax(hg_lower_bounds.astype(F32), axis=0), axis=0)
    lb_all = lb_all - lb_all[0:1]

    for l in range(depth):
        lambda_init = DA_LAMBDA_INIT_BASE - DA_LAMBDA_INIT_SCALE * math.exp(-DA_LAMBDA_INIT_RATE * l)
        w_l = jnp.concatenate([w_in[l, :, REF_GATE_START:], w_in[l, :, :REF_GATE_START]], axis=1)
        proj = _norm_proj(x2, norm_mix[l].astype(F32)[None, :], w_l.astype(BF16), tm, 768)

        qk_gain = jnp.concatenate([jnp.tile(da_q_gain[l].astype(F32), 2 * DA_HEADS),
                                   jnp.tile(da_k_gain[l].astype(F32), 2 * DA_HEADS)])[None, :]
        qk = _qk_prep(proj, qk_gain, cos_t, sina_t, sinb_t, tm)
        y_a = _diff_attn(qk, proj,
                         da_lambda_q1[l].astype(F32)[None, :], da_lambda_k1[l].astype(F32)[None, :],
                         da_lambda_q2[l].astype(F32)[None, :], da_lambda_k2[l].astype(F32)[None, :],
                         da_subln_gain[l].astype(F32)[None, :], bsz, seq, tq, lambda_init)
        y_b = _stick_break(proj, bsz, seq, tq)
        y_c = _hgrn(proj, lb_all[l][None, :], jnp.tile(hg_out_gain[l].astype(F32), HG_HEADS)[None, :],
                    bsz, seq, tb)
        s5p = _s5_params(s5_lambda_re[l], s5_lambda_im[l], s5_log_step[l], s5_b_re[l], s5_b_im[l],
                         s5_c_re[l], s5_c_im[l], tb)
        y_d = _s5(proj, s5p, s5_d[l].astype(F32)[None, :], s5_glu_w[l].astype(BF16),
                  s5_glu_b[l].astype(F32)[None, :], bsz, seq, tb)

        wr = jnp.concatenate([router_group_w[l], router_expert_w[l]], axis=1).astype(F32)
        wr = jnp.pad(wr, ((0, 0), (0, LANES - wr.shape[1])))
        wr_hi = wr.astype(BF16)
        wr_lo = (wr - wr_hi.astype(F32)).astype(BF16)
        br = jnp.concatenate([router_group_b[l], router_expert_b[l]]).astype(F32)
        br = jnp.pad(br, (0, LANES - br.shape[0]))[None, :]
        x2, h, logits = _merge(x2, y_a, y_b, y_c, y_d, proj,
                               w_branch_attn[l].astype(BF16), w_branch_sb[l].astype(BF16),
                               w_branch_hgrn[l].astype(BF16), w_branch_s5[l].astype(BF16),
                               w_out[l].astype(BF16), norm_ffn[l].astype(F32)[None, :],
                               wr_hi, wr_lo, br, _pick(n, 512))
        x2 = _moe(h, x2, logits, expert_w_gate[l].astype(BF16), expert_w_up[l].astype(BF16),
                  expert_w_down[l].astype(BF16), tm)

    return x2.reshape(bsz, seq, d).astype(x.dtype)
```

```python
import functools
import math

import jax
import jax.numpy as jnp
import numpy as np
from jax import lax
from jax.experimental import pallas as pl
from jax.experimental.pallas import tpu as pltpu

F32 = jnp.float32
BF16 = jnp.bfloat16

RMS_EPS = 1e-6
LANES = 128

DA_HEADS = 4
DA_QK_DIM = 64
ROPE_THETA = 500000.0
ROPE_DIM = DA_QK_DIM // 4
DA_LAMBDA_INIT_BASE = 0.8
DA_LAMBDA_INIT_SCALE = 0.6
DA_LAMBDA_INIT_RATE = 0.3

SB_HEADS = 4
SB_HEAD_DIM = 64

HG_HEADS = 4
HG_DIM = 64
HG_COLS = HG_HEADS * HG_DIM
HG_SUB = 16

S5_GROUPS = 16
S5_GROUP_CH = 16
S5_STATE = 64
S5_WIDTH = S5_GROUPS * S5_GROUP_CH
S5_NSTATE = S5_GROUPS * S5_STATE
S5_EIG_CLIP = -1e-4

N_GROUPS = 4
EXPERTS_PER_GROUP = 4
N_EXPERTS = N_GROUPS * EXPERTS_PER_GROUP

REF_GATE_START = 3584
COL_GATE = 0
COL_DA_Q = 4096
COL_DA_K = 4608
COL_DA_V = 5120
COL_SB_Q = 5632
COL_SB_K = 5888
COL_SB_V = 6144
COL_HG_F = 6400
COL_HG_I = 6656
COL_HG_Q = 6912
COL_HG_G = 7168
COL_S5_U = 7424
IN_COLS = 7680

VMEM_LIMIT = 48 * 1024 * 1024

DA_TQ = 512
DA_GROUP = 2
SB_GROUP = 4

_NT = (((1,), (1,)), ((), ()))
_TN = (((0,), (0,)), ((), ()))


def _cparams(sem):
    return pltpu.CompilerParams(dimension_semantics=sem, vmem_limit_bytes=VMEM_LIMIT)


def _dot(a, b):
    return jnp.dot(a, b, preferred_element_type=F32)


def _norm_proj_kernel(x_ref, g_ref, w_ref, o_ref, h_ref):
    @pl.when(pl.program_id(1) == 0)
    def _():
        x = x_ref[...]
        ms = jnp.mean(x * x, axis=-1, keepdims=True)
        h_ref[...] = (x * lax.rsqrt(ms + RMS_EPS) * g_ref[...]).astype(BF16)

    o_ref[...] = _dot(h_ref[...], w_ref[...]).astype(o_ref.dtype)


def _norm_proj(x2, gain, w, tm, tn):
    n, d = x2.shape
    cols = w.shape[1]
    return pl.pallas_call(
        _norm_proj_kernel,
        grid=(n // tm, cols // tn),
        in_specs=[pl.BlockSpec((tm, d), lambda i, j: (i, 0)),
                  pl.BlockSpec((1, d), lambda i, j: (0, 0)),
                  pl.BlockSpec((d, tn), lambda i, j: (0, j))],
        out_specs=pl.BlockSpec((tm, tn), lambda i, j: (i, j)),
        out_shape=jax.ShapeDtypeStruct((n, cols), BF16),
        scratch_shapes=[pltpu.VMEM((tm, d), BF16)],
        compiler_params=_cparams(("parallel", "arbitrary")),
        name="norm_proj",
    )(x2, gain, w)


def _rope_kernel(pos_ref, invf_ref, sa_ref, sb_ref, cos_ref, sina_ref, sinb_ref):
    ang = pos_ref[...].astype(F32) * invf_ref[...]
    c = jnp.cos(ang)
    s = jnp.sin(ang)
    cos_ref[...] = c
    sina_ref[...] = s * sa_ref[...]
    sinb_ref[...] = s * sb_ref[...]


def _rope_tables(pos_col, tm):
    n = pos_col.shape[0]
    half = ROPE_DIM // 2
    inv_freq = jnp.exp(-math.log(ROPE_THETA) * jnp.arange(half, dtype=F32) * (2.0 / ROPE_DIM))
    lane = np.arange(LANES) % DA_QK_DIM
    invf = jnp.where(lane < ROPE_DIM, inv_freq[lane % half], 0.0).astype(F32)[None, :]
    sgn_a = jnp.asarray(np.where(lane < half, -1.0, 0.0), F32)[None, :]
    sgn_b = jnp.asarray(np.where((lane >= half) & (lane < ROPE_DIM), 1.0, 0.0), F32)[None, :]
    row = pl.BlockSpec((1, LANES), lambda i: (0, 0))
    tab = pl.BlockSpec((tm, LANES), lambda i: (i, 0))
    shp = jax.ShapeDtypeStruct((n, LANES), F32)
    return pl.pallas_call(
        _rope_kernel,
        grid=(n // tm,),
        in_specs=[pl.BlockSpec((tm, 1), lambda i: (i, 0)), row, row, row],
        out_specs=[tab, tab, tab],
        out_shape=[shp, shp, shp],
        compiler_params=_cparams(("parallel",)),
        name="rope_tables",
    )(pos_col, invf, sgn_a, sgn_b)


def _qk_prep_kernel(x_ref, gain_ref, cos_ref, sina_ref, sinb_ref, bd_ref, o_ref):
    c = cos_ref[...]
    sa = sina_ref[...]
    sb = sinb_ref[...]
    bd = bd_ref[...]
    n_tiles = x_ref.shape[1] // LANES
    for j in range(n_tiles):
        sl = slice(j * LANES, (j + 1) * LANES)
        t = x_ref[:, sl].astype(F32)
        ss = _dot((t * t).astype(BF16), bd)
        y = t * lax.rsqrt(ss * (1.0 / DA_QK_DIM) + RMS_EPS) * gain_ref[:, sl]
        y = (y * c + pltpu.roll(y, LANES - ROPE_DIM // 2, 1) * sa
             + pltpu.roll(y, ROPE_DIM // 2, 1) * sb)
        if j < n_tiles // 2:
            y = y * (DA_QK_DIM ** -0.5)
        o_ref[:, sl] = y.astype(BF16)


def _qk_prep(proj, gain_row, cos_t, sina_t, sinb_t, tm):
    n = proj.shape[0]
    w = 2 * DA_HEADS * 2 * DA_QK_DIM
    seg = np.arange(LANES) // DA_QK_DIM
    bd = jnp.asarray(seg[:, None] == seg[None, :], BF16)
    tab = pl.BlockSpec((tm, LANES), lambda i: (i, 0))
    return pl.pallas_call(
        _qk_prep_kernel,
        grid=(n // tm,),
        in_specs=[pl.BlockSpec((tm, w), lambda i: (i, COL_DA_Q // w)),
                  pl.BlockSpec((1, w), lambda i: (0, 0)),
                  tab, tab, tab,
                  pl.BlockSpec((LANES, LANES), lambda i: (0, 0))],
        out_specs=pl.BlockSpec((tm, w), lambda i: (i, 0)),
        out_shape=jax.ShapeDtypeStruct((n, w), BF16),
        compiler_params=_cparams(("parallel",)),
        name="qk_prep",
    )(proj, gain_row, cos_t, sina_t, sinb_t, bd)


def _diff_attn_kernel(q_ref, k_ref, v_ref, lq1_ref, lk1_ref, lq2_ref, lk2_ref, sg_ref, o_ref,
                      m_ref, l_ref, acc_ref, *, tq, group, lambda_init):
    qi = pl.program_id(2)
    q = q_ref[...]
    lane = lax.broadcasted_iota(jnp.int32, q.shape, 1)
    zero = jnp.zeros_like(q)
    qs = (jnp.where(lane < DA_QK_DIM, q, zero), jnp.where(lane >= DA_QK_DIM, q, zero))
    m_ref[...] = jnp.full(m_ref.shape, -jnp.inf, F32)
    l_ref[...] = jnp.zeros(l_ref.shape, F32)
    acc_ref[...] = jnp.zeros(acc_ref.shape, F32)

    def tile(start, width, masked):
        kb = k_ref[pl.ds(start, width), :]
        vb = v_ref[pl.ds(start, width), :]
        n_lt = width // LANES
        for c in range(2):
            s = lax.dot_general(qs[c], kb, _NT, preferred_element_type=F32)
            if masked:
                row = lax.broadcasted_iota(jnp.int32, s.shape, 0)
                col = lax.broadcasted_iota(jnp.int32, s.shape, 1)
                s = jnp.where(col <= row, s, -jnp.inf)
            st = [s[:, j * LANES:(j + 1) * LANES] for j in range(n_lt)]
            smax = st[0]
            for j in range(1, n_lt):
                smax = jnp.maximum(smax, st[j])
            m_prev = m_ref[c]
            m_new = jnp.maximum(m_prev, jnp.max(smax, axis=-1, keepdims=True))
            alpha = jnp.exp(m_prev - m_new)
            ps = [jnp.exp(t - m_new) for t in st]
            lsum = ps[0]
            for j in range(1, n_lt):
                lsum = lsum + ps[j]
            p = jnp.concatenate([t.astype(BF16) for t in ps], axis=1)
            l_ref[c] = alpha * l_ref[c] + lsum
            acc_ref[c] = alpha * acc_ref[c] + _dot(p, vb)
            m_ref[c] = m_new

    n_group = qi // group

    def body_group(gi, carry):
        tile(pl.multiple_of(gi * (group * tq), group * tq), group * tq, False)
        return carry

    def body_single(ki, carry):
        tile(pl.multiple_of(ki * tq, tq), tq, False)
        return carry

    lax.fori_loop(0, n_group, body_group, 0)
    lax.fori_loop(n_group * group, qi, body_single, 0)
    tile(pl.multiple_of(qi * tq, tq), tq, True)

    lam = (jnp.exp(jnp.sum(lq1_ref[...] * lk1_ref[...], axis=-1, keepdims=True))
           - jnp.exp(jnp.sum(lq2_ref[...] * lk2_ref[...], axis=-1, keepdims=True)) + lambda_init)
    l1 = jnp.sum(l_ref[0], axis=-1, keepdims=True)
    l2 = jnp.sum(l_ref[1], axis=-1, keepdims=True)
    o = acc_ref[0] / l1 - lam * (acc_ref[1] / l2)
    ms = jnp.mean(o * o, axis=-1, keepdims=True)
    o = o * lax.rsqrt(ms + RMS_EPS) * sg_ref[...] * (1.0 - lambda_init)
    o_ref[...] = o.astype(o_ref.dtype)


def _diff_attn(qk, proj, lq1, lk1, lq2, lk2, subln, bsz, seq, tq, group, lambda_init):
    n = qk.shape[0]
    nq = seq // tq
    kcol = (DA_HEADS * 2 * DA_QK_DIM) // LANES
    vcol = COL_DA_V // LANES
    vec = pl.BlockSpec((1, DA_QK_DIM), lambda b, h, i: (0, 0))
    return pl.pallas_call(
        functools.partial(_diff_attn_kernel, tq=tq, group=group, lambda_init=lambda_init),
        grid=(bsz, DA_HEADS, nq),
        in_specs=[pl.BlockSpec((tq, LANES), lambda b, h, i: (b * nq + i, h)),
                  pl.BlockSpec((seq, LANES), lambda b, h, i: (b, kcol + h)),
                  pl.BlockSpec((seq, LANES), lambda b, h, i: (b, vcol + h)),
                  vec, vec, vec, vec,
                  pl.BlockSpec((1, LANES), lambda b, h, i: (0, 0))],
        out_specs=pl.BlockSpec((tq, LANES), lambda b, h, i: (b * nq + i, h)),
        out_shape=jax.ShapeDtypeStruct((n, DA_HEADS * LANES), BF16),
        scratch_shapes=[pltpu.VMEM((2, tq, LANES), F32), pltpu.VMEM((2, tq, LANES), F32),
                        pltpu.VMEM((2, tq, LANES), F32)],
        compiler_params=_cparams(("parallel", "parallel", "arbitrary")),
        name="diff_attn",
    )(qk, qk, proj, lq1, lk1, lq2, lk2, subln)


def _stick_break_kernel(q_ref, k_ref, v_ref, o_ref, r_ref, acc_ref, *, tq, group):
    qi = pl.program_id(2)
    q = q_ref[...] * (SB_HEAD_DIM ** -0.5)
    lane = lax.broadcasted_iota(jnp.int32, q.shape, 1)
    zero = jnp.zeros_like(q)
    qs = (jnp.where(lane < SB_HEAD_DIM, q, zero), jnp.where(lane >= SB_HEAD_DIM, q, zero))
    row = lax.broadcasted_iota(jnp.int32, (tq, tq), 0)
    col = lax.broadcasted_iota(jnp.int32, (tq, tq), 1)
    incl = jnp.where(row >= col, 1.0, 0.0).astype(BF16)
    strict = col < row
    n_lt = tq // LANES
    r_ref[...] = jnp.zeros(r_ref.shape, F32)
    acc_ref[...] = jnp.zeros(acc_ref.shape, F32)

    def sub_block(ki, masked, r):
        start = pl.multiple_of(ki * tq, tq)
        kb = k_ref[pl.ds(start, tq), :]
        vb = v_ref[pl.ds(start, tq), :]
        ws, r_new = [], []
        for c in range(2):
            z = lax.dot_general(qs[c], kb, _NT, preferred_element_type=F32)
            log_keep = -(jnp.maximum(z, 0.0) + jnp.log(1.0 + jnp.exp(-jnp.abs(z))))
            if masked:
                log_keep = jnp.where(strict, log_keep, 0.0)
            cum = _dot(log_keep.astype(BF16), incl)
            w = jnp.exp(z + cum + jnp.concatenate([r[c]] * n_lt, axis=1))
            if masked:
                w = jnp.where(strict, w, 0.0)
            ws.append(w.astype(BF16))
            r_new.append(r[c] + cum[:, 0:1])
        lane_v = lax.broadcasted_iota(jnp.int32, vb.shape, 1)
        zv = jnp.zeros_like(vb)
        v2 = jnp.concatenate([jnp.where(lane_v < SB_HEAD_DIM, vb, zv),
                              jnp.where(lane_v >= SB_HEAD_DIM, vb, zv)], axis=0)
        return _dot(jnp.concatenate(ws, axis=1), v2), r_new

    def run(blocks, masked):
        r = [r_ref[0], r_ref[1]]
        total = None
        for ki in blocks:
            part, r = sub_block(ki, masked, r)
            total = part if total is None else total + part
        acc_ref[...] += total
        r_ref[0] = r[0]
        r_ref[1] = r[1]

    run([qi], True)
    rem = qi % group

    def body_single(j, carry):
        run([qi - 1 - j], False)
        return carry

    def body_group(gi, carry):
        top = qi - rem - gi * group
        run([top - 1 - j for j in range(group)], False)
        return carry

    lax.fori_loop(0, rem, body_single, 0)
    lax.fori_loop(0, qi // group, body_group, 0)
    o_ref[...] = acc_ref[...].astype(o_ref.dtype)


def _stick_break(proj, bsz, seq, tq, group):
    n = proj.shape[0]
    nq = seq // tq
    pairs = (SB_HEADS * SB_HEAD_DIM) // LANES
    qc, kc, vc = COL_SB_Q // LANES, COL_SB_K // LANES, COL_SB_V // LANES
    return pl.pallas_call(
        functools.partial(_stick_break_kernel, tq=tq, group=group),
        grid=(bsz, pairs, nq),
        in_specs=[pl.BlockSpec((tq, LANES), lambda b, p, i: (b * nq + i, qc + p)),
                  pl.BlockSpec((seq, LANES), lambda b, p, i: (b, kc + p)),
                  pl.BlockSpec((seq, LANES), lambda b, p, i: (b, vc + p))],
        out_specs=pl.BlockSpec((tq, LANES), lambda b, p, i: (b * nq + i, p)),
        out_shape=jax.ShapeDtypeStruct((n, pairs * LANES), BF16),
        scratch_shapes=[pltpu.VMEM((2, tq, LANES), F32), pltpu.VMEM((tq, LANES), F32)],
        compiler_params=_cparams(("parallel", "parallel", "arbitrary")),
        name="stick_break",
    )(proj, proj, proj)


def _split_bf16(x):
    hi = x.astype(BF16)
    lo = (x - hi.astype(F32)).astype(BF16)
    return hi, lo


def _hgrn_kernel(f_ref, i_ref, q_ref, g_ref, lb_ref, gain_ref, o_ref,
                 st_ref, b_ref, e_ref, qq_ref, kk_ref, vv_ref, oo_ref, gm_ref, *, tb):
    @pl.when(pl.program_id(1) == 0)
    def _():
        st_ref[...] = jnp.zeros(st_ref.shape, F32)

    nsub = tb // HG_SUB
    z = f_ref[...].astype(F32)
    lb = lb_ref[...]
    sp = jnp.maximum(z, 0.0) + jnp.log(1.0 + jnp.exp(-jnp.abs(z)))
    log_sig = z - sp
    a = jnp.log(lb)
    c = jnp.log(1.0 - lb) + log_sig
    mx = jnp.maximum(a, c)
    log_f = mx + jnp.log(jnp.exp(a - mx) + jnp.exp(c - mx))
    key = (1.0 - lb) * jax.nn.sigmoid(-z)

    row = lax.broadcasted_iota(jnp.int32, (tb, tb), 0)
    col = lax.broadcasted_iota(jnp.int32, (tb, tb), 1)
    sub_shift = HG_SUB.bit_length() - 1
    dim_shift = HG_DIM.bit_length() - 1
    same = (row >> sub_shift) == (col >> sub_shift)
    tri = jnp.where(same & (col <= row), 1.0, 0.0).astype(BF16)
    blk = jnp.where(same, 1.0, 0.0).astype(BF16)
    hi, lo = _split_bf16(log_f)
    b = _dot(tri, hi) + _dot(tri, lo)
    e = _dot(blk, hi) + _dot(blk, lo)
    b_ref[...] = b
    e_ref[...] = e
    qq_ref[...] = q_ref[...].astype(F32)
    kk_ref[...] = key
    vv_ref[...] = i_ref[...].astype(F32)

    seg_r = lax.broadcasted_iota(jnp.int32, (HG_COLS, HG_COLS), 0) >> dim_shift
    seg_c = lax.broadcasted_iota(jnp.int32, (HG_COLS, HG_COLS), 1) >> dim_shift
    head_mask = seg_r == seg_c
    ones_bd = jnp.where(head_mask, 1.0, 0.0).astype(BF16)
    sel = jnp.where(lax.broadcasted_iota(jnp.int32, (HG_SUB, HG_SUB * HG_SUB), 0)
                    == (lax.broadcasted_iota(jnp.int32, (HG_SUB, HG_SUB * HG_SUB), 1) >> sub_shift),
                    1.0, 0.0).astype(BF16)
    srow = lax.broadcasted_iota(jnp.int32, (HG_SUB, HG_COLS), 0)

    def chunk(ci, carry):
        r0 = pl.multiple_of(ci * HG_SUB, HG_SUB)
        bi = b_ref[pl.ds(r0, HG_SUB), :]
        ei = e_ref[pl.ds(r0, HG_SUB), :]
        qi = qq_ref[pl.ds(r0, HG_SUB), :]
        ki = kk_ref[pl.ds(r0, HG_SUB), :]
        vi = vv_ref[pl.ds(r0, HG_SUB), :]
        for t in range(HG_SUB):
            d = jnp.exp(jnp.minimum(bi[t:t + 1, :] - bi, 0.0))
            g = jnp.where(srow <= t, qi[t:t + 1, :] * ki * d, 0.0)
            gm_ref[t * HG_SUB:(t + 1) * HG_SUB, :] = g.astype(BF16)
        sc = _dot(gm_ref[...], ones_bd)
        vt = jnp.concatenate([vi] * HG_SUB, axis=0)
        o_intra = _dot(sel, (sc * vt).astype(BF16))
        st = st_ref[...]
        o_inter = lax.dot_general((qi * jnp.exp(bi)).astype(BF16), st.astype(BF16), _NT,
                                  preferred_element_type=F32)
        oo_ref[pl.ds(r0, HG_SUB), :] = o_intra + o_inter
        ke = ki * jnp.exp(ei - bi)
        upd = lax.dot_general(vi.astype(BF16), ke.astype(BF16), _TN, preferred_element_type=F32)
        st_ref[...] = st * jnp.exp(ei[0:1, :]) + jnp.where(head_mask, upd, 0.0)
        return carry

    lax.fori_loop(0, nsub, chunk, 0)

    o = oo_ref[...]
    ms = _dot((o * o).astype(BF16), ones_bd) * (1.0 / HG_DIM)
    gate = g_ref[...].astype(F32)
    gate = gate * jax.nn.sigmoid(gate)
    o_ref[...] = (o * lax.rsqrt(ms + RMS_EPS) * gain_ref[...] * gate).astype(o_ref.dtype)


def _hgrn(proj, lb_row, gain_row, bsz, seq, tb):
    n = proj.shape[0]
    nb = seq // tb
    w = HG_COLS
    cf, ci, cq, cg = COL_HG_F // w, COL_HG_I // w, COL_HG_Q // w, COL_HG_G // w

    def col(cc):
        return pl.BlockSpec((tb, w), lambda b, i: (b * nb + i, cc))

    rowspec = pl.BlockSpec((1, w), lambda b, i: (0, 0))
    big = pltpu.VMEM((tb, w), F32)
    return pl.pallas_call(
        functools.partial(_hgrn_kernel, tb=tb),
        grid=(bsz, nb),
        in_specs=[col(cf), col(ci), col(cq), col(cg), rowspec, rowspec],
        out_specs=pl.BlockSpec((tb, w), lambda b, i: (b * nb + i, 0)),
        out_shape=jax.ShapeDtypeStruct((n, w), BF16),
        scratch_shapes=[pltpu.VMEM((w, w), F32), big, big, big, big, big, big,
                        pltpu.VMEM((HG_SUB * HG_SUB, w), BF16)],
        compiler_params=_cparams(("parallel", "arbitrary")),
        name="hgrn2",
    )(proj, proj, proj, proj, lb_row, gain_row)


def _s5_kernel(u_ref, bblk_ref, lev_re_ref, lev_im_ref, pw_re_ref, pw_im_ref, cblk_ref, d_ref,
               gw_ref, gb_ref, o_ref, cr_ref, ci_ref, *, tb):
    @pl.when(pl.program_id(1) == 0)
    def _():
        cr_ref[...] = jnp.zeros(cr_ref.shape, F32)
        ci_ref[...] = jnp.zeros(ci_ref.shape, F32)

    u = u_ref[...]
    bu = _dot(u, bblk_ref[...])
    xr = bu[:, :S5_NSTATE]
    xi = bu[:, S5_NSTATE:]
    row = lax.broadcasted_iota(jnp.int32, xr.shape, 0)
    nlev = lev_re_ref.shape[0]
    for j in range(nlev):
        d = 1 << j
        ar = lev_re_ref[j:j + 1, :]
        ai = lev_im_ref[j:j + 1, :]
        keep = row >= d
        sr = jnp.where(keep, pltpu.roll(xr, d, 0), 0.0)
        si = jnp.where(keep, pltpu.roll(xi, d, 0), 0.0)
        xr, xi = xr + ar * sr - ai * si, xi + ar * si + ai * sr
    cr = cr_ref[...]
    ci = ci_ref[...]
    pr = pw_re_ref[...]
    pi = pw_im_ref[...]
    xr, xi = xr + pr * cr - pi * ci, xi + pr * ci + pi * cr
    cr_ref[...] = xr[tb - 1:tb, :]
    ci_ref[...] = xi[tb - 1:tb, :]
    y = (_dot(xr.astype(BF16), cblk_ref[:S5_NSTATE, :]) + _dot(xi.astype(BF16), cblk_ref[S5_NSTATE:, :])
         + d_ref[...] * u.astype(F32))
    y = jax.nn.gelu(y)
    zg = _dot(y.astype(BF16), gw_ref[...]) + gb_ref[...]
    o_ref[...] = (y * jax.nn.sigmoid(zg)).astype(o_ref.dtype)


def _s5_params(lam_re, lam_im, log_step, b_re, b_im, c_re, c_im, tb):
    lam_re = jnp.minimum(lam_re.astype(F32), S5_EIG_CLIP)
    lam_im = lam_im.astype(F32)
    step = jnp.exp(log_step.astype(F32))[:, None]
    mag = jnp.exp(lam_re * step)
    phase = lam_im * step
    a_re = mag * jnp.cos(phase)
    a_im = mag * jnp.sin(phase)
    denom = lam_re * lam_re + lam_im * lam_im
    num_re = a_re - 1.0
    gam_re = (num_re * lam_re + a_im * lam_im) / denom
    gam_im = (a_im * lam_re - num_re * lam_im) / denom
    b_re = b_re.astype(F32)
    b_im = b_im.astype(F32)
    bb_re = gam_re[..., None] * b_re - gam_im[..., None] * b_im
    bb_im = gam_re[..., None] * b_im + gam_im[..., None] * b_re
    eye = jnp.eye(S5_GROUPS, dtype=F32)

    def in_blk(bb):
        return jnp.einsum('gnc,gh->gchn', bb, eye).reshape(S5_WIDTH, S5_NSTATE)

    def out_blk(cc):
        return jnp.einsum('gcn,gh->gnhc', cc.astype(F32), eye).reshape(S5_NSTATE, S5_WIDTH)

    bblk = jnp.concatenate([in_blk(bb_re), in_blk(bb_im)], axis=1).astype(BF16)
    cblk = jnp.concatenate([out_blk(c_re), -out_blk(c_im)], axis=0).astype(BF16)
    ar = a_re.reshape(1, S5_NSTATE)
    ai = a_im.reshape(1, S5_NSTATE)
    nlev = int(math.log2(tb))
    lev_re, lev_im = [ar], [ai]
    pw_re, pw_im = ar, ai
    for _ in range(nlev):
        sr, si = lev_re[-1], lev_im[-1]
        pw_re, pw_im = (jnp.concatenate([pw_re, pw_re * sr - pw_im * si], axis=0),
                        jnp.concatenate([pw_im, pw_re * si + pw_im * sr], axis=0))
        lev_re.append(sr * sr - si * si)
        lev_im.append(2.0 * sr * si)
    lev_re = jnp.concatenate(lev_re[:nlev], axis=0)
    lev_im = jnp.concatenate(lev_im[:nlev], axis=0)
    return bblk, cblk, lev_re, lev_im, pw_re, pw_im


def _s5(proj, params, d_row, glu_w, glu_b, bsz, seq, tb):
    n = proj.shape[0]
    nb = seq // tb
    bblk, cblk, lev_re, lev_im, pw_re, pw_im = params
    nlev = lev_re.shape[0]
    ucol = COL_S5_U // S5_WIDTH

    def full(shape):
        return pl.BlockSpec(shape, lambda b, i: (0,) * len(shape))

    return pl.pallas_call(
        functools.partial(_s5_kernel, tb=tb),
        grid=(bsz, nb),
        in_specs=[pl.BlockSpec((tb, S5_WIDTH), lambda b, i: (b * nb + i, ucol)),
                  full((S5_WIDTH, 2 * S5_NSTATE)),
                  full((nlev, S5_NSTATE)), full((nlev, S5_NSTATE)),
                  full((tb, S5_NSTATE)), full((tb, S5_NSTATE)),
                  full((2 * S5_NSTATE, S5_WIDTH)),
                  full((1, S5_WIDTH)), full((S5_WIDTH, S5_WIDTH)), full((1, S5_WIDTH))],
        out_specs=pl.BlockSpec((tb, S5_WIDTH), lambda b, i: (b * nb + i, 0)),
        out_shape=jax.ShapeDtypeStruct((n, S5_WIDTH), BF16),
        scratch_shapes=[pltpu.VMEM((1, S5_NSTATE), F32), pltpu.VMEM((1, S5_NSTATE), F32)],
        compiler_params=_cparams(("parallel", "arbitrary")),
        name="s5",
    )(proj, bblk, lev_re, lev_im, pw_re, pw_im, cblk, d_row, glu_w, glu_b)


def _merge_kernel(x_ref, ya_ref, yb_ref, yc_ref, yd_ref, gate_ref, wa_ref, wb_ref, wc_ref, wd_ref,
                  wo_ref, gn_ref, wrh_ref, wrl_ref, br_ref, xo_ref, h_ref, lg_ref):
    d = x_ref.shape[1]
    merged = None
    for i, (y_ref, w_ref) in enumerate(((ya_ref, wa_ref), (yb_ref, wb_ref),
                                        (yc_ref, wc_ref), (yd_ref, wd_ref))):
        gate = jax.nn.sigmoid(gate_ref[:, i * d:(i + 1) * d].astype(F32))
        term = gate * _dot(y_ref[...], w_ref[...])
        merged = term if merged is None else merged + term
    xn = x_ref[...] + _dot(merged.astype(BF16), wo_ref[...])
    xo_ref[...] = xn
    ms = jnp.mean(xn * xn, axis=-1, keepdims=True)
    h = xn * lax.rsqrt(ms + RMS_EPS) * gn_ref[...]
    hi, lo = _split_bf16(h)
    h_ref[...] = hi
    wrh = wrh_ref[...]
    lg_ref[...] = _dot(hi, wrh) + _dot(lo, wrh) + _dot(hi, wrl_ref[...]) + br_ref[...]


def _merge(x2, ya, yb, yc, yd, proj, wa, wb, wc, wd, wo, gn_row, wr_hi, wr_lo, br_row, tm):
    n, d = x2.shape
    assert COL_GATE % (4 * d) == 0
    gcol = COL_GATE // (4 * d)

    def rows(width, cc=0):
        return pl.BlockSpec((tm, width), lambda i: (i, cc))

    def full(arr):
        return pl.BlockSpec(arr.shape, lambda i: (0, 0))

    return pl.pallas_call(
        _merge_kernel,
        grid=(n // tm,),
        in_specs=[rows(d), rows(ya.shape[1]), rows(yb.shape[1]), rows(yc.shape[1]), rows(yd.shape[1]),
                  rows(4 * d, gcol),
                  full(wa), full(wb), full(wc), full(wd), full(wo), full(gn_row),
                  full(wr_hi), full(wr_lo), full(br_row)],
        out_specs=[rows(d), rows(d), rows(LANES)],
        out_shape=[jax.ShapeDtypeStruct((n, d), F32), jax.ShapeDtypeStruct((n, d), BF16),
                   jax.ShapeDtypeStruct((n, LANES), F32)],
        compiler_params=_cparams(("parallel",)),
        name="merge",
    )(x2, ya, yb, yc, yd, proj, wa, wb, wc, wd, wo, gn_row, wr_hi, wr_lo, br_row)


def _first_index(mask, lane):
    return jnp.min(jnp.where(mask, lane, float(LANES)), axis=-1, keepdims=True)


def _moe_kernel(h_ref, x_ref, lg_ref, wg_ref, wu_ref, wd_ref, o_ref, cw_ref, acc_ref):
    e = pl.program_id(1)
    h = h_ref[...]

    @pl.when(e == 0)
    def _():
        logits = lg_ref[...]
        lane = lax.broadcasted_iota(jnp.int32, logits.shape, 1).astype(F32)
        neg = -jnp.inf
        gl = jnp.where(lane < N_GROUPS, logits, neg)
        gmax = jnp.max(gl, axis=-1, keepdims=True)
        gsum = jnp.sum(jnp.exp(gl - gmax), axis=-1, keepdims=True)
        g_val = 1.0 / gsum
        g_idx = _first_index(gl == gmax, lane)
        lo = N_GROUPS + EXPERTS_PER_GROUP * g_idx
        el = jnp.where((lane >= lo) & (lane < lo + EXPERTS_PER_GROUP), logits, neg)
        emax = jnp.max(el, axis=-1, keepdims=True)
        esum = jnp.sum(jnp.exp(el - emax), axis=-1, keepdims=True)
        i1 = _first_index(el == emax, lane)
        el2 = jnp.where(lane == i1, neg, el)
        e2max = jnp.max(el2, axis=-1, keepdims=True)
        i2 = _first_index(el2 == e2max, lane)
        p1 = 1.0 / esum
        p2 = jnp.exp(e2max - emax) / esum
        tot = p1 + p2
        cw_ref[...] = (jnp.where(lane == i1, g_val * (p1 / tot), 0.0)
                       + jnp.where(lane == i2, g_val * (p2 / tot), 0.0))
        acc_ref[...] = jnp.zeros(acc_ref.shape, F32)

    a = _dot(h, wg_ref[0])
    hid = (a * jax.nn.sigmoid(a)) * _dot(h, wu_ref[0])
    lane = lax.broadcasted_iota(jnp.int32, cw_ref.shape, 1)
    cw = jnp.sum(jnp.where(lane == e + N_GROUPS, cw_ref[...], 0.0), axis=-1, keepdims=True)
    acc_ref[...] += cw * _dot(hid.astype(BF16), wd_ref[0])

    @pl.when(e == pl.num_programs(1) - 1)
    def _():
        o_ref[...] = x_ref[...] + acc_ref[...]


def _moe(h, x2, logits, wg, wu, wd, tm):
    n, d = x2.shape
    ne, _, de = wg.shape
    return pl.pallas_call(
        _moe_kernel,
        grid=(n // tm, ne),
        in_specs=[pl.BlockSpec((tm, d), lambda i, e: (i, 0)),
                  pl.BlockSpec((tm, d), lambda i, e: (i, 0)),
                  pl.BlockSpec((tm, LANES), lambda i, e: (i, 0)),
                  pl.BlockSpec((1, d, de), lambda i, e: (e, 0, 0)),
                  pl.BlockSpec((1, d, de), lambda i, e: (e, 0, 0)),
                  pl.BlockSpec((1, de, d), lambda i, e: (e, 0, 0))],
        out_specs=pl.BlockSpec((tm, d), lambda i, e: (i, 0)),
        out_shape=jax.ShapeDtypeStruct((n, d), F32),
        scratch_shapes=[pltpu.VMEM((tm, LANES), F32), pltpu.VMEM((tm, d), F32)],
        compiler_params=_cparams(("parallel", "arbitrary")),
        name="moe",
    )(h, x2, logits, wg, wu, wd)


def _pick(n, pref):
    t = min(n, pref)
    while n % t:
        t //= 2
    return t


def kernel(x, positions, norm_mix, w_in, da_q_gain, da_k_gain, da_lambda_q1, da_lambda_k1,
           da_lambda_q2, da_lambda_k2, da_subln_gain, hg_lower_bounds, hg_out_gain,
           s5_lambda_re, s5_lambda_im, s5_log_step, s5_b_re, s5_b_im, s5_c_re, s5_c_im,
           s5_d, s5_glu_w, s5_glu_b, w_branch_attn, w_branch_sb, w_branch_hgrn, w_branch_s5,
           w_out, norm_ffn, router_group_w, router_group_b, router_expert_w, router_expert_b,
           expert_w_gate, expert_w_up, expert_w_down):
    bsz, seq, d = x.shape
    depth = w_in.shape[0]
    n = bsz * seq
    assert w_in.shape[2] == IN_COLS and seq % 128 == 0
    tm = _pick(n, 1024)
    tq = _pick(seq, 256)
    tq_da = _pick(seq, DA_TQ)
    tb = _pick(seq, 256)

    x2 = x.reshape(n, d).astype(F32)
    cos_t, sina_t, sinb_t = _rope_tables(positions.reshape(n, 1).astype(jnp.int32), tm)

    lb_all = jnp.cumsum(jax.nn.softmax(hg_lower_bounds.astype(F32), axis=0), axis=0)
    lb_all = lb_all - lb_all[0:1]

    for l in range(depth):
        lambda_init = DA_LAMBDA_INIT_BASE - DA_LAMBDA_INIT_SCALE * math.exp(-DA_LAMBDA_INIT_RATE * l)
        w_l = jnp.concatenate([w_in[l, :, REF_GATE_START:], w_in[l, :, :REF_GATE_START]], axis=1)
        proj = _norm_proj(x2, norm_mix[l].astype(F32)[None, :], w_l.astype(BF16), tm, 768)

        qk_gain = jnp.concatenate([jnp.tile(da_q_gain[l].astype(F32), 2 * DA_HEADS),
                                   jnp.tile(da_k_gain[l].astype(F32), 2 * DA_HEADS)])[None, :]
        qk = _qk_prep(proj, qk_gain, cos_t, sina_t, sinb_t, tm)
        y_a = _diff_attn(qk, proj,
                         da_lambda_q1[l].astype(F32)[None, :], da_lambda_k1[l].astype(F32)[None, :],
                         da_lambda_q2[l].astype(F32)[None, :], da_lambda_k2[l].astype(F32)[None, :],
                         da_subln_gain[l].astype(F32)[None, :], bsz, seq, tq_da, DA_GROUP, lambda_init)
        y_b = _stick_break(proj, bsz, seq, tq, SB_GROUP)
        y_c = _hgrn(proj, lb_all[l][None, :], jnp.tile(hg_out_gain[l].astype(F32), HG_HEADS)[None, :],
                    bsz, seq, tb)
        s5p = _s5_params(s5_lambda_re[l], s5_lambda_im[l], s5_log_step[l], s5_b_re[l], s5_b_im[l],
                         s5_c_re[l], s5_c_im[l], tb)
        y_d = _s5(proj, s5p, s5_d[l].astype(F32)[None, :], s5_glu_w[l].astype(BF16),
                  s5_glu_b[l].astype(F32)[None, :], bsz, seq, tb)

        wr = jnp.concatenate([router_group_w[l], router_expert_w[l]], axis=1).astype(F32)
        wr = jnp.pad(wr, ((0, 0), (0, LANES - wr.shape[1])))
        wr_hi = wr.astype(BF16)
        wr_lo = (wr - wr_hi.astype(F32)).astype(BF16)
        br = jnp.concatenate([router_group_b[l], router_expert_b[l]]).astype(F32)
        br = jnp.pad(br, (0, LANES - br.shape[0]))[None, :]
        x2, h, logits = _merge(x2, y_a, y_b, y_c, y_d, proj,
                               w_branch_attn[l].astype(BF16), w_branch_sb[l].astype(BF16),
                               w_branch_hgrn[l].astype(BF16), w_branch_s5[l].astype(BF16),
                               w_out[l].astype(BF16), norm_ffn[l].astype(F32)[None, :],
                               wr_hi, wr_lo, br, _pick(n, 512))
        x2 = _moe(h, x2, logits, expert_w_gate[l].astype(BF16), expert_w_up[l].astype(BF16),
                  expert_w_down[l].astype(BF16), tm)

    return x2.reshape(bsz, seq, d).astype(x.dtype)
```

```python
import functools
import math

import jax
import jax.numpy as jnp
import numpy as np
from jax import lax
from jax.experimental import pallas as pl
from jax.experimental.pallas import tpu as pltpu

F32 = jnp.float32
BF16 = jnp.bfloat16

RMS_EPS = 1e-6
LANES = 128
LOG2E = 1.4426950408889634

DA_HEADS = 4
DA_QK_DIM = 64
ROPE_THETA = 500000.0
ROPE_DIM = DA_QK_DIM // 4
DA_LAMBDA_INIT_BASE = 0.8
DA_LAMBDA_INIT_SCALE = 0.6
DA_LAMBDA_INIT_RATE = 0.3

SB_HEADS = 4
SB_HEAD_DIM = 64

HG_HEADS = 4
HG_DIM = 64
HG_COLS = HG_HEADS * HG_DIM
HG_SUB = 16
HG_UNROLL = 4

S5_GROUPS = 16
S5_GROUP_CH = 16
S5_STATE = 64
S5_WIDTH = S5_GROUPS * S5_GROUP_CH
S5_NSTATE = S5_GROUPS * S5_STATE
S5_EIG_CLIP = -1e-4

N_GROUPS = 4
EXPERTS_PER_GROUP = 4
N_EXPERTS = N_GROUPS * EXPERTS_PER_GROUP

REF_GATE_START = 3584
COL_GATE = 0
COL_DA_Q = 4096
COL_DA_K = 4608
COL_DA_V = 5120
COL_SB_Q = 5632
COL_SB_K = 5888
COL_SB_V = 6144
COL_HG_F = 6400
COL_HG_I = 6656
COL_HG_Q = 6912
COL_HG_G = 7168
COL_S5_U = 7424
IN_COLS = 7680

VMEM_LIMIT = 48 * 1024 * 1024

DA_TQ = 512
DA_GROUP = 2
SB_GROUP = 4

_NT = (((1,), (1,)), ((), ()))
_TN = (((0,), (0,)), ((), ()))


def _cparams(sem):
    return pltpu.CompilerParams(dimension_semantics=sem, vmem_limit_bytes=VMEM_LIMIT)


def _dot(a, b):
    return jnp.dot(a, b, preferred_element_type=F32)


def _norm_proj_kernel(x_ref, g_ref, w_ref, o_ref, h_ref):
    @pl.when(pl.program_id(1) == 0)
    def _():
        x = x_ref[...]
        ms = jnp.mean(x * x, axis=-1, keepdims=True)
        h_ref[...] = (x * lax.rsqrt(ms + RMS_EPS) * g_ref[...]).astype(BF16)

    o_ref[...] = _dot(h_ref[...], w_ref[...]).astype(o_ref.dtype)


def _norm_proj(x2, gain, w, tm, tn):
    n, d = x2.shape
    cols = w.shape[1]
    return pl.pallas_call(
        _norm_proj_kernel,
        grid=(n // tm, cols // tn),
        in_specs=[pl.BlockSpec((tm, d), lambda i, j: (i, 0)),
                  pl.BlockSpec((1, d), lambda i, j: (0, 0)),
                  pl.BlockSpec((d, tn), lambda i, j: (0, j))],
        out_specs=pl.BlockSpec((tm, tn), lambda i, j: (i, j)),
        out_shape=jax.ShapeDtypeStruct((n, cols), BF16),
        scratch_shapes=[pltpu.VMEM((tm, d), BF16)],
        compiler_params=_cparams(("parallel", "arbitrary")),
        name="norm_proj",
    )(x2, gain, w)


def _rope_kernel(pos_ref, invf_ref, sa_ref, sb_ref, cos_ref, sina_ref, sinb_ref):
    ang = pos_ref[...].astype(F32) * invf_ref[...]
    c = jnp.cos(ang)
    s = jnp.sin(ang)
    cos_ref[...] = c
    sina_ref[...] = s * sa_ref[...]
    sinb_ref[...] = s * sb_ref[...]


def _rope_tables(pos_col, tm):
    n = pos_col.shape[0]
    half = ROPE_DIM // 2
    inv_freq = jnp.exp(-math.log(ROPE_THETA) * jnp.arange(half, dtype=F32) * (2.0 / ROPE_DIM))
    lane = np.arange(LANES) % DA_QK_DIM
    invf = jnp.where(lane < ROPE_DIM, inv_freq[lane % half], 0.0).astype(F32)[None, :]
    sgn_a = jnp.asarray(np.where(lane < half, -1.0, 0.0), F32)[None, :]
    sgn_b = jnp.asarray(np.where((lane >= half) & (lane < ROPE_DIM), 1.0, 0.0), F32)[None, :]
    row = pl.BlockSpec((1, LANES), lambda i: (0, 0))
    tab = pl.BlockSpec((tm, LANES), lambda i: (i, 0))
    shp = jax.ShapeDtypeStruct((n, LANES), F32)
    return pl.pallas_call(
        _rope_kernel,
        grid=(n // tm,),
        in_specs=[pl.BlockSpec((tm, 1), lambda i: (i, 0)), row, row, row],
        out_specs=[tab, tab, tab],
        out_shape=[shp, shp, shp],
        compiler_params=_cparams(("parallel",)),
        name="rope_tables",
    )(pos_col, invf, sgn_a, sgn_b)


def _qk_prep_kernel(x_ref, gain_ref, cos_ref, sina_ref, sinb_ref, bd_ref, o_ref):
    c = cos_ref[...]
    sa = sina_ref[...]
    sb = sinb_ref[...]
    bd = bd_ref[...]
    n_tiles = x_ref.shape[1] // LANES
    for j in range(n_tiles):
        sl = slice(j * LANES, (j + 1) * LANES)
        t = x_ref[:, sl].astype(F32)
        ss = _dot((t * t).astype(BF16), bd)
        y = t * lax.rsqrt(ss * (1.0 / DA_QK_DIM) + RMS_EPS) * gain_ref[:, sl]
        y = (y * c + pltpu.roll(y, LANES - ROPE_DIM // 2, 1) * sa
             + pltpu.roll(y, ROPE_DIM // 2, 1) * sb)
        if j < n_tiles // 2:
            y = y * (DA_QK_DIM ** -0.5)
        o_ref[:, sl] = y.astype(BF16)


def _qk_prep(proj, gain_row, cos_t, sina_t, sinb_t, tm):
    n = proj.shape[0]
    w = 2 * DA_HEADS * 2 * DA_QK_DIM
    seg = np.arange(LANES) // DA_QK_DIM
    bd = jnp.asarray(seg[:, None] == seg[None, :], BF16)
    tab = pl.BlockSpec((tm, LANES), lambda i: (i, 0))
    return pl.pallas_call(
        _qk_prep_kernel,
        grid=(n // tm,),
        in_specs=[pl.BlockSpec((tm, w), lambda i: (i, COL_DA_Q // w)),
                  pl.BlockSpec((1, w), lambda i: (0, 0)),
                  tab, tab, tab,
                  pl.BlockSpec((LANES, LANES), lambda i: (0, 0))],
        out_specs=pl.BlockSpec((tm, w), lambda i: (i, 0)),
        out_shape=jax.ShapeDtypeStruct((n, w), BF16),
        compiler_params=_cparams(("parallel",)),
        name="qk_prep",
    )(proj, gain_row, cos_t, sina_t, sinb_t, bd)


def _diff_attn_kernel(q_ref, k_ref, v_ref, lq1_ref, lk1_ref, lq2_ref, lk2_ref, sg_ref, o_ref,
                      m_ref, l_ref, acc_ref, *, tq, group, lambda_init):
    qi = pl.program_id(2)
    q = q_ref[...]
    lane = lax.broadcasted_iota(jnp.int32, q.shape, 1)
    zero = jnp.zeros_like(q)
    qs = (jnp.where(lane < DA_QK_DIM, q, zero), jnp.where(lane >= DA_QK_DIM, q, zero))
    m_ref[...] = jnp.full(m_ref.shape, -jnp.inf, F32)
    l_ref[...] = jnp.zeros(l_ref.shape, F32)
    acc_ref[...] = jnp.zeros(acc_ref.shape, F32)

    def tile(start, width, masked):
        kb = k_ref[pl.ds(start, width), :]
        vb = v_ref[pl.ds(start, width), :]
        n_lt = width // LANES
        for c in range(2):
            s = lax.dot_general(qs[c], kb, _NT, preferred_element_type=F32)
            if masked:
                row = lax.broadcasted_iota(jnp.int32, s.shape, 0)
                col = lax.broadcasted_iota(jnp.int32, s.shape, 1)
                s = jnp.where(col <= row, s, -jnp.inf)
            st = [s[:, j * LANES:(j + 1) * LANES] for j in range(n_lt)]
            smax = st[0]
            for j in range(1, n_lt):
                smax = jnp.maximum(smax, st[j])
            m_prev = m_ref[c]
            m_new = jnp.maximum(m_prev, jnp.max(smax, axis=-1, keepdims=True))
            alpha = jnp.exp(m_prev - m_new)
            ps = [jnp.exp(t - m_new) for t in st]
            lsum = ps[0]
            for j in range(1, n_lt):
                lsum = lsum + ps[j]
            p = jnp.concatenate([t.astype(BF16) for t in ps], axis=1)
            l_ref[c] = alpha * l_ref[c] + lsum
            acc_ref[c] = alpha * acc_ref[c] + _dot(p, vb)
            m_ref[c] = m_new

    n_group = qi // group

    def body_group(gi, carry):
        tile(pl.multiple_of(gi * (group * tq), group * tq), group * tq, False)
        return carry

    def body_single(ki, carry):
        tile(pl.multiple_of(ki * tq, tq), tq, False)
        return carry

    lax.fori_loop(0, n_group, body_group, 0)
    lax.fori_loop(n_group * group, qi, body_single, 0)
    tile(pl.multiple_of(qi * tq, tq), tq, True)

    lam = (jnp.exp(jnp.sum(lq1_ref[...] * lk1_ref[...], axis=-1, keepdims=True))
           - jnp.exp(jnp.sum(lq2_ref[...] * lk2_ref[...], axis=-1, keepdims=True)) + lambda_init)
    l1 = jnp.sum(l_ref[0], axis=-1, keepdims=True)
    l2 = jnp.sum(l_ref[1], axis=-1, keepdims=True)
    o = acc_ref[0] / l1 - lam * (acc_ref[1] / l2)
    ms = jnp.mean(o * o, axis=-1, keepdims=True)
    o = o * lax.rsqrt(ms + RMS_EPS) * sg_ref[...] * (1.0 - lambda_init)
    o_ref[...] = o.astype(o_ref.dtype)


def _diff_attn(qk, proj, lq1, lk1, lq2, lk2, subln, bsz, seq, tq, group, lambda_init):
    n = qk.shape[0]
    nq = seq // tq
    kcol = (DA_HEADS * 2 * DA_QK_DIM) // LANES
    vcol = COL_DA_V // LANES
    vec = pl.BlockSpec((1, DA_QK_DIM), lambda b, h, i: (0, 0))
    return pl.pallas_call(
        functools.partial(_diff_attn_kernel, tq=tq, group=group, lambda_init=lambda_init),
        grid=(bsz, DA_HEADS, nq),
        in_specs=[pl.BlockSpec((tq, LANES), lambda b, h, i: (b * nq + i, h)),
                  pl.BlockSpec((seq, LANES), lambda b, h, i: (b, kcol + h)),
                  pl.BlockSpec((seq, LANES), lambda b, h, i: (b, vcol + h)),
                  vec, vec, vec, vec,
                  pl.BlockSpec((1, LANES), lambda b, h, i: (0, 0))],
        out_specs=pl.BlockSpec((tq, LANES), lambda b, h, i: (b * nq + i, h)),
        out_shape=jax.ShapeDtypeStruct((n, DA_HEADS * LANES), BF16),
        scratch_shapes=[pltpu.VMEM((2, tq, LANES), F32), pltpu.VMEM((2, tq, LANES), F32),
                        pltpu.VMEM((2, tq, LANES), F32)],
        compiler_params=_cparams(("parallel", "parallel", "arbitrary")),
        name="diff_attn",
    )(qk, qk, proj, lq1, lk1, lq2, lk2, subln)


def _stick_break_kernel(q_ref, k_ref, v_ref, o_ref, r_ref, acc_ref, *, tq, group):
    qi = pl.program_id(2)
    q = q_ref[...]
    lane = lax.broadcasted_iota(jnp.int32, q.shape, 1)
    zero = jnp.zeros_like(q)
    qs = (jnp.where(lane < SB_HEAD_DIM, q, zero), jnp.where(lane >= SB_HEAD_DIM, q, zero))
    row = lax.broadcasted_iota(jnp.int32, (tq, tq), 0)
    col = lax.broadcasted_iota(jnp.int32, (tq, tq), 1)
    incl = jnp.where(row >= col, 1.0, 0.0).astype(BF16)
    strict = col < row
    n_lt = tq // LANES
    r_ref[...] = jnp.zeros(r_ref.shape, F32)
    acc_ref[...] = jnp.zeros(acc_ref.shape, F32)

    def sub_block(ki, masked, r):
        start = pl.multiple_of(ki * tq, tq)
        kb = k_ref[pl.ds(start, tq), :]
        vb = v_ref[pl.ds(start, tq), :]
        ws, r_new = [], []
        for c in range(2):
            u = lax.dot_general(qs[c], kb, _NT, preferred_element_type=F32)
            neg_abs = lax.bitcast_convert_type(
                lax.bitcast_convert_type(u, jnp.int32) | jnp.int32(-2 ** 31), F32)
            sp = jnp.maximum(u, 0.0) + jnp.log(1.0 + jnp.exp2(neg_abs)) * LOG2E
            if masked:
                sp = jnp.where(strict, sp, 0.0)
            cum = _dot(sp.astype(BF16), incl)
            w = jnp.exp2(u - (cum + jnp.concatenate([r[c]] * n_lt, axis=1)))
            if masked:
                w = jnp.where(strict, w, 0.0)
            ws.append(w.astype(BF16))
            r_new.append(r[c] + cum[:, 0:1])
        lane_v = lax.broadcasted_iota(jnp.int32, vb.shape, 1)
        zv = jnp.zeros_like(vb)
        v2 = jnp.concatenate([jnp.where(lane_v < SB_HEAD_DIM, vb, zv),
                              jnp.where(lane_v >= SB_HEAD_DIM, vb, zv)], axis=0)
        return _dot(jnp.concatenate(ws, axis=1), v2), r_new

    def run(blocks, masked):
        r = [r_ref[0], r_ref[1]]
        total = None
        for ki in blocks:
            part, r = sub_block(ki, masked, r)
            total = part if total is None else total + part
        acc_ref[...] += total
        r_ref[0] = r[0]
        r_ref[1] = r[1]

    run([qi], True)
    rem = qi % group

    def body_single(j, carry):
        run([qi - 1 - j], False)
        return carry

    def body_group(gi, carry):
        top = qi - rem - gi * group
        run([top - 1 - j for j in range(group)], False)
        return carry

    lax.fori_loop(0, rem, body_single, 0)
    lax.fori_loop(0, qi // group, body_group, 0)
    o_ref[...] = acc_ref[...].astype(o_ref.dtype)


def _stick_break(proj, bsz, seq, tq, group):
    n = proj.shape[0]
    nq = seq // tq
    pairs = (SB_HEADS * SB_HEAD_DIM) // LANES
    qc, kc, vc = COL_SB_Q // LANES, COL_SB_K // LANES, COL_SB_V // LANES
    return pl.pallas_call(
        functools.partial(_stick_break_kernel, tq=tq, group=group),
        grid=(bsz, pairs, nq),
        in_specs=[pl.BlockSpec((tq, LANES), lambda b, p, i: (b * nq + i, qc + p)),
                  pl.BlockSpec((seq, LANES), lambda b, p, i: (b, kc + p)),
                  pl.BlockSpec((seq, LANES), lambda b, p, i: (b, vc + p))],
        out_specs=pl.BlockSpec((tq, LANES), lambda b, p, i: (b * nq + i, p)),
        out_shape=jax.ShapeDtypeStruct((n, pairs * LANES), BF16),
        scratch_shapes=[pltpu.VMEM((2, tq, LANES), F32), pltpu.VMEM((tq, LANES), F32)],
        compiler_params=_cparams(("parallel", "parallel", "arbitrary")),
        name="stick_break",
    )(proj, proj, proj)


def _split_bf16(x):
    hi = x.astype(BF16)
    lo = (x - hi.astype(F32)).astype(BF16)
    return hi, lo


def _hgrn_kernel(f_ref, i_ref, q_ref, g_ref, lb_ref, gain_ref, o_ref,
                 st_ref, b_ref, dec_ref, qq_ref, kk_ref, vv_ref, oo_ref, qe_ref, ke_ref, gm_ref, uu_ref,
                 *, tb):
    @pl.when(pl.program_id(1) == 0)
    def _():
        st_ref[...] = jnp.zeros(st_ref.shape, F32)

    nsub = tb // HG_SUB
    z = f_ref[...].astype(F32)
    lb = lb_ref[...]
    sp = jnp.maximum(z, 0.0) + jnp.log(1.0 + jnp.exp(-jnp.abs(z)))
    log_sig = z - sp
    a = jnp.log(lb)
    c = jnp.log(1.0 - lb) + log_sig
    mx = jnp.maximum(a, c)
    log_f = mx + jnp.log(jnp.exp(a - mx) + jnp.exp(c - mx))
    key = (1.0 - lb) * jax.nn.sigmoid(-z)

    row = lax.broadcasted_iota(jnp.int32, (tb, tb), 0)
    col = lax.broadcasted_iota(jnp.int32, (tb, tb), 1)
    sub_shift = HG_SUB.bit_length() - 1
    dim_shift = HG_DIM.bit_length() - 1
    same = (row >> sub_shift) == (col >> sub_shift)
    tri = jnp.where(same & (col <= row), 1.0, 0.0).astype(BF16)
    blk = jnp.where(same, 1.0, 0.0).astype(BF16)
    hi, lo = _split_bf16(log_f)
    b = _dot(tri, hi) + _dot(tri, lo)
    e = _dot(blk, hi) + _dot(blk, lo)
    qf = q_ref[...].astype(F32)
    b_ref[...] = b
    qq_ref[...] = qf
    kk_ref[...] = key
    vv_ref[...] = i_ref[...].astype(F32)
    qe_ref[...] = (qf * jnp.exp(b)).astype(BF16)
    ke_ref[...] = (key * jnp.exp(e - b)).astype(BF16)
    dec_ref[...] = jnp.exp(e)

    seg_r = lax.broadcasted_iota(jnp.int32, (HG_COLS, HG_COLS), 0) >> dim_shift
    seg_c = lax.broadcasted_iota(jnp.int32, (HG_COLS, HG_COLS), 1) >> dim_shift
    head_mask = seg_r == seg_c
    ones_bd = jnp.where(head_mask, 1.0, 0.0).astype(BF16)
    sel = jnp.where(lax.broadcasted_iota(jnp.int32, (HG_SUB, HG_SUB * HG_SUB), 0)
                    == (lax.broadcasted_iota(jnp.int32, (HG_SUB, HG_SUB * HG_SUB), 1) >> sub_shift),
                    1.0, 0.0).astype(BF16)
    srow = lax.broadcasted_iota(jnp.int32, (HG_SUB, HG_COLS), 0)

    def intra(ci, slot):
        r0 = pl.multiple_of(ci * HG_SUB, HG_SUB)
        bi = b_ref[pl.ds(r0, HG_SUB), :]
        qi = qq_ref[pl.ds(r0, HG_SUB), :]
        ki = kk_ref[pl.ds(r0, HG_SUB), :]
        vi = vv_ref[pl.ds(r0, HG_SUB), :]
        for t in range(HG_SUB):
            d = jnp.exp(jnp.minimum(bi[t:t + 1, :] - bi, 0.0))
            g = jnp.where(srow <= t, qi[t:t + 1, :] * ki * d, 0.0)
            gm_ref[slot, t * HG_SUB:(t + 1) * HG_SUB, :] = g.astype(BF16)
        sc = _dot(gm_ref[slot], ones_bd)
        vt = jnp.concatenate([vi] * HG_SUB, axis=0)
        oo_ref[pl.ds(r0, HG_SUB), :] = _dot(sel, (sc * vt).astype(BF16))
        upd = lax.dot_general(vi.astype(BF16), ke_ref[pl.ds(r0, HG_SUB), :], _TN,
                              preferred_element_type=F32)
        uu_ref[ci] = jnp.where(head_mask, upd, 0.0)

    def intra_group(gi, carry):
        for slot in range(HG_UNROLL):
            intra(gi * HG_UNROLL + slot, slot)
        return carry

    lax.fori_loop(0, nsub // HG_UNROLL, intra_group, 0)

    st = st_ref[...]
    for ci in range(nsub):
        rows = slice(ci * HG_SUB, (ci + 1) * HG_SUB)
        oo_ref[rows, :] += lax.dot_general(qe_ref[rows, :], st.astype(BF16), _NT,
                                           preferred_element_type=F32)
        st = st * dec_ref[ci * HG_SUB:ci * HG_SUB + 1, :] + uu_ref[ci]
    st_ref[...] = st

    o = oo_ref[...]
    ms = _dot((o * o).astype(BF16), ones_bd) * (1.0 / HG_DIM)
    gate = g_ref[...].astype(F32)
    gate = gate * jax.nn.sigmoid(gate)
    o_ref[...] = (o * lax.rsqrt(ms + RMS_EPS) * gain_ref[...] * gate).astype(o_ref.dtype)


def _hgrn(proj, lb_row, gain_row, bsz, seq, tb):
    n = proj.shape[0]
    nb = seq // tb
    w = HG_COLS
    cf, ci, cq, cg = COL_HG_F // w, COL_HG_I // w, COL_HG_Q // w, COL_HG_G // w

    def col(cc):
        return pl.BlockSpec((tb, w), lambda b, i: (b * nb + i, cc))

    rowspec = pl.BlockSpec((1, w), lambda b, i: (0, 0))
    big = pltpu.VMEM((tb, w), F32)
    return pl.pallas_call(
        functools.partial(_hgrn_kernel, tb=tb),
        grid=(bsz, nb),
        in_specs=[col(cf), col(ci), col(cq), col(cg), rowspec, rowspec],
        out_specs=pl.BlockSpec((tb, w), lambda b, i: (b * nb + i, 0)),
        out_shape=jax.ShapeDtypeStruct((n, w), BF16),
        scratch_shapes=[pltpu.VMEM((w, w), F32), big, big, big, big, big, big,
                        pltpu.VMEM((tb, w), BF16), pltpu.VMEM((tb, w), BF16),
                        pltpu.VMEM((HG_UNROLL, HG_SUB * HG_SUB, w), BF16),
                        pltpu.VMEM((tb // HG_SUB, w, w), F32)],
        compiler_params=_cparams(("parallel", "arbitrary")),
        name="hgrn2",
    )(proj, proj, proj, proj, lb_row, gain_row)


def _s5_kernel(u_ref, bblk_ref, lev_re_ref, lev_im_ref, pw_re_ref, pw_im_ref, cblk_ref, d_ref,
               gw_ref, gb_ref, o_ref, cr_ref, ci_ref, *, tb):
    @pl.when(pl.program_id(1) == 0)
    def _():
        cr_ref[...] = jnp.zeros(cr_ref.shape, F32)
        ci_ref[...] = jnp.zeros(ci_ref.shape, F32)

    u = u_ref[...]
    bu = _dot(u, bblk_ref[...])
    xr = bu[:, :S5_NSTATE]
    xi = bu[:, S5_NSTATE:]
    row = lax.broadcasted_iota(jnp.int32, xr.shape, 0)
    nlev = lev_re_ref.shape[0]
    for j in range(nlev):
        d = 1 << j
        ar = lev_re_ref[j:j + 1, :]
        ai = lev_im_ref[j:j + 1, :]
        keep = row >= d
        sr = jnp.where(keep, pltpu.roll(xr, d, 0), 0.0)
        si = jnp.where(keep, pltpu.roll(xi, d, 0), 0.0)
        xr, xi = xr + ar * sr - ai * si, xi + ar * si + ai * sr
    cr = cr_ref[...]
    ci = ci_ref[...]
    pr = pw_re_ref[...]
    pi = pw_im_ref[...]
    xr, xi = xr + pr * cr - pi * ci, xi + pr * ci + pi * cr
    cr_ref[...] = xr[tb - 1:tb, :]
    ci_ref[...] = xi[tb - 1:tb, :]
    y = (_dot(xr.astype(BF16), cblk_ref[:S5_NSTATE, :]) + _dot(xi.astype(BF16), cblk_ref[S5_NSTATE:, :])
         + d_ref[...] * u.astype(F32))
    y = jax.nn.gelu(y)
    zg = _dot(y.astype(BF16), gw_ref[...]) + gb_ref[...]
    o_ref[...] = (y * jax.nn.sigmoid(zg)).astype(o_ref.dtype)


def _s5_params(lam_re, lam_im, log_step, b_re, b_im, c_re, c_im, tb):
    lam_re = jnp.minimum(lam_re.astype(F32), S5_EIG_CLIP)
    lam_im = lam_im.astype(F32)
    step = jnp.exp(log_step.astype(F32))[:, None]
    mag = jnp.exp(lam_re * step)
    phase = lam_im * step
    a_re = mag * jnp.cos(phase)
    a_im = mag * jnp.sin(phase)
    denom = lam_re * lam_re + lam_im * lam_im
    num_re = a_re - 1.0
    gam_re = (num_re * lam_re + a_im * lam_im) / denom
    gam_im = (a_im * lam_re - num_re * lam_im) / denom
    b_re = b_re.astype(F32)
    b_im = b_im.astype(F32)
    bb_re = gam_re[..., None] * b_re - gam_im[..., None] * b_im
    bb_im = gam_re[..., None] * b_im + gam_im[..., None] * b_re
    eye = jnp.eye(S5_GROUPS, dtype=F32)

    def in_blk(bb):
        return jnp.einsum('gnc,gh->gchn', bb, eye).reshape(S5_WIDTH, S5_NSTATE)

    def out_blk(cc):
        return jnp.einsum('gcn,gh->gnhc', cc.astype(F32), eye).reshape(S5_NSTATE, S5_WIDTH)

    bblk = jnp.concatenate([in_blk(bb_re), in_blk(bb_im)], axis=1).astype(BF16)
    cblk = jnp.concatenate([out_blk(c_re), -out_blk(c_im)], axis=0).astype(BF16)
    ar = a_re.reshape(1, S5_NSTATE)
    ai = a_im.reshape(1, S5_NSTATE)
    nlev = int(math.log2(tb))
    lev_re, lev_im = [ar], [ai]
    pw_re, pw_im = ar, ai
    for _ in range(nlev):
        sr, si = lev_re[-1], lev_im[-1]
        pw_re, pw_im = (jnp.concatenate([pw_re, pw_re * sr - pw_im * si], axis=0),
                        jnp.concatenate([pw_im, pw_re * si + pw_im * sr], axis=0))
        lev_re.append(sr * sr - si * si)
        lev_im.append(2.0 * sr * si)
    lev_re = jnp.concatenate(lev_re[:nlev], axis=0)
    lev_im = jnp.concatenate(lev_im[:nlev], axis=0)
    return bblk, cblk, lev_re, lev_im, pw_re, pw_im


def _s5(proj, params, d_row, glu_w, glu_b, bsz, seq, tb):
    n = proj.shape[0]
    nb = seq // tb
    bblk, cblk, lev_re, lev_im, pw_re, pw_im = params
    nlev = lev_re.shape[0]
    ucol = COL_S5_U // S5_WIDTH

    def full(shape):
        return pl.BlockSpec(shape, lambda b, i: (0,) * len(shape))

    return pl.pallas_call(
        functools.partial(_s5_kernel, tb=tb),
        grid=(bsz, nb),
        in_specs=[pl.BlockSpec((tb, S5_WIDTH), lambda b, i: (b * nb + i, ucol)),
                  full((S5_WIDTH, 2 * S5_NSTATE)),
                  full((nlev, S5_NSTATE)), full((nlev, S5_NSTATE)),
                  full((tb, S5_NSTATE)), full((tb, S5_NSTATE)),
                  full((2 * S5_NSTATE, S5_WIDTH)),
                  full((1, S5_WIDTH)), full((S5_WIDTH, S5_WIDTH)), full((1, S5_WIDTH))],
        out_specs=pl.BlockSpec((tb, S5_WIDTH), lambda b, i: (b * nb + i, 0)),
        out_shape=jax.ShapeDtypeStruct((n, S5_WIDTH), BF16),
        scratch_shapes=[pltpu.VMEM((1, S5_NSTATE), F32), pltpu.VMEM((1, S5_NSTATE), F32)],
        compiler_params=_cparams(("parallel", "arbitrary")),
        name="s5",
    )(proj, bblk, lev_re, lev_im, pw_re, pw_im, cblk, d_row, glu_w, glu_b)


def _merge_kernel(x_ref, ya_ref, yb_ref, yc_ref, yd_ref, gate_ref, wa_ref, wb_ref, wc_ref, wd_ref,
                  wo_ref, gn_ref, wrh_ref, wrl_ref, br_ref, xo_ref, h_ref, lg_ref):
    d = x_ref.shape[1]
    merged = None
    for i, (y_ref, w_ref) in enumerate(((ya_ref, wa_ref), (yb_ref, wb_ref),
                                        (yc_ref, wc_ref), (yd_ref, wd_ref))):
        gate = jax.nn.sigmoid(gate_ref[:, i * d:(i + 1) * d].astype(F32))
        term = gate * _dot(y_ref[...], w_ref[...])
        merged = term if merged is None else merged + term
    xn = x_ref[...] + _dot(merged.astype(BF16), wo_ref[...])
    xo_ref[...] = xn
    ms = jnp.mean(xn * xn, axis=-1, keepdims=True)
    h = xn * lax.rsqrt(ms + RMS_EPS) * gn_ref[...]
    hi, lo = _split_bf16(h)
    h_ref[...] = hi
    wrh = wrh_ref[...]
    lg_ref[...] = _dot(hi, wrh) + _dot(lo, wrh) + _dot(hi, wrl_ref[...]) + br_ref[...]


def _merge(x2, ya, yb, yc, yd, proj, wa, wb, wc, wd, wo, gn_row, wr_hi, wr_lo, br_row, tm):
    n, d = x2.shape
    assert COL_GATE % (4 * d) == 0
    gcol = COL_GATE // (4 * d)

    def rows(width, cc=0):
        return pl.BlockSpec((tm, width), lambda i: (i, cc))

    def full(arr):
        return pl.BlockSpec(arr.shape, lambda i: (0, 0))

    return pl.pallas_call(
        _merge_kernel,
        grid=(n // tm,),
        in_specs=[rows(d), rows(ya.shape[1]), rows(yb.shape[1]), rows(yc.shape[1]), rows(yd.shape[1]),
                  rows(4 * d, gcol),
                  full(wa), full(wb), full(wc), full(wd), full(wo), full(gn_row),
                  full(wr_hi), full(wr_lo), full(br_row)],
        out_specs=[rows(d), rows(d), rows(LANES)],
        out_shape=[jax.ShapeDtypeStruct((n, d), F32), jax.ShapeDtypeStruct((n, d), BF16),
                   jax.ShapeDtypeStruct((n, LANES), F32)],
        compiler_params=_cparams(("parallel",)),
        name="merge",
    )(x2, ya, yb, yc, yd, proj, wa, wb, wc, wd, wo, gn_row, wr_hi, wr_lo, br_row)


def _first_index(mask, lane):
    return jnp.min(jnp.where(mask, lane, float(LANES)), axis=-1, keepdims=True)


def _moe_kernel(h_ref, x_ref, lg_ref, wg_ref, wu_ref, wd_ref, o_ref, cw_ref, acc_ref):
    e = pl.program_id(1)
    h = h_ref[...]

    @pl.when(e == 0)
    def _():
        logits = lg_ref[...]
        lane = lax.broadcasted_iota(jnp.int32, logits.shape, 1).astype(F32)
        neg = -jnp.inf
        gl = jnp.where(lane < N_GROUPS, logits, neg)
        gmax = jnp.max(gl, axis=-1, keepdims=True)
        gsum = jnp.sum(jnp.exp(gl - gmax), axis=-1, keepdims=True)
        g_val = 1.0 / gsum
        g_idx = _first_index(gl == gmax, lane)
        lo = N_GROUPS + EXPERTS_PER_GROUP * g_idx
        el = jnp.where((lane >= lo) & (lane < lo + EXPERTS_PER_GROUP), logits, neg)
        emax = jnp.max(el, axis=-1, keepdims=True)
        esum = jnp.sum(jnp.exp(el - emax), axis=-1, keepdims=True)
        i1 = _first_index(el == emax, lane)
        el2 = jnp.where(lane == i1, neg, el)
        e2max = jnp.max(el2, axis=-1, keepdims=True)
        i2 = _first_index(el2 == e2max, lane)
        p1 = 1.0 / esum
        p2 = jnp.exp(e2max - emax) / esum
        tot = p1 + p2
        cw_ref[...] = (jnp.where(lane == i1, g_val * (p1 / tot), 0.0)
                       + jnp.where(lane == i2, g_val * (p2 / tot), 0.0))
        acc_ref[...] = jnp.zeros(acc_ref.shape, F32)

    a = _dot(h, wg_ref[0])
    hid = (a * jax.nn.sigmoid(a)) * _dot(h, wu_ref[0])
    lane = lax.broadcasted_iota(jnp.int32, cw_ref.shape, 1)
    cw_all = cw_ref[...]
    cw0 = jnp.sum(jnp.where(lane == 2 * e + N_GROUPS, cw_all, 0.0), axis=-1, keepdims=True)
    cw1 = jnp.sum(jnp.where(lane == 2 * e + 1 + N_GROUPS, cw_all, 0.0), axis=-1, keepdims=True)
    de = hid.shape[1] // 2
    hcol = lax.broadcasted_iota(jnp.int32, hid.shape, 1)
    hid = hid * jnp.where(hcol < de, cw0, cw1)
    acc_ref[...] += _dot(hid.astype(BF16), wd_ref[0])

    @pl.when(e == pl.num_programs(1) - 1)
    def _():
        o_ref[...] = x_ref[...] + acc_ref[...]


def _pair_cols(w):
    ne, d, de = w.shape
    return w.astype(BF16).reshape(ne // 2, 2, d, de).transpose(0, 2, 1, 3).reshape(ne // 2, d, 2 * de)


def _pair_rows(w):
    ne, de, d = w.shape
    return w.astype(BF16).reshape(ne // 2, 2 * de, d)


def _moe(h, x2, logits, wg, wu, wd, tm):
    n, d = x2.shape
    ne, _, de = wg.shape
    return pl.pallas_call(
        _moe_kernel,
        grid=(n // tm, ne),
        in_specs=[pl.BlockSpec((tm, d), lambda i, e: (i, 0)),
                  pl.BlockSpec((tm, d), lambda i, e: (i, 0)),
                  pl.BlockSpec((tm, LANES), lambda i, e: (i, 0)),
                  pl.BlockSpec((1, d, de), lambda i, e: (e, 0, 0)),
                  pl.BlockSpec((1, d, de), lambda i, e: (e, 0, 0)),
                  pl.BlockSpec((1, de, d), lambda i, e: (e, 0, 0))],
        out_specs=pl.BlockSpec((tm, d), lambda i, e: (i, 0)),
        out_shape=jax.ShapeDtypeStruct((n, d), F32),
        scratch_shapes=[pltpu.VMEM((tm, LANES), F32), pltpu.VMEM((tm, d), F32)],
        compiler_params=_cparams(("parallel", "arbitrary")),
        name="moe",
    )(h, x2, logits, wg, wu, wd)


def _pick(n, pref):
    t = min(n, pref)
    while n % t:
        t //= 2
    return t


def kernel(x, positions, norm_mix, w_in, da_q_gain, da_k_gain, da_lambda_q1, da_lambda_k1,
           da_lambda_q2, da_lambda_k2, da_subln_gain, hg_lower_bounds, hg_out_gain,
           s5_lambda_re, s5_lambda_im, s5_log_step, s5_b_re, s5_b_im, s5_c_re, s5_c_im,
           s5_d, s5_glu_w, s5_glu_b, w_branch_attn, w_branch_sb, w_branch_hgrn, w_branch_s5,
           w_out, norm_ffn, router_group_w, router_group_b, router_expert_w, router_expert_b,
           expert_w_gate, expert_w_up, expert_w_down):
    bsz, seq, d = x.shape
    depth = w_in.shape[0]
    n = bsz * seq
    assert w_in.shape[2] == IN_COLS and seq % 128 == 0
    tm = _pick(n, 1024)
    tq = _pick(seq, 256)
    tq_da = _pick(seq, DA_TQ)
    tb = _pick(seq, 256)

    x2 = x.reshape(n, d).astype(F32)
    cos_t, sina_t, sinb_t = _rope_tables(positions.reshape(n, 1).astype(jnp.int32), tm)

    lb_all = jnp.cumsum(jax.nn.softmax(hg_lower_bounds.astype(F32), axis=0), axis=0)
    lb_all = lb_all - lb_all[0:1]

    for l in range(depth):
        lambda_init = DA_LAMBDA_INIT_BASE - DA_LAMBDA_INIT_SCALE * math.exp(-DA_LAMBDA_INIT_RATE * l)
        w_f = w_in[l].astype(F32)
        sbq = COL_SB_Q - COL_DA_Q
        w_l = jnp.concatenate([w_f[:, REF_GATE_START:], w_f[:, :sbq],
                               w_f[:, sbq:sbq + SB_HEADS * SB_HEAD_DIM] * (SB_HEAD_DIM ** -0.5 * LOG2E),
                               w_f[:, sbq + SB_HEADS * SB_HEAD_DIM:REF_GATE_START]], axis=1)
        proj = _norm_proj(x2, norm_mix[l].astype(F32)[None, :], w_l.astype(BF16), tm, 768)

        qk_gain = jnp.concatenate([jnp.tile(da_q_gain[l].astype(F32), 2 * DA_HEADS),
                                   jnp.tile(da_k_gain[l].astype(F32), 2 * DA_HEADS)])[None, :]
        qk = _qk_prep(proj, qk_gain, cos_t, sina_t, sinb_t, tm)
        y_a = _diff_attn(qk, proj,
                         da_lambda_q1[l].astype(F32)[None, :], da_lambda_k1[l].astype(F32)[None, :],
                         da_lambda_q2[l].astype(F32)[None, :], da_lambda_k2[l].astype(F32)[None, :],
                         da_subln_gain[l].astype(F32)[None, :], bsz, seq, tq_da, DA_GROUP, lambda_init)
        y_b = _stick_break(proj, bsz, seq, tq, SB_GROUP)
        y_c = _hgrn(proj, lb_all[l][None, :], jnp.tile(hg_out_gain[l].astype(F32), HG_HEADS)[None, :],
                    bsz, seq, tb)
        s5p = _s5_params(s5_lambda_re[l], s5_lambda_im[l], s5_log_step[l], s5_b_re[l], s5_b_im[l],
                         s5_c_re[l], s5_c_im[l], tb)
        y_d = _s5(proj, s5p, s5_d[l].astype(F32)[None, :], s5_glu_w[l].astype(BF16),
                  s5_glu_b[l].astype(F32)[None, :], bsz, seq, tb)

        wr = jnp.concatenate([router_group_w[l], router_expert_w[l]], axis=1).astype(F32)
        wr = jnp.pad(wr, ((0, 0), (0, LANES - wr.shape[1])))
        wr_hi = wr.astype(BF16)
        wr_lo = (wr - wr_hi.astype(F32)).astype(BF16)
        br = jnp.concatenate([router_group_b[l], router_expert_b[l]]).astype(F32)
        br = jnp.pad(br, (0, LANES - br.shape[0]))[None, :]
        x2, h, logits = _merge(x2, y_a, y_b, y_c, y_d, proj,
                               w_branch_attn[l].astype(BF16), w_branch_sb[l].astype(BF16),
                               w_branch_hgrn[l].astype(BF16), w_branch_s5[l].astype(BF16),
                               w_out[l].astype(BF16), norm_ffn[l].astype(F32)[None, :],
                               wr_hi, wr_lo, br, _pick(n, 512))
        x2 = _moe(h, x2, logits, _pair_cols(expert_w_gate[l]), _pair_cols(expert_w_up[l]),
                  _pair_rows(expert_w_down[l]), tm)

    return x2.reshape(bsz, seq, d).astype(x.dtype)
```

```python
import functools
import math

import jax
import jax.numpy as jnp
import numpy as np
from jax import lax
from jax.experimental import pallas as pl
from jax.experimental.pallas import tpu as pltpu

F32 = jnp.float32
BF16 = jnp.bfloat16

RMS_EPS = 1e-6
LANES = 128
LOG2E = 1.4426950408889634

DA_HEADS = 4
DA_QK_DIM = 64
ROPE_THETA = 500000.0
ROPE_DIM = DA_QK_DIM // 4
DA_LAMBDA_INIT_BASE = 0.8
DA_LAMBDA_INIT_SCALE = 0.6
DA_LAMBDA_INIT_RATE = 0.3

SB_HEADS = 4
SB_HEAD_DIM = 64

HG_HEADS = 4
HG_DIM = 64
HG_COLS = HG_HEADS * HG_DIM
HG_SUB = 16
HG_UNROLL = 4

S5_GROUPS = 16
S5_GROUP_CH = 16
S5_STATE = 64
S5_WIDTH = S5_GROUPS * S5_GROUP_CH
S5_NSTATE = S5_GROUPS * S5_STATE
S5_EIG_CLIP = -1e-4

N_GROUPS = 4
EXPERTS_PER_GROUP = 4
N_EXPERTS = N_GROUPS * EXPERTS_PER_GROUP

REF_GATE_START = 3584
COL_GATE = 0
COL_DA_Q = 4096
COL_DA_K = 4608
COL_DA_V = 5120
COL_SB_Q = 5632
COL_SB_K = 5888
COL_SB_V = 6144
COL_HG_F = 6400
COL_HG_I = 6656
COL_HG_Q = 6912
COL_HG_G = 7168
COL_S5_U = 7424
IN_COLS = 7680

VMEM_LIMIT = 48 * 1024 * 1024

DA_TQ = 1024
DA_GROUP = 1
SB_TQ = 512
SB_KW = 256
SB_GROUP = 4

_NT = (((1,), (1,)), ((), ()))
_TN = (((0,), (0,)), ((), ()))


def _cparams(sem):
    return pltpu.CompilerParams(dimension_semantics=sem, vmem_limit_bytes=VMEM_LIMIT)


def _dot(a, b):
    return jnp.dot(a, b, preferred_element_type=F32)


def _norm_proj_kernel(x_ref, g_ref, w_ref, o_ref, h_ref):
    @pl.when(pl.program_id(1) == 0)
    def _():
        x = x_ref[...]
        ms = jnp.mean(x * x, axis=-1, keepdims=True)
        h_ref[...] = (x * lax.rsqrt(ms + RMS_EPS) * g_ref[...]).astype(BF16)

    o_ref[...] = _dot(h_ref[...], w_ref[...]).astype(o_ref.dtype)


def _norm_proj(x2, gain, w, tm, tn):
    n, d = x2.shape
    cols = w.shape[1]
    return pl.pallas_call(
        _norm_proj_kernel,
        grid=(n // tm, cols // tn),
        in_specs=[pl.BlockSpec((tm, d), lambda i, j: (i, 0)),
                  pl.BlockSpec((1, d), lambda i, j: (0, 0)),
                  pl.BlockSpec((d, tn), lambda i, j: (0, j))],
        out_specs=pl.BlockSpec((tm, tn), lambda i, j: (i, j)),
        out_shape=jax.ShapeDtypeStruct((n, cols), BF16),
        scratch_shapes=[pltpu.VMEM((tm, d), BF16)],
        compiler_params=_cparams(("parallel", "arbitrary")),
        name="norm_proj",
    )(x2, gain, w)


def _rope_kernel(pos_ref, invf_ref, sa_ref, sb_ref, cos_ref, sina_ref, sinb_ref):
    ang = pos_ref[...].astype(F32) * invf_ref[...]
    c = jnp.cos(ang)
    s = jnp.sin(ang)
    cos_ref[...] = c
    sina_ref[...] = s * sa_ref[...]
    sinb_ref[...] = s * sb_ref[...]


def _rope_tables(pos_col, tm):
    n = pos_col.shape[0]
    half = ROPE_DIM // 2
    inv_freq = jnp.exp(-math.log(ROPE_THETA) * jnp.arange(half, dtype=F32) * (2.0 / ROPE_DIM))
    lane = np.arange(LANES) % DA_QK_DIM
    invf = jnp.where(lane < ROPE_DIM, inv_freq[lane % half], 0.0).astype(F32)[None, :]
    sgn_a = jnp.asarray(np.where(lane < half, -1.0, 0.0), F32)[None, :]
    sgn_b = jnp.asarray(np.where((lane >= half) & (lane < ROPE_DIM), 1.0, 0.0), F32)[None, :]
    row = pl.BlockSpec((1, LANES), lambda i: (0, 0))
    tab = pl.BlockSpec((tm, LANES), lambda i: (i, 0))
    shp = jax.ShapeDtypeStruct((n, LANES), F32)
    return pl.pallas_call(
        _rope_kernel,
        grid=(n // tm,),
        in_specs=[pl.BlockSpec((tm, 1), lambda i: (i, 0)), row, row, row],
        out_specs=[tab, tab, tab],
        out_shape=[shp, shp, shp],
        compiler_params=_cparams(("parallel",)),
        name="rope_tables",
    )(pos_col, invf, sgn_a, sgn_b)


def _qk_prep_kernel(x_ref, gain_ref, cos_ref, sina_ref, sinb_ref, bd_ref, o_ref):
    c = cos_ref[...]
    sa = sina_ref[...]
    sb = sinb_ref[...]
    bd = bd_ref[...]
    n_tiles = x_ref.shape[1] // LANES
    for j in range(n_tiles):
        sl = slice(j * LANES, (j + 1) * LANES)
        t = x_ref[:, sl].astype(F32)
        ss = _dot((t * t).astype(BF16), bd)
        y = t * lax.rsqrt(ss * (1.0 / DA_QK_DIM) + RMS_EPS) * gain_ref[:, sl]
        y = (y * c + pltpu.roll(y, LANES - ROPE_DIM // 2, 1) * sa
             + pltpu.roll(y, ROPE_DIM // 2, 1) * sb)
        if j < n_tiles // 2:
            y = y * (DA_QK_DIM ** -0.5 * LOG2E)
        o_ref[:, sl] = y.astype(BF16)


def _qk_prep(proj, gain_row, cos_t, sina_t, sinb_t, tm):
    n = proj.shape[0]
    w = 2 * DA_HEADS * 2 * DA_QK_DIM
    seg = np.arange(LANES) // DA_QK_DIM
    bd = jnp.asarray(seg[:, None] == seg[None, :], BF16)
    tab = pl.BlockSpec((tm, LANES), lambda i: (i, 0))
    return pl.pallas_call(
        _qk_prep_kernel,
        grid=(n // tm,),
        in_specs=[pl.BlockSpec((tm, w), lambda i: (i, COL_DA_Q // w)),
                  pl.BlockSpec((1, w), lambda i: (0, 0)),
                  tab, tab, tab,
                  pl.BlockSpec((LANES, LANES), lambda i: (0, 0))],
        out_specs=pl.BlockSpec((tm, w), lambda i: (i, 0)),
        out_shape=jax.ShapeDtypeStruct((n, w), BF16),
        compiler_params=_cparams(("parallel",)),
        name="qk_prep",
    )(proj, gain_row, cos_t, sina_t, sinb_t, bd)


def _diff_attn_kernel(q_ref, k_ref, v_ref, lq1_ref, lk1_ref, lq2_ref, lk2_ref, sg_ref, o_ref,
                      m_ref, l_ref, acc_ref, *, tq, group, lambda_init):
    qi = pl.program_id(2)
    q = q_ref[...]
    lane = lax.broadcasted_iota(jnp.int32, q.shape, 1)
    zero = jnp.zeros_like(q)
    q2 = jnp.concatenate([jnp.where(lane < DA_QK_DIM, q, zero),
                          jnp.where(lane >= DA_QK_DIM, q, zero)], axis=0)
    m_ref[...] = jnp.full(m_ref.shape, -jnp.inf, F32)
    l_ref[...] = jnp.zeros(l_ref.shape, F32)
    acc_ref[...] = jnp.zeros(acc_ref.shape, F32)

    def tile(start, width, masked):
        kb = k_ref[pl.ds(start, width), :]
        vb = v_ref[pl.ds(start, width), :]
        n_lt = width // LANES
        s = lax.dot_general(q2, kb, _NT, preferred_element_type=F32)
        if masked:
            row = lax.broadcasted_iota(jnp.int32, (tq, width), 0)
            col = lax.broadcasted_iota(jnp.int32, (tq, width), 1)
            keep = col <= row
            s = jnp.where(jnp.concatenate([keep, keep], axis=0), s, -jnp.inf)
        st = [s[:, j * LANES:(j + 1) * LANES] for j in range(n_lt)]
        smax = st[0]
        for j in range(1, n_lt):
            smax = jnp.maximum(smax, st[j])
        m_prev = m_ref[...]
        m_new = jnp.maximum(m_prev, jnp.max(smax, axis=-1, keepdims=True))
        alpha = jnp.exp2(m_prev - m_new)
        ps = [jnp.exp2(t - m_new) for t in st]
        lsum = ps[0]
        for j in range(1, n_lt):
            lsum = lsum + ps[j]
        p = jnp.concatenate([t.astype(BF16) for t in ps], axis=1)
        l_ref[...] = alpha * l_ref[...] + lsum
        acc_ref[...] = alpha * acc_ref[...] + _dot(p, vb)
        m_ref[...] = m_new

    n_group = qi // group

    def body_group(gi, carry):
        tile(pl.multiple_of(gi * (group * tq), group * tq), group * tq, False)
        return carry

    def body_single(ki, carry):
        tile(pl.multiple_of(ki * tq, tq), tq, False)
        return carry

    lax.fori_loop(0, n_group, body_group, 0)
    lax.fori_loop(n_group * group, qi, body_single, 0)
    tile(pl.multiple_of(qi * tq, tq), tq, True)

    lam = (jnp.exp(jnp.sum(lq1_ref[...] * lk1_ref[...], axis=-1, keepdims=True))
           - jnp.exp(jnp.sum(lq2_ref[...] * lk2_ref[...], axis=-1, keepdims=True)) + lambda_init)
    o = acc_ref[...] / jnp.sum(l_ref[...], axis=-1, keepdims=True)
    o = o[:tq] - lam * o[tq:]
    ms = jnp.mean(o * o, axis=-1, keepdims=True)
    o = o * lax.rsqrt(ms + RMS_EPS) * sg_ref[...] * (1.0 - lambda_init)
    o_ref[...] = o.astype(o_ref.dtype)


def _diff_attn(qk, proj, lq1, lk1, lq2, lk2, subln, bsz, seq, tq, group, lambda_init):
    n = qk.shape[0]
    nq = seq // tq
    kcol = (DA_HEADS * 2 * DA_QK_DIM) // LANES
    vcol = COL_DA_V // LANES
    vec = pl.BlockSpec((1, DA_QK_DIM), lambda b, h, i: (0, 0))
    return pl.pallas_call(
        functools.partial(_diff_attn_kernel, tq=tq, group=group, lambda_init=lambda_init),
        grid=(bsz, DA_HEADS, nq),
        in_specs=[pl.BlockSpec((tq, LANES), lambda b, h, i: (b * nq + i, h)),
                  pl.BlockSpec((seq, LANES), lambda b, h, i: (b, kcol + h)),
                  pl.BlockSpec((seq, LANES), lambda b, h, i: (b, vcol + h)),
                  vec, vec, vec, vec,
                  pl.BlockSpec((1, LANES), lambda b, h, i: (0, 0))],
        out_specs=pl.BlockSpec((tq, LANES), lambda b, h, i: (b * nq + i, h)),
        out_shape=jax.ShapeDtypeStruct((n, DA_HEADS * LANES), BF16),
        scratch_shapes=[pltpu.VMEM((2 * tq, LANES), F32), pltpu.VMEM((2 * tq, LANES), F32),
                        pltpu.VMEM((2 * tq, LANES), F32)],
        compiler_params=_cparams(("parallel", "parallel", "arbitrary")),
        name="diff_attn",
    )(qk, qk, proj, lq1, lk1, lq2, lk2, subln)


def _stick_break_kernel(q_ref, k_ref, v_ref, o_ref, r_ref, acc_ref, *, tq, kw, group):
    qi = pl.program_id(2)
    q = q_ref[...]
    lane = lax.broadcasted_iota(jnp.int32, q.shape, 1)
    zero = jnp.zeros_like(q)
    q2 = jnp.concatenate([jnp.where(lane < SB_HEAD_DIM, q, zero),
                          jnp.where(lane >= SB_HEAD_DIM, q, zero)], axis=0)
    incl = jnp.where(lax.broadcasted_iota(jnp.int32, (kw, kw), 0)
                     >= lax.broadcasted_iota(jnp.int32, (kw, kw), 1), 1.0, 0.0).astype(BF16)
    n_lt = kw // LANES
    n_diag = tq // kw
    r_ref[...] = jnp.zeros(r_ref.shape, F32)
    acc_ref[...] = jnp.zeros(acc_ref.shape, F32)

    def run(starts, offsets):
        masks, us, sps = [], [], []
        for start, off in zip(starts, offsets):
            u = lax.dot_general(q2, k_ref[pl.ds(start, kw), :], _NT, preferred_element_type=F32)
            neg_abs = lax.bitcast_convert_type(
                lax.bitcast_convert_type(u, jnp.int32) | jnp.int32(-2 ** 31), F32)
            sp = jnp.maximum(u, 0.0) + jnp.log(1.0 + jnp.exp2(neg_abs)) * LOG2E
            mask = None
            if off is not None:
                keep = (lax.broadcasted_iota(jnp.int32, (tq, kw), 1) + off
                        < lax.broadcasted_iota(jnp.int32, (tq, kw), 0))
                mask = jnp.concatenate([keep, keep], axis=0)
                sp = jnp.where(mask, sp, 0.0)
            masks.append(mask)
            us.append(u)
            sps.append(sp.astype(BF16))
        cum_all = _dot(jnp.concatenate(sps, axis=0), incl)
        r = r_ref[...]
        total = None
        for i, start in enumerate(starts):
            cum = cum_all[i * 2 * tq:(i + 1) * 2 * tq]
            w = jnp.exp2(us[i] - (cum + jnp.concatenate([r] * n_lt, axis=1)))
            if masks[i] is not None:
                w = jnp.where(masks[i], w, 0.0)
            part = _dot(w.astype(BF16), v_ref[pl.ds(start, kw), :])
            total = part if total is None else total + part
            r = r + cum[:, 0:1]
        acc_ref[...] += total
        r_ref[...] = r

    base = qi * tq
    run([pl.multiple_of(base + (n_diag - 1 - j) * kw, kw) for j in range(n_diag)],
        [(n_diag - 1 - j) * kw for j in range(n_diag)])

    n_below = qi * n_diag
    rem = n_below % group

    def body_single(j, carry):
        run([pl.multiple_of((n_below - 1 - j) * kw, kw)], [None])
        return carry

    def body_group(gi, carry):
        top = n_below - rem - gi * group
        run([pl.multiple_of((top - 1 - j) * kw, kw) for j in range(group)], [None] * group)
        return carry

    lax.fori_loop(0, rem, body_single, 0)
    lax.fori_loop(0, n_below // group, body_group, 0)
    lane_o = lax.broadcasted_iota(jnp.int32, (tq, LANES), 1)
    o_ref[...] = jnp.where(lane_o < SB_HEAD_DIM, acc_ref[:tq, :], acc_ref[tq:, :]).astype(o_ref.dtype)


def _stick_break(proj, bsz, seq, tq, kw, group):
    n = proj.shape[0]
    nq = seq // tq
    pairs = (SB_HEADS * SB_HEAD_DIM) // LANES
    qc, kc, vc = COL_SB_Q // LANES, COL_SB_K // LANES, COL_SB_V // LANES
    return pl.pallas_call(
        functools.partial(_stick_break_kernel, tq=tq, kw=kw, group=group),
        grid=(bsz, pairs, nq),
        in_specs=[pl.BlockSpec((tq, LANES), lambda b, p, i: (b * nq + i, qc + p)),
                  pl.BlockSpec((seq, LANES), lambda b, p, i: (b, kc + p)),
                  pl.BlockSpec((seq, LANES), lambda b, p, i: (b, vc + p))],
        out_specs=pl.BlockSpec((tq, LANES), lambda b, p, i: (b * nq + i, p)),
        out_shape=jax.ShapeDtypeStruct((n, pairs * LANES), BF16),
        scratch_shapes=[pltpu.VMEM((2 * tq, LANES), F32), pltpu.VMEM((2 * tq, LANES), F32)],
        compiler_params=_cparams(("parallel", "parallel", "arbitrary")),
        name="stick_break",
    )(proj, proj, proj)


def _split_bf16(x):
    hi = x.astype(BF16)
    lo = (x - hi.astype(F32)).astype(BF16)
    return hi, lo


def _hgrn_kernel(f_ref, i_ref, q_ref, g_ref, lb_ref, gain_ref, o_ref,
                 st_ref, b_ref, dec_ref, qq_ref, kk_ref, vv_ref, oo_ref, qe_ref, ke_ref, gm_ref, uu_ref,
                 *, tb):
    @pl.when(pl.program_id(1) == 0)
    def _():
        st_ref[...] = jnp.zeros(st_ref.shape, F32)

    nsub = tb // HG_SUB
    z = f_ref[...].astype(F32)
    lb = lb_ref[...]
    sp = jnp.maximum(z, 0.0) + jnp.log(1.0 + jnp.exp(-jnp.abs(z)))
    log_sig = z - sp
    a = jnp.log(lb)
    c = jnp.log(1.0 - lb) + log_sig
    mx = jnp.maximum(a, c)
    log_f = mx + jnp.log(jnp.exp(a - mx) + jnp.exp(c - mx))
    key = (1.0 - lb) * jax.nn.sigmoid(-z)

    row = lax.broadcasted_iota(jnp.int32, (tb, tb), 0)
    col = lax.broadcasted_iota(jnp.int32, (tb, tb), 1)
    sub_shift = HG_SUB.bit_length() - 1
    dim_shift = HG_DIM.bit_length() - 1
    same = (row >> sub_shift) == (col >> sub_shift)
    tri = jnp.where(same & (col <= row), 1.0, 0.0).astype(BF16)
    blk = jnp.where(same, 1.0, 0.0).astype(BF16)
    hi, lo = _split_bf16(log_f)
    b = _dot(tri, hi) + _dot(tri, lo)
    e = _dot(blk, hi) + _dot(blk, lo)
    qf = q_ref[...].astype(F32)
    b_ref[...] = b
    qq_ref[...] = qf
    kk_ref[...] = key
    vv_ref[...] = i_ref[...].astype(F32)
    qe_ref[...] = (qf * jnp.exp(b)).astype(BF16)
    ke_ref[...] = (key * jnp.exp(e - b)).astype(BF16)
    dec_ref[...] = jnp.exp(e)

    seg_r = lax.broadcasted_iota(jnp.int32, (HG_COLS, HG_COLS), 0) >> dim_shift
    seg_c = lax.broadcasted_iota(jnp.int32, (HG_COLS, HG_COLS), 1) >> dim_shift
    head_mask = seg_r == seg_c
    ones_bd = jnp.where(head_mask, 1.0, 0.0).astype(BF16)
    sel = jnp.where(lax.broadcasted_iota(jnp.int32, (HG_SUB, HG_SUB * HG_SUB), 0)
                    == (lax.broadcasted_iota(jnp.int32, (HG_SUB, HG_SUB * HG_SUB), 1) >> sub_shift),
                    1.0, 0.0).astype(BF16)
    srow = lax.broadcasted_iota(jnp.int32, (HG_SUB, HG_COLS), 0)

    def intra(ci, slot):
        r0 = pl.multiple_of(ci * HG_SUB, HG_SUB)
        bi = b_ref[pl.ds(r0, HG_SUB), :]
        qi = qq_ref[pl.ds(r0, HG_SUB), :]
        ki = kk_ref[pl.ds(r0, HG_SUB), :]
        vi = vv_ref[pl.ds(r0, HG_SUB), :]
        for t in range(HG_SUB):
            d = jnp.exp(jnp.minimum(bi[t:t + 1, :] - bi, 0.0))
            g = jnp.where(srow <= t, qi[t:t + 1, :] * ki * d, 0.0)
            gm_ref[slot, t * HG_SUB:(t + 1) * HG_SUB, :] = g.astype(BF16)
        sc = _dot(gm_ref[slot], ones_bd)
        vt = jnp.concatenate([vi] * HG_SUB, axis=0)
        oo_ref[pl.ds(r0, HG_SUB), :] = _dot(sel, (sc * vt).astype(BF16))
        upd = lax.dot_general(vi.astype(BF16), ke_ref[pl.ds(r0, HG_SUB), :], _TN,
                              preferred_element_type=F32)
        uu_ref[ci] = jnp.where(head_mask, upd, 0.0)

    def intra_group(gi, carry):
        for slot in range(HG_UNROLL):
            intra(gi * HG_UNROLL + slot, slot)
        return carry

    lax.fori_loop(0, nsub // HG_UNROLL, intra_group, 0)

    st = st_ref[...]
    for ci in range(nsub):
        rows = slice(ci * HG_SUB, (ci + 1) * HG_SUB)
        oo_ref[rows, :] += lax.dot_general(qe_ref[rows, :], st.astype(BF16), _NT,
                                           preferred_element_type=F32)
        st = st * dec_ref[ci * HG_SUB:ci * HG_SUB + 1, :] + uu_ref[ci]
    st_ref[...] = st

    o = oo_ref[...]
    ms = _dot((o * o).astype(BF16), ones_bd) * (1.0 / HG_DIM)
    gate = g_ref[...].astype(F32)
    gate = gate * jax.nn.sigmoid(gate)
    o_ref[...] = (o * lax.rsqrt(ms + RMS_EPS) * gain_ref[...] * gate).astype(o_ref.dtype)


def _hgrn(proj, lb_row, gain_row, bsz, seq, tb):
    n = proj.shape[0]
    nb = seq // tb
    w = HG_COLS
    cf, ci, cq, cg = COL_HG_F // w, COL_HG_I // w, COL_HG_Q // w, COL_HG_G // w

    def col(cc):
        return pl.BlockSpec((tb, w), lambda b, i: (b * nb + i, cc))

    rowspec = pl.BlockSpec((1, w), lambda b, i: (0, 0))
    big = pltpu.VMEM((tb, w), F32)
    return pl.pallas_call(
        functools.partial(_hgrn_kernel, tb=tb),
        grid=(bsz, nb),
        in_specs=[col(cf), col(ci), col(cq), col(cg), rowspec, rowspec],
        out_specs=pl.BlockSpec((tb, w), lambda b, i: (b * nb + i, 0)),
        out_shape=jax.ShapeDtypeStruct((n, w), BF16),
        scratch_shapes=[pltpu.VMEM((w, w), F32), big, big, big, big, big, big,
                        pltpu.VMEM((tb, w), BF16), pltpu.VMEM((tb, w), BF16),
                        pltpu.VMEM((HG_UNROLL, HG_SUB * HG_SUB, w), BF16),
                        pltpu.VMEM((tb // HG_SUB, w, w), F32)],
        compiler_params=_cparams(("parallel", "arbitrary")),
        name="hgrn2",
    )(proj, proj, proj, proj, lb_row, gain_row)


def _s5_kernel(u_ref, bblk_ref, lev_re_ref, lev_im_ref, pw_re_ref, pw_im_ref, cblk_ref, d_ref,
               gw_ref, gb_ref, o_ref, cr_ref, ci_ref, *, tb):
    @pl.when(pl.program_id(1) == 0)
    def _():
        cr_ref[...] = jnp.zeros(cr_ref.shape, F32)
        ci_ref[...] = jnp.zeros(ci_ref.shape, F32)

    u = u_ref[...]
    bu = _dot(u, bblk_ref[...])
    xr = bu[:, :S5_NSTATE]
    xi = bu[:, S5_NSTATE:]
    row = lax.broadcasted_iota(jnp.int32, xr.shape, 0)
    nlev = lev_re_ref.shape[0]
    for j in range(nlev):
        d = 1 << j
        ar = lev_re_ref[j:j + 1, :]
        ai = lev_im_ref[j:j + 1, :]
        keep = row >= d
        sr = jnp.where(keep, pltpu.roll(xr, d, 0), 0.0)
        si = jnp.where(keep, pltpu.roll(xi, d, 0), 0.0)
        xr, xi = xr + ar * sr - ai * si, xi + ar * si + ai * sr
    cr = cr_ref[...]
    ci = ci_ref[...]
    pr = pw_re_ref[...]
    pi = pw_im_ref[...]
    xr, xi = xr + pr * cr - pi * ci, xi + pr * ci + pi * cr
    cr_ref[...] = xr[tb - 1:tb, :]
    ci_ref[...] = xi[tb - 1:tb, :]
    y = (_dot(xr.astype(BF16), cblk_ref[:S5_NSTATE, :]) + _dot(xi.astype(BF16), cblk_ref[S5_NSTATE:, :])
         + d_ref[...] * u.astype(F32))
    y = jax.nn.gelu(y)
    zg = _dot(y.astype(BF16), gw_ref[...]) + gb_ref[...]
    o_ref[...] = (y * jax.nn.sigmoid(zg)).astype(o_ref.dtype)


def _s5_params(lam_re, lam_im, log_step, b_re, b_im, c_re, c_im, tb):
    lam_re = jnp.minimum(lam_re.astype(F32), S5_EIG_CLIP)
    lam_im = lam_im.astype(F32)
    step = jnp.exp(log_step.astype(F32))[:, None]
    mag = jnp.exp(lam_re * step)
    phase = lam_im * step
    a_re = mag * jnp.cos(phase)
    a_im = mag * jnp.sin(phase)
    denom = lam_re * lam_re + lam_im * lam_im
    num_re = a_re - 1.0
    gam_re = (num_re * lam_re + a_im * lam_im) / denom
    gam_im = (a_im * lam_re - num_re * lam_im) / denom
    b_re = b_re.astype(F32)
    b_im = b_im.astype(F32)
    bb_re = gam_re[..., None] * b_re - gam_im[..., None] * b_im
    bb_im = gam_re[..., None] * b_im + gam_im[..., None] * b_re
    eye = jnp.eye(S5_GROUPS, dtype=F32)

    def in_blk(bb):
        return jnp.einsum('gnc,gh->gchn', bb, eye).reshape(S5_WIDTH, S5_NSTATE)

    def out_blk(cc):
        return jnp.einsum('gcn,gh->gnhc', cc.astype(F32), eye).reshape(S5_NSTATE, S5_WIDTH)

    bblk = jnp.concatenate([in_blk(bb_re), in_blk(bb_im)], axis=1).astype(BF16)
    cblk = jnp.concatenate([out_blk(c_re), -out_blk(c_im)], axis=0).astype(BF16)
    ar = a_re.reshape(1, S5_NSTATE)
    ai = a_im.reshape(1, S5_NSTATE)
    nlev = int(math.log2(tb))
    lev_re, lev_im = [ar], [ai]
    pw_re, pw_im = ar, ai
    for _ in range(nlev):
        sr, si = lev_re[-1], lev_im[-1]
        pw_re, pw_im = (jnp.concatenate([pw_re, pw_re * sr - pw_im * si], axis=0),
                        jnp.concatenate([pw_im, pw_re * si + pw_im * sr], axis=0))
        lev_re.append(sr * sr - si * si)
        lev_im.append(2.0 * sr * si)
    lev_re = jnp.concatenate(lev_re[:nlev], axis=0)
    lev_im = jnp.concatenate(lev_im[:nlev], axis=0)
    return bblk, cblk, lev_re, lev_im, pw_re, pw_im


def _s5(proj, params, d_row, glu_w, glu_b, bsz, seq, tb):
    n = proj.shape[0]
    nb = seq // tb
    bblk, cblk, lev_re, lev_im, pw_re, pw_im = params
    nlev = lev_re.shape[0]
    ucol = COL_S5_U // S5_WIDTH

    def full(shape):
        return pl.BlockSpec(shape, lambda b, i: (0,) * len(shape))

    return pl.pallas_call(
        functools.partial(_s5_kernel, tb=tb),
        grid=(bsz, nb),
        in_specs=[pl.BlockSpec((tb, S5_WIDTH), lambda b, i: (b * nb + i, ucol)),
                  full((S5_WIDTH, 2 * S5_NSTATE)),
                  full((nlev, S5_NSTATE)), full((nlev, S5_NSTATE)),
                  full((tb, S5_NSTATE)), full((tb, S5_NSTATE)),
                  full((2 * S5_NSTATE, S5_WIDTH)),
                  full((1, S5_WIDTH)), full((S5_WIDTH, S5_WIDTH)), full((1, S5_WIDTH))],
        out_specs=pl.BlockSpec((tb, S5_WIDTH), lambda b, i: (b * nb + i, 0)),
        out_shape=jax.ShapeDtypeStruct((n, S5_WIDTH), BF16),
        scratch_shapes=[pltpu.VMEM((1, S5_NSTATE), F32), pltpu.VMEM((1, S5_NSTATE), F32)],
        compiler_params=_cparams(("parallel", "arbitrary")),
        name="s5",
    )(proj, bblk, lev_re, lev_im, pw_re, pw_im, cblk, d_row, glu_w, glu_b)


def _merge_kernel(x_ref, ya_ref, yb_ref, yc_ref, yd_ref, gate_ref, wa_ref, wb_ref, wc_ref, wd_ref,
                  wo_ref, gn_ref, wrh_ref, wrl_ref, br_ref, xo_ref, h_ref, lg_ref):
    d = x_ref.shape[1]
    merged = None
    for i, (y_ref, w_ref) in enumerate(((ya_ref, wa_ref), (yb_ref, wb_ref),
                                        (yc_ref, wc_ref), (yd_ref, wd_ref))):
        gate = jax.nn.sigmoid(gate_ref[:, i * d:(i + 1) * d].astype(F32))
        term = gate * _dot(y_ref[...], w_ref[...])
        merged = term if merged is None else merged + term
    xn = x_ref[...] + _dot(merged.astype(BF16), wo_ref[...])
    xo_ref[...] = xn
    ms = jnp.mean(xn * xn, axis=-1, keepdims=True)
    h = xn * lax.rsqrt(ms + RMS_EPS) * gn_ref[...]
    hi, lo = _split_bf16(h)
    h_ref[...] = hi
    wrh = wrh_ref[...]
    lg_ref[...] = _dot(hi, wrh) + _dot(lo, wrh) + _dot(hi, wrl_ref[...]) + br_ref[...]


def _merge(x2, ya, yb, yc, yd, proj, wa, wb, wc, wd, wo, gn_row, wr_hi, wr_lo, br_row, tm):
    n, d = x2.shape
    assert COL_GATE % (4 * d) == 0
    gcol = COL_GATE // (4 * d)

    def rows(width, cc=0):
        return pl.BlockSpec((tm, width), lambda i: (i, cc))

    def full(arr):
        return pl.BlockSpec(arr.shape, lambda i: (0, 0))

    return pl.pallas_call(
        _merge_kernel,
        grid=(n // tm,),
        in_specs=[rows(d), rows(ya.shape[1]), rows(yb.shape[1]), rows(yc.shape[1]), rows(yd.shape[1]),
                  rows(4 * d, gcol),
                  full(wa), full(wb), full(wc), full(wd), full(wo), full(gn_row),
                  full(wr_hi), full(wr_lo), full(br_row)],
        out_specs=[rows(d), rows(d), rows(LANES)],
        out_shape=[jax.ShapeDtypeStruct((n, d), F32), jax.ShapeDtypeStruct((n, d), BF16),
                   jax.ShapeDtypeStruct((n, LANES), F32)],
        compiler_params=_cparams(("parallel",)),
        name="merge",
    )(x2, ya, yb, yc, yd, proj, wa, wb, wc, wd, wo, gn_row, wr_hi, wr_lo, br_row)


def _first_index(mask, lane):
    return jnp.min(jnp.where(mask, lane, float(LANES)), axis=-1, keepdims=True)


def _moe_kernel(h_ref, x_ref, lg_ref, wg_ref, wu_ref, wd_ref, o_ref, cw_ref, acc_ref):
    e = pl.program_id(1)
    h = h_ref[...]

    @pl.when(e == 0)
    def _():
        logits = lg_ref[...]
        lane = lax.broadcasted_iota(jnp.int32, logits.shape, 1).astype(F32)
        neg = -jnp.inf
        gl = jnp.where(lane < N_GROUPS, logits, neg)
        gmax = jnp.max(gl, axis=-1, keepdims=True)
        gsum = jnp.sum(jnp.exp(gl - gmax), axis=-1, keepdims=True)
        g_val = 1.0 / gsum
        g_idx = _first_index(gl == gmax, lane)
        lo = N_GROUPS + EXPERTS_PER_GROUP * g_idx
        el = jnp.where((lane >= lo) & (lane < lo + EXPERTS_PER_GROUP), logits, neg)
        emax = jnp.max(el, axis=-1, keepdims=True)
        esum = jnp.sum(jnp.exp(el - emax), axis=-1, keepdims=True)
        i1 = _first_index(el == emax, lane)
        el2 = jnp.where(lane == i1, neg, el)
        e2max = jnp.max(el2, axis=-1, keepdims=True)
        i2 = _first_index(el2 == e2max, lane)
        p1 = 1.0 / esum
        p2 = jnp.exp(e2max - emax) / esum
        tot = p1 + p2
        cw_ref[...] = (jnp.where(lane == i1, g_val * (p1 / tot), 0.0)
                       + jnp.where(lane == i2, g_val * (p2 / tot), 0.0))
        acc_ref[...] = jnp.zeros(acc_ref.shape, F32)

    a = _dot(h, wg_ref[0])
    hid = (a * jax.nn.sigmoid(a)) * _dot(h, wu_ref[0])
    lane = lax.broadcasted_iota(jnp.int32, cw_ref.shape, 1)
    cw_all = cw_ref[...]
    cw0 = jnp.sum(jnp.where(lane == 2 * e + N_GROUPS, cw_all, 0.0), axis=-1, keepdims=True)
    cw1 = jnp.sum(jnp.where(lane == 2 * e + 1 + N_GROUPS, cw_all, 0.0), axis=-1, keepdims=True)
    de = hid.shape[1] // 2
    hcol = lax.broadcasted_iota(jnp.int32, hid.shape, 1)
    hid = hid * jnp.where(hcol < de, cw0, cw1)
    acc_ref[...] += _dot(hid.astype(BF16), wd_ref[0])

    @pl.when(e == pl.num_programs(1) - 1)
    def _():
        o_ref[...] = x_ref[...] + acc_ref[...]


def _pair_cols(w):
    ne, d, de = w.shape
    return w.astype(BF16).reshape(ne // 2, 2, d, de).transpose(0, 2, 1, 3).reshape(ne // 2, d, 2 * de)


def _pair_rows(w):
    ne, de, d = w.shape
    return w.astype(BF16).reshape(ne // 2, 2 * de, d)


def _moe(h, x2, logits, wg, wu, wd, tm):
    n, d = x2.shape
    ne, _, de = wg.shape
    return pl.pallas_call(
        _moe_kernel,
        grid=(n // tm, ne),
        in_specs=[pl.BlockSpec((tm, d), lambda i, e: (i, 0)),
                  pl.BlockSpec((tm, d), lambda i, e: (i, 0)),
                  pl.BlockSpec((tm, LANES), lambda i, e: (i, 0)),
                  pl.BlockSpec((1, d, de), lambda i, e: (e, 0, 0)),
                  pl.BlockSpec((1, d, de), lambda i, e: (e, 0, 0)),
                  pl.BlockSpec((1, de, d), lambda i, e: (e, 0, 0))],
        out_specs=pl.BlockSpec((tm, d), lambda i, e: (i, 0)),
        out_shape=jax.ShapeDtypeStruct((n, d), F32),
        scratch_shapes=[pltpu.VMEM((tm, LANES), F32), pltpu.VMEM((tm, d), F32)],
        compiler_params=_cparams(("parallel", "arbitrary")),
        name="moe",
    )(h, x2, logits, wg, wu, wd)


def _pick(n, pref):
    t = min(n, pref)
    while n % t:
        t //= 2
    return t


def kernel(x, positions, norm_mix, w_in, da_q_gain, da_k_gain, da_lambda_q1, da_lambda_k1,
           da_lambda_q2, da_lambda_k2, da_subln_gain, hg_lower_bounds, hg_out_gain,
           s5_lambda_re, s5_lambda_im, s5_log_step, s5_b_re, s5_b_im, s5_c_re, s5_c_im,
           s5_d, s5_glu_w, s5_glu_b, w_branch_attn, w_branch_sb, w_branch_hgrn, w_branch_s5,
           w_out, norm_ffn, router_group_w, router_group_b, router_expert_w, router_expert_b,
           expert_w_gate, expert_w_up, expert_w_down):
    bsz, seq, d = x.shape
    depth = w_in.shape[0]
    n = bsz * seq
    assert w_in.shape[2] == IN_COLS and seq % 128 == 0
    tm = _pick(n, 1024)
    tq = _pick(seq, 256)
    tq_da = _pick(seq, DA_TQ)
    tb = _pick(seq, 256)

    x2 = x.reshape(n, d).astype(F32)
    cos_t, sina_t, sinb_t = _rope_tables(positions.reshape(n, 1).astype(jnp.int32), tm)

    lb_all = jnp.cumsum(jax.nn.softmax(hg_lower_bounds.astype(F32), axis=0), axis=0)
    lb_all = lb_all - lb_all[0:1]

    for l in range(depth):
        lambda_init = DA_LAMBDA_INIT_BASE - DA_LAMBDA_INIT_SCALE * math.exp(-DA_LAMBDA_INIT_RATE * l)
        w_f = w_in[l].astype(F32)
        sbq = COL_SB_Q - COL_DA_Q
        w_l = jnp.concatenate([w_f[:, REF_GATE_START:], w_f[:, :sbq],
                               w_f[:, sbq:sbq + SB_HEADS * SB_HEAD_DIM] * (SB_HEAD_DIM ** -0.5 * LOG2E),
                               w_f[:, sbq + SB_HEADS * SB_HEAD_DIM:REF_GATE_START]], axis=1)
        proj = _norm_proj(x2, norm_mix[l].astype(F32)[None, :], w_l.astype(BF16), tm, 768)

        qk_gain = jnp.concatenate([jnp.tile(da_q_gain[l].astype(F32), 2 * DA_HEADS),
                                   jnp.tile(da_k_gain[l].astype(F32), 2 * DA_HEADS)])[None, :]
        qk = _qk_prep(proj, qk_gain, cos_t, sina_t, sinb_t, tm)
        y_a = _diff_attn(qk, proj,
                         da_lambda_q1[l].astype(F32)[None, :], da_lambda_k1[l].astype(F32)[None, :],
                         da_lambda_q2[l].astype(F32)[None, :], da_lambda_k2[l].astype(F32)[None, :],
                         da_subln_gain[l].astype(F32)[None, :], bsz, seq, tq_da, DA_GROUP, lambda_init)
        y_b = _stick_break(proj, bsz, seq, _pick(seq, SB_TQ), SB_KW, SB_GROUP)
        y_c = _hgrn(proj, lb_all[l][None, :], jnp.tile(hg_out_gain[l].astype(F32), HG_HEADS)[None, :],
                    bsz, seq, tb)
        s5p = _s5_params(s5_lambda_re[l], s5_lambda_im[l], s5_log_step[l], s5_b_re[l], s5_b_im[l],
                         s5_c_re[l], s5_c_im[l], tb)
        y_d = _s5(proj, s5p, s5_d[l].astype(F32)[None, :], s5_glu_w[l].astype(BF16),
                  s5_glu_b[l].astype(F32)[None, :], bsz, seq, tb)

        wr = jnp.concatenate([router_group_w[l], router_expert_w[l]], axis=1).astype(F32)
        wr = jnp.pad(wr, ((0, 0), (0, LANES - wr.shape[1])))
        wr_hi = wr.astype(BF16)
        wr_lo = (wr - wr_hi.astype(F32)).astype(BF16)
        br = jnp.concatenate([router_group_b[l], router_expert_b[l]]).astype(F32)
        br = jnp.pad(br, (0, LANES - br.shape[0]))[None, :]
        x2, h, logits = _merge(x2, y_a, y_b, y_c, y_d, proj,
                               w_branch_attn[l].astype(BF16), w_branch_sb[l].astype(BF16),
                               w_branch_hgrn[l].astype(BF16), w_branch_s5[l].astype(BF16),
                               w_out[l].astype(BF16), norm_ffn[l].astype(F32)[None, :],
                               wr_hi, wr_lo, br, _pick(n, 512))
        x2 = _moe(h, x2, logits, _pair_cols(expert_w_gate[l]), _pair_cols(expert_w_up[l]),
                  _pair_rows(expert_w_down[l]), tm)

    return x2.reshape(bsz, seq, d).astype(x.dtype)
```

```python
import functools
import math

import jax
import jax.numpy as jnp
import numpy as np
from jax import lax
from jax.experimental import pallas as pl
from jax.experimental.pallas import tpu as pltpu

F32 = jnp.float32
BF16 = jnp.bfloat16

RMS_EPS = 1e-6
LANES = 128
LOG2E = 1.4426950408889634

DA_HEADS = 4
DA_QK_DIM = 64
ROPE_THETA = 500000.0
ROPE_DIM = DA_QK_DIM // 4
DA_LAMBDA_INIT_BASE = 0.8
DA_LAMBDA_INIT_SCALE = 0.6
DA_LAMBDA_INIT_RATE = 0.3

SB_HEADS = 4
SB_HEAD_DIM = 64

HG_HEADS = 4
HG_DIM = 64
HG_COLS = HG_HEADS * HG_DIM
HG_SUB = 16
HG_UNROLL = 4

S5_GROUPS = 16
S5_GROUP_CH = 16
S5_STATE = 64
S5_WIDTH = S5_GROUPS * S5_GROUP_CH
S5_NSTATE = S5_GROUPS * S5_STATE
S5_EIG_CLIP = -1e-4

N_GROUPS = 4
EXPERTS_PER_GROUP = 4
N_EXPERTS = N_GROUPS * EXPERTS_PER_GROUP

REF_GATE_START = 3584
COL_GATE = 0
COL_DA_Q = 4096
COL_DA_K = 4608
COL_DA_V = 5120
COL_SB_Q = 5632
COL_SB_K = 5888
COL_SB_V = 6144
COL_HG_F = 6400
COL_HG_I = 6656
COL_HG_Q = 6912
COL_HG_G = 7168
COL_S5_U = 7424
IN_COLS = 7680

VMEM_LIMIT = 48 * 1024 * 1024

DA_TQ = 1024
DA_GROUP = 1
SB_TQ = 512
SB_KW = 256
SB_GROUP = 4
MOE_ROW_BLOCK = 512
MOE_DMA_BURST = 256

_NT = (((1,), (1,)), ((), ()))
_TN = (((0,), (0,)), ((), ()))


def _cparams(sem):
    return pltpu.CompilerParams(dimension_semantics=sem, vmem_limit_bytes=VMEM_LIMIT)


def _dot(a, b):
    return jnp.dot(a, b, preferred_element_type=F32)


def _norm_proj_kernel(x_ref, g_ref, w_ref, o_ref, h_ref):
    @pl.when(pl.program_id(1) == 0)
    def _():
        x = x_ref[...]
        ms = jnp.mean(x * x, axis=-1, keepdims=True)
        h_ref[...] = (x * lax.rsqrt(ms + RMS_EPS) * g_ref[...]).astype(BF16)

    o_ref[...] = _dot(h_ref[...], w_ref[...]).astype(o_ref.dtype)


def _norm_proj(x2, gain, w, tm, tn):
    n, d = x2.shape
    cols = w.shape[1]
    return pl.pallas_call(
        _norm_proj_kernel,
        grid=(n // tm, cols // tn),
        in_specs=[pl.BlockSpec((tm, d), lambda i, j: (i, 0)),
                  pl.BlockSpec((1, d), lambda i, j: (0, 0)),
                  pl.BlockSpec((d, tn), lambda i, j: (0, j))],
        out_specs=pl.BlockSpec((tm, tn), lambda i, j: (i, j)),
        out_shape=jax.ShapeDtypeStruct((n, cols), BF16),
        scratch_shapes=[pltpu.VMEM((tm, d), BF16)],
        compiler_params=_cparams(("parallel", "arbitrary")),
        name="norm_proj",
    )(x2, gain, w)


def _rope_kernel(pos_ref, invf_ref, sa_ref, sb_ref, cos_ref, sina_ref, sinb_ref):
    ang = pos_ref[...].astype(F32) * invf_ref[...]
    c = jnp.cos(ang)
    s = jnp.sin(ang)
    cos_ref[...] = c
    sina_ref[...] = s * sa_ref[...]
    sinb_ref[...] = s * sb_ref[...]


def _rope_tables(pos_col, tm):
    n = pos_col.shape[0]
    half = ROPE_DIM // 2
    inv_freq = jnp.exp(-math.log(ROPE_THETA) * jnp.arange(half, dtype=F32) * (2.0 / ROPE_DIM))
    lane = np.arange(LANES) % DA_QK_DIM
    invf = jnp.where(lane < ROPE_DIM, inv_freq[lane % half], 0.0).astype(F32)[None, :]
    sgn_a = jnp.asarray(np.where(lane < half, -1.0, 0.0), F32)[None, :]
    sgn_b = jnp.asarray(np.where((lane >= half) & (lane < ROPE_DIM), 1.0, 0.0), F32)[None, :]
    row = pl.BlockSpec((1, LANES), lambda i: (0, 0))
    tab = pl.BlockSpec((tm, LANES), lambda i: (i, 0))
    shp = jax.ShapeDtypeStruct((n, LANES), F32)
    return pl.pallas_call(
        _rope_kernel,
        grid=(n // tm,),
        in_specs=[pl.BlockSpec((tm, 1), lambda i: (i, 0)), row, row, row],
        out_specs=[tab, tab, tab],
        out_shape=[shp, shp, shp],
        compiler_params=_cparams(("parallel",)),
        name="rope_tables",
    )(pos_col, invf, sgn_a, sgn_b)


def _qk_prep_kernel(x_ref, gain_ref, cos_ref, sina_ref, sinb_ref, bd_ref, o_ref):
    c = cos_ref[...]
    sa = sina_ref[...]
    sb = sinb_ref[...]
    bd = bd_ref[...]
    n_tiles = x_ref.shape[1] // LANES
    for j in range(n_tiles):
        sl = slice(j * LANES, (j + 1) * LANES)
        t = x_ref[:, sl].astype(F32)
        ss = _dot((t * t).astype(BF16), bd)
        y = t * lax.rsqrt(ss * (1.0 / DA_QK_DIM) + RMS_EPS) * gain_ref[:, sl]
        y = (y * c + pltpu.roll(y, LANES - ROPE_DIM // 2, 1) * sa
             + pltpu.roll(y, ROPE_DIM // 2, 1) * sb)
        if j < n_tiles // 2:
            y = y * (DA_QK_DIM ** -0.5 * LOG2E)
        o_ref[:, sl] = y.astype(BF16)


def _qk_prep(proj, gain_row, cos_t, sina_t, sinb_t, tm):
    n = proj.shape[0]
    w = 2 * DA_HEADS * 2 * DA_QK_DIM
    seg = np.arange(LANES) // DA_QK_DIM
    bd = jnp.asarray(seg[:, None] == seg[None, :], BF16)
    tab = pl.BlockSpec((tm, LANES), lambda i: (i, 0))
    return pl.pallas_call(
        _qk_prep_kernel,
        grid=(n // tm,),
        in_specs=[pl.BlockSpec((tm, w), lambda i: (i, COL_DA_Q // w)),
                  pl.BlockSpec((1, w), lambda i: (0, 0)),
                  tab, tab, tab,
                  pl.BlockSpec((LANES, LANES), lambda i: (0, 0))],
        out_specs=pl.BlockSpec((tm, w), lambda i: (i, 0)),
        out_shape=jax.ShapeDtypeStruct((n, w), BF16),
        compiler_params=_cparams(("parallel",)),
        name="qk_prep",
    )(proj, gain_row, cos_t, sina_t, sinb_t, bd)


def _diff_attn_kernel(q_ref, k_ref, v_ref, lq1_ref, lk1_ref, lq2_ref, lk2_ref, sg_ref, o_ref,
                      m_ref, l_ref, acc_ref, *, tq, group, lambda_init):
    qi = pl.program_id(2)
    q = q_ref[...]
    lane = lax.broadcasted_iota(jnp.int32, q.shape, 1)
    zero = jnp.zeros_like(q)
    q2 = jnp.concatenate([jnp.where(lane < DA_QK_DIM, q, zero),
                          jnp.where(lane >= DA_QK_DIM, q, zero)], axis=0)
    m_ref[...] = jnp.full(m_ref.shape, -jnp.inf, F32)
    l_ref[...] = jnp.zeros(l_ref.shape, F32)
    acc_ref[...] = jnp.zeros(acc_ref.shape, F32)

    def tile(start, width, masked):
        kb = k_ref[pl.ds(start, width), :]
        vb = v_ref[pl.ds(start, width), :]
        n_lt = width // LANES
        s = lax.dot_general(q2, kb, _NT, preferred_element_type=F32)
        if masked:
            row = lax.broadcasted_iota(jnp.int32, (tq, width), 0)
            col = lax.broadcasted_iota(jnp.int32, (tq, width), 1)
            keep = col <= row
            s = jnp.where(jnp.concatenate([keep, keep], axis=0), s, -jnp.inf)
        st = [s[:, j * LANES:(j + 1) * LANES] for j in range(n_lt)]
        smax = st[0]
        for j in range(1, n_lt):
            smax = jnp.maximum(smax, st[j])
        m_prev = m_ref[...]
        m_new = jnp.maximum(m_prev, jnp.max(smax, axis=-1, keepdims=True))
        alpha = jnp.exp2(m_prev - m_new)
        ps = [jnp.exp2(t - m_new) for t in st]
        lsum = ps[0]
        for j in range(1, n_lt):
            lsum = lsum + ps[j]
        p = jnp.concatenate([t.astype(BF16) for t in ps], axis=1)
        l_ref[...] = alpha * l_ref[...] + lsum
        acc_ref[...] = alpha * acc_ref[...] + _dot(p, vb)
        m_ref[...] = m_new

    n_group = qi // group

    def body_group(gi, carry):
        tile(pl.multiple_of(gi * (group * tq), group * tq), group * tq, False)
        return carry

    def body_single(ki, carry):
        tile(pl.multiple_of(ki * tq, tq), tq, False)
        return carry

    lax.fori_loop(0, n_group, body_group, 0)
    lax.fori_loop(n_group * group, qi, body_single, 0)
    tile(pl.multiple_of(qi * tq, tq), tq, True)

    lam = (jnp.exp(jnp.sum(lq1_ref[...] * lk1_ref[...], axis=-1, keepdims=True))
           - jnp.exp(jnp.sum(lq2_ref[...] * lk2_ref[...], axis=-1, keepdims=True)) + lambda_init)
    o = acc_ref[...] / jnp.sum(l_ref[...], axis=-1, keepdims=True)
    o = o[:tq] - lam * o[tq:]
    ms = jnp.mean(o * o, axis=-1, keepdims=True)
    o = o * lax.rsqrt(ms + RMS_EPS) * sg_ref[...] * (1.0 - lambda_init)
    o_ref[...] = o.astype(o_ref.dtype)


def _diff_attn(qk, proj, lq1, lk1, lq2, lk2, subln, bsz, seq, tq, group, lambda_init):
    n = qk.shape[0]
    nq = seq // tq
    kcol = (DA_HEADS * 2 * DA_QK_DIM) // LANES
    vcol = COL_DA_V // LANES
    vec = pl.BlockSpec((1, DA_QK_DIM), lambda b, h, i: (0, 0))
    return pl.pallas_call(
        functools.partial(_diff_attn_kernel, tq=tq, group=group, lambda_init=lambda_init),
        grid=(bsz, DA_HEADS, nq),
        in_specs=[pl.BlockSpec((tq, LANES), lambda b, h, i: (b * nq + i, h)),
                  pl.BlockSpec((seq, LANES), lambda b, h, i: (b, kcol + h)),
                  pl.BlockSpec((seq, LANES), lambda b, h, i: (b, vcol + h)),
                  vec, vec, vec, vec,
                  pl.BlockSpec((1, LANES), lambda b, h, i: (0, 0))],
        out_specs=pl.BlockSpec((tq, LANES), lambda b, h, i: (b * nq + i, h)),
        out_shape=jax.ShapeDtypeStruct((n, DA_HEADS * LANES), BF16),
        scratch_shapes=[pltpu.VMEM((2 * tq, LANES), F32), pltpu.VMEM((2 * tq, LANES), F32),
                        pltpu.VMEM((2 * tq, LANES), F32)],
        compiler_params=_cparams(("parallel", "parallel", "arbitrary")),
        name="diff_attn",
    )(qk, qk, proj, lq1, lk1, lq2, lk2, subln)


def _stick_break_kernel(q_ref, k_ref, v_ref, o_ref, r_ref, acc_ref, *, tq, kw, group):
    qi = pl.program_id(2)
    q = q_ref[...]
    lane = lax.broadcasted_iota(jnp.int32, q.shape, 1)
    zero = jnp.zeros_like(q)
    q2 = jnp.concatenate([jnp.where(lane < SB_HEAD_DIM, q, zero),
                          jnp.where(lane >= SB_HEAD_DIM, q, zero)], axis=0)
    incl = jnp.where(lax.broadcasted_iota(jnp.int32, (kw, kw), 0)
                     >= lax.broadcasted_iota(jnp.int32, (kw, kw), 1), 1.0, 0.0).astype(BF16)
    n_lt = kw // LANES
    n_diag = tq // kw
    r_ref[...] = jnp.zeros(r_ref.shape, F32)
    acc_ref[...] = jnp.zeros(acc_ref.shape, F32)

    def run(starts, offsets):
        masks, us, sps = [], [], []
        for start, off in zip(starts, offsets):
            u = lax.dot_general(q2, k_ref[pl.ds(start, kw), :], _NT, preferred_element_type=F32)
            neg_abs = lax.bitcast_convert_type(
                lax.bitcast_convert_type(u, jnp.int32) | jnp.int32(-2 ** 31), F32)
            sp = jnp.maximum(u, 0.0) + jnp.log(1.0 + jnp.exp2(neg_abs)) * LOG2E
            mask = None
            if off is not None:
                keep = (lax.broadcasted_iota(jnp.int32, (tq, kw), 1) + off
                        < lax.broadcasted_iota(jnp.int32, (tq, kw), 0))
                mask = jnp.concatenate([keep, keep], axis=0)
                sp = jnp.where(mask, sp, 0.0)
            masks.append(mask)
            us.append(u)
            sps.append(sp.astype(BF16))
        cum_all = _dot(jnp.concatenate(sps, axis=0), incl)
        r = r_ref[...]
        total = None
        for i, start in enumerate(starts):
            cum = cum_all[i * 2 * tq:(i + 1) * 2 * tq]
            w = jnp.exp2(us[i] - (cum + jnp.concatenate([r] * n_lt, axis=1)))
            if masks[i] is not None:
                w = jnp.where(masks[i], w, 0.0)
            part = _dot(w.astype(BF16), v_ref[pl.ds(start, kw), :])
            total = part if total is None else total + part
            r = r + cum[:, 0:1]
        acc_ref[...] += total
        r_ref[...] = r

    base = qi * tq
    run([pl.multiple_of(base + (n_diag - 1 - j) * kw, kw) for j in range(n_diag)],
        [(n_diag - 1 - j) * kw for j in range(n_diag)])

    n_below = qi * n_diag
    rem = n_below % group

    def body_single(j, carry):
        run([pl.multiple_of((n_below - 1 - j) * kw, kw)], [None])
        return carry

    def body_group(gi, carry):
        top = n_below - rem - gi * group
        run([pl.multiple_of((top - 1 - j) * kw, kw) for j in range(group)], [None] * group)
        return carry

    lax.fori_loop(0, rem, body_single, 0)
    lax.fori_loop(0, n_below // group, body_group, 0)
    lane_o = lax.broadcasted_iota(jnp.int32, (tq, LANES), 1)
    o_ref[...] = jnp.where(lane_o < SB_HEAD_DIM, acc_ref[:tq, :], acc_ref[tq:, :]).astype(o_ref.dtype)


def _stick_break(proj, bsz, seq, tq, kw, group):
    n = proj.shape[0]
    nq = seq // tq
    pairs = (SB_HEADS * SB_HEAD_DIM) // LANES
    qc, kc, vc = COL_SB_Q // LANES, COL_SB_K // LANES, COL_SB_V // LANES
    return pl.pallas_call(
        functools.partial(_stick_break_kernel, tq=tq, kw=kw, group=group),
        grid=(bsz, pairs, nq),
        in_specs=[pl.BlockSpec((tq, LANES), lambda b, p, i: (b * nq + i, qc + p)),
                  pl.BlockSpec((seq, LANES), lambda b, p, i: (b, kc + p)),
                  pl.BlockSpec((seq, LANES), lambda b, p, i: (b, vc + p))],
        out_specs=pl.BlockSpec((tq, LANES), lambda b, p, i: (b * nq + i, p)),
        out_shape=jax.ShapeDtypeStruct((n, pairs * LANES), BF16),
        scratch_shapes=[pltpu.VMEM((2 * tq, LANES), F32), pltpu.VMEM((2 * tq, LANES), F32)],
        compiler_params=_cparams(("parallel", "parallel", "arbitrary")),
        name="stick_break",
    )(proj, proj, proj)


def _split_bf16(x):
    hi = x.astype(BF16)
    lo = (x - hi.astype(F32)).astype(BF16)
    return hi, lo


def _hgrn_kernel(f_ref, i_ref, q_ref, g_ref, lb_ref, gain_ref, o_ref,
                 st_ref, b_ref, dec_ref, qq_ref, kk_ref, vv_ref, oo_ref, qe_ref, ke_ref, gm_ref, uu_ref,
                 *, tb):
    @pl.when(pl.program_id(1) == 0)
    def _():
        st_ref[...] = jnp.zeros(st_ref.shape, F32)

    nsub = tb // HG_SUB
    z = f_ref[...].astype(F32)
    lb = lb_ref[...]
    sp = jnp.maximum(z, 0.0) + jnp.log(1.0 + jnp.exp(-jnp.abs(z)))
    log_sig = z - sp
    a = jnp.log(lb)
    c = jnp.log(1.0 - lb) + log_sig
    mx = jnp.maximum(a, c)
    log_f = mx + jnp.log(jnp.exp(a - mx) + jnp.exp(c - mx))
    key = (1.0 - lb) * jax.nn.sigmoid(-z)

    row = lax.broadcasted_iota(jnp.int32, (tb, tb), 0)
    col = lax.broadcasted_iota(jnp.int32, (tb, tb), 1)
    sub_shift = HG_SUB.bit_length() - 1
    dim_shift = HG_DIM.bit_length() - 1
    same = (row >> sub_shift) == (col >> sub_shift)
    tri = jnp.where(same & (col <= row), 1.0, 0.0).astype(BF16)
    blk = jnp.where(same, 1.0, 0.0).astype(BF16)
    hi, lo = _split_bf16(log_f)
    b = _dot(tri, hi) + _dot(tri, lo)
    e = _dot(blk, hi) + _dot(blk, lo)
    qf = q_ref[...].astype(F32)
    b_ref[...] = b
    qq_ref[...] = qf
    kk_ref[...] = key
    vv_ref[...] = i_ref[...].astype(F32)
    qe_ref[...] = (qf * jnp.exp(b)).astype(BF16)
    ke_ref[...] = (key * jnp.exp(e - b)).astype(BF16)
    dec_ref[...] = jnp.exp(e)

    seg_r = lax.broadcasted_iota(jnp.int32, (HG_COLS, HG_COLS), 0) >> dim_shift
    seg_c = lax.broadcasted_iota(jnp.int32, (HG_COLS, HG_COLS), 1) >> dim_shift
    head_mask = seg_r == seg_c
    ones_bd = jnp.where(head_mask, 1.0, 0.0).astype(BF16)
    sel = jnp.where(lax.broadcasted_iota(jnp.int32, (HG_SUB, HG_SUB * HG_SUB), 0)
                    == (lax.broadcasted_iota(jnp.int32, (HG_SUB, HG_SUB * HG_SUB), 1) >> sub_shift),
                    1.0, 0.0).astype(BF16)
    srow = lax.broadcasted_iota(jnp.int32, (HG_SUB, HG_COLS), 0)

    def intra(ci, slot):
        r0 = pl.multiple_of(ci * HG_SUB, HG_SUB)
        bi = b_ref[pl.ds(r0, HG_SUB), :]
        qi = qq_ref[pl.ds(r0, HG_SUB), :]
        ki = kk_ref[pl.ds(r0, HG_SUB), :]
        vi = vv_ref[pl.ds(r0, HG_SUB), :]
        for t in range(HG_SUB):
            d = jnp.exp(jnp.minimum(bi[t:t + 1, :] - bi, 0.0))
            g = jnp.where(srow <= t, qi[t:t + 1, :] * ki * d, 0.0)
            gm_ref[slot, t * HG_SUB:(t + 1) * HG_SUB, :] = g.astype(BF16)
        sc = _dot(gm_ref[slot], ones_bd)
        vt = jnp.concatenate([vi] * HG_SUB, axis=0)
        oo_ref[pl.ds(r0, HG_SUB), :] = _dot(sel, (sc * vt).astype(BF16))
        upd = lax.dot_general(vi.astype(BF16), ke_ref[pl.ds(r0, HG_SUB), :], _TN,
                              preferred_element_type=F32)
        uu_ref[ci] = jnp.where(head_mask, upd, 0.0)

    def intra_group(gi, carry):
        for slot in range(HG_UNROLL):
            intra(gi * HG_UNROLL + slot, slot)
        return carry

    lax.fori_loop(0, nsub // HG_UNROLL, intra_group, 0)

    st = st_ref[...]
    for ci in range(nsub):
        rows = slice(ci * HG_SUB, (ci + 1) * HG_SUB)
        oo_ref[rows, :] += lax.dot_general(qe_ref[rows, :], st.astype(BF16), _NT,
                                           preferred_element_type=F32)
        st = st * dec_ref[ci * HG_SUB:ci * HG_SUB + 1, :] + uu_ref[ci]
    st_ref[...] = st

    o = oo_ref[...]
    ms = _dot((o * o).astype(BF16), ones_bd) * (1.0 / HG_DIM)
    gate = g_ref[...].astype(F32)
    gate = gate * jax.nn.sigmoid(gate)
    o_ref[...] = (o * lax.rsqrt(ms + RMS_EPS) * gain_ref[...] * gate).astype(o_ref.dtype)


def _hgrn(proj, lb_row, gain_row, bsz, seq, tb):
    n = proj.shape[0]
    nb = seq // tb
    w = HG_COLS
    cf, ci, cq, cg = COL_HG_F // w, COL_HG_I // w, COL_HG_Q // w, COL_HG_G // w

    def col(cc):
        return pl.BlockSpec((tb, w), lambda b, i: (b * nb + i, cc))

    rowspec = pl.BlockSpec((1, w), lambda b, i: (0, 0))
    big = pltpu.VMEM((tb, w), F32)
    return pl.pallas_call(
        functools.partial(_hgrn_kernel, tb=tb),
        grid=(bsz, nb),
        in_specs=[col(cf), col(ci), col(cq), col(cg), rowspec, rowspec],
        out_specs=pl.BlockSpec((tb, w), lambda b, i: (b * nb + i, 0)),
        out_shape=jax.ShapeDtypeStruct((n, w), BF16),
        scratch_shapes=[pltpu.VMEM((w, w), F32), big, big, big, big, big, big,
                        pltpu.VMEM((tb, w), BF16), pltpu.VMEM((tb, w), BF16),
                        pltpu.VMEM((HG_UNROLL, HG_SUB * HG_SUB, w), BF16),
                        pltpu.VMEM((tb // HG_SUB, w, w), F32)],
        compiler_params=_cparams(("parallel", "arbitrary")),
        name="hgrn2",
    )(proj, proj, proj, proj, lb_row, gain_row)


def _s5_kernel(u_ref, bblk_ref, lev_re_ref, lev_im_ref, pw_re_ref, pw_im_ref, cblk_ref, d_ref,
               gw_ref, gb_ref, o_ref, cr_ref, ci_ref, *, tb):
    @pl.when(pl.program_id(1) == 0)
    def _():
        cr_ref[...] = jnp.zeros(cr_ref.shape, F32)
        ci_ref[...] = jnp.zeros(ci_ref.shape, F32)

    u = u_ref[...]
    bu = _dot(u, bblk_ref[...])
    xr = bu[:, :S5_NSTATE]
    xi = bu[:, S5_NSTATE:]
    row = lax.broadcasted_iota(jnp.int32, xr.shape, 0)
    nlev = lev_re_ref.shape[0]
    for j in range(nlev):
        d = 1 << j
        ar = lev_re_ref[j:j + 1, :]
        ai = lev_im_ref[j:j + 1, :]
        keep = row >= d
        sr = jnp.where(keep, pltpu.roll(xr, d, 0), 0.0)
        si = jnp.where(keep, pltpu.roll(xi, d, 0), 0.0)
        xr, xi = xr + ar * sr - ai * si, xi + ar * si + ai * sr
    cr = cr_ref[...]
    ci = ci_ref[...]
    pr = pw_re_ref[...]
    pi = pw_im_ref[...]
    xr, xi = xr + pr * cr - pi * ci, xi + pr * ci + pi * cr
    cr_ref[...] = xr[tb - 1:tb, :]
    ci_ref[...] = xi[tb - 1:tb, :]
    y = (_dot(xr.astype(BF16), cblk_ref[:S5_NSTATE, :]) + _dot(xi.astype(BF16), cblk_ref[S5_NSTATE:, :])
         + d_ref[...] * u.astype(F32))
    y = jax.nn.gelu(y)
    zg = _dot(y.astype(BF16), gw_ref[...]) + gb_ref[...]
    o_ref[...] = (y * jax.nn.sigmoid(zg)).astype(o_ref.dtype)


def _s5_params(lam_re, lam_im, log_step, b_re, b_im, c_re, c_im, tb):
    lam_re = jnp.minimum(lam_re.astype(F32), S5_EIG_CLIP)
    lam_im = lam_im.astype(F32)
    step = jnp.exp(log_step.astype(F32))[:, None]
    mag = jnp.exp(lam_re * step)
    phase = lam_im * step
    a_re = mag * jnp.cos(phase)
    a_im = mag * jnp.sin(phase)
    denom = lam_re * lam_re + lam_im * lam_im
    num_re = a_re - 1.0
    gam_re = (num_re * lam_re + a_im * lam_im) / denom
    gam_im = (a_im * lam_re - num_re * lam_im) / denom
    b_re = b_re.astype(F32)
    b_im = b_im.astype(F32)
    bb_re = gam_re[..., None] * b_re - gam_im[..., None] * b_im
    bb_im = gam_re[..., None] * b_im + gam_im[..., None] * b_re
    eye = jnp.eye(S5_GROUPS, dtype=F32)

    def in_blk(bb):
        return jnp.einsum('gnc,gh->gchn', bb, eye).reshape(S5_WIDTH, S5_NSTATE)

    def out_blk(cc):
        return jnp.einsum('gcn,gh->gnhc', cc.astype(F32), eye).reshape(S5_NSTATE, S5_WIDTH)

    bblk = jnp.concatenate([in_blk(bb_re), in_blk(bb_im)], axis=1).astype(BF16)
    cblk = jnp.concatenate([out_blk(c_re), -out_blk(c_im)], axis=0).astype(BF16)
    ar = a_re.reshape(1, S5_NSTATE)
    ai = a_im.reshape(1, S5_NSTATE)
    nlev = int(math.log2(tb))
    lev_re, lev_im = [ar], [ai]
    pw_re, pw_im = ar, ai
    for _ in range(nlev):
        sr, si = lev_re[-1], lev_im[-1]
        pw_re, pw_im = (jnp.concatenate([pw_re, pw_re * sr - pw_im * si], axis=0),
                        jnp.concatenate([pw_im, pw_re * si + pw_im * sr], axis=0))
        lev_re.append(sr * sr - si * si)
        lev_im.append(2.0 * sr * si)
    lev_re = jnp.concatenate(lev_re[:nlev], axis=0)
    lev_im = jnp.concatenate(lev_im[:nlev], axis=0)
    return bblk, cblk, lev_re, lev_im, pw_re, pw_im


def _s5(proj, params, d_row, glu_w, glu_b, bsz, seq, tb):
    n = proj.shape[0]
    nb = seq // tb
    bblk, cblk, lev_re, lev_im, pw_re, pw_im = params
    nlev = lev_re.shape[0]
    ucol = COL_S5_U // S5_WIDTH

    def full(shape):
        return pl.BlockSpec(shape, lambda b, i: (0,) * len(shape))

    return pl.pallas_call(
        functools.partial(_s5_kernel, tb=tb),
        grid=(bsz, nb),
        in_specs=[pl.BlockSpec((tb, S5_WIDTH), lambda b, i: (b * nb + i, ucol)),
                  full((S5_WIDTH, 2 * S5_NSTATE)),
                  full((nlev, S5_NSTATE)), full((nlev, S5_NSTATE)),
                  full((tb, S5_NSTATE)), full((tb, S5_NSTATE)),
                  full((2 * S5_NSTATE, S5_WIDTH)),
                  full((1, S5_WIDTH)), full((S5_WIDTH, S5_WIDTH)), full((1, S5_WIDTH))],
        out_specs=pl.BlockSpec((tb, S5_WIDTH), lambda b, i: (b * nb + i, 0)),
        out_shape=jax.ShapeDtypeStruct((n, S5_WIDTH), BF16),
        scratch_shapes=[pltpu.VMEM((1, S5_NSTATE), F32), pltpu.VMEM((1, S5_NSTATE), F32)],
        compiler_params=_cparams(("parallel", "arbitrary")),
        name="s5",
    )(proj, bblk, lev_re, lev_im, pw_re, pw_im, cblk, d_row, glu_w, glu_b)


def _merge_kernel(x_ref, ya_ref, yb_ref, yc_ref, yd_ref, gate_ref, wa_ref, wb_ref, wc_ref, wd_ref,
                  wo_ref, gn_ref, wrh_ref, wrl_ref, br_ref, xo_ref, hx_ref):
    d = x_ref.shape[1]
    merged = None
    for i, (y_ref, w_ref) in enumerate(((ya_ref, wa_ref), (yb_ref, wb_ref),
                                        (yc_ref, wc_ref), (yd_ref, wd_ref))):
        gate = jax.nn.sigmoid(gate_ref[:, i * d:(i + 1) * d].astype(F32))
        term = gate * _dot(y_ref[...], w_ref[...])
        merged = term if merged is None else merged + term
    xn = x_ref[...] + _dot(merged.astype(BF16), wo_ref[...])
    xo_ref[...] = xn
    ms = jnp.mean(xn * xn, axis=-1, keepdims=True)
    h = xn * lax.rsqrt(ms + RMS_EPS) * gn_ref[...]
    hi, lo = _split_bf16(h)
    hx_ref[:, :d] = hi.astype(F32)
    wrh = wrh_ref[...]
    hx_ref[:, d:] = _dot(hi, wrh) + _dot(lo, wrh) + _dot(hi, wrl_ref[...]) + br_ref[...]


def _merge(x2, ya, yb, yc, yd, proj, wa, wb, wc, wd, wo, gn_row, wr_hi, wr_lo, br_row, tm):
    n, d = x2.shape
    assert COL_GATE % (4 * d) == 0
    gcol = COL_GATE // (4 * d)

    def rows(width, cc=0):
        return pl.BlockSpec((tm, width), lambda i: (i, cc))

    def full(arr):
        return pl.BlockSpec(arr.shape, lambda i: (0, 0))

    return pl.pallas_call(
        _merge_kernel,
        grid=(n // tm,),
        in_specs=[rows(d), rows(ya.shape[1]), rows(yb.shape[1]), rows(yc.shape[1]), rows(yd.shape[1]),
                  rows(4 * d, gcol),
                  full(wa), full(wb), full(wc), full(wd), full(wo), full(gn_row),
                  full(wr_hi), full(wr_lo), full(br_row)],
        out_specs=[rows(d), rows(d + LANES)],
        out_shape=[jax.ShapeDtypeStruct((n, d), F32), jax.ShapeDtypeStruct((n, d + LANES), F32)],
        compiler_params=_cparams(("parallel",)),
        name="merge",
    )(x2, ya, yb, yc, yd, proj, wa, wb, wc, wd, wo, gn_row, wr_hi, wr_lo, br_row)


def _first_index(mask, lane):
    return jnp.min(jnp.where(mask, lane, float(LANES)), axis=-1, keepdims=True)


def _group_onehot(logits):
    lane = lax.broadcasted_iota(jnp.int32, logits.shape, 1).astype(F32)
    gl = jnp.where(lane < N_GROUPS, logits, -jnp.inf)
    g_idx = _first_index(gl == jnp.max(gl, axis=-1, keepdims=True), lane)
    return jnp.where(lane == g_idx, 1.0, 0.0)


def _route_kernel(lg_ref, tri_ref, pos_ref, meta_ref, cnt_ref, run_ref, off_ref, *, row_block):
    phase = pl.program_id(0)
    i = pl.program_id(1)
    onehot = _group_onehot(lg_ref[...])
    lane = lax.broadcasted_iota(jnp.int32, (1, LANES), 1)

    @pl.when((phase == 0) & (i == 0))
    def _():
        cnt_ref[...] = jnp.zeros(cnt_ref.shape, F32)

    @pl.when(phase == 0)
    def _():
        cnt_ref[...] += jnp.sum(onehot, axis=0, keepdims=True)
        pos_ref[...] = jnp.zeros(pos_ref.shape, jnp.int32)

    @pl.when((phase == 1) & (i == 0))
    def _():
        padded = jnp.floor((cnt_ref[...] + (row_block - 1)) * (1.0 / row_block)) * row_block
        off = jnp.zeros((1, LANES), F32)
        acc = jnp.zeros((1, 1), F32)
        for g in range(1, N_GROUPS):
            acc = acc + jnp.sum(jnp.where(lane == g - 1, padded, 0.0), axis=-1, keepdims=True)
            off = off + jnp.where(lane == g, acc, 0.0)
        off_ref[...] = off
        run_ref[...] = jnp.zeros(run_ref.shape, F32)
        meta_ref[...] = jnp.zeros(meta_ref.shape, F32)
        meta_ref[0:1, :] = cnt_ref[...]
        meta_ref[1:2, :] = off

    @pl.when(phase == 1)
    def _():
        before = _dot(tri_ref[...], onehot.astype(BF16))
        slot = jnp.sum(onehot * (before + run_ref[...] + off_ref[...]), axis=-1, keepdims=True)
        pos_ref[...] = slot.astype(jnp.int32)
        run_ref[...] += jnp.sum(onehot, axis=0, keepdims=True)


def _moe_route(hx, d, tr, row_block):
    n = hx.shape[0]
    nb = n // tr
    tri = jnp.asarray(np.tril(np.ones((tr, tr), np.float32), -1), BF16)
    pos, meta = pl.pallas_call(
        functools.partial(_route_kernel, row_block=row_block),
        grid=(2, nb),
        in_specs=[pl.BlockSpec((tr, LANES), lambda p, i: (i, d // LANES)),
                  pl.BlockSpec((tr, tr), lambda p, i: (0, 0))],
        out_specs=[pl.BlockSpec((tr, 1), lambda p, i: (p * nb + i, 0)),
                   pl.BlockSpec((8, LANES), lambda p, i: (0, 0))],
        out_shape=[jax.ShapeDtypeStruct((2 * n, 1), jnp.int32), jax.ShapeDtypeStruct((8, LANES), F32)],
        scratch_shapes=[pltpu.VMEM((1, LANES), F32)] * 3,
        compiler_params=_cparams(("arbitrary", "arbitrary")),
        name="moe_route",
    )(hx, tri)
    return pos[n:], meta


def _row_dma_wait(src_hbm, dst_ref, sem, rows):
    pltpu.make_async_copy(src_hbm.at[pl.ds(0, rows)], dst_ref.at[pl.ds(0, rows)], sem).wait()


def _scatter_kernel(pos_ref, hx_hbm, hs_in_hbm, hs_hbm, sem, *, tr, burst):
    del hs_in_hbm
    base = pl.program_id(0) * tr

    def one_burst(bi, carry):
        def issue(t, c):
            row = bi * burst + t
            pltpu.make_async_copy(hx_hbm.at[pl.ds(base + row, 1)],
                                  hs_hbm.at[pl.ds(pos_ref[0, 0, row], 1)], sem).start()
            return c
        lax.fori_loop(0, burst, issue, 0)
        _row_dma_wait(hx_hbm, hs_hbm, sem, burst)
        return carry

    lax.fori_loop(0, tr // burst, one_burst, 0)


def _moe_scatter(hx, pos3, n_sorted, tr):
    n, w = hx.shape
    nb = n // tr
    return pl.pallas_call(
        functools.partial(_scatter_kernel, tr=tr, burst=min(MOE_DMA_BURST, tr)),
        grid=(nb,),
        in_specs=[pl.BlockSpec((1, 1, tr), lambda i: (i, 0, 0), memory_space=pltpu.SMEM),
                  pl.BlockSpec(memory_space=pl.ANY),
                  pl.BlockSpec(memory_space=pl.ANY)],
        out_specs=pl.BlockSpec(memory_space=pl.ANY),
        out_shape=jax.ShapeDtypeStruct((n_sorted, w), F32),
        scratch_shapes=[pltpu.SemaphoreType.DMA],
        input_output_aliases={2: 0},
        compiler_params=_cparams(("arbitrary",)),
        name="moe_scatter",
    )(pos3, hx, jnp.zeros((n_sorted, w), F32))


def _experts_kernel(gmap_ref, valid_ref, hs_ref, wg_ref, wu_ref, wd_ref, o_ref, cw_ref, acc_ref, *, d):
    i = pl.program_id(0)
    j = pl.program_id(1)
    valid = valid_ref[i] > 0

    @pl.when(j == 0)
    def _():
        logits = hs_ref[:, d:]
        lane = lax.broadcasted_iota(jnp.int32, logits.shape, 1).astype(F32)
        neg = -jnp.inf
        gl = jnp.where(lane < N_GROUPS, logits, neg)
        gmax = jnp.max(gl, axis=-1, keepdims=True)
        gsum = jnp.sum(jnp.exp(gl - gmax), axis=-1, keepdims=True)
        g_val = 1.0 / gsum
        g_idx = _first_index(gl == gmax, lane)
        lo = N_GROUPS + EXPERTS_PER_GROUP * g_idx
        el = jnp.where((lane >= lo) & (lane < lo + EXPERTS_PER_GROUP), logits, neg)
        emax = jnp.max(el, axis=-1, keepdims=True)
        esum = jnp.sum(jnp.exp(el - emax), axis=-1, keepdims=True)
        i1 = _first_index(el == emax, lane)
        el2 = jnp.where(lane == i1, neg, el)
        e2max = jnp.max(el2, axis=-1, keepdims=True)
        i2 = _first_index(el2 == e2max, lane)
        p1 = 1.0 / esum
        p2 = jnp.exp(e2max - emax) / esum
        tot = p1 + p2
        cw_ref[...] = (jnp.where(lane == i1, g_val * (p1 / tot), 0.0)
                       + jnp.where(lane == i2, g_val * (p2 / tot), 0.0))
        acc_ref[...] = jnp.zeros(acc_ref.shape, F32)

    @pl.when(valid)
    def _():
        h = hs_ref[:, :d].astype(BF16)
        a = _dot(h, wg_ref[0])
        hid = (a * jax.nn.sigmoid(a)) * _dot(h, wu_ref[0])
        first = N_GROUPS + EXPERTS_PER_GROUP * gmap_ref[i] + 2 * j
        lane = lax.broadcasted_iota(jnp.int32, cw_ref.shape, 1)
        cw_all = cw_ref[...]
        cw0 = jnp.sum(jnp.where(lane == first, cw_all, 0.0), axis=-1, keepdims=True)
        cw1 = jnp.sum(jnp.where(lane == first + 1, cw_all, 0.0), axis=-1, keepdims=True)
        de = hid.shape[1] // 2
        hcol = lax.broadcasted_iota(jnp.int32, hid.shape, 1)
        hid = hid * jnp.where(hcol < de, cw0, cw1)
        acc_ref[...] += _dot(hid.astype(BF16), wd_ref[0])

    @pl.when(j == pl.num_programs(1) - 1)
    def _():
        o_ref[...] = acc_ref[...]


def _moe_experts(hs, gmap, valid, wg, wu, wd, d, row_block):
    n_sorted, w = hs.shape
    _, _, de2 = wg.shape
    pairs = EXPERTS_PER_GROUP // 2
    grid_spec = pltpu.PrefetchScalarGridSpec(
        num_scalar_prefetch=2,
        grid=(n_sorted // row_block, pairs),
        in_specs=[pl.BlockSpec((row_block, w), lambda i, j, gm, va: (i, 0)),
                  pl.BlockSpec((1, d, de2), lambda i, j, gm, va: (gm[i] * pairs + j, 0, 0)),
                  pl.BlockSpec((1, d, de2), lambda i, j, gm, va: (gm[i] * pairs + j, 0, 0)),
                  pl.BlockSpec((1, de2, d), lambda i, j, gm, va: (gm[i] * pairs + j, 0, 0))],
        out_specs=pl.BlockSpec((row_block, d), lambda i, j, gm, va: (i, 0)),
        scratch_shapes=[pltpu.VMEM((row_block, LANES), F32), pltpu.VMEM((row_block, d), F32)])
    return pl.pallas_call(
        functools.partial(_experts_kernel, d=d),
        grid_spec=grid_spec,
        out_shape=jax.ShapeDtypeStruct((n_sorted, d), F32),
        compiler_params=_cparams(("arbitrary", "arbitrary")),
        name="moe_experts",
    )(gmap, valid, hs, wg, wu, wd)


def _combine_kernel(pos_ref, ys_hbm, x_ref, o_ref, buf_ref, sem, *, tr, burst):
    def one_burst(bi, carry):
        def issue(t, c):
            row = bi * burst + t
            pltpu.make_async_copy(ys_hbm.at[pl.ds(pos_ref[0, 0, row], 1)],
                                  buf_ref.at[pl.ds(row, 1)], sem).start()
            return c
        lax.fori_loop(0, burst, issue, 0)
        _row_dma_wait(ys_hbm, buf_ref, sem, burst)
        return carry

    lax.fori_loop(0, tr // burst, one_burst, 0)
    o_ref[...] = x_ref[...] + buf_ref[...]


def _moe_combine(ys, pos3, x2, tr):
    n, d = x2.shape
    return pl.pallas_call(
        functools.partial(_combine_kernel, tr=tr, burst=min(MOE_DMA_BURST, tr)),
        grid=(n // tr,),
        in_specs=[pl.BlockSpec((1, 1, tr), lambda i: (i, 0, 0), memory_space=pltpu.SMEM),
                  pl.BlockSpec(memory_space=pl.ANY),
                  pl.BlockSpec((tr, d), lambda i: (i, 0))],
        out_specs=pl.BlockSpec((tr, d), lambda i: (i, 0)),
        out_shape=jax.ShapeDtypeStruct((n, d), F32),
        scratch_shapes=[pltpu.VMEM((tr, d), F32), pltpu.SemaphoreType.DMA],
        compiler_params=_cparams(("arbitrary",)),
        name="moe_combine",
    )(pos3, ys, x2)


def _pair_cols(w):
    ne, d, de = w.shape
    return w.astype(BF16).reshape(ne // 2, 2, d, de).transpose(0, 2, 1, 3).reshape(ne // 2, d, 2 * de)


def _pair_rows(w):
    ne, de, d = w.shape
    return w.astype(BF16).reshape(ne // 2, 2 * de, d)


def _moe(hx, x2, wg, wu, wd, tr):
    n, d = x2.shape
    row_block = min(MOE_ROW_BLOCK, n)
    n_blocks = n // row_block + N_GROUPS
    pos, meta = _moe_route(hx, d, tr, row_block)
    counts = meta[0, :N_GROUPS]
    ends = meta[1, :N_GROUPS] + jnp.ceil(counts / row_block) * row_block
    starts = jnp.arange(n_blocks, dtype=F32) * row_block
    grp = jnp.sum((starts[:, None] >= ends[None, :]).astype(jnp.int32), axis=1)
    valid = (grp < N_GROUPS).astype(jnp.int32)
    gmap = jnp.minimum(grp, N_GROUPS - 1)
    pos3 = pos.reshape(n // tr, 1, tr)
    hs = _moe_scatter(hx, pos3, n_blocks * row_block, tr)
    ys = _moe_experts(hs, gmap, valid, wg, wu, wd, d, row_block)
    return _moe_combine(ys, pos3, x2, tr)


def _pick(n, pref):
    t = min(n, pref)
    while n % t:
        t //= 2
    return t


def kernel(x, positions, norm_mix, w_in, da_q_gain, da_k_gain, da_lambda_q1, da_lambda_k1,
           da_lambda_q2, da_lambda_k2, da_subln_gain, hg_lower_bounds, hg_out_gain,
           s5_lambda_re, s5_lambda_im, s5_log_step, s5_b_re, s5_b_im, s5_c_re, s5_c_im,
           s5_d, s5_glu_w, s5_glu_b, w_branch_attn, w_branch_sb, w_branch_hgrn, w_branch_s5,
           w_out, norm_ffn, router_group_w, router_group_b, router_expert_w, router_expert_b,
           expert_w_gate, expert_w_up, expert_w_down):
    bsz, seq, d = x.shape
    depth = w_in.shape[0]
    n = bsz * seq
    assert w_in.shape[2] == IN_COLS and seq % 128 == 0
    tm = _pick(n, 1024)
    tq = _pick(seq, 256)
    tq_da = _pick(seq, DA_TQ)
    tb = _pick(seq, 256)

    x2 = x.reshape(n, d).astype(F32)
    cos_t, sina_t, sinb_t = _rope_tables(positions.reshape(n, 1).astype(jnp.int32), tm)

    lb_all = jnp.cumsum(jax.nn.softmax(hg_lower_bounds.astype(F32), axis=0), axis=0)
    lb_all = lb_all - lb_all[0:1]

    for l in range(depth):
        lambda_init = DA_LAMBDA_INIT_BASE - DA_LAMBDA_INIT_SCALE * math.exp(-DA_LAMBDA_INIT_RATE * l)
        w_f = w_in[l].astype(F32)
        sbq = COL_SB_Q - COL_DA_Q
        w_l = jnp.concatenate([w_f[:, REF_GATE_START:], w_f[:, :sbq],
                               w_f[:, sbq:sbq + SB_HEADS * SB_HEAD_DIM] * (SB_HEAD_DIM ** -0.5 * LOG2E),
                               w_f[:, sbq + SB_HEADS * SB_HEAD_DIM:REF_GATE_START]], axis=1)
        proj = _norm_proj(x2, norm_mix[l].astype(F32)[None, :], w_l.astype(BF16), tm, 768)

        qk_gain = jnp.concatenate([jnp.tile(da_q_gain[l].astype(F32), 2 * DA_HEADS),
                                   jnp.tile(da_k_gain[l].astype(F32), 2 * DA_HEADS)])[None, :]
        qk = _qk_prep(proj, qk_gain, cos_t, sina_t, sinb_t, tm)
        y_a = _diff_attn(qk, proj,
                         da_lambda_q1[l].astype(F32)[None, :], da_lambda_k1[l].astype(F32)[None, :],
                         da_lambda_q2[l].astype(F32)[None, :], da_lambda_k2[l].astype(F32)[None, :],
                         da_subln_gain[l].astype(F32)[None, :], bsz, seq, tq_da, DA_GROUP, lambda_init)
        y_b = _stick_break(proj, bsz, seq, _pick(seq, SB_TQ), SB_KW, SB_GROUP)
        y_c = _hgrn(proj, lb_all[l][None, :], jnp.tile(hg_out_gain[l].astype(F32), HG_HEADS)[None, :],
                    bsz, seq, tb)
        s5p = _s5_params(s5_lambda_re[l], s5_lambda_im[l], s5_log_step[l], s5_b_re[l], s5_b_im[l],
                         s5_c_re[l], s5_c_im[l], tb)
        y_d = _s5(proj, s5p, s5_d[l].astype(F32)[None, :], s5_glu_w[l].astype(BF16),
                  s5_glu_b[l].astype(F32)[None, :], bsz, seq, tb)

        wr = jnp.concatenate([router_group_w[l], router_expert_w[l]], axis=1).astype(F32)
        wr = jnp.pad(wr, ((0, 0), (0, LANES - wr.shape[1])))
        wr_hi = wr.astype(BF16)
        wr_lo = (wr - wr_hi.astype(F32)).astype(BF16)
        br = jnp.concatenate([router_group_b[l], router_expert_b[l]]).astype(F32)
        br = jnp.pad(br, (0, LANES - br.shape[0]))[None, :]
        x2, hx = _merge(x2, y_a, y_b, y_c, y_d, proj,
                        w_branch_attn[l].astype(BF16), w_branch_sb[l].astype(BF16),
                        w_branch_hgrn[l].astype(BF16), w_branch_s5[l].astype(BF16),
                        w_out[l].astype(BF16), norm_ffn[l].astype(F32)[None, :],
                        wr_hi, wr_lo, br, _pick(n, 512))
        x2 = _moe(hx, x2, _pair_cols(expert_w_gate[l]), _pair_cols(expert_w_up[l]),
                  _pair_rows(expert_w_down[l]), tm)

    return x2.reshape(bsz, seq, d).astype(x.dtype)
```

```python
import functools
import math

import jax
import jax.numpy as jnp
import numpy as np
from jax import lax
from jax.experimental import pallas as pl
from jax.experimental.pallas import tpu as pltpu

F32 = jnp.float32
BF16 = jnp.bfloat16

RMS_EPS = 1e-6
LANES = 128
LOG2E = 1.4426950408889634

DA_HEADS = 4
DA_QK_DIM = 64
ROPE_THETA = 500000.0
ROPE_DIM = DA_QK_DIM // 4
DA_LAMBDA_INIT_BASE = 0.8
DA_LAMBDA_INIT_SCALE = 0.6
DA_LAMBDA_INIT_RATE = 0.3

SB_HEADS = 4
SB_HEAD_DIM = 64

HG_HEADS = 4
HG_DIM = 64
HG_COLS = HG_HEADS * HG_DIM
HG_SUB = 16
HG_UNROLL = 4

S5_GROUPS = 16
S5_GROUP_CH = 16
S5_STATE = 64
S5_WIDTH = S5_GROUPS * S5_GROUP_CH
S5_NSTATE = S5_GROUPS * S5_STATE
S5_EIG_CLIP = -1e-4

N_GROUPS = 4
EXPERTS_PER_GROUP = 4
N_EXPERTS = N_GROUPS * EXPERTS_PER_GROUP

REF_GATE_START = 3584
COL_GATE = 0
COL_DA_Q = 4096
COL_DA_K = 4608
COL_DA_V = 5120
COL_SB_Q = 5632
COL_SB_K = 5888
COL_SB_V = 6144
COL_HG_F = 6400
COL_HG_I = 6656
COL_HG_Q = 6912
COL_HG_G = 7168
COL_S5_U = 7424
IN_COLS = 7680

VMEM_LIMIT = 48 * 1024 * 1024

DA_TQ = 1024
DA_GROUP = 1
SB_TQ = 512
SB_KW = 256
SB_GROUP = 4
MOE_ROW_BLOCK = 512
MOE_DMA_BURST = 256

_NT = (((1,), (1,)), ((), ()))
_TN = (((0,), (0,)), ((), ()))


def _cparams(sem):
    return pltpu.CompilerParams(dimension_semantics=sem, vmem_limit_bytes=VMEM_LIMIT)


def _dot(a, b):
    return jnp.dot(a, b, preferred_element_type=F32)


def _norm_proj_kernel(x_ref, g_ref, w_ref, o_ref, h_ref):
    @pl.when(pl.program_id(1) == 0)
    def _():
        x = x_ref[...]
        ms = jnp.mean(x * x, axis=-1, keepdims=True)
        h_ref[...] = (x * lax.rsqrt(ms + RMS_EPS) * g_ref[...]).astype(BF16)

    o_ref[...] = _dot(h_ref[...], w_ref[...]).astype(o_ref.dtype)


def _norm_proj(x2, gain, w, tm, tn):
    n, d = x2.shape
    cols = w.shape[1]
    return pl.pallas_call(
        _norm_proj_kernel,
        grid=(n // tm, cols // tn),
        in_specs=[pl.BlockSpec((tm, d), lambda i, j: (i, 0)),
                  pl.BlockSpec((1, d), lambda i, j: (0, 0)),
                  pl.BlockSpec((d, tn), lambda i, j: (0, j))],
        out_specs=pl.BlockSpec((tm, tn), lambda i, j: (i, j)),
        out_shape=jax.ShapeDtypeStruct((n, cols), BF16),
        scratch_shapes=[pltpu.VMEM((tm, d), BF16)],
        compiler_params=_cparams(("parallel", "arbitrary")),
        name="norm_proj",
    )(x2, gain, w)


def _rope_kernel(pos_ref, invf_ref, sa_ref, sb_ref, cos_ref, sina_ref, sinb_ref):
    ang = pos_ref[...].astype(F32) * invf_ref[...]
    c = jnp.cos(ang)
    s = jnp.sin(ang)
    cos_ref[...] = c
    sina_ref[...] = s * sa_ref[...]
    sinb_ref[...] = s * sb_ref[...]


def _rope_tables(pos_col, tm):
    n = pos_col.shape[0]
    half = ROPE_DIM // 2
    inv_freq = jnp.exp(-math.log(ROPE_THETA) * jnp.arange(half, dtype=F32) * (2.0 / ROPE_DIM))
    lane = np.arange(LANES) % DA_QK_DIM
    invf = jnp.where(lane < ROPE_DIM, inv_freq[lane % half], 0.0).astype(F32)[None, :]
    sgn_a = jnp.asarray(np.where(lane < half, -1.0, 0.0), F32)[None, :]
    sgn_b = jnp.asarray(np.where((lane >= half) & (lane < ROPE_DIM), 1.0, 0.0), F32)[None, :]
    row = pl.BlockSpec((1, LANES), lambda i: (0, 0))
    tab = pl.BlockSpec((tm, LANES), lambda i: (i, 0))
    shp = jax.ShapeDtypeStruct((n, LANES), F32)
    return pl.pallas_call(
        _rope_kernel,
        grid=(n // tm,),
        in_specs=[pl.BlockSpec((tm, 1), lambda i: (i, 0)), row, row, row],
        out_specs=[tab, tab, tab],
        out_shape=[shp, shp, shp],
        compiler_params=_cparams(("parallel",)),
        name="rope_tables",
    )(pos_col, invf, sgn_a, sgn_b)


def _qk_prep_kernel(x_ref, gain_ref, cos_ref, sina_ref, sinb_ref, bd_ref, o_ref):
    c = cos_ref[...]
    sa = sina_ref[...]
    sb = sinb_ref[...]
    bd = bd_ref[...]
    n_tiles = x_ref.shape[1] // LANES
    for j in range(n_tiles):
        sl = slice(j * LANES, (j + 1) * LANES)
        t = x_ref[:, sl].astype(F32)
        ss = _dot((t * t).astype(BF16), bd)
        y = t * lax.rsqrt(ss * (1.0 / DA_QK_DIM) + RMS_EPS) * gain_ref[:, sl]
        y = (y * c + pltpu.roll(y, LANES - ROPE_DIM // 2, 1) * sa
             + pltpu.roll(y, ROPE_DIM // 2, 1) * sb)
        if j < n_tiles // 2:
            y = y * (DA_QK_DIM ** -0.5 * LOG2E)
        o_ref[:, sl] = y.astype(BF16)


def _qk_prep(proj, gain_row, cos_t, sina_t, sinb_t, tm):
    n = proj.shape[0]
    w = 2 * DA_HEADS * 2 * DA_QK_DIM
    seg = np.arange(LANES) // DA_QK_DIM
    bd = jnp.asarray(seg[:, None] == seg[None, :], BF16)
    tab = pl.BlockSpec((tm, LANES), lambda i: (i, 0))
    return pl.pallas_call(
        _qk_prep_kernel,
        grid=(n // tm,),
        in_specs=[pl.BlockSpec((tm, w), lambda i: (i, COL_DA_Q // w)),
                  pl.BlockSpec((1, w), lambda i: (0, 0)),
                  tab, tab, tab,
                  pl.BlockSpec((LANES, LANES), lambda i: (0, 0))],
        out_specs=pl.BlockSpec((tm, w), lambda i: (i, 0)),
        out_shape=jax.ShapeDtypeStruct((n, w), BF16),
        compiler_params=_cparams(("parallel",)),
        name="qk_prep",
    )(proj, gain_row, cos_t, sina_t, sinb_t, bd)


def _diff_attn_kernel(q_ref, k_ref, v_ref, lq1_ref, lk1_ref, lq2_ref, lk2_ref, sg_ref, o_ref,
                      m_ref, l_ref, acc_ref, *, tq, group, lambda_init):
    qi = pl.program_id(2)
    q = q_ref[...]
    lane = lax.broadcasted_iota(jnp.int32, q.shape, 1)
    zero = jnp.zeros_like(q)
    q2 = jnp.concatenate([jnp.where(lane < DA_QK_DIM, q, zero),
                          jnp.where(lane >= DA_QK_DIM, q, zero)], axis=0)
    m_ref[...] = jnp.full(m_ref.shape, -jnp.inf, F32)
    l_ref[...] = jnp.zeros(l_ref.shape, F32)
    acc_ref[...] = jnp.zeros(acc_ref.shape, F32)

    def tile(start, width, masked):
        kb = k_ref[pl.ds(start, width), :]
        vb = v_ref[pl.ds(start, width), :]
        n_lt = width // LANES
        s = lax.dot_general(q2, kb, _NT, preferred_element_type=F32)
        if masked:
            row = lax.broadcasted_iota(jnp.int32, (tq, width), 0)
            col = lax.broadcasted_iota(jnp.int32, (tq, width), 1)
            keep = col <= row
            s = jnp.where(jnp.concatenate([keep, keep], axis=0), s, -jnp.inf)
        st = [s[:, j * LANES:(j + 1) * LANES] for j in range(n_lt)]
        smax = st[0]
        for j in range(1, n_lt):
            smax = jnp.maximum(smax, st[j])
        m_prev = m_ref[...]
        m_new = jnp.maximum(m_prev, jnp.max(smax, axis=-1, keepdims=True))
        alpha = jnp.exp2(m_prev - m_new)
        ps = [jnp.exp2(t - m_new) for t in st]
        lsum = ps[0]
        for j in range(1, n_lt):
            lsum = lsum + ps[j]
        p = jnp.concatenate([t.astype(BF16) for t in ps], axis=1)
        l_ref[...] = alpha * l_ref[...] + lsum
        acc_ref[...] = alpha * acc_ref[...] + _dot(p, vb)
        m_ref[...] = m_new

    n_group = qi // group

    def body_group(gi, carry):
        tile(pl.multiple_of(gi * (group * tq), group * tq), group * tq, False)
        return carry

    def body_single(ki, carry):
        tile(pl.multiple_of(ki * tq, tq), tq, False)
        return carry

    lax.fori_loop(0, n_group, body_group, 0)
    lax.fori_loop(n_group * group, qi, body_single, 0)
    tile(pl.multiple_of(qi * tq, tq), tq, True)

    lam = (jnp.exp(jnp.sum(lq1_ref[...] * lk1_ref[...], axis=-1, keepdims=True))
           - jnp.exp(jnp.sum(lq2_ref[...] * lk2_ref[...], axis=-1, keepdims=True)) + lambda_init)
    o = acc_ref[...] / jnp.sum(l_ref[...], axis=-1, keepdims=True)
    o = o[:tq] - lam * o[tq:]
    ms = jnp.mean(o * o, axis=-1, keepdims=True)
    o = o * lax.rsqrt(ms + RMS_EPS) * sg_ref[...] * (1.0 - lambda_init)
    o_ref[...] = o.astype(o_ref.dtype)


def _diff_attn(qk, proj, lq1, lk1, lq2, lk2, subln, bsz, seq, tq, group, lambda_init):
    n = qk.shape[0]
    nq = seq // tq
    kcol = (DA_HEADS * 2 * DA_QK_DIM) // LANES
    vcol = COL_DA_V // LANES
    vec = pl.BlockSpec((1, DA_QK_DIM), lambda b, h, i: (0, 0))
    return pl.pallas_call(
        functools.partial(_diff_attn_kernel, tq=tq, group=group, lambda_init=lambda_init),
        grid=(bsz, DA_HEADS, nq),
        in_specs=[pl.BlockSpec((tq, LANES), lambda b, h, i: (b * nq + i, h)),
                  pl.BlockSpec((seq, LANES), lambda b, h, i: (b, kcol + h)),
                  pl.BlockSpec((seq, LANES), lambda b, h, i: (b, vcol + h)),
                  vec, vec, vec, vec,
                  pl.BlockSpec((1, LANES), lambda b, h, i: (0, 0))],
        out_specs=pl.BlockSpec((tq, LANES), lambda b, h, i: (b * nq + i, h)),
        out_shape=jax.ShapeDtypeStruct((n, DA_HEADS * LANES), BF16),
        scratch_shapes=[pltpu.VMEM((2 * tq, LANES), F32), pltpu.VMEM((2 * tq, LANES), F32),
                        pltpu.VMEM((2 * tq, LANES), F32)],
        compiler_params=_cparams(("parallel", "parallel", "arbitrary")),
        name="diff_attn",
    )(qk, qk, proj, lq1, lk1, lq2, lk2, subln)


def _stick_break_kernel(q_ref, k_ref, v_ref, o_ref, r_ref, acc_ref, *, tq, kw, group):
    qi = pl.program_id(2)
    q = q_ref[...]
    lane = lax.broadcasted_iota(jnp.int32, q.shape, 1)
    zero = jnp.zeros_like(q)
    q2 = jnp.concatenate([jnp.where(lane < SB_HEAD_DIM, q, zero),
                          jnp.where(lane >= SB_HEAD_DIM, q, zero)], axis=0)
    incl = jnp.where(lax.broadcasted_iota(jnp.int32, (kw, kw), 0)
                     >= lax.broadcasted_iota(jnp.int32, (kw, kw), 1), 1.0, 0.0).astype(BF16)
    n_lt = kw // LANES
    n_diag = tq // kw
    r_ref[...] = jnp.zeros(r_ref.shape, F32)
    acc_ref[...] = jnp.zeros(acc_ref.shape, F32)

    def run(starts, offsets):
        masks, us, sps = [], [], []
        for start, off in zip(starts, offsets):
            u = lax.dot_general(q2, k_ref[pl.ds(start, kw), :], _NT, preferred_element_type=F32)
            neg_abs = lax.bitcast_convert_type(
                lax.bitcast_convert_type(u, jnp.int32) | jnp.int32(-2 ** 31), F32)
            sp = jnp.maximum(u, 0.0) + jnp.log(1.0 + jnp.exp2(neg_abs)) * LOG2E
            mask = None
            if off is not None:
                keep = (lax.broadcasted_iota(jnp.int32, (tq, kw), 1) + off
                        < lax.broadcasted_iota(jnp.int32, (tq, kw), 0))
                mask = jnp.concatenate([keep, keep], axis=0)
                sp = jnp.where(mask, sp, 0.0)
            masks.append(mask)
            us.append(u)
            sps.append(sp.astype(BF16))
        cum_all = _dot(jnp.concatenate(sps, axis=0), incl)
        r = r_ref[...]
        total = None
        for i, start in enumerate(starts):
            cum = cum_all[i * 2 * tq:(i + 1) * 2 * tq]
            w = jnp.exp2(us[i] - (cum + jnp.concatenate([r] * n_lt, axis=1)))
            if masks[i] is not None:
                w = jnp.where(masks[i], w, 0.0)
            part = _dot(w.astype(BF16), v_ref[pl.ds(start, kw), :])
            total = part if total is None else total + part
            r = r + cum[:, 0:1]
        acc_ref[...] += total
        r_ref[...] = r

    base = qi * tq
    run([pl.multiple_of(base + (n_diag - 1 - j) * kw, kw) for j in range(n_diag)],
        [(n_diag - 1 - j) * kw for j in range(n_diag)])

    n_below = qi * n_diag
    rem = n_below % group

    def body_single(j, carry):
        run([pl.multiple_of((n_below - 1 - j) * kw, kw)], [None])
        return carry

    def body_group(gi, carry):
        top = n_below - rem - gi * group
        run([pl.multiple_of((top - 1 - j) * kw, kw) for j in range(group)], [None] * group)
        return carry

    lax.fori_loop(0, rem, body_single, 0)
    lax.fori_loop(0, n_below // group, body_group, 0)
    lane_o = lax.broadcasted_iota(jnp.int32, (tq, LANES), 1)
    o_ref[...] = jnp.where(lane_o < SB_HEAD_DIM, acc_ref[:tq, :], acc_ref[tq:, :]).astype(o_ref.dtype)


def _stick_break(proj, bsz, seq, tq, kw, group):
    n = proj.shape[0]
    nq = seq // tq
    pairs = (SB_HEADS * SB_HEAD_DIM) // LANES
    qc, kc, vc = COL_SB_Q // LANES, COL_SB_K // LANES, COL_SB_V // LANES
    return pl.pallas_call(
        functools.partial(_stick_break_kernel, tq=tq, kw=kw, group=group),
        grid=(bsz, pairs, nq),
        in_specs=[pl.BlockSpec((tq, LANES), lambda b, p, i: (b * nq + i, qc + p)),
                  pl.BlockSpec((seq, LANES), lambda b, p, i: (b, kc + p)),
                  pl.BlockSpec((seq, LANES), lambda b, p, i: (b, vc + p))],
        out_specs=pl.BlockSpec((tq, LANES), lambda b, p, i: (b * nq + i, p)),
        out_shape=jax.ShapeDtypeStruct((n, pairs * LANES), BF16),
        scratch_shapes=[pltpu.VMEM((2 * tq, LANES), F32), pltpu.VMEM((2 * tq, LANES), F32)],
        compiler_params=_cparams(("parallel", "parallel", "arbitrary")),
        name="stick_break",
    )(proj, proj, proj)


def _split_bf16(x):
    hi = x.astype(BF16)
    lo = (x - hi.astype(F32)).astype(BF16)
    return hi, lo


def _hgrn_kernel(f_ref, i_ref, q_ref, g_ref, lb_ref, gain_ref, o_ref,
                 st_ref, b_ref, dec_ref, qq_ref, kk_ref, vv_ref, oo_ref, qe_ref, ke_ref, gm_ref, uu_ref,
                 *, tb):
    @pl.when(pl.program_id(1) == 0)
    def _():
        st_ref[...] = jnp.zeros(st_ref.shape, F32)

    nsub = tb // HG_SUB
    z = f_ref[...].astype(F32)
    lb = lb_ref[...]
    sp = jnp.maximum(z, 0.0) + jnp.log(1.0 + jnp.exp(-jnp.abs(z)))
    log_sig = z - sp
    a = jnp.log(lb)
    c = jnp.log(1.0 - lb) + log_sig
    mx = jnp.maximum(a, c)
    log_f = mx + jnp.log(jnp.exp(a - mx) + jnp.exp(c - mx))
    key = (1.0 - lb) * jax.nn.sigmoid(-z)

    row = lax.broadcasted_iota(jnp.int32, (tb, tb), 0)
    col = lax.broadcasted_iota(jnp.int32, (tb, tb), 1)
    sub_shift = HG_SUB.bit_length() - 1
    dim_shift = HG_DIM.bit_length() - 1
    same = (row >> sub_shift) == (col >> sub_shift)
    tri = jnp.where(same & (col <= row), 1.0, 0.0).astype(BF16)
    blk = jnp.where(same, 1.0, 0.0).astype(BF16)
    hi, lo = _split_bf16(log_f)
    b = _dot(tri, hi) + _dot(tri, lo)
    e = _dot(blk, hi) + _dot(blk, lo)
    qf = q_ref[...].astype(F32)
    b_ref[...] = b
    qq_ref[...] = qf
    kk_ref[...] = key
    vv_ref[...] = i_ref[...].astype(F32)
    qe_ref[...] = (qf * jnp.exp(b)).astype(BF16)
    ke_ref[...] = (key * jnp.exp(e - b)).astype(BF16)
    dec_ref[...] = jnp.exp(e)

    seg_r = lax.broadcasted_iota(jnp.int32, (HG_COLS, HG_COLS), 0) >> dim_shift
    seg_c = lax.broadcasted_iota(jnp.int32, (HG_COLS, HG_COLS), 1) >> dim_shift
    head_mask = seg_r == seg_c
    ones_bd = jnp.where(head_mask, 1.0, 0.0).astype(BF16)
    sel = jnp.where(lax.broadcasted_iota(jnp.int32, (HG_SUB, HG_SUB * HG_SUB), 0)
                    == (lax.broadcasted_iota(jnp.int32, (HG_SUB, HG_SUB * HG_SUB), 1) >> sub_shift),
                    1.0, 0.0).astype(BF16)
    srow = lax.broadcasted_iota(jnp.int32, (HG_SUB, HG_COLS), 0)

    def intra(ci, slot):
        r0 = pl.multiple_of(ci * HG_SUB, HG_SUB)
        bi = b_ref[pl.ds(r0, HG_SUB), :]
        qi = qq_ref[pl.ds(r0, HG_SUB), :]
        ki = kk_ref[pl.ds(r0, HG_SUB), :]
        vi = vv_ref[pl.ds(r0, HG_SUB), :]
        for t in range(HG_SUB):
            d = jnp.exp(jnp.minimum(bi[t:t + 1, :] - bi, 0.0))
            g = jnp.where(srow <= t, qi[t:t + 1, :] * ki * d, 0.0)
            gm_ref[slot, t * HG_SUB:(t + 1) * HG_SUB, :] = g.astype(BF16)
        sc = _dot(gm_ref[slot], ones_bd)
        vt = jnp.concatenate([vi] * HG_SUB, axis=0)
        oo_ref[pl.ds(r0, HG_SUB), :] = _dot(sel, (sc * vt).astype(BF16))
        upd = lax.dot_general(vi.astype(BF16), ke_ref[pl.ds(r0, HG_SUB), :], _TN,
                              preferred_element_type=F32)
        uu_ref[ci] = jnp.where(head_mask, upd, 0.0)

    def intra_group(gi, carry):
        for slot in range(HG_UNROLL):
            intra(gi * HG_UNROLL + slot, slot)
        return carry

    lax.fori_loop(0, nsub // HG_UNROLL, intra_group, 0)

    st = st_ref[...]
    for ci in range(nsub):
        rows = slice(ci * HG_SUB, (ci + 1) * HG_SUB)
        oo_ref[rows, :] += lax.dot_general(qe_ref[rows, :], st.astype(BF16), _NT,
                                           preferred_element_type=F32)
        st = st * dec_ref[ci * HG_SUB:ci * HG_SUB + 1, :] + uu_ref[ci]
    st_ref[...] = st

    o = oo_ref[...]
    ms = _dot((o * o).astype(BF16), ones_bd) * (1.0 / HG_DIM)
    gate = g_ref[...].astype(F32)
    gate = gate * jax.nn.sigmoid(gate)
    o_ref[...] = (o * lax.rsqrt(ms + RMS_EPS) * gain_ref[...] * gate).astype(o_ref.dtype)


def _hgrn(proj, lb_row, gain_row, bsz, seq, tb):
    n = proj.shape[0]
    nb = seq // tb
    w = HG_COLS
    cf, ci, cq, cg = COL_HG_F // w, COL_HG_I // w, COL_HG_Q // w, COL_HG_G // w

    def col(cc):
        return pl.BlockSpec((tb, w), lambda b, i: (b * nb + i, cc))

    rowspec = pl.BlockSpec((1, w), lambda b, i: (0, 0))
    big = pltpu.VMEM((tb, w), F32)
    return pl.pallas_call(
        functools.partial(_hgrn_kernel, tb=tb),
        grid=(bsz, nb),
        in_specs=[col(cf), col(ci), col(cq), col(cg), rowspec, rowspec],
        out_specs=pl.BlockSpec((tb, w), lambda b, i: (b * nb + i, 0)),
        out_shape=jax.ShapeDtypeStruct((n, w), BF16),
        scratch_shapes=[pltpu.VMEM((w, w), F32), big, big, big, big, big, big,
                        pltpu.VMEM((tb, w), BF16), pltpu.VMEM((tb, w), BF16),
                        pltpu.VMEM((HG_UNROLL, HG_SUB * HG_SUB, w), BF16),
                        pltpu.VMEM((tb // HG_SUB, w, w), F32)],
        compiler_params=_cparams(("parallel", "arbitrary")),
        name="hgrn2",
    )(proj, proj, proj, proj, lb_row, gain_row)


def _s5_kernel(u_ref, bblk_ref, lev_re_ref, lev_im_ref, pw_re_ref, pw_im_ref, cblk_ref, d_ref,
               gw_ref, gb_ref, o_ref, cr_ref, ci_ref, *, tb):
    @pl.when(pl.program_id(1) == 0)
    def _():
        cr_ref[...] = jnp.zeros(cr_ref.shape, F32)
        ci_ref[...] = jnp.zeros(ci_ref.shape, F32)

    u = u_ref[...]
    bu = _dot(u, bblk_ref[...])
    xr = bu[:, :S5_NSTATE]
    xi = bu[:, S5_NSTATE:]
    row = lax.broadcasted_iota(jnp.int32, xr.shape, 0)
    nlev = lev_re_ref.shape[0]
    for j in range(nlev):
        d = 1 << j
        ar = lev_re_ref[j:j + 1, :]
        ai = lev_im_ref[j:j + 1, :]
        keep = row >= d
        sr = jnp.where(keep, pltpu.roll(xr, d, 0), 0.0)
        si = jnp.where(keep, pltpu.roll(xi, d, 0), 0.0)
        xr, xi = xr + ar * sr - ai * si, xi + ar * si + ai * sr
    cr = cr_ref[...]
    ci = ci_ref[...]
    pr = pw_re_ref[...]
    pi = pw_im_ref[...]
    xr, xi = xr + pr * cr - pi * ci, xi + pr * ci + pi * cr
    cr_ref[...] = xr[tb - 1:tb, :]
    ci_ref[...] = xi[tb - 1:tb, :]
    y = (_dot(xr.astype(BF16), cblk_ref[:S5_NSTATE, :]) + _dot(xi.astype(BF16), cblk_ref[S5_NSTATE:, :])
         + d_ref[...] * u.astype(F32))
    y = jax.nn.gelu(y)
    zg = _dot(y.astype(BF16), gw_ref[...]) + gb_ref[...]
    o_ref[...] = (y * jax.nn.sigmoid(zg)).astype(o_ref.dtype)


def _s5_params(lam_re, lam_im, log_step, b_re, b_im, c_re, c_im, tb):
    lam_re = jnp.minimum(lam_re.astype(F32), S5_EIG_CLIP)
    lam_im = lam_im.astype(F32)
    step = jnp.exp(log_step.astype(F32))[:, None]
    mag = jnp.exp(lam_re * step)
    phase = lam_im * step
    a_re = mag * jnp.cos(phase)
    a_im = mag * jnp.sin(phase)
    denom = lam_re * lam_re + lam_im * lam_im
    num_re = a_re - 1.0
    gam_re = (num_re * lam_re + a_im * lam_im) / denom
    gam_im = (a_im * lam_re - num_re * lam_im) / denom
    b_re = b_re.astype(F32)
    b_im = b_im.astype(F32)
    bb_re = gam_re[..., None] * b_re - gam_im[..., None] * b_im
    bb_im = gam_re[..., None] * b_im + gam_im[..., None] * b_re
    eye = jnp.eye(S5_GROUPS, dtype=F32)

    def in_blk(bb):
        return jnp.einsum('gnc,gh->gchn', bb, eye).reshape(S5_WIDTH, S5_NSTATE)

    def out_blk(cc):
        return jnp.einsum('gcn,gh->gnhc', cc.astype(F32), eye).reshape(S5_NSTATE, S5_WIDTH)

    bblk = jnp.concatenate([in_blk(bb_re), in_blk(bb_im)], axis=1).astype(BF16)
    cblk = jnp.concatenate([out_blk(c_re), -out_blk(c_im)], axis=0).astype(BF16)
    ar = a_re.reshape(1, S5_NSTATE)
    ai = a_im.reshape(1, S5_NSTATE)
    nlev = int(math.log2(tb))
    lev_re, lev_im = [ar], [ai]
    pw_re, pw_im = ar, ai
    for _ in range(nlev):
        sr, si = lev_re[-1], lev_im[-1]
        pw_re, pw_im = (jnp.concatenate([pw_re, pw_re * sr - pw_im * si], axis=0),
                        jnp.concatenate([pw_im, pw_re * si + pw_im * sr], axis=0))
        lev_re.append(sr * sr - si * si)
        lev_im.append(2.0 * sr * si)
    lev_re = jnp.concatenate(lev_re[:nlev], axis=0)
    lev_im = jnp.concatenate(lev_im[:nlev], axis=0)
    return bblk, cblk, lev_re, lev_im, pw_re, pw_im


def _s5(proj, params, d_row, glu_w, glu_b, bsz, seq, tb):
    n = proj.shape[0]
    nb = seq // tb
    bblk, cblk, lev_re, lev_im, pw_re, pw_im = params
    nlev = lev_re.shape[0]
    ucol = COL_S5_U // S5_WIDTH

    def full(shape):
        return pl.BlockSpec(shape, lambda b, i: (0,) * len(shape))

    return pl.pallas_call(
        functools.partial(_s5_kernel, tb=tb),
        grid=(bsz, nb),
        in_specs=[pl.BlockSpec((tb, S5_WIDTH), lambda b, i: (b * nb + i, ucol)),
                  full((S5_WIDTH, 2 * S5_NSTATE)),
                  full((nlev, S5_NSTATE)), full((nlev, S5_NSTATE)),
                  full((tb, S5_NSTATE)), full((tb, S5_NSTATE)),
                  full((2 * S5_NSTATE, S5_WIDTH)),
                  full((1, S5_WIDTH)), full((S5_WIDTH, S5_WIDTH)), full((1, S5_WIDTH))],
        out_specs=pl.BlockSpec((tb, S5_WIDTH), lambda b, i: (b * nb + i, 0)),
        out_shape=jax.ShapeDtypeStruct((n, S5_WIDTH), BF16),
        scratch_shapes=[pltpu.VMEM((1, S5_NSTATE), F32), pltpu.VMEM((1, S5_NSTATE), F32)],
        compiler_params=_cparams(("parallel", "arbitrary")),
        name="s5",
    )(proj, bblk, lev_re, lev_im, pw_re, pw_im, cblk, d_row, glu_w, glu_b)


def _merge_kernel(x_ref, ya_ref, yb_ref, yc_ref, yd_ref, gate_ref, wa_ref, wb_ref, wc_ref, wd_ref,
                  wo_ref, gn_ref, wrh_ref, wrl_ref, br_ref, xo_ref, hx_ref):
    d = x_ref.shape[1]
    merged = None
    for i, (y_ref, w_ref) in enumerate(((ya_ref, wa_ref), (yb_ref, wb_ref),
                                        (yc_ref, wc_ref), (yd_ref, wd_ref))):
        gate = jax.nn.sigmoid(gate_ref[:, i * d:(i + 1) * d].astype(F32))
        term = gate * _dot(y_ref[...], w_ref[...])
        merged = term if merged is None else merged + term
    xn = x_ref[...] + _dot(merged.astype(BF16), wo_ref[...])
    xo_ref[...] = xn
    ms = jnp.mean(xn * xn, axis=-1, keepdims=True)
    h = xn * lax.rsqrt(ms + RMS_EPS) * gn_ref[...]
    hi, lo = _split_bf16(h)
    hx_ref[:, :d] = hi.astype(F32)
    wrh = wrh_ref[...]
    hx_ref[:, d:] = _dot(hi, wrh) + _dot(lo, wrh) + _dot(hi, wrl_ref[...]) + br_ref[...]


def _merge(x2, ya, yb, yc, yd, proj, wa, wb, wc, wd, wo, gn_row, wr_hi, wr_lo, br_row, tm):
    n, d = x2.shape
    assert COL_GATE % (4 * d) == 0
    gcol = COL_GATE // (4 * d)

    def rows(width, cc=0):
        return pl.BlockSpec((tm, width), lambda i: (i, cc))

    def full(arr):
        return pl.BlockSpec(arr.shape, lambda i: (0, 0))

    return pl.pallas_call(
        _merge_kernel,
        grid=(n // tm,),
        in_specs=[rows(d), rows(ya.shape[1]), rows(yb.shape[1]), rows(yc.shape[1]), rows(yd.shape[1]),
                  rows(4 * d, gcol),
                  full(wa), full(wb), full(wc), full(wd), full(wo), full(gn_row),
                  full(wr_hi), full(wr_lo), full(br_row)],
        out_specs=[rows(d), rows(d + LANES)],
        out_shape=[jax.ShapeDtypeStruct((n, d), F32), jax.ShapeDtypeStruct((n, d + LANES), F32)],
        compiler_params=_cparams(("parallel",)),
        name="merge",
    )(x2, ya, yb, yc, yd, proj, wa, wb, wc, wd, wo, gn_row, wr_hi, wr_lo, br_row)


def _first_index(mask, lane):
    return jnp.min(jnp.where(mask, lane, float(LANES)), axis=-1, keepdims=True)


def _group_onehot(logits):
    lane = lax.broadcasted_iota(jnp.int32, logits.shape, 1).astype(F32)
    gl = jnp.where(lane < N_GROUPS, logits, -jnp.inf)
    g_idx = _first_index(gl == jnp.max(gl, axis=-1, keepdims=True), lane)
    return jnp.where(lane == g_idx, 1.0, 0.0)


def _route_kernel(lg_ref, tri_ref, pos_ref, meta_ref, cnt_ref, run_ref, off_ref, *, row_block):
    phase = pl.program_id(0)
    i = pl.program_id(1)
    onehot = _group_onehot(lg_ref[...])
    lane = lax.broadcasted_iota(jnp.int32, (1, LANES), 1)

    @pl.when((phase == 0) & (i == 0))
    def _():
        cnt_ref[...] = jnp.zeros(cnt_ref.shape, F32)

    @pl.when(phase == 0)
    def _():
        cnt_ref[...] += jnp.sum(onehot, axis=0, keepdims=True)
        pos_ref[...] = jnp.zeros(pos_ref.shape, jnp.int32)

    @pl.when((phase == 1) & (i == 0))
    def _():
        padded = jnp.floor((cnt_ref[...] + (row_block - 1)) * (1.0 / row_block)) * row_block
        off = jnp.zeros((1, LANES), F32)
        acc = jnp.zeros((1, 1), F32)
        for g in range(1, N_GROUPS):
            acc = acc + jnp.sum(jnp.where(lane == g - 1, padded, 0.0), axis=-1, keepdims=True)
            off = off + jnp.where(lane == g, acc, 0.0)
        off_ref[...] = off
        run_ref[...] = jnp.zeros(run_ref.shape, F32)
        meta_ref[...] = jnp.zeros(meta_ref.shape, F32)
        meta_ref[0:1, :] = cnt_ref[...]
        meta_ref[1:2, :] = off

    @pl.when(phase == 1)
    def _():
        before = _dot(tri_ref[...], onehot.astype(BF16))
        slot = jnp.sum(onehot * (before + run_ref[...] + off_ref[...]), axis=-1, keepdims=True)
        pos_ref[...] = slot.astype(jnp.int32)
        run_ref[...] += jnp.sum(onehot, axis=0, keepdims=True)


def _moe_route(hx, d, tr, row_block):
    n = hx.shape[0]
    nb = n // tr
    tri = jnp.asarray(np.tril(np.ones((tr, tr), np.float32), -1), BF16)
    pos, meta = pl.pallas_call(
        functools.partial(_route_kernel, row_block=row_block),
        grid=(2, nb),
        in_specs=[pl.BlockSpec((tr, LANES), lambda p, i: (i, d // LANES)),
                  pl.BlockSpec((tr, tr), lambda p, i: (0, 0))],
        out_specs=[pl.BlockSpec((tr, 1), lambda p, i: (p * nb + i, 0)),
                   pl.BlockSpec((8, LANES), lambda p, i: (0, 0))],
        out_shape=[jax.ShapeDtypeStruct((2 * n, 1), jnp.int32), jax.ShapeDtypeStruct((8, LANES), F32)],
        scratch_shapes=[pltpu.VMEM((1, LANES), F32)] * 3,
        compiler_params=_cparams(("arbitrary", "arbitrary")),
        name="moe_route",
    )(hx, tri)
    return pos[n:], meta


def _row_dma_wait(src_hbm, dst_ref, sem, rows):
    pltpu.make_async_copy(src_hbm.at[pl.ds(0, rows)], dst_ref.at[pl.ds(0, rows)], sem).wait()


def _scatter_kernel(pos_ref, hx_ref, hs_in_hbm, hs_hbm, sem, *, tr, burst):
    del hs_in_hbm

    def one_burst(bi, carry):
        def issue(t, c):
            row = bi * burst + t
            pltpu.make_async_copy(hx_ref.at[pl.ds(row, 1)],
                                  hs_hbm.at[pl.ds(pos_ref[0, 0, row], 1)], sem).start()
            return c
        lax.fori_loop(0, burst, issue, 0, unroll=8)
        _row_dma_wait(hx_ref, hs_hbm, sem, burst)
        return carry

    lax.fori_loop(0, tr // burst, one_burst, 0)


def _moe_scatter(hx, pos3, n_sorted, tr):
    n, w = hx.shape
    nb = n // tr
    return pl.pallas_call(
        functools.partial(_scatter_kernel, tr=tr, burst=min(MOE_DMA_BURST, tr)),
        grid=(nb,),
        in_specs=[pl.BlockSpec((1, 1, tr), lambda i: (i, 0, 0), memory_space=pltpu.SMEM),
                  pl.BlockSpec((tr, w), lambda i: (i, 0)),
                  pl.BlockSpec(memory_space=pl.ANY)],
        out_specs=pl.BlockSpec(memory_space=pl.ANY),
        out_shape=jax.ShapeDtypeStruct((n_sorted, w), F32),
        scratch_shapes=[pltpu.SemaphoreType.DMA],
        input_output_aliases={2: 0},
        compiler_params=_cparams(("arbitrary",)),
        name="moe_scatter",
    )(pos3, hx, jnp.zeros((n_sorted, w), F32))


def _experts_kernel(gmap_ref, valid_ref, hs_ref, wg_ref, wu_ref, wd_ref, o_ref, cw_ref, acc_ref, *, d):
    i = pl.program_id(0)
    j = pl.program_id(1)
    valid = valid_ref[i] > 0

    @pl.when(j == 0)
    def _():
        logits = hs_ref[:, d:]
        lane = lax.broadcasted_iota(jnp.int32, logits.shape, 1).astype(F32)
        neg = -jnp.inf
        gl = jnp.where(lane < N_GROUPS, logits, neg)
        gmax = jnp.max(gl, axis=-1, keepdims=True)
        gsum = jnp.sum(jnp.exp(gl - gmax), axis=-1, keepdims=True)
        g_val = 1.0 / gsum
        g_idx = _first_index(gl == gmax, lane)
        lo = N_GROUPS + EXPERTS_PER_GROUP * g_idx
        el = jnp.where((lane >= lo) & (lane < lo + EXPERTS_PER_GROUP), logits, neg)
        emax = jnp.max(el, axis=-1, keepdims=True)
        esum = jnp.sum(jnp.exp(el - emax), axis=-1, keepdims=True)
        i1 = _first_index(el == emax, lane)
        el2 = jnp.where(lane == i1, neg, el)
        e2max = jnp.max(el2, axis=-1, keepdims=True)
        i2 = _first_index(el2 == e2max, lane)
        p1 = 1.0 / esum
        p2 = jnp.exp(e2max - emax) / esum
        tot = p1 + p2
        cw_ref[...] = (jnp.where(lane == i1, g_val * (p1 / tot), 0.0)
                       + jnp.where(lane == i2, g_val * (p2 / tot), 0.0))
        acc_ref[...] = jnp.zeros(acc_ref.shape, F32)

    @pl.when(valid)
    def _():
        h = hs_ref[:, :d].astype(BF16)
        a = _dot(h, wg_ref[0])
        hid = (a * jax.nn.sigmoid(a)) * _dot(h, wu_ref[0])
        first = N_GROUPS + EXPERTS_PER_GROUP * gmap_ref[i] + 2 * j
        lane = lax.broadcasted_iota(jnp.int32, cw_ref.shape, 1)
        cw_all = cw_ref[...]
        cw0 = jnp.sum(jnp.where(lane == first, cw_all, 0.0), axis=-1, keepdims=True)
        cw1 = jnp.sum(jnp.where(lane == first + 1, cw_all, 0.0), axis=-1, keepdims=True)
        de = hid.shape[1] // 2
        hcol = lax.broadcasted_iota(jnp.int32, hid.shape, 1)
        hid = hid * jnp.where(hcol < de, cw0, cw1)
        acc_ref[...] += _dot(hid.astype(BF16), wd_ref[0])

    @pl.when(j == pl.num_programs(1) - 1)
    def _():
        o_ref[...] = acc_ref[...]


def _moe_experts(hs, gmap, valid, wg, wu, wd, d, row_block):
    n_sorted, w = hs.shape
    _, _, de2 = wg.shape
    pairs = EXPERTS_PER_GROUP // 2
    grid_spec = pltpu.PrefetchScalarGridSpec(
        num_scalar_prefetch=2,
        grid=(n_sorted // row_block, pairs),
        in_specs=[pl.BlockSpec((row_block, w), lambda i, j, gm, va: (i, 0)),
                  pl.BlockSpec((1, d, de2), lambda i, j, gm, va: (gm[i] * pairs + j, 0, 0)),
                  pl.BlockSpec((1, d, de2), lambda i, j, gm, va: (gm[i] * pairs + j, 0, 0)),
                  pl.BlockSpec((1, de2, d), lambda i, j, gm, va: (gm[i] * pairs + j, 0, 0))],
        out_specs=pl.BlockSpec((row_block, d), lambda i, j, gm, va: (i, 0)),
        scratch_shapes=[pltpu.VMEM((row_block, LANES), F32), pltpu.VMEM((row_block, d), F32)])
    return pl.pallas_call(
        functools.partial(_experts_kernel, d=d),
        grid_spec=grid_spec,
        out_shape=jax.ShapeDtypeStruct((n_sorted, d), F32),
        compiler_params=_cparams(("arbitrary", "arbitrary")),
        name="moe_experts",
    )(gmap, valid, hs, wg, wu, wd)


def _combine_kernel(pos_ref, ys_hbm, x_ref, o_ref, buf_ref, sem, *, tr, burst):
    def one_burst(bi, carry):
        def issue(t, c):
            row = bi * burst + t
            pltpu.make_async_copy(ys_hbm.at[pl.ds(pos_ref[0, 0, row], 1)],
                                  buf_ref.at[pl.ds(row, 1)], sem).start()
            return c
        lax.fori_loop(0, burst, issue, 0, unroll=8)
        _row_dma_wait(ys_hbm, buf_ref, sem, burst)
        return carry

    lax.fori_loop(0, tr // burst, one_burst, 0)
    o_ref[...] = x_ref[...] + buf_ref[...]


def _moe_combine(ys, pos3, x2, tr):
    n, d = x2.shape
    return pl.pallas_call(
        functools.partial(_combine_kernel, tr=tr, burst=min(MOE_DMA_BURST, tr)),
        grid=(n // tr,),
        in_specs=[pl.BlockSpec((1, 1, tr), lambda i: (i, 0, 0), memory_space=pltpu.SMEM),
                  pl.BlockSpec(memory_space=pl.ANY),
                  pl.BlockSpec((tr, d), lambda i: (i, 0))],
        out_specs=pl.BlockSpec((tr, d), lambda i: (i, 0)),
        out_shape=jax.ShapeDtypeStruct((n, d), F32),
        scratch_shapes=[pltpu.VMEM((tr, d), F32), pltpu.SemaphoreType.DMA],
        compiler_params=_cparams(("arbitrary",)),
        name="moe_combine",
    )(pos3, ys, x2)


def _pair_cols(w):
    ne, d, de = w.shape
    return w.astype(BF16).reshape(ne // 2, 2, d, de).transpose(0, 2, 1, 3).reshape(ne // 2, d, 2 * de)


def _pair_rows(w):
    ne, de, d = w.shape
    return w.astype(BF16).reshape(ne // 2, 2 * de, d)


def _moe(hx, x2, wg, wu, wd, tr):
    n, d = x2.shape
    row_block = min(MOE_ROW_BLOCK, n)
    n_blocks = n // row_block + N_GROUPS
    pos, meta = _moe_route(hx, d, tr, row_block)
    counts = meta[0, :N_GROUPS]
    ends = meta[1, :N_GROUPS] + jnp.ceil(counts / row_block) * row_block
    starts = jnp.arange(n_blocks, dtype=F32) * row_block
    grp = jnp.sum((starts[:, None] >= ends[None, :]).astype(jnp.int32), axis=1)
    valid = (grp < N_GROUPS).astype(jnp.int32)
    gmap = jnp.minimum(grp, N_GROUPS - 1)
    pos3 = pos.reshape(n // tr, 1, tr)
    hs = _moe_scatter(hx, pos3, n_blocks * row_block, tr)
    ys = _moe_experts(hs, gmap, valid, wg, wu, wd, d, row_block)
    return _moe_combine(ys, pos3, x2, tr)


def _pick(n, pref):
    t = min(n, pref)
    while n % t:
        t //= 2
    return t


def kernel(x, positions, norm_mix, w_in, da_q_gain, da_k_gain, da_lambda_q1, da_lambda_k1,
           da_lambda_q2, da_lambda_k2, da_subln_gain, hg_lower_bounds, hg_out_gain,
           s5_lambda_re, s5_lambda_im, s5_log_step, s5_b_re, s5_b_im, s5_c_re, s5_c_im,
           s5_d, s5_glu_w, s5_glu_b, w_branch_attn, w_branch_sb, w_branch_hgrn, w_branch_s5,
           w_out, norm_ffn, router_group_w, router_group_b, router_expert_w, router_expert_b,
           expert_w_gate, expert_w_up, expert_w_down):
    bsz, seq, d = x.shape
    depth = w_in.shape[0]
    n = bsz * seq
    assert w_in.shape[2] == IN_COLS and seq % 128 == 0
    tm = _pick(n, 1024)
    tq = _pick(seq, 256)
    tq_da = _pick(seq, DA_TQ)
    tb = _pick(seq, 256)

    x2 = x.reshape(n, d).astype(F32)
    cos_t, sina_t, sinb_t = _rope_tables(positions.reshape(n, 1).astype(jnp.int32), tm)

    lb_all = jnp.cumsum(jax.nn.softmax(hg_lower_bounds.astype(F32), axis=0), axis=0)
    lb_all = lb_all - lb_all[0:1]

    for l in range(depth):
        lambda_init = DA_LAMBDA_INIT_BASE - DA_LAMBDA_INIT_SCALE * math.exp(-DA_LAMBDA_INIT_RATE * l)
        w_f = w_in[l].astype(F32)
        sbq = COL_SB_Q - COL_DA_Q
        w_l = jnp.concatenate([w_f[:, REF_GATE_START:], w_f[:, :sbq],
                               w_f[:, sbq:sbq + SB_HEADS * SB_HEAD_DIM] * (SB_HEAD_DIM ** -0.5 * LOG2E),
                               w_f[:, sbq + SB_HEADS * SB_HEAD_DIM:REF_GATE_START]], axis=1)
        proj = _norm_proj(x2, norm_mix[l].astype(F32)[None, :], w_l.astype(BF16), tm, 768)

        qk_gain = jnp.concatenate([jnp.tile(da_q_gain[l].astype(F32), 2 * DA_HEADS),
                                   jnp.tile(da_k_gain[l].astype(F32), 2 * DA_HEADS)])[None, :]
        qk = _qk_prep(proj, qk_gain, cos_t, sina_t, sinb_t, tm)
        y_a = _diff_attn(qk, proj,
                         da_lambda_q1[l].astype(F32)[None, :], da_lambda_k1[l].astype(F32)[None, :],
                         da_lambda_q2[l].astype(F32)[None, :], da_lambda_k2[l].astype(F32)[None, :],
                         da_subln_gain[l].astype(F32)[None, :], bsz, seq, tq_da, DA_GROUP, lambda_init)
        y_b = _stick_break(proj, bsz, seq, _pick(seq, SB_TQ), SB_KW, SB_GROUP)
        y_c = _hgrn(proj, lb_all[l][None, :], jnp.tile(hg_out_gain[l].astype(F32), HG_HEADS)[None, :],
                    bsz, seq, tb)
        s5p = _s5_params(s5_lambda_re[l], s5_lambda_im[l], s5_log_step[l], s5_b_re[l], s5_b_im[l],
                         s5_c_re[l], s5_c_im[l], tb)
        y_d = _s5(proj, s5p, s5_d[l].astype(F32)[None, :], s5_glu_w[l].astype(BF16),
                  s5_glu_b[l].astype(F32)[None, :], bsz, seq, tb)

        wr = jnp.concatenate([router_group_w[l], router_expert_w[l]], axis=1).astype(F32)
        wr = jnp.pad(wr, ((0, 0), (0, LANES - wr.shape[1])))
        wr_hi = wr.astype(BF16)
        wr_lo = (wr - wr_hi.astype(F32)).astype(BF16)
        br = jnp.concatenate([router_group_b[l], router_expert_b[l]]).astype(F32)
        br = jnp.pad(br, (0, LANES - br.shape[0]))[None, :]
        x2, hx = _merge(x2, y_a, y_b, y_c, y_d, proj,
                        w_branch_attn[l].astype(BF16), w_branch_sb[l].astype(BF16),
                        w_branch_hgrn[l].astype(BF16), w_branch_s5[l].astype(BF16),
                        w_out[l].astype(BF16), norm_ffn[l].astype(F32)[None, :],
                        wr_hi, wr_lo, br, _pick(n, 512))
        x2 = _moe(hx, x2, _pair_cols(expert_w_gate[l]), _pair_cols(expert_w_up[l]),
                  _pair_rows(expert_w_down[l]), tm)

    return x2.reshape(bsz, seq, d).astype(x.dtype)
```

```python
import functools
import math

import jax
import jax.numpy as jnp
import numpy as np
from jax import lax
from jax.experimental import pallas as pl
from jax.experimental.pallas import tpu as pltpu

F32 = jnp.float32
BF16 = jnp.bfloat16

RMS_EPS = 1e-6
LANES = 128
LOG2E = 1.4426950408889634

DA_HEADS = 4
DA_QK_DIM = 64
ROPE_THETA = 500000.0
ROPE_DIM = DA_QK_DIM // 4
DA_LAMBDA_INIT_BASE = 0.8
DA_LAMBDA_INIT_SCALE = 0.6
DA_LAMBDA_INIT_RATE = 0.3

SB_HEADS = 4
SB_HEAD_DIM = 64

HG_HEADS = 4
HG_DIM = 64
HG_COLS = HG_HEADS * HG_DIM
HG_SUB = 16
HG_UNROLL = 4

S5_GROUPS = 16
S5_GROUP_CH = 16
S5_STATE = 64
S5_WIDTH = S5_GROUPS * S5_GROUP_CH
S5_NSTATE = S5_GROUPS * S5_STATE
S5_EIG_CLIP = -1e-4
S5_SUB = 8

N_GROUPS = 4
EXPERTS_PER_GROUP = 4
N_EXPERTS = N_GROUPS * EXPERTS_PER_GROUP

REF_GATE_START = 3584
COL_GATE = 0
COL_DA_Q = 4096
COL_DA_K = 4608
COL_DA_V = 5120
COL_SB_Q = 5632
COL_SB_K = 5888
COL_SB_V = 6144
COL_HG_F = 6400
COL_HG_I = 6656
COL_HG_Q = 6912
COL_HG_G = 7168
COL_S5_U = 7424
IN_COLS = 7680

VMEM_LIMIT = 48 * 1024 * 1024

DA_TQ = 1024
DA_GROUP = 1
SB_TQ = 512
SB_KW = 256
SB_GROUP = 4
MOE_ROW_BLOCK = 512
MOE_DMA_BURST = 256
MOE_DMA_UNROLL = 8
NP_TM = 2048

_NT = (((1,), (1,)), ((), ()))
_TN = (((0,), (0,)), ((), ()))


def _cparams(sem):
    return pltpu.CompilerParams(dimension_semantics=sem, vmem_limit_bytes=VMEM_LIMIT)


def _dot(a, b):
    return jnp.dot(a, b, preferred_element_type=F32)


def _norm_proj_kernel(x_ref, g_ref, w_ref, o_ref, h_ref):
    @pl.when(pl.program_id(1) == 0)
    def _():
        x = x_ref[...]
        ms = jnp.mean(x * x, axis=-1, keepdims=True)
        h_ref[...] = (x * lax.rsqrt(ms + RMS_EPS) * g_ref[...]).astype(BF16)

    o_ref[...] = _dot(h_ref[...], w_ref[...]).astype(o_ref.dtype)


def _norm_proj(x2, gain, w, tm, tn):
    n, d = x2.shape
    cols = w.shape[1]
    return pl.pallas_call(
        _norm_proj_kernel,
        grid=(n // tm, cols // tn),
        in_specs=[pl.BlockSpec((tm, d), lambda i, j: (i, 0)),
                  pl.BlockSpec((1, d), lambda i, j: (0, 0)),
                  pl.BlockSpec((d, tn), lambda i, j: (0, j))],
        out_specs=pl.BlockSpec((tm, tn), lambda i, j: (i, j)),
        out_shape=jax.ShapeDtypeStruct((n, cols), BF16),
        scratch_shapes=[pltpu.VMEM((tm, d), BF16)],
        compiler_params=_cparams(("parallel", "arbitrary")),
        name="norm_proj",
    )(x2, gain, w)


def _rope_kernel(pos_ref, invf_ref, sa_ref, sb_ref, cos_ref, sina_ref, sinb_ref):
    ang = pos_ref[...].astype(F32) * invf_ref[...]
    c = jnp.cos(ang)
    s = jnp.sin(ang)
    cos_ref[...] = c
    sina_ref[...] = s * sa_ref[...]
    sinb_ref[...] = s * sb_ref[...]


def _rope_tables(pos_col, tm):
    n = pos_col.shape[0]
    half = ROPE_DIM // 2
    inv_freq = jnp.exp(-math.log(ROPE_THETA) * jnp.arange(half, dtype=F32) * (2.0 / ROPE_DIM))
    lane = np.arange(LANES) % DA_QK_DIM
    invf = jnp.where(lane < ROPE_DIM, inv_freq[lane % half], 0.0).astype(F32)[None, :]
    sgn_a = jnp.asarray(np.where(lane < half, -1.0, 0.0), F32)[None, :]
    sgn_b = jnp.asarray(np.where((lane >= half) & (lane < ROPE_DIM), 1.0, 0.0), F32)[None, :]
    row = pl.BlockSpec((1, LANES), lambda i: (0, 0))
    tab = pl.BlockSpec((tm, LANES), lambda i: (i, 0))
    shp = jax.ShapeDtypeStruct((n, LANES), F32)
    return pl.pallas_call(
        _rope_kernel,
        grid=(n // tm,),
        in_specs=[pl.BlockSpec((tm, 1), lambda i: (i, 0)), row, row, row],
        out_specs=[tab, tab, tab],
        out_shape=[shp, shp, shp],
        compiler_params=_cparams(("parallel",)),
        name="rope_tables",
    )(pos_col, invf, sgn_a, sgn_b)


def _qk_prep_kernel(x_ref, gain_ref, cos_ref, sina_ref, sinb_ref, bd_ref, o_ref):
    c = cos_ref[...]
    sa = sina_ref[...]
    sb = sinb_ref[...]
    bd = bd_ref[...]
    n_tiles = x_ref.shape[1] // LANES
    for j in range(n_tiles):
        sl = slice(j * LANES, (j + 1) * LANES)
        t = x_ref[:, sl].astype(F32)
        ss = _dot((t * t).astype(BF16), bd)
        y = t * lax.rsqrt(ss * (1.0 / DA_QK_DIM) + RMS_EPS) * gain_ref[:, sl]
        y = (y * c + pltpu.roll(y, LANES - ROPE_DIM // 2, 1) * sa
             + pltpu.roll(y, ROPE_DIM // 2, 1) * sb)
        if j < n_tiles // 2:
            y = y * (DA_QK_DIM ** -0.5 * LOG2E)
        o_ref[:, sl] = y.astype(BF16)


def _qk_prep(proj, gain_row, cos_t, sina_t, sinb_t, tm):
    n = proj.shape[0]
    w = 2 * DA_HEADS * 2 * DA_QK_DIM
    seg = np.arange(LANES) // DA_QK_DIM
    bd = jnp.asarray(seg[:, None] == seg[None, :], BF16)
    tab = pl.BlockSpec((tm, LANES), lambda i: (i, 0))
    return pl.pallas_call(
        _qk_prep_kernel,
        grid=(n // tm,),
        in_specs=[pl.BlockSpec((tm, w), lambda i: (i, COL_DA_Q // w)),
                  pl.BlockSpec((1, w), lambda i: (0, 0)),
                  tab, tab, tab,
                  pl.BlockSpec((LANES, LANES), lambda i: (0, 0))],
        out_specs=pl.BlockSpec((tm, w), lambda i: (i, 0)),
        out_shape=jax.ShapeDtypeStruct((n, w), BF16),
        compiler_params=_cparams(("parallel",)),
        name="qk_prep",
    )(proj, gain_row, cos_t, sina_t, sinb_t, bd)


def _diff_attn_kernel(q_ref, k_ref, v_ref, lq1_ref, lk1_ref, lq2_ref, lk2_ref, sg_ref, o_ref,
                      m_ref, l_ref, acc_ref, *, tq, group, lambda_init):
    qi = pl.program_id(2)
    q = q_ref[...]
    lane = lax.broadcasted_iota(jnp.int32, q.shape, 1)
    zero = jnp.zeros_like(q)
    q2 = jnp.concatenate([jnp.where(lane < DA_QK_DIM, q, zero),
                          jnp.where(lane >= DA_QK_DIM, q, zero)], axis=0)
    m_ref[...] = jnp.full(m_ref.shape, -jnp.inf, F32)
    l_ref[...] = jnp.zeros(l_ref.shape, F32)
    acc_ref[...] = jnp.zeros(acc_ref.shape, F32)

    def tile(start, width, masked):
        kb = k_ref[pl.ds(start, width), :]
        vb = v_ref[pl.ds(start, width), :]
        n_lt = width // LANES
        s = lax.dot_general(q2, kb, _NT, preferred_element_type=F32)
        if masked:
            row = lax.broadcasted_iota(jnp.int32, (tq, width), 0)
            col = lax.broadcasted_iota(jnp.int32, (tq, width), 1)
            keep = col <= row
            s = jnp.where(jnp.concatenate([keep, keep], axis=0), s, -jnp.inf)
        st = [s[:, j * LANES:(j + 1) * LANES] for j in range(n_lt)]
        smax = st[0]
        for j in range(1, n_lt):
            smax = jnp.maximum(smax, st[j])
        m_prev = m_ref[...]
        m_new = jnp.maximum(m_prev, jnp.max(smax, axis=-1, keepdims=True))
        alpha = jnp.exp2(m_prev - m_new)
        ps = [jnp.exp2(t - m_new) for t in st]
        lsum = ps[0]
        for j in range(1, n_lt):
            lsum = lsum + ps[j]
        p = jnp.concatenate([t.astype(BF16) for t in ps], axis=1)
        l_ref[...] = alpha * l_ref[...] + lsum
        acc_ref[...] = alpha * acc_ref[...] + _dot(p, vb)
        m_ref[...] = m_new

    n_group = qi // group

    def body_group(gi, carry):
        tile(pl.multiple_of(gi * (group * tq), group * tq), group * tq, False)
        return carry

    def body_single(ki, carry):
        tile(pl.multiple_of(ki * tq, tq), tq, False)
        return carry

    lax.fori_loop(0, n_group, body_group, 0)
    lax.fori_loop(n_group * group, qi, body_single, 0)
    tile(pl.multiple_of(qi * tq, tq), tq, True)

    lam = (jnp.exp(jnp.sum(lq1_ref[...] * lk1_ref[...], axis=-1, keepdims=True))
           - jnp.exp(jnp.sum(lq2_ref[...] * lk2_ref[...], axis=-1, keepdims=True)) + lambda_init)
    o = acc_ref[...] / jnp.sum(l_ref[...], axis=-1, keepdims=True)
    o = o[:tq] - lam * o[tq:]
    ms = jnp.mean(o * o, axis=-1, keepdims=True)
    o = o * lax.rsqrt(ms + RMS_EPS) * sg_ref[...] * (1.0 - lambda_init)
    o_ref[...] = o.astype(o_ref.dtype)


def _diff_attn(qk, proj, lq1, lk1, lq2, lk2, subln, bsz, seq, tq, group, lambda_init):
    n = qk.shape[0]
    nq = seq // tq
    kcol = (DA_HEADS * 2 * DA_QK_DIM) // LANES
    vcol = COL_DA_V // LANES
    vec = pl.BlockSpec((1, DA_QK_DIM), lambda b, h, i: (0, 0))
    return pl.pallas_call(
        functools.partial(_diff_attn_kernel, tq=tq, group=group, lambda_init=lambda_init),
        grid=(bsz, DA_HEADS, nq),
        in_specs=[pl.BlockSpec((tq, LANES), lambda b, h, i: (b * nq + i, h)),
                  pl.BlockSpec((seq, LANES), lambda b, h, i: (b, kcol + h)),
                  pl.BlockSpec((seq, LANES), lambda b, h, i: (b, vcol + h)),
                  vec, vec, vec, vec,
                  pl.BlockSpec((1, LANES), lambda b, h, i: (0, 0))],
        out_specs=pl.BlockSpec((tq, LANES), lambda b, h, i: (b * nq + i, h)),
        out_shape=jax.ShapeDtypeStruct((n, DA_HEADS * LANES), BF16),
        scratch_shapes=[pltpu.VMEM((2 * tq, LANES), F32), pltpu.VMEM((2 * tq, LANES), F32),
                        pltpu.VMEM((2 * tq, LANES), F32)],
        compiler_params=_cparams(("parallel", "parallel", "arbitrary")),
        name="diff_attn",
    )(qk, qk, proj, lq1, lk1, lq2, lk2, subln)


def _stick_break_kernel(q_ref, k_ref, v_ref, o_ref, r_ref, acc_ref, *, tq, kw, group):
    qi = pl.program_id(2)
    q = q_ref[...]
    lane = lax.broadcasted_iota(jnp.int32, q.shape, 1)
    zero = jnp.zeros_like(q)
    q2 = jnp.concatenate([jnp.where(lane < SB_HEAD_DIM, q, zero),
                          jnp.where(lane >= SB_HEAD_DIM, q, zero)], axis=0)
    incl = jnp.where(lax.broadcasted_iota(jnp.int32, (kw, kw), 0)
                     >= lax.broadcasted_iota(jnp.int32, (kw, kw), 1), 1.0, 0.0).astype(BF16)
    n_lt = kw // LANES
    n_diag = tq // kw
    r_ref[...] = jnp.zeros(r_ref.shape, F32)
    acc_ref[...] = jnp.zeros(acc_ref.shape, F32)

    def run(starts, offsets):
        masks, us, sps = [], [], []
        for start, off in zip(starts, offsets):
            u = lax.dot_general(q2, k_ref[pl.ds(start, kw), :], _NT, preferred_element_type=F32)
            neg_abs = lax.bitcast_convert_type(
                lax.bitcast_convert_type(u, jnp.int32) | jnp.int32(-2 ** 31), F32)
            sp = jnp.maximum(u, 0.0) + jnp.log(1.0 + jnp.exp2(neg_abs)) * LOG2E
            mask = None
            if off is not None:
                keep = (lax.broadcasted_iota(jnp.int32, (tq, kw), 1) + off
                        < lax.broadcasted_iota(jnp.int32, (tq, kw), 0))
                mask = jnp.concatenate([keep, keep], axis=0)
                sp = jnp.where(mask, sp, 0.0)
            masks.append(mask)
            us.append(u)
            sps.append(sp.astype(BF16))
        cum_all = _dot(jnp.concatenate(sps, axis=0), incl)
        r = r_ref[...]
        total = None
        for i, start in enumerate(starts):
            cum = cum_all[i * 2 * tq:(i + 1) * 2 * tq]
            w = jnp.exp2(us[i] - (cum + jnp.concatenate([r] * n_lt, axis=1)))
            if masks[i] is not None:
                w = jnp.where(masks[i], w, 0.0)
            part = _dot(w.astype(BF16), v_ref[pl.ds(start, kw), :])
            total = part if total is None else total + part
            r = r + cum[:, 0:1]
        acc_ref[...] += total
        r_ref[...] = r

    base = qi * tq
    run([pl.multiple_of(base + (n_diag - 1 - j) * kw, kw) for j in range(n_diag)],
        [(n_diag - 1 - j) * kw for j in range(n_diag)])

    n_below = qi * n_diag
    rem = n_below % group

    def body_single(j, carry):
        run([pl.multiple_of((n_below - 1 - j) * kw, kw)], [None])
        return carry

    def body_group(gi, carry):
        top = n_below - rem - gi * group
        run([pl.multiple_of((top - 1 - j) * kw, kw) for j in range(group)], [None] * group)
        return carry

    lax.fori_loop(0, rem, body_single, 0)
    lax.fori_loop(0, n_below // group, body_group, 0)
    lane_o = lax.broadcasted_iota(jnp.int32, (tq, LANES), 1)
    o_ref[...] = jnp.where(lane_o < SB_HEAD_DIM, acc_ref[:tq, :], acc_ref[tq:, :]).astype(o_ref.dtype)


def _stick_break(proj, bsz, seq, tq, kw, group):
    n = proj.shape[0]
    nq = seq // tq
    pairs = (SB_HEADS * SB_HEAD_DIM) // LANES
    qc, kc, vc = COL_SB_Q // LANES, COL_SB_K // LANES, COL_SB_V // LANES
    return pl.pallas_call(
        functools.partial(_stick_break_kernel, tq=tq, kw=kw, group=group),
        grid=(bsz, pairs, nq),
        in_specs=[pl.BlockSpec((tq, LANES), lambda b, p, i: (b * nq + i, qc + p)),
                  pl.BlockSpec((seq, LANES), lambda b, p, i: (b, kc + p)),
                  pl.BlockSpec((seq, LANES), lambda b, p, i: (b, vc + p))],
        out_specs=pl.BlockSpec((tq, LANES), lambda b, p, i: (b * nq + i, p)),
        out_shape=jax.ShapeDtypeStruct((n, pairs * LANES), BF16),
        scratch_shapes=[pltpu.VMEM((2 * tq, LANES), F32), pltpu.VMEM((2 * tq, LANES), F32)],
        compiler_params=_cparams(("parallel", "parallel", "arbitrary")),
        name="stick_break",
    )(proj, proj, proj)


def _split_bf16(x):
    hi = x.astype(BF16)
    lo = (x - hi.astype(F32)).astype(BF16)
    return hi, lo


def _hgrn_kernel(f_ref, i_ref, q_ref, g_ref, lb_ref, gain_ref, o_ref,
                 st_ref, b_ref, dec_ref, qq_ref, kk_ref, vv_ref, oo_ref, qe_ref, ke_ref, gm_ref, uu_ref,
                 *, tb):
    @pl.when(pl.program_id(1) == 0)
    def _():
        st_ref[...] = jnp.zeros(st_ref.shape, F32)

    nsub = tb // HG_SUB
    z = f_ref[...].astype(F32)
    lb = lb_ref[...]
    sp = jnp.maximum(z, 0.0) + jnp.log(1.0 + jnp.exp(-jnp.abs(z)))
    log_sig = z - sp
    a = jnp.log(lb)
    c = jnp.log(1.0 - lb) + log_sig
    mx = jnp.maximum(a, c)
    log_f = mx + jnp.log(jnp.exp(a - mx) + jnp.exp(c - mx))
    key = (1.0 - lb) * jax.nn.sigmoid(-z)

    row = lax.broadcasted_iota(jnp.int32, (tb, tb), 0)
    col = lax.broadcasted_iota(jnp.int32, (tb, tb), 1)
    sub_shift = HG_SUB.bit_length() - 1
    dim_shift = HG_DIM.bit_length() - 1
    same = (row >> sub_shift) == (col >> sub_shift)
    tri = jnp.where(same & (col <= row), 1.0, 0.0).astype(BF16)
    blk = jnp.where(same, 1.0, 0.0).astype(BF16)
    hi, lo = _split_bf16(log_f)
    b = _dot(tri, hi) + _dot(tri, lo)
    e = _dot(blk, hi) + _dot(blk, lo)
    qf = q_ref[...].astype(F32)
    b_ref[...] = b
    qq_ref[...] = qf
    kk_ref[...] = key
    vv_ref[...] = i_ref[...].astype(F32)
    qe_ref[...] = (qf * jnp.exp(b)).astype(BF16)
    ke_ref[...] = (key * jnp.exp(e - b)).astype(BF16)
    dec_ref[...] = jnp.exp(e)

    seg_r = lax.broadcasted_iota(jnp.int32, (HG_COLS, HG_COLS), 0) >> dim_shift
    seg_c = lax.broadcasted_iota(jnp.int32, (HG_COLS, HG_COLS), 1) >> dim_shift
    head_mask = seg_r == seg_c
    ones_bd = jnp.where(head_mask, 1.0, 0.0).astype(BF16)
    sel = jnp.where(lax.broadcasted_iota(jnp.int32, (HG_SUB, HG_SUB * HG_SUB), 0)
                    == (lax.broadcasted_iota(jnp.int32, (HG_SUB, HG_SUB * HG_SUB), 1) >> sub_shift),
                    1.0, 0.0).astype(BF16)
    srow = lax.broadcasted_iota(jnp.int32, (HG_SUB, HG_COLS), 0)

    def intra(ci, slot):
        r0 = pl.multiple_of(ci * HG_SUB, HG_SUB)
        bi = b_ref[pl.ds(r0, HG_SUB), :]
        qi = qq_ref[pl.ds(r0, HG_SUB), :]
        ki = kk_ref[pl.ds(r0, HG_SUB), :]
        vi = vv_ref[pl.ds(r0, HG_SUB), :]
        for t in range(HG_SUB):
            d = jnp.exp(jnp.minimum(bi[t:t + 1, :] - bi, 0.0))
            g = jnp.where(srow <= t, qi[t:t + 1, :] * ki * d, 0.0)
            gm_ref[slot, t * HG_SUB:(t + 1) * HG_SUB, :] = g.astype(BF16)
        sc = _dot(gm_ref[slot], ones_bd)
        vt = jnp.concatenate([vi] * HG_SUB, axis=0)
        oo_ref[pl.ds(r0, HG_SUB), :] = _dot(sel, (sc * vt).astype(BF16))
        upd = lax.dot_general(vi.astype(BF16), ke_ref[pl.ds(r0, HG_SUB), :], _TN,
                              preferred_element_type=F32)
        uu_ref[ci] = jnp.where(head_mask, upd, 0.0)

    def intra_group(gi, carry):
        for slot in range(HG_UNROLL):
            intra(gi * HG_UNROLL + slot, slot)
        return carry

    lax.fori_loop(0, nsub // HG_UNROLL, intra_group, 0)

    st = st_ref[...]
    for ci in range(nsub):
        rows = slice(ci * HG_SUB, (ci + 1) * HG_SUB)
        oo_ref[rows, :] += lax.dot_general(qe_ref[rows, :], st.astype(BF16), _NT,
                                           preferred_element_type=F32)
        st = st * dec_ref[ci * HG_SUB:ci * HG_SUB + 1, :] + uu_ref[ci]
    st_ref[...] = st

    o = oo_ref[...]
    ms = _dot((o * o).astype(BF16), ones_bd) * (1.0 / HG_DIM)
    gate = g_ref[...].astype(F32)
    gate = gate * jax.nn.sigmoid(gate)
    o_ref[...] = (o * lax.rsqrt(ms + RMS_EPS) * gain_ref[...] * gate).astype(o_ref.dtype)


def _hgrn(proj, lb_row, gain_row, bsz, seq, tb):
    n = proj.shape[0]
    nb = seq // tb
    w = HG_COLS
    cf, ci, cq, cg = COL_HG_F // w, COL_HG_I // w, COL_HG_Q // w, COL_HG_G // w

    def col(cc):
        return pl.BlockSpec((tb, w), lambda b, i: (b * nb + i, cc))

    rowspec = pl.BlockSpec((1, w), lambda b, i: (0, 0))
    big = pltpu.VMEM((tb, w), F32)
    return pl.pallas_call(
        functools.partial(_hgrn_kernel, tb=tb),
        grid=(bsz, nb),
        in_specs=[col(cf), col(ci), col(cq), col(cg), rowspec, rowspec],
        out_specs=pl.BlockSpec((tb, w), lambda b, i: (b * nb + i, 0)),
        out_shape=jax.ShapeDtypeStruct((n, w), BF16),
        scratch_shapes=[pltpu.VMEM((w, w), F32), big, big, big, big, big, big,
                        pltpu.VMEM((tb, w), BF16), pltpu.VMEM((tb, w), BF16),
                        pltpu.VMEM((HG_UNROLL, HG_SUB * HG_SUB, w), BF16),
                        pltpu.VMEM((tb // HG_SUB, w, w), F32)],
        compiler_params=_cparams(("parallel", "arbitrary")),
        name="hgrn2",
    )(proj, proj, proj, proj, lb_row, gain_row)


def _s5_kernel(u_ref, bblk_ref, lev_re_ref, lev_im_ref, pw_re_ref, pw_im_ref, cblk_ref, d_ref,
               gw_ref, gb_ref, o_ref, cr_ref, ci_ref, xr_ref, xi_ref, *, tb):
    @pl.when(pl.program_id(1) == 0)
    def _():
        cr_ref[...] = jnp.zeros(cr_ref.shape, F32)
        ci_ref[...] = jnp.zeros(ci_ref.shape, F32)

    u = u_ref[...]
    bu = _dot(u, bblk_ref[...])
    xr = bu[:, :S5_NSTATE]
    xi = bu[:, S5_NSTATE:]
    row = lax.broadcasted_iota(jnp.int32, xr.shape, 0)
    sub_lev = S5_SUB.bit_length() - 1
    for j in range(sub_lev):
        d = 1 << j
        ar = lev_re_ref[j:j + 1, :]
        ai = lev_im_ref[j:j + 1, :]
        keep = row >= d
        sr = jnp.where(keep, pltpu.roll(xr, d, 0), 0.0)
        si = jnp.where(keep, pltpu.roll(xi, d, 0), 0.0)
        xr, xi = xr + ar * sr - ai * si, xi + ar * si + ai * sr
    cr = cr_ref[...]
    ci = ci_ref[...]
    pr = pw_re_ref[...]
    pi = pw_im_ref[...]
    gr = xr[:S5_SUB] + pr * cr - pi * ci
    gi = xi[:S5_SUB] + pr * ci + pi * cr
    xr_ref[:S5_SUB, :] = gr
    xi_ref[:S5_SUB, :] = gi
    ar = lev_re_ref[sub_lev:sub_lev + 1, :]
    ai = lev_im_ref[sub_lev:sub_lev + 1, :]
    for g in range(1, tb // S5_SUB):
        rows = slice(g * S5_SUB, (g + 1) * S5_SUB)
        gr, gi = xr[rows] + ar * gr - ai * gi, xi[rows] + ar * gi + ai * gr
        xr_ref[rows, :] = gr
        xi_ref[rows, :] = gi
    cr_ref[...] = gr[S5_SUB - 1:S5_SUB, :]
    ci_ref[...] = gi[S5_SUB - 1:S5_SUB, :]
    y = (_dot(xr_ref[...].astype(BF16), cblk_ref[:S5_NSTATE, :])
         + _dot(xi_ref[...].astype(BF16), cblk_ref[S5_NSTATE:, :]) + d_ref[...] * u.astype(F32))
    y = jax.nn.gelu(y)
    zg = _dot(y.astype(BF16), gw_ref[...]) + gb_ref[...]
    o_ref[...] = (y * jax.nn.sigmoid(zg)).astype(o_ref.dtype)


def _s5_params(lam_re, lam_im, log_step, b_re, b_im, c_re, c_im, tb):
    lam_re = jnp.minimum(lam_re.astype(F32), S5_EIG_CLIP)
    lam_im = lam_im.astype(F32)
    step = jnp.exp(log_step.astype(F32))[:, None]
    mag = jnp.exp(lam_re * step)
    phase = lam_im * step
    a_re = mag * jnp.cos(phase)
    a_im = mag * jnp.sin(phase)
    denom = lam_re * lam_re + lam_im * lam_im
    num_re = a_re - 1.0
    gam_re = (num_re * lam_re + a_im * lam_im) / denom
    gam_im = (a_im * lam_re - num_re * lam_im) / denom
    b_re = b_re.astype(F32)
    b_im = b_im.astype(F32)
    bb_re = gam_re[..., None] * b_re - gam_im[..., None] * b_im
    bb_im = gam_re[..., None] * b_im + gam_im[..., None] * b_re
    eye = jnp.eye(S5_GROUPS, dtype=F32)

    def in_blk(bb):
        return jnp.einsum('gnc,gh->gchn', bb, eye).reshape(S5_WIDTH, S5_NSTATE)

    def out_blk(cc):
        return jnp.einsum('gcn,gh->gnhc', cc.astype(F32), eye).reshape(S5_NSTATE, S5_WIDTH)

    bblk = jnp.concatenate([in_blk(bb_re), in_blk(bb_im)], axis=1).astype(BF16)
    cblk = jnp.concatenate([out_blk(c_re), -out_blk(c_im)], axis=0).astype(BF16)
    ar = a_re.reshape(1, S5_NSTATE)
    ai = a_im.reshape(1, S5_NSTATE)
    sub_lev = S5_SUB.bit_length() - 1
    lev_re, lev_im = [ar], [ai]
    pw_re, pw_im = ar, ai
    for _ in range(sub_lev):
        sr, si = lev_re[-1], lev_im[-1]
        pw_re, pw_im = (jnp.concatenate([pw_re, pw_re * sr - pw_im * si], axis=0),
                        jnp.concatenate([pw_im, pw_re * si + pw_im * sr], axis=0))
        lev_re.append(sr * sr - si * si)
        lev_im.append(2.0 * sr * si)
    lev_re = jnp.concatenate(lev_re, axis=0)
    lev_im = jnp.concatenate(lev_im, axis=0)
    return bblk, cblk, lev_re, lev_im, pw_re, pw_im


def _s5(proj, params, d_row, glu_w, glu_b, bsz, seq, tb):
    n = proj.shape[0]
    nb = seq // tb
    bblk, cblk, lev_re, lev_im, pw_re, pw_im = params
    nlev = lev_re.shape[0]
    ucol = COL_S5_U // S5_WIDTH

    def full(shape):
        return pl.BlockSpec(shape, lambda b, i: (0,) * len(shape))

    return pl.pallas_call(
        functools.partial(_s5_kernel, tb=tb),
        grid=(bsz, nb),
        in_specs=[pl.BlockSpec((tb, S5_WIDTH), lambda b, i: (b * nb + i, ucol)),
                  full((S5_WIDTH, 2 * S5_NSTATE)),
                  full((nlev, S5_NSTATE)), full((nlev, S5_NSTATE)),
                  full((S5_SUB, S5_NSTATE)), full((S5_SUB, S5_NSTATE)),
                  full((2 * S5_NSTATE, S5_WIDTH)),
                  full((1, S5_WIDTH)), full((S5_WIDTH, S5_WIDTH)), full((1, S5_WIDTH))],
        out_specs=pl.BlockSpec((tb, S5_WIDTH), lambda b, i: (b * nb + i, 0)),
        out_shape=jax.ShapeDtypeStruct((n, S5_WIDTH), BF16),
        scratch_shapes=[pltpu.VMEM((1, S5_NSTATE), F32), pltpu.VMEM((1, S5_NSTATE), F32),
                        pltpu.VMEM((tb, S5_NSTATE), F32), pltpu.VMEM((tb, S5_NSTATE), F32)],
        compiler_params=_cparams(("parallel", "arbitrary")),
        name="s5",
    )(proj, bblk, lev_re, lev_im, pw_re, pw_im, cblk, d_row, glu_w, glu_b)


def _merge_kernel(x_ref, ya_ref, yb_ref, yc_ref, yd_ref, gate_ref, wa_ref, wb_ref, wc_ref, wd_ref,
                  wo_ref, gn_ref, wrh_ref, wrl_ref, br_ref, xo_ref, hx_ref):
    d = x_ref.shape[1]
    merged = None
    for i, (y_ref, w_ref) in enumerate(((ya_ref, wa_ref), (yb_ref, wb_ref),
                                        (yc_ref, wc_ref), (yd_ref, wd_ref))):
        gate = 0.5 * jnp.tanh(0.5 * gate_ref[:, i * d:(i + 1) * d].astype(F32)) + 0.5
        term = gate * _dot(y_ref[...], w_ref[...])
        merged = term if merged is None else merged + term
    xn = x_ref[...] + _dot(merged.astype(BF16), wo_ref[...])
    xo_ref[...] = xn
    ms = jnp.mean(xn * xn, axis=-1, keepdims=True)
    h = xn * lax.rsqrt(ms + RMS_EPS) * gn_ref[...]
    hi, lo = _split_bf16(h)
    hx_ref[:, :d] = hi.astype(F32)
    wrh = wrh_ref[...]
    hx_ref[:, d:] = _dot(hi, wrh) + _dot(lo, wrh) + _dot(hi, wrl_ref[...]) + br_ref[...]


def _merge(x2, ya, yb, yc, yd, proj, wa, wb, wc, wd, wo, gn_row, wr_hi, wr_lo, br_row, tm):
    n, d = x2.shape
    assert COL_GATE % (4 * d) == 0
    gcol = COL_GATE // (4 * d)

    def rows(width, cc=0):
        return pl.BlockSpec((tm, width), lambda i: (i, cc))

    def full(arr):
        return pl.BlockSpec(arr.shape, lambda i: (0, 0))

    return pl.pallas_call(
        _merge_kernel,
        grid=(n // tm,),
        in_specs=[rows(d), rows(ya.shape[1]), rows(yb.shape[1]), rows(yc.shape[1]), rows(yd.shape[1]),
                  rows(4 * d, gcol),
                  full(wa), full(wb), full(wc), full(wd), full(wo), full(gn_row),
                  full(wr_hi), full(wr_lo), full(br_row)],
        out_specs=[rows(d), rows(d + LANES)],
        out_shape=[jax.ShapeDtypeStruct((n, d), F32), jax.ShapeDtypeStruct((n, d + LANES), F32)],
        compiler_params=_cparams(("parallel",)),
        name="merge",
    )(x2, ya, yb, yc, yd, proj, wa, wb, wc, wd, wo, gn_row, wr_hi, wr_lo, br_row)


def _first_index(mask, lane):
    return jnp.min(jnp.where(mask, lane, float(LANES)), axis=-1, keepdims=True)


def _group_onehot(logits):
    lane = lax.broadcasted_iota(jnp.int32, logits.shape, 1).astype(F32)
    gl = jnp.where(lane < N_GROUPS, logits, -jnp.inf)
    g_idx = _first_index(gl == jnp.max(gl, axis=-1, keepdims=True), lane)
    return jnp.where(lane == g_idx, 1.0, 0.0)


def _route_kernel(lg_ref, tri_ref, pos_ref, meta_ref, cnt_ref, run_ref, off_ref, *, row_block):
    phase = pl.program_id(0)
    i = pl.program_id(1)
    onehot = _group_onehot(lg_ref[...])
    lane = lax.broadcasted_iota(jnp.int32, (1, LANES), 1)

    @pl.when((phase == 0) & (i == 0))
    def _():
        cnt_ref[...] = jnp.zeros(cnt_ref.shape, F32)

    @pl.when(phase == 0)
    def _():
        cnt_ref[...] += jnp.sum(onehot, axis=0, keepdims=True)
        pos_ref[...] = jnp.zeros(pos_ref.shape, jnp.int32)

    @pl.when((phase == 1) & (i == 0))
    def _():
        padded = jnp.floor((cnt_ref[...] + (row_block - 1)) * (1.0 / row_block)) * row_block
        off = jnp.zeros((1, LANES), F32)
        acc = jnp.zeros((1, 1), F32)
        for g in range(1, N_GROUPS):
            acc = acc + jnp.sum(jnp.where(lane == g - 1, padded, 0.0), axis=-1, keepdims=True)
            off = off + jnp.where(lane == g, acc, 0.0)
        off_ref[...] = off
        run_ref[...] = jnp.zeros(run_ref.shape, F32)
        meta_ref[...] = jnp.zeros(meta_ref.shape, F32)
        meta_ref[0:1, :] = cnt_ref[...]
        meta_ref[1:2, :] = off

    @pl.when(phase == 1)
    def _():
        before = _dot(tri_ref[...], onehot.astype(BF16))
        slot = jnp.sum(onehot * (before + run_ref[...] + off_ref[...]), axis=-1, keepdims=True)
        pos_ref[...] = slot.astype(jnp.int32)
        run_ref[...] += jnp.sum(onehot, axis=0, keepdims=True)


def _moe_route(hx, d, tr, row_block):
    n = hx.shape[0]
    nb = n // tr
    tri = jnp.asarray(np.tril(np.ones((tr, tr), np.float32), -1), BF16)
    pos, meta = pl.pallas_call(
        functools.partial(_route_kernel, row_block=row_block),
        grid=(2, nb),
        in_specs=[pl.BlockSpec((tr, LANES), lambda p, i: (i, d // LANES)),
                  pl.BlockSpec((tr, tr), lambda p, i: (0, 0))],
        out_specs=[pl.BlockSpec((tr, 1), lambda p, i: (p * nb + i, 0)),
                   pl.BlockSpec((8, LANES), lambda p, i: (0, 0))],
        out_shape=[jax.ShapeDtypeStruct((2 * n, 1), jnp.int32), jax.ShapeDtypeStruct((8, LANES), F32)],
        scratch_shapes=[pltpu.VMEM((1, LANES), F32)] * 3,
        compiler_params=_cparams(("arbitrary", "arbitrary")),
        name="moe_route",
    )(hx, tri)
    return pos[n:], meta


def _row_dma_wait(src_hbm, dst_ref, sem, rows):
    pltpu.make_async_copy(src_hbm.at[pl.ds(0, rows)], dst_ref.at[pl.ds(0, rows)], sem).wait()


def _scatter_kernel(pos_ref, hx_ref, hs_in_hbm, hs_hbm, sem, *, tr, burst):
    del hs_in_hbm

    def one_burst(bi, carry):
        def issue(t8, c):
            for k in range(MOE_DMA_UNROLL):
                row = bi * burst + t8 * MOE_DMA_UNROLL + k
                pltpu.make_async_copy(hx_ref.at[pl.ds(row, 1)],
                                      hs_hbm.at[pl.ds(pos_ref[0, 0, row], 1)], sem).start(priority=k % 2)
            return c
        lax.fori_loop(0, burst // MOE_DMA_UNROLL, issue, 0)
        _row_dma_wait(hx_ref, hs_hbm, sem, burst)
        return carry

    lax.fori_loop(0, tr // burst, one_burst, 0)


def _moe_scatter(hx, pos3, n_sorted, tr):
    n, w = hx.shape
    nb = n // tr
    return pl.pallas_call(
        functools.partial(_scatter_kernel, tr=tr, burst=min(MOE_DMA_BURST, tr)),
        grid=(nb,),
        in_specs=[pl.BlockSpec((1, 1, tr), lambda i: (i, 0, 0), memory_space=pltpu.SMEM),
                  pl.BlockSpec((tr, w), lambda i: (i, 0)),
                  pl.BlockSpec(memory_space=pl.ANY)],
        out_specs=pl.BlockSpec(memory_space=pl.ANY),
        out_shape=jax.ShapeDtypeStruct((n_sorted, w), F32),
        scratch_shapes=[pltpu.SemaphoreType.DMA],
        input_output_aliases={2: 0},
        compiler_params=_cparams(("arbitrary",)),
        name="moe_scatter",
    )(pos3, hx, jnp.zeros((n_sorted, w), F32))


def _experts_kernel(gmap_ref, valid_ref, hs_ref, wg_ref, wu_ref, wd_ref, o_ref, cw_ref, acc_ref, *, d):
    i = pl.program_id(0)
    j = pl.program_id(1)
    valid = valid_ref[i] > 0

    @pl.when(j == 0)
    def _():
        logits = hs_ref[:, d:]
        lane = lax.broadcasted_iota(jnp.int32, logits.shape, 1).astype(F32)
        neg = -jnp.inf
        gl = jnp.where(lane < N_GROUPS, logits, neg)
        gmax = jnp.max(gl, axis=-1, keepdims=True)
        gsum = jnp.sum(jnp.exp(gl - gmax), axis=-1, keepdims=True)
        g_val = 1.0 / gsum
        g_idx = _first_index(gl == gmax, lane)
        lo = N_GROUPS + EXPERTS_PER_GROUP * g_idx
        el = jnp.where((lane >= lo) & (lane < lo + EXPERTS_PER_GROUP), logits, neg)
        emax = jnp.max(el, axis=-1, keepdims=True)
        esum = jnp.sum(jnp.exp(el - emax), axis=-1, keepdims=True)
        i1 = _first_index(el == emax, lane)
        el2 = jnp.where(lane == i1, neg, el)
        e2max = jnp.max(el2, axis=-1, keepdims=True)
        i2 = _first_index(el2 == e2max, lane)
        p1 = 1.0 / esum
        p2 = jnp.exp(e2max - emax) / esum
        tot = p1 + p2
        cw_ref[...] = (jnp.where(lane == i1, g_val * (p1 / tot), 0.0)
                       + jnp.where(lane == i2, g_val * (p2 / tot), 0.0))
        acc_ref[...] = jnp.zeros(acc_ref.shape, F32)

    @pl.when(valid)
    def _():
        h = hs_ref[:, :d].astype(BF16)
        a = _dot(h, jnp.concatenate([wg_ref[0, 0], wg_ref[0, 1]], axis=1))
        hid = (a * jax.nn.sigmoid(a)) * _dot(h, jnp.concatenate([wu_ref[0, 0], wu_ref[0, 1]], axis=1))
        first = N_GROUPS + EXPERTS_PER_GROUP * gmap_ref[i] + 2 * j
        lane = lax.broadcasted_iota(jnp.int32, cw_ref.shape, 1)
        cw_all = cw_ref[...]
        cw0 = jnp.sum(jnp.where(lane == first, cw_all, 0.0), axis=-1, keepdims=True)
        cw1 = jnp.sum(jnp.where(lane == first + 1, cw_all, 0.0), axis=-1, keepdims=True)
        de = hid.shape[1] // 2
        hcol = lax.broadcasted_iota(jnp.int32, hid.shape, 1)
        hid = hid * jnp.where(hcol < de, cw0, cw1)
        acc_ref[...] += _dot(hid.astype(BF16), wd_ref[0])

    @pl.when(j == pl.num_programs(1) - 1)
    def _():
        o_ref[...] = acc_ref[...]


def _moe_experts(hs, gmap, valid, wg, wu, wd, d, row_block):
    n_sorted, w = hs.shape
    _, _, _, de = wg.shape
    de2 = 2 * de
    pairs = EXPERTS_PER_GROUP // 2
    grid_spec = pltpu.PrefetchScalarGridSpec(
        num_scalar_prefetch=2,
        grid=(n_sorted // row_block, pairs),
        in_specs=[pl.BlockSpec((row_block, w), lambda i, j, gm, va: (i, 0)),
                  pl.BlockSpec((1, 2, d, de), lambda i, j, gm, va: (gm[i] * pairs + j, 0, 0, 0)),
                  pl.BlockSpec((1, 2, d, de), lambda i, j, gm, va: (gm[i] * pairs + j, 0, 0, 0)),
                  pl.BlockSpec((1, de2, d), lambda i, j, gm, va: (gm[i] * pairs + j, 0, 0))],
        out_specs=pl.BlockSpec((row_block, d), lambda i, j, gm, va: (i, 0)),
        scratch_shapes=[pltpu.VMEM((row_block, LANES), F32), pltpu.VMEM((row_block, d), F32)])
    return pl.pallas_call(
        functools.partial(_experts_kernel, d=d),
        grid_spec=grid_spec,
        out_shape=jax.ShapeDtypeStruct((n_sorted, d), F32),
        compiler_params=_cparams(("arbitrary", "arbitrary")),
        name="moe_experts",
    )(gmap, valid, hs, wg, wu, wd)


def _combine_kernel(pos_ref, ys_hbm, x_ref, o_ref, buf_ref, sem, *, tr, burst):
    def one_burst(bi, carry):
        def issue(t8, c):
            for k in range(MOE_DMA_UNROLL):
                row = bi * burst + t8 * MOE_DMA_UNROLL + k
                pltpu.make_async_copy(ys_hbm.at[pl.ds(pos_ref[0, 0, row], 1)],
                                      buf_ref.at[pl.ds(row, 1)], sem).start(priority=k % 2)
            return c
        lax.fori_loop(0, burst // MOE_DMA_UNROLL, issue, 0)
        _row_dma_wait(ys_hbm, buf_ref, sem, burst)
        return carry

    lax.fori_loop(0, tr // burst, one_burst, 0)
    o_ref[...] = x_ref[...] + buf_ref[...]


def _moe_combine(ys, pos3, x2, tr):
    n, d = x2.shape
    return pl.pallas_call(
        functools.partial(_combine_kernel, tr=tr, burst=min(MOE_DMA_BURST, tr)),
        grid=(n // tr,),
        in_specs=[pl.BlockSpec((1, 1, tr), lambda i: (i, 0, 0), memory_space=pltpu.SMEM),
                  pl.BlockSpec(memory_space=pl.ANY),
                  pl.BlockSpec((tr, d), lambda i: (i, 0))],
        out_specs=pl.BlockSpec((tr, d), lambda i: (i, 0)),
        out_shape=jax.ShapeDtypeStruct((n, d), F32),
        scratch_shapes=[pltpu.VMEM((tr, d), F32), pltpu.SemaphoreType.DMA],
        compiler_params=_cparams(("arbitrary",)),
        name="moe_combine",
    )(pos3, ys, x2)


def _pair_cols(w):
    ne, d, de = w.shape
    return w.astype(BF16).reshape(ne // 2, 2, d, de)


def _pair_rows(w):
    ne, de, d = w.shape
    return w.astype(BF16).reshape(ne // 2, 2 * de, d)


def _moe(hx, x2, wg, wu, wd, tr):
    n, d = x2.shape
    row_block = min(MOE_ROW_BLOCK, n)
    n_blocks = n // row_block + N_GROUPS
    pos, meta = _moe_route(hx, d, tr, row_block)
    counts = meta[0, :N_GROUPS]
    ends = meta[1, :N_GROUPS] + jnp.ceil(counts / row_block) * row_block
    starts = jnp.arange(n_blocks, dtype=F32) * row_block
    grp = jnp.sum((starts[:, None] >= ends[None, :]).astype(jnp.int32), axis=1)
    valid = (grp < N_GROUPS).astype(jnp.int32)
    gmap = jnp.minimum(grp, N_GROUPS - 1)
    pos3 = pos.reshape(n // tr, 1, tr)
    hs = _moe_scatter(hx, pos3, n_blocks * row_block, tr)
    ys = _moe_experts(hs, gmap, valid, wg, wu, wd, d, row_block)
    return _moe_combine(ys, pos3, x2, tr)


def _pick(n, pref):
    t = min(n, pref)
    while n % t:
        t //= 2
    return t


def kernel(x, positions, norm_mix, w_in, da_q_gain, da_k_gain, da_lambda_q1, da_lambda_k1,
           da_lambda_q2, da_lambda_k2, da_subln_gain, hg_lower_bounds, hg_out_gain,
           s5_lambda_re, s5_lambda_im, s5_log_step, s5_b_re, s5_b_im, s5_c_re, s5_c_im,
           s5_d, s5_glu_w, s5_glu_b, w_branch_attn, w_branch_sb, w_branch_hgrn, w_branch_s5,
           w_out, norm_ffn, router_group_w, router_group_b, router_expert_w, router_expert_b,
           expert_w_gate, expert_w_up, expert_w_down):
    bsz, seq, d = x.shape
    depth = w_in.shape[0]
    n = bsz * seq
    assert w_in.shape[2] == IN_COLS and seq % 128 == 0
    tm = _pick(n, 1024)
    tq = _pick(seq, 256)
    tq_da = _pick(seq, DA_TQ)
    tb = _pick(seq, 256)

    x2 = x.reshape(n, d).astype(F32)
    cos_t, sina_t, sinb_t = _rope_tables(positions.reshape(n, 1).astype(jnp.int32), tm)

    lb_all = jnp.cumsum(jax.nn.softmax(hg_lower_bounds.astype(F32), axis=0), axis=0)
    lb_all = lb_all - lb_all[0:1]

    for l in range(depth):
        lambda_init = DA_LAMBDA_INIT_BASE - DA_LAMBDA_INIT_SCALE * math.exp(-DA_LAMBDA_INIT_RATE * l)
        w_f = w_in[l].astype(F32)
        sbq = COL_SB_Q - COL_DA_Q
        w_l = jnp.concatenate([w_f[:, REF_GATE_START:], w_f[:, :sbq],
                               w_f[:, sbq:sbq + SB_HEADS * SB_HEAD_DIM] * (SB_HEAD_DIM ** -0.5 * LOG2E),
                               w_f[:, sbq + SB_HEADS * SB_HEAD_DIM:REF_GATE_START]], axis=1)
        proj = _norm_proj(x2, norm_mix[l].astype(F32)[None, :], w_l.astype(BF16), _pick(n, NP_TM), 768)

        qk_gain = jnp.concatenate([jnp.tile(da_q_gain[l].astype(F32), 2 * DA_HEADS),
                                   jnp.tile(da_k_gain[l].astype(F32), 2 * DA_HEADS)])[None, :]
        qk = _qk_prep(proj, qk_gain, cos_t, sina_t, sinb_t, tm)
        y_a = _diff_attn(qk, proj,
                         da_lambda_q1[l].astype(F32)[None, :], da_lambda_k1[l].astype(F32)[None, :],
                         da_lambda_q2[l].astype(F32)[None, :], da_lambda_k2[l].astype(F32)[None, :],
                         da_subln_gain[l].astype(F32)[None, :], bsz, seq, tq_da, DA_GROUP, lambda_init)
        y_b = _stick_break(proj, bsz, seq, _pick(seq, SB_TQ), SB_KW, SB_GROUP)
        y_c = _hgrn(proj, lb_all[l][None, :], jnp.tile(hg_out_gain[l].astype(F32), HG_HEADS)[None, :],
                    bsz, seq, tb)
        s5p = _s5_params(s5_lambda_re[l], s5_lambda_im[l], s5_log_step[l], s5_b_re[l], s5_b_im[l],
                         s5_c_re[l], s5_c_im[l], tb)
        y_d = _s5(proj, s5p, s5_d[l].astype(F32)[None, :], s5_glu_w[l].astype(BF16),
                  s5_glu_b[l].astype(F32)[None, :], bsz, seq, tb)

        wr = jnp.concatenate([router_group_w[l], router_expert_w[l]], axis=1).astype(F32)
        wr = jnp.pad(wr, ((0, 0), (0, LANES - wr.shape[1])))
        wr_hi = wr.astype(BF16)
        wr_lo = (wr - wr_hi.astype(F32)).astype(BF16)
        br = jnp.concatenate([router_group_b[l], router_expert_b[l]]).astype(F32)
        br = jnp.pad(br, (0, LANES - br.shape[0]))[None, :]
        x2, hx = _merge(x2, y_a, y_b, y_c, y_d, proj,
                        w_branch_attn[l].astype(BF16), w_branch_sb[l].astype(BF16),
                        w_branch_hgrn[l].astype(BF16), w_branch_s5[l].astype(BF16),
                        w_out[l].astype(BF16), norm_ffn[l].astype(F32)[None, :],
                        wr_hi, wr_lo, br, _pick(n, 512))
        x2 = _moe(hx, x2, _pair_cols(expert_w_gate[l]), _pair_cols(expert_w_up[l]),
                  _pair_rows(expert_w_down[l]), tm)

    return x2.reshape(bsz, seq, d).astype(x.dtype)
```

```python
import functools
import math

import jax
import jax.numpy as jnp
import numpy as np
from jax import lax
from jax.experimental import pallas as pl
from jax.experimental.pallas import tpu as pltpu

F32 = jnp.float32
BF16 = jnp.bfloat16

RMS_EPS = 1e-6
LANES = 128
LOG2E = 1.4426950408889634

DA_HEADS = 4
DA_QK_DIM = 64
ROPE_THETA = 500000.0
ROPE_DIM = DA_QK_DIM // 4
DA_LAMBDA_INIT_BASE = 0.8
DA_LAMBDA_INIT_SCALE = 0.6
DA_LAMBDA_INIT_RATE = 0.3

SB_HEADS = 4
SB_HEAD_DIM = 64

HG_HEADS = 4
HG_DIM = 64
HG_COLS = HG_HEADS * HG_DIM
HG_SUB = 16
HG_UNROLL = 4

S5_GROUPS = 16
S5_GROUP_CH = 16
S5_STATE = 64
S5_WIDTH = S5_GROUPS * S5_GROUP_CH
S5_NSTATE = S5_GROUPS * S5_STATE
S5_EIG_CLIP = -1e-4
S5_SUB = 8

N_GROUPS = 4
EXPERTS_PER_GROUP = 4
N_EXPERTS = N_GROUPS * EXPERTS_PER_GROUP

REF_GATE_START = 3584
COL_GATE = 0
COL_DA_Q = 4096
COL_DA_K = 4608
COL_DA_V = 5120
COL_SB_Q = 5632
COL_SB_K = 5888
COL_SB_V = 6144
COL_HG_F = 6400
COL_HG_I = 6656
COL_HG_Q = 6912
COL_HG_G = 7168
COL_S5_U = 7424
IN_COLS = 7680

VMEM_LIMIT = 48 * 1024 * 1024

DA_TQ = 1024
DA_GROUP = 1
SB_TQ = 512
SB_KW = 256
SB_GROUP = 4
MOE_ROW_BLOCK = 512
MOE_DMA_BURST = 256
MOE_DMA_UNROLL = 8
NP_TM = 2048

_NT = (((1,), (1,)), ((), ()))
_TN = (((0,), (0,)), ((), ()))


def _cparams(sem):
    return pltpu.CompilerParams(dimension_semantics=sem, vmem_limit_bytes=VMEM_LIMIT)


def _dot(a, b):
    return jnp.dot(a, b, preferred_element_type=F32)


def _norm_proj_kernel(x_ref, g_ref, w_ref, o_ref, h_ref):
    @pl.when(pl.program_id(1) == 0)
    def _():
        x = x_ref[...]
        ms = jnp.mean(x * x, axis=-1, keepdims=True)
        h_ref[...] = (x * lax.rsqrt(ms + RMS_EPS) * g_ref[...]).astype(BF16)

    o_ref[...] = _dot(h_ref[...], w_ref[...]).astype(o_ref.dtype)


def _norm_proj(x2, gain, w, tm, tn):
    n, d = x2.shape
    cols = w.shape[1]
    return pl.pallas_call(
        _norm_proj_kernel,
        grid=(n // tm, cols // tn),
        in_specs=[pl.BlockSpec((tm, d), lambda i, j: (i, 0)),
                  pl.BlockSpec((1, d), lambda i, j: (0, 0)),
                  pl.BlockSpec((d, tn), lambda i, j: (0, j))],
        out_specs=pl.BlockSpec((tm, tn), lambda i, j: (i, j)),
        out_shape=jax.ShapeDtypeStruct((n, cols), BF16),
        scratch_shapes=[pltpu.VMEM((tm, d), BF16)],
        compiler_params=_cparams(("parallel", "arbitrary")),
        name="norm_proj",
    )(x2, gain, w)


def _rope_kernel(pos_ref, invf_ref, sa_ref, sb_ref, cos_ref, sina_ref, sinb_ref):
    ang = pos_ref[...].astype(F32) * invf_ref[...]
    c = jnp.cos(ang)
    s = jnp.sin(ang)
    cos_ref[...] = c
    sina_ref[...] = s * sa_ref[...]
    sinb_ref[...] = s * sb_ref[...]


def _rope_tables(pos_col, tm):
    n = pos_col.shape[0]
    half = ROPE_DIM // 2
    inv_freq = jnp.exp(-math.log(ROPE_THETA) * jnp.arange(half, dtype=F32) * (2.0 / ROPE_DIM))
    lane = np.arange(LANES) % DA_QK_DIM
    invf = jnp.where(lane < ROPE_DIM, inv_freq[lane % half], 0.0).astype(F32)[None, :]
    sgn_a = jnp.asarray(np.where(lane < half, -1.0, 0.0), F32)[None, :]
    sgn_b = jnp.asarray(np.where((lane >= half) & (lane < ROPE_DIM), 1.0, 0.0), F32)[None, :]
    row = pl.BlockSpec((1, LANES), lambda i: (0, 0))
    tab = pl.BlockSpec((tm, LANES), lambda i: (i, 0))
    shp = jax.ShapeDtypeStruct((n, LANES), F32)
    return pl.pallas_call(
        _rope_kernel,
        grid=(n // tm,),
        in_specs=[pl.BlockSpec((tm, 1), lambda i: (i, 0)), row, row, row],
        out_specs=[tab, tab, tab],
        out_shape=[shp, shp, shp],
        compiler_params=_cparams(("parallel",)),
        name="rope_tables",
    )(pos_col, invf, sgn_a, sgn_b)


def _qk_prep_kernel(x_ref, gain_ref, cos_ref, sina_ref, sinb_ref, bd_ref, o_ref):
    c = cos_ref[...]
    sa = sina_ref[...]
    sb = sinb_ref[...]
    bd = bd_ref[...]
    n_tiles = x_ref.shape[1] // LANES
    for j in range(n_tiles):
        sl = slice(j * LANES, (j + 1) * LANES)
        t = x_ref[:, sl].astype(F32)
        ss = _dot((t * t).astype(BF16), bd)
        y = t * lax.rsqrt(ss * (1.0 / DA_QK_DIM) + RMS_EPS) * gain_ref[:, sl]
        y = (y * c + pltpu.roll(y, LANES - ROPE_DIM // 2, 1) * sa
             + pltpu.roll(y, ROPE_DIM // 2, 1) * sb)
        if j < n_tiles // 2:
            y = y * (DA_QK_DIM ** -0.5 * LOG2E)
        o_ref[:, sl] = y.astype(BF16)


def _qk_prep(proj, gain_row, cos_t, sina_t, sinb_t, tm):
    n = proj.shape[0]
    w = 2 * DA_HEADS * 2 * DA_QK_DIM
    seg = np.arange(LANES) // DA_QK_DIM
    bd = jnp.asarray(seg[:, None] == seg[None, :], BF16)
    tab = pl.BlockSpec((tm, LANES), lambda i: (i, 0))
    return pl.pallas_call(
        _qk_prep_kernel,
        grid=(n // tm,),
        in_specs=[pl.BlockSpec((tm, w), lambda i: (i, COL_DA_Q // w)),
                  pl.BlockSpec((1, w), lambda i: (0, 0)),
                  tab, tab, tab,
                  pl.BlockSpec((LANES, LANES), lambda i: (0, 0))],
        out_specs=pl.BlockSpec((tm, w), lambda i: (i, 0)),
        out_shape=jax.ShapeDtypeStruct((n, w), BF16),
        compiler_params=_cparams(("parallel",)),
        name="qk_prep",
    )(proj, gain_row, cos_t, sina_t, sinb_t, bd)


def _diff_attn_kernel(q_ref, k_ref, v_ref, lq1_ref, lk1_ref, lq2_ref, lk2_ref, sg_ref, o_ref,
                      m_ref, l_ref, acc_ref, *, tq, group, lambda_init):
    qi = pl.program_id(2)
    q = q_ref[...]
    lane = lax.broadcasted_iota(jnp.int32, q.shape, 1)
    zero = jnp.zeros_like(q)
    q2 = jnp.concatenate([jnp.where(lane < DA_QK_DIM, q, zero),
                          jnp.where(lane >= DA_QK_DIM, q, zero)], axis=0)
    m_ref[...] = jnp.full(m_ref.shape, -jnp.inf, F32)
    l_ref[...] = jnp.zeros(l_ref.shape, F32)
    acc_ref[...] = jnp.zeros(acc_ref.shape, F32)

    def tile(start, width, masked):
        kb = k_ref[pl.ds(start, width), :]
        vb = v_ref[pl.ds(start, width), :]
        n_lt = width // LANES
        s = lax.dot_general(q2, kb, _NT, preferred_element_type=F32)
        if masked:
            row = lax.broadcasted_iota(jnp.int32, (tq, width), 0)
            col = lax.broadcasted_iota(jnp.int32, (tq, width), 1)
            keep = col <= row
            s = jnp.where(jnp.concatenate([keep, keep], axis=0), s, -jnp.inf)
        st = [s[:, j * LANES:(j + 1) * LANES] for j in range(n_lt)]
        smax = st[0]
        for j in range(1, n_lt):
            smax = jnp.maximum(smax, st[j])
        m_prev = m_ref[...]
        m_new = jnp.maximum(m_prev, jnp.max(smax, axis=-1, keepdims=True))
        alpha = jnp.exp2(m_prev - m_new)
        ps = [jnp.exp2(t - m_new) for t in st]
        lsum = ps[0]
        for j in range(1, n_lt):
            lsum = lsum + ps[j]
        p = jnp.concatenate([t.astype(BF16) for t in ps], axis=1)
        l_ref[...] = alpha * l_ref[...] + lsum
        acc_ref[...] = alpha * acc_ref[...] + _dot(p, vb)
        m_ref[...] = m_new

    n_group = qi // group

    def body_group(gi, carry):
        tile(pl.multiple_of(gi * (group * tq), group * tq), group * tq, False)
        return carry

    def body_single(ki, carry):
        tile(pl.multiple_of(ki * tq, tq), tq, False)
        return carry

    lax.fori_loop(0, n_group, body_group, 0)
    lax.fori_loop(n_group * group, qi, body_single, 0)
    tile(pl.multiple_of(qi * tq, tq), tq, True)

    lam = (jnp.exp(jnp.sum(lq1_ref[...] * lk1_ref[...], axis=-1, keepdims=True))
           - jnp.exp(jnp.sum(lq2_ref[...] * lk2_ref[...], axis=-1, keepdims=True)) + lambda_init)
    o = acc_ref[...] / jnp.sum(l_ref[...], axis=-1, keepdims=True)
    o = o[:tq] - lam * o[tq:]
    ms = jnp.mean(o * o, axis=-1, keepdims=True)
    o = o * lax.rsqrt(ms + RMS_EPS) * sg_ref[...] * (1.0 - lambda_init)
    o_ref[...] = o.astype(o_ref.dtype)


def _diff_attn(qk, proj, lq1, lk1, lq2, lk2, subln, bsz, seq, tq, group, lambda_init):
    n = qk.shape[0]
    nq = seq // tq
    kcol = (DA_HEADS * 2 * DA_QK_DIM) // LANES
    vcol = COL_DA_V // LANES
    vec = pl.BlockSpec((1, DA_QK_DIM), lambda b, h, i: (0, 0))
    return pl.pallas_call(
        functools.partial(_diff_attn_kernel, tq=tq, group=group, lambda_init=lambda_init),
        grid=(bsz, DA_HEADS, nq),
        in_specs=[pl.BlockSpec((tq, LANES), lambda b, h, i: (b * nq + i, h)),
                  pl.BlockSpec((seq, LANES), lambda b, h, i: (b, kcol + h)),
                  pl.BlockSpec((seq, LANES), lambda b, h, i: (b, vcol + h)),
                  vec, vec, vec, vec,
                  pl.BlockSpec((1, LANES), lambda b, h, i: (0, 0))],
        out_specs=pl.BlockSpec((tq, LANES), lambda b, h, i: (b * nq + i, h)),
        out_shape=jax.ShapeDtypeStruct((n, DA_HEADS * LANES), BF16),
        scratch_shapes=[pltpu.VMEM((2 * tq, LANES), F32), pltpu.VMEM((2 * tq, LANES), F32),
                        pltpu.VMEM((2 * tq, LANES), F32)],
        compiler_params=_cparams(("parallel", "parallel", "arbitrary")),
        name="diff_attn",
    )(qk, qk, proj, lq1, lk1, lq2, lk2, subln)


def _stick_break_kernel(q_ref, k_ref, v_ref, o_ref, r_ref, acc_ref, *, tq, kw, group):
    qi = pl.program_id(2)
    q = q_ref[...]
    lane = lax.broadcasted_iota(jnp.int32, q.shape, 1)
    zero = jnp.zeros_like(q)
    q2 = jnp.concatenate([jnp.where(lane < SB_HEAD_DIM, q, zero),
                          jnp.where(lane >= SB_HEAD_DIM, q, zero)], axis=0)
    incl = jnp.where(lax.broadcasted_iota(jnp.int32, (kw, kw), 0)
                     >= lax.broadcasted_iota(jnp.int32, (kw, kw), 1), 1.0, 0.0).astype(BF16)
    n_lt = kw // LANES
    n_diag = tq // kw
    r_ref[...] = jnp.zeros(r_ref.shape, F32)
    acc_ref[...] = jnp.zeros(acc_ref.shape, F32)

    def run(starts, offsets):
        r = r_ref[...]
        total = None
        for start, off in zip(starts, offsets):
            u = lax.dot_general(q2, k_ref[pl.ds(start, kw), :], _NT, preferred_element_type=F32)
            neg_abs = lax.bitcast_convert_type(
                lax.bitcast_convert_type(u, jnp.int32) | jnp.int32(-2 ** 31), F32)
            sp = jnp.maximum(u, 0.0) + jnp.log(1.0 + jnp.exp2(neg_abs)) * LOG2E
            mask = None
            if off is not None:
                keep = (lax.broadcasted_iota(jnp.int32, (tq, kw), 1) + off
                        < lax.broadcasted_iota(jnp.int32, (tq, kw), 0))
                mask = jnp.concatenate([keep, keep], axis=0)
                sp = jnp.where(mask, sp, 0.0)
            cum = _dot(sp.astype(BF16), incl)
            w = jnp.exp2(u - (cum + jnp.concatenate([r] * n_lt, axis=1)))
            if mask is not None:
                w = jnp.where(mask, w, 0.0)
            part = _dot(w.astype(BF16), v_ref[pl.ds(start, kw), :])
            total = part if total is None else total + part
            r = r + cum[:, 0:1]
        acc_ref[...] += total
        r_ref[...] = r

    base = qi * tq
    run([pl.multiple_of(base + (n_diag - 1 - j) * kw, kw) for j in range(n_diag)],
        [(n_diag - 1 - j) * kw for j in range(n_diag)])

    n_below = qi * n_diag
    rem = n_below % group

    def body_single(j, carry):
        run([pl.multiple_of((n_below - 1 - j) * kw, kw)], [None])
        return carry

    def body_group(gi, carry):
        top = n_below - rem - gi * group
        run([pl.multiple_of((top - 1 - j) * kw, kw) for j in range(group)], [None] * group)
        return carry

    lax.fori_loop(0, rem, body_single, 0)
    lax.fori_loop(0, n_below // group, body_group, 0)
    lane_o = lax.broadcasted_iota(jnp.int32, (tq, LANES), 1)
    o_ref[...] = jnp.where(lane_o < SB_HEAD_DIM, acc_ref[:tq, :], acc_ref[tq:, :]).astype(o_ref.dtype)


def _stick_break(proj, bsz, seq, tq, kw, group):
    n = proj.shape[0]
    nq = seq // tq
    pairs = (SB_HEADS * SB_HEAD_DIM) // LANES
    qc, kc, vc = COL_SB_Q // LANES, COL_SB_K // LANES, COL_SB_V // LANES
    return pl.pallas_call(
        functools.partial(_stick_break_kernel, tq=tq, kw=kw, group=group),
        grid=(bsz, pairs, nq),
        in_specs=[pl.BlockSpec((tq, LANES), lambda b, p, i: (b * nq + i, qc + p)),
                  pl.BlockSpec((seq, LANES), lambda b, p, i: (b, kc + p)),
                  pl.BlockSpec((seq, LANES), lambda b, p, i: (b, vc + p))],
        out_specs=pl.BlockSpec((tq, LANES), lambda b, p, i: (b * nq + i, p)),
        out_shape=jax.ShapeDtypeStruct((n, pairs * LANES), BF16),
        scratch_shapes=[pltpu.VMEM((2 * tq, LANES), F32), pltpu.VMEM((2 * tq, LANES), F32)],
        compiler_params=_cparams(("parallel", "parallel", "arbitrary")),
        name="stick_break",
    )(proj, proj, proj)


def _split_bf16(x):
    hi = x.astype(BF16)
    lo = (x - hi.astype(F32)).astype(BF16)
    return hi, lo


def _hgrn_kernel(f_ref, i_ref, q_ref, g_ref, lb_ref, gain_ref, o_ref,
                 st_ref, b_ref, dec_ref, qq_ref, kk_ref, vv_ref, oo_ref, qe_ref, ke_ref, gm_ref, uu_ref,
                 *, tb):
    @pl.when(pl.program_id(1) == 0)
    def _():
        st_ref[...] = jnp.zeros(st_ref.shape, F32)

    nsub = tb // HG_SUB
    z = f_ref[...].astype(F32)
    lb = lb_ref[...]
    sp = jnp.maximum(z, 0.0) + jnp.log(1.0 + jnp.exp(-jnp.abs(z)))
    log_sig = z - sp
    a = jnp.log(lb)
    c = jnp.log(1.0 - lb) + log_sig
    mx = jnp.maximum(a, c)
    log_f = mx + jnp.log(jnp.exp(a - mx) + jnp.exp(c - mx))
    key = (1.0 - lb) * jax.nn.sigmoid(-z)

    row = lax.broadcasted_iota(jnp.int32, (tb, tb), 0)
    col = lax.broadcasted_iota(jnp.int32, (tb, tb), 1)
    sub_shift = HG_SUB.bit_length() - 1
    dim_shift = HG_DIM.bit_length() - 1
    same = (row >> sub_shift) == (col >> sub_shift)
    tri = jnp.where(same & (col <= row), 1.0, 0.0).astype(BF16)
    blk = jnp.where(same, 1.0, 0.0).astype(BF16)
    hi, lo = _split_bf16(log_f)
    b = _dot(tri, hi) + _dot(tri, lo)
    e = _dot(blk, hi) + _dot(blk, lo)
    qf = q_ref[...].astype(F32)
    b_ref[...] = b
    qq_ref[...] = qf
    kk_ref[...] = key
    vv_ref[...] = i_ref[...].astype(F32)
    qe_ref[...] = (qf * jnp.exp(b)).astype(BF16)
    ke_ref[...] = (key * jnp.exp(e - b)).astype(BF16)
    dec_ref[...] = jnp.exp(e)

    seg_r = lax.broadcasted_iota(jnp.int32, (HG_COLS, HG_COLS), 0) >> dim_shift
    seg_c = lax.broadcasted_iota(jnp.int32, (HG_COLS, HG_COLS), 1) >> dim_shift
    head_mask = seg_r == seg_c
    ones_bd = jnp.where(head_mask, 1.0, 0.0).astype(BF16)
    sel = jnp.where(lax.broadcasted_iota(jnp.int32, (HG_SUB, HG_SUB * HG_SUB), 0)
                    == (lax.broadcasted_iota(jnp.int32, (HG_SUB, HG_SUB * HG_SUB), 1) >> sub_shift),
                    1.0, 0.0).astype(BF16)
    srow = lax.broadcasted_iota(jnp.int32, (HG_SUB, HG_COLS), 0)

    def intra(ci, slot):
        r0 = pl.multiple_of(ci * HG_SUB, HG_SUB)
        bi = b_ref[pl.ds(r0, HG_SUB), :]
        qi = qq_ref[pl.ds(r0, HG_SUB), :]
        ki = kk_ref[pl.ds(r0, HG_SUB), :]
        vi = vv_ref[pl.ds(r0, HG_SUB), :]
        for t in range(HG_SUB):
            d = jnp.exp(jnp.minimum(bi[t:t + 1, :] - bi, 0.0))
            g = jnp.where(srow <= t, qi[t:t + 1, :] * ki * d, 0.0)
            gm_ref[slot, t * HG_SUB:(t + 1) * HG_SUB, :] = g.astype(BF16)
        sc = _dot(gm_ref[slot], ones_bd)
        vt = jnp.concatenate([vi] * HG_SUB, axis=0)
        oo_ref[pl.ds(r0, HG_SUB), :] = _dot(sel, (sc * vt).astype(BF16))
        upd = lax.dot_general(vi.astype(BF16), ke_ref[pl.ds(r0, HG_SUB), :], _TN,
                              preferred_element_type=F32)
        uu_ref[ci] = jnp.where(head_mask, upd, 0.0)

    def intra_group(gi, carry):
        for slot in range(HG_UNROLL):
            intra(gi * HG_UNROLL + slot, slot)
        return carry

    lax.fori_loop(0, nsub // HG_UNROLL, intra_group, 0)

    st = st_ref[...]
    for ci in range(nsub):
        rows = slice(ci * HG_SUB, (ci + 1) * HG_SUB)
        oo_ref[rows, :] += lax.dot_general(qe_ref[rows, :], st.astype(BF16), _NT,
                                           preferred_element_type=F32)
        st = st * dec_ref[ci * HG_SUB:ci * HG_SUB + 1, :] + uu_ref[ci]
    st_ref[...] = st

    o = oo_ref[...]
    ms = _dot((o * o).astype(BF16), ones_bd) * (1.0 / HG_DIM)
    gate = g_ref[...].astype(F32)
    gate = gate * jax.nn.sigmoid(gate)
    o_ref[...] = (o * lax.rsqrt(ms + RMS_EPS) * gain_ref[...] * gate).astype(o_ref.dtype)


def _hgrn(proj, lb_row, gain_row, bsz, seq, tb):
    n = proj.shape[0]
    nb = seq // tb
    w = HG_COLS
    cf, ci, cq, cg = COL_HG_F // w, COL_HG_I // w, COL_HG_Q // w, COL_HG_G // w

    def col(cc):
        return pl.BlockSpec((tb, w), lambda b, i: (b * nb + i, cc))

    rowspec = pl.BlockSpec((1, w), lambda b, i: (0, 0))
    big = pltpu.VMEM((tb, w), F32)
    return pl.pallas_call(
        functools.partial(_hgrn_kernel, tb=tb),
        grid=(bsz, nb),
        in_specs=[col(cf), col(ci), col(cq), col(cg), rowspec, rowspec],
        out_specs=pl.BlockSpec((tb, w), lambda b, i: (b * nb + i, 0)),
        out_shape=jax.ShapeDtypeStruct((n, w), BF16),
        scratch_shapes=[pltpu.VMEM((w, w), F32), big, big, big, big, big, big,
                        pltpu.VMEM((tb, w), BF16), pltpu.VMEM((tb, w), BF16),
                        pltpu.VMEM((HG_UNROLL, HG_SUB * HG_SUB, w), BF16),
                        pltpu.VMEM((tb // HG_SUB, w, w), F32)],
        compiler_params=_cparams(("parallel", "arbitrary")),
        name="hgrn2",
    )(proj, proj, proj, proj, lb_row, gain_row)


def _s5_kernel(u_ref, bblk_ref, lev_re_ref, lev_im_ref, pw_re_ref, pw_im_ref, cblk_ref, d_ref,
               gw_ref, gb_ref, o_ref, cr_ref, ci_ref, xr_ref, xi_ref, *, tb):
    @pl.when(pl.program_id(1) == 0)
    def _():
        cr_ref[...] = jnp.zeros(cr_ref.shape, F32)
        ci_ref[...] = jnp.zeros(ci_ref.shape, F32)

    u = u_ref[...]
    bu = _dot(u, bblk_ref[...])
    xr = bu[:, :S5_NSTATE]
    xi = bu[:, S5_NSTATE:]
    row = lax.broadcasted_iota(jnp.int32, xr.shape, 0)
    sub_lev = S5_SUB.bit_length() - 1
    for j in range(sub_lev):
        d = 1 << j
        ar = lev_re_ref[j:j + 1, :]
        ai = lev_im_ref[j:j + 1, :]
        keep = row >= d
        sr = jnp.where(keep, pltpu.roll(xr, d, 0), 0.0)
        si = jnp.where(keep, pltpu.roll(xi, d, 0), 0.0)
        xr, xi = xr + ar * sr - ai * si, xi + ar * si + ai * sr
    cr = cr_ref[...]
    ci = ci_ref[...]
    pr = pw_re_ref[...]
    pi = pw_im_ref[...]
    gr = xr[:S5_SUB] + pr * cr - pi * ci
    gi = xi[:S5_SUB] + pr * ci + pi * cr
    xr_ref[:S5_SUB, :] = gr
    xi_ref[:S5_SUB, :] = gi
    ar = lev_re_ref[sub_lev:sub_lev + 1, :]
    ai = lev_im_ref[sub_lev:sub_lev + 1, :]
    for g in range(1, tb // S5_SUB):
        rows = slice(g * S5_SUB, (g + 1) * S5_SUB)
        gr, gi = xr[rows] + ar * gr - ai * gi, xi[rows] + ar * gi + ai * gr
        xr_ref[rows, :] = gr
        xi_ref[rows, :] = gi
    cr_ref[...] = gr[S5_SUB - 1:S5_SUB, :]
    ci_ref[...] = gi[S5_SUB - 1:S5_SUB, :]
    y = (_dot(xr_ref[...].astype(BF16), cblk_ref[:S5_NSTATE, :])
         + _dot(xi_ref[...].astype(BF16), cblk_ref[S5_NSTATE:, :]) + d_ref[...] * u.astype(F32))
    y = jax.nn.gelu(y)
    zg = _dot(y.astype(BF16), gw_ref[...]) + gb_ref[...]
    o_ref[...] = (y * jax.nn.sigmoid(zg)).astype(o_ref.dtype)


def _s5_params(lam_re, lam_im, log_step, b_re, b_im, c_re, c_im, tb):
    lam_re = jnp.minimum(lam_re.astype(F32), S5_EIG_CLIP)
    lam_im = lam_im.astype(F32)
    step = jnp.exp(log_step.astype(F32))[:, None]
    mag = jnp.exp(lam_re * step)
    phase = lam_im * step
    a_re = mag * jnp.cos(phase)
    a_im = mag * jnp.sin(phase)
    denom = lam_re * lam_re + lam_im * lam_im
    num_re = a_re - 1.0
    gam_re = (num_re * lam_re + a_im * lam_im) / denom
    gam_im = (a_im * lam_re - num_re * lam_im) / denom
    b_re = b_re.astype(F32)
    b_im = b_im.astype(F32)
    bb_re = gam_re[..., None] * b_re - gam_im[..., None] * b_im
    bb_im = gam_re[..., None] * b_im + gam_im[..., None] * b_re
    eye = jnp.eye(S5_GROUPS, dtype=F32)

    def in_blk(bb):
        return jnp.einsum('gnc,gh->gchn', bb, eye).reshape(S5_WIDTH, S5_NSTATE)

    def out_blk(cc):
        return jnp.einsum('gcn,gh->gnhc', cc.astype(F32), eye).reshape(S5_NSTATE, S5_WIDTH)

    bblk = jnp.concatenate([in_blk(bb_re), in_blk(bb_im)], axis=1).astype(BF16)
    cblk = jnp.concatenate([out_blk(c_re), -out_blk(c_im)], axis=0).astype(BF16)
    ar = a_re.reshape(1, S5_NSTATE)
    ai = a_im.reshape(1, S5_NSTATE)
    sub_lev = S5_SUB.bit_length() - 1
    lev_re, lev_im = [ar], [ai]
    pw_re, pw_im = ar, ai
    for _ in range(sub_lev):
        sr, si = lev_re[-1], lev_im[-1]
        pw_re, pw_im = (jnp.concatenate([pw_re, pw_re * sr - pw_im * si], axis=0),
                        jnp.concatenate([pw_im, pw_re * si + pw_im * sr], axis=0))
        lev_re.append(sr * sr - si * si)
        lev_im.append(2.0 * sr * si)
    lev_re = jnp.concatenate(lev_re, axis=0)
    lev_im = jnp.concatenate(lev_im, axis=0)
    return bblk, cblk, lev_re, lev_im, pw_re, pw_im


def _s5(proj, params, d_row, glu_w, glu_b, bsz, seq, tb):
    n = proj.shape[0]
    nb = seq // tb
    bblk, cblk, lev_re, lev_im, pw_re, pw_im = params
    nlev = lev_re.shape[0]
    ucol = COL_S5_U // S5_WIDTH

    def full(shape):
        return pl.BlockSpec(shape, lambda b, i: (0,) * len(shape))

    return pl.pallas_call(
        functools.partial(_s5_kernel, tb=tb),
        grid=(bsz, nb),
        in_specs=[pl.BlockSpec((tb, S5_WIDTH), lambda b, i: (b * nb + i, ucol)),
                  full((S5_WIDTH, 2 * S5_NSTATE)),
                  full((nlev, S5_NSTATE)), full((nlev, S5_NSTATE)),
                  full((S5_SUB, S5_NSTATE)), full((S5_SUB, S5_NSTATE)),
                  full((2 * S5_NSTATE, S5_WIDTH)),
                  full((1, S5_WIDTH)), full((S5_WIDTH, S5_WIDTH)), full((1, S5_WIDTH))],
        out_specs=pl.BlockSpec((tb, S5_WIDTH), lambda b, i: (b * nb + i, 0)),
        out_shape=jax.ShapeDtypeStruct((n, S5_WIDTH), BF16),
        scratch_shapes=[pltpu.VMEM((1, S5_NSTATE), F32), pltpu.VMEM((1, S5_NSTATE), F32),
                        pltpu.VMEM((tb, S5_NSTATE), F32), pltpu.VMEM((tb, S5_NSTATE), F32)],
        compiler_params=_cparams(("parallel", "arbitrary")),
        name="s5",
    )(proj, bblk, lev_re, lev_im, pw_re, pw_im, cblk, d_row, glu_w, glu_b)


def _merge_kernel(x_ref, ya_ref, yb_ref, yc_ref, yd_ref, gate_ref, wa_ref, wb_ref, wc_ref, wd_ref,
                  wo_ref, gn_ref, wrh_ref, wrl_ref, br_ref, xo_ref, hx_ref):
    d = x_ref.shape[1]
    merged = None
    for i, (y_ref, w_ref) in enumerate(((ya_ref, wa_ref), (yb_ref, wb_ref),
                                        (yc_ref, wc_ref), (yd_ref, wd_ref))):
        gate = 0.5 * jnp.tanh(0.5 * gate_ref[:, i * d:(i + 1) * d].astype(F32)) + 0.5
        term = gate * _dot(y_ref[...], w_ref[...])
        merged = term if merged is None else merged + term
    xn = x_ref[...] + _dot(merged.astype(BF16), wo_ref[...])
    xo_ref[...] = xn
    ms = jnp.mean(xn * xn, axis=-1, keepdims=True)
    h = xn * lax.rsqrt(ms + RMS_EPS) * gn_ref[...]
    hi, lo = _split_bf16(h)
    hx_ref[:, :d] = hi.astype(F32)
    wrh = wrh_ref[...]
    hx_ref[:, d:] = _dot(hi, wrh) + _dot(lo, wrh) + _dot(hi, wrl_ref[...]) + br_ref[...]


def _merge(x2, ya, yb, yc, yd, proj, wa, wb, wc, wd, wo, gn_row, wr_hi, wr_lo, br_row, tm):
    n, d = x2.shape
    assert COL_GATE % (4 * d) == 0
    gcol = COL_GATE // (4 * d)

    def rows(width, cc=0):
        return pl.BlockSpec((tm, width), lambda i: (i, cc))

    def full(arr):
        return pl.BlockSpec(arr.shape, lambda i: (0, 0))

    return pl.pallas_call(
        _merge_kernel,
        grid=(n // tm,),
        in_specs=[rows(d), rows(ya.shape[1]), rows(yb.shape[1]), rows(yc.shape[1]), rows(yd.shape[1]),
                  rows(4 * d, gcol),
                  full(wa), full(wb), full(wc), full(wd), full(wo), full(gn_row),
                  full(wr_hi), full(wr_lo), full(br_row)],
        out_specs=[rows(d), rows(d + LANES)],
        out_shape=[jax.ShapeDtypeStruct((n, d), F32), jax.ShapeDtypeStruct((n, d + LANES), F32)],
        compiler_params=_cparams(("parallel",)),
        name="merge",
    )(x2, ya, yb, yc, yd, proj, wa, wb, wc, wd, wo, gn_row, wr_hi, wr_lo, br_row)


def _first_index(mask, lane):
    return jnp.min(jnp.where(mask, lane, float(LANES)), axis=-1, keepdims=True)


def _group_onehot(logits):
    lane = lax.broadcasted_iota(jnp.int32, logits.shape, 1).astype(F32)
    gl = jnp.where(lane < N_GROUPS, logits, -jnp.inf)
    g_idx = _first_index(gl == jnp.max(gl, axis=-1, keepdims=True), lane)
    return jnp.where(lane == g_idx, 1.0, 0.0)


def _route_kernel(lg_ref, tri_ref, pos_ref, meta_ref, cnt_ref, run_ref, off_ref, *, row_block):
    phase = pl.program_id(0)
    i = pl.program_id(1)
    onehot = _group_onehot(lg_ref[...])
    lane = lax.broadcasted_iota(jnp.int32, (1, LANES), 1)

    @pl.when((phase == 0) & (i == 0))
    def _():
        cnt_ref[...] = jnp.zeros(cnt_ref.shape, F32)

    @pl.when(phase == 0)
    def _():
        cnt_ref[...] += jnp.sum(onehot, axis=0, keepdims=True)
        pos_ref[...] = jnp.zeros(pos_ref.shape, jnp.int32)

    @pl.when((phase == 1) & (i == 0))
    def _():
        padded = jnp.floor((cnt_ref[...] + (row_block - 1)) * (1.0 / row_block)) * row_block
        off = jnp.zeros((1, LANES), F32)
        acc = jnp.zeros((1, 1), F32)
        for g in range(1, N_GROUPS):
            acc = acc + jnp.sum(jnp.where(lane == g - 1, padded, 0.0), axis=-1, keepdims=True)
            off = off + jnp.where(lane == g, acc, 0.0)
        off_ref[...] = off
        run_ref[...] = jnp.zeros(run_ref.shape, F32)
        meta_ref[...] = jnp.zeros(meta_ref.shape, F32)
        meta_ref[0:1, :] = cnt_ref[...]
        meta_ref[1:2, :] = off

    @pl.when(phase == 1)
    def _():
        before = _dot(tri_ref[...], onehot.astype(BF16))
        slot = jnp.sum(onehot * (before + run_ref[...] + off_ref[...]), axis=-1, keepdims=True)
        pos_ref[...] = slot.astype(jnp.int32)
        run_ref[...] += jnp.sum(onehot, axis=0, keepdims=True)


def _moe_route(hx, d, tr, row_block):
    n = hx.shape[0]
    nb = n // tr
    tri = jnp.asarray(np.tril(np.ones((tr, tr), np.float32), -1), BF16)
    pos, meta = pl.pallas_call(
        functools.partial(_route_kernel, row_block=row_block),
        grid=(2, nb),
        in_specs=[pl.BlockSpec((tr, LANES), lambda p, i: (i, d // LANES)),
                  pl.BlockSpec((tr, tr), lambda p, i: (0, 0))],
        out_specs=[pl.BlockSpec((tr, 1), lambda p, i: (p * nb + i, 0)),
                   pl.BlockSpec((8, LANES), lambda p, i: (0, 0))],
        out_shape=[jax.ShapeDtypeStruct((2 * n, 1), jnp.int32), jax.ShapeDtypeStruct((8, LANES), F32)],
        scratch_shapes=[pltpu.VMEM((1, LANES), F32)] * 3,
        compiler_params=_cparams(("arbitrary", "arbitrary")),
        name="moe_route",
    )(hx, tri)
    return pos[n:], meta


def _row_dma_wait(src_hbm, dst_ref, sem, rows):
    pltpu.make_async_copy(src_hbm.at[pl.ds(0, rows)], dst_ref.at[pl.ds(0, rows)], sem).wait()


def _scatter_kernel(pos_ref, hx_ref, hs_in_hbm, hs_hbm, sem, *, tr, burst):
    del hs_in_hbm

    def one_burst(bi, carry):
        def issue(t8, c):
            for k in range(MOE_DMA_UNROLL):
                row = bi * burst + t8 * MOE_DMA_UNROLL + k
                pltpu.make_async_copy(hx_ref.at[pl.ds(row, 1)],
                                      hs_hbm.at[pl.ds(pos_ref[0, 0, row], 1)], sem).start(priority=k % 2)
            return c
        lax.fori_loop(0, burst // MOE_DMA_UNROLL, issue, 0)
        _row_dma_wait(hx_ref, hs_hbm, sem, burst)
        return carry

    lax.fori_loop(0, tr // burst, one_burst, 0)


def _moe_scatter(hx, pos3, n_sorted, tr):
    n, w = hx.shape
    nb = n // tr
    return pl.pallas_call(
        functools.partial(_scatter_kernel, tr=tr, burst=min(MOE_DMA_BURST, tr)),
        grid=(nb,),
        in_specs=[pl.BlockSpec((1, 1, tr), lambda i: (i, 0, 0), memory_space=pltpu.SMEM),
                  pl.BlockSpec((tr, w), lambda i: (i, 0)),
                  pl.BlockSpec(memory_space=pl.ANY)],
        out_specs=pl.BlockSpec(memory_space=pl.ANY),
        out_shape=jax.ShapeDtypeStruct((n_sorted, w), F32),
        scratch_shapes=[pltpu.SemaphoreType.DMA],
        input_output_aliases={2: 0},
        compiler_params=_cparams(("arbitrary",)),
        name="moe_scatter",
    )(pos3, hx, jnp.zeros((n_sorted, w), F32))


def _experts_kernel(gmap_ref, valid_ref, hs_ref, wg_ref, wu_ref, wd_ref, o_ref, cw_ref, acc_ref, *, d):
    i = pl.program_id(0)
    j = pl.program_id(1)
    valid = valid_ref[i] > 0

    @pl.when(j == 0)
    def _():
        logits = hs_ref[:, d:]
        lane = lax.broadcasted_iota(jnp.int32, logits.shape, 1).astype(F32)
        neg = -jnp.inf
        gl = jnp.where(lane < N_GROUPS, logits, neg)
        gmax = jnp.max(gl, axis=-1, keepdims=True)
        gsum = jnp.sum(jnp.exp(gl - gmax), axis=-1, keepdims=True)
        g_val = 1.0 / gsum
        g_idx = _first_index(gl == gmax, lane)
        lo = N_GROUPS + EXPERTS_PER_GROUP * g_idx
        el = jnp.where((lane >= lo) & (lane < lo + EXPERTS_PER_GROUP), logits, neg)
        emax = jnp.max(el, axis=-1, keepdims=True)
        esum = jnp.sum(jnp.exp(el - emax), axis=-1, keepdims=True)
        i1 = _first_index(el == emax, lane)
        el2 = jnp.where(lane == i1, neg, el)
        e2max = jnp.max(el2, axis=-1, keepdims=True)
        i2 = _first_index(el2 == e2max, lane)
        p1 = 1.0 / esum
        p2 = jnp.exp(e2max - emax) / esum
        tot = p1 + p2
        cw_ref[...] = (jnp.where(lane == i1, g_val * (p1 / tot), 0.0)
                       + jnp.where(lane == i2, g_val * (p2 / tot), 0.0))
        acc_ref[...] = jnp.zeros(acc_ref.shape, F32)

    @pl.when(valid)
    def _():
        h = hs_ref[:, :d].astype(BF16)
        a = _dot(h, jnp.concatenate([wg_ref[0, 0], wg_ref[0, 1]], axis=1))
        hid = (a * jax.nn.sigmoid(a)) * _dot(h, jnp.concatenate([wu_ref[0, 0], wu_ref[0, 1]], axis=1))
        first = N_GROUPS + EXPERTS_PER_GROUP * gmap_ref[i] + 2 * j
        lane = lax.broadcasted_iota(jnp.int32, cw_ref.shape, 1)
        cw_all = cw_ref[...]
        cw0 = jnp.sum(jnp.where(lane == first, cw_all, 0.0), axis=-1, keepdims=True)
        cw1 = jnp.sum(jnp.where(lane == first + 1, cw_all, 0.0), axis=-1, keepdims=True)
        de = hid.shape[1] // 2
        hcol = lax.broadcasted_iota(jnp.int32, hid.shape, 1)
        hid = hid * jnp.where(hcol < de, cw0, cw1)
        acc_ref[...] += _dot(hid.astype(BF16), wd_ref[0])

    @pl.when(j == pl.num_programs(1) - 1)
    def _():
        o_ref[...] = acc_ref[...]


def _moe_experts(hs, gmap, valid, wg, wu, wd, d, row_block):
    n_sorted, w = hs.shape
    _, _, _, de = wg.shape
    de2 = 2 * de
    pairs = EXPERTS_PER_GROUP // 2
    grid_spec = pltpu.PrefetchScalarGridSpec(
        num_scalar_prefetch=2,
        grid=(n_sorted // row_block, pairs),
        in_specs=[pl.BlockSpec((row_block, w), lambda i, j, gm, va: (i, 0)),
                  pl.BlockSpec((1, 2, d, de), lambda i, j, gm, va: (gm[i] * pairs + j, 0, 0, 0)),
                  pl.BlockSpec((1, 2, d, de), lambda i, j, gm, va: (gm[i] * pairs + j, 0, 0, 0)),
                  pl.BlockSpec((1, de2, d), lambda i, j, gm, va: (gm[i] * pairs + j, 0, 0))],
        out_specs=pl.BlockSpec((row_block, d), lambda i, j, gm, va: (i, 0)),
        scratch_shapes=[pltpu.VMEM((row_block, LANES), F32), pltpu.VMEM((row_block, d), F32)])
    return pl.pallas_call(
        functools.partial(_experts_kernel, d=d),
        grid_spec=grid_spec,
        out_shape=jax.ShapeDtypeStruct((n_sorted, d), F32),
        compiler_params=_cparams(("arbitrary", "arbitrary")),
        name="moe_experts",
    )(gmap, valid, hs, wg, wu, wd)


def _combine_kernel(pos_ref, ys_hbm, x_ref, o_ref, buf_ref, sem, *, tr, burst):
    def one_burst(bi, carry):
        def issue(t8, c):
            for k in range(MOE_DMA_UNROLL):
                row = bi * burst + t8 * MOE_DMA_UNROLL + k
                pltpu.make_async_copy(ys_hbm.at[pl.ds(pos_ref[0, 0, row], 1)],
                                      buf_ref.at[pl.ds(row, 1)], sem).start(priority=k % 2)
            return c
        lax.fori_loop(0, burst // MOE_DMA_UNROLL, issue, 0)
        _row_dma_wait(ys_hbm, buf_ref, sem, burst)
        return carry

    lax.fori_loop(0, tr // burst, one_burst, 0)
    o_ref[...] = x_ref[...] + buf_ref[...]


def _moe_combine(ys, pos3, x2, tr):
    n, d = x2.shape
    return pl.pallas_call(
        functools.partial(_combine_kernel, tr=tr, burst=min(MOE_DMA_BURST, tr)),
        grid=(n // tr,),
        in_specs=[pl.BlockSpec((1, 1, tr), lambda i: (i, 0, 0), memory_space=pltpu.SMEM),
                  pl.BlockSpec(memory_space=pl.ANY),
                  pl.BlockSpec((tr, d), lambda i: (i, 0))],
        out_specs=pl.BlockSpec((tr, d), lambda i: (i, 0)),
        out_shape=jax.ShapeDtypeStruct((n, d), F32),
        scratch_shapes=[pltpu.VMEM((tr, d), F32), pltpu.SemaphoreType.DMA],
        compiler_params=_cparams(("arbitrary",)),
        name="moe_combine",
    )(pos3, ys, x2)


def _pair_cols(w):
    ne, d, de = w.shape
    return w.astype(BF16).reshape(ne // 2, 2, d, de)


def _pair_rows(w):
    ne, de, d = w.shape
    return w.astype(BF16).reshape(ne // 2, 2 * de, d)


def _moe(hx, x2, wg, wu, wd, tr):
    n, d = x2.shape
    row_block = min(MOE_ROW_BLOCK, n)
    n_blocks = n // row_block + N_GROUPS
    pos, meta = _moe_route(hx, d, tr, row_block)
    counts = meta[0, :N_GROUPS]
    ends = meta[1, :N_GROUPS] + jnp.ceil(counts / row_block) * row_block
    starts = jnp.arange(n_blocks, dtype=F32) * row_block
    grp = jnp.sum((starts[:, None] >= ends[None, :]).astype(jnp.int32), axis=1)
    valid = (grp < N_GROUPS).astype(jnp.int32)
    gmap = jnp.minimum(grp, N_GROUPS - 1)
    pos3 = pos.reshape(n // tr, 1, tr)
    hs = _moe_scatter(hx, pos3, n_blocks * row_block, tr)
    ys = _moe_experts(hs, gmap, valid, wg, wu, wd, d, row_block)
    return _moe_combine(ys, pos3, x2, tr)


def _pick(n, pref):
    t = min(n, pref)
    while n % t:
        t //= 2
    return t


def kernel(x, positions, norm_mix, w_in, da_q_gain, da_k_gain, da_lambda_q1, da_lambda_k1,
           da_lambda_q2, da_lambda_k2, da_subln_gain, hg_lower_bounds, hg_out_gain,
           s5_lambda_re, s5_lambda_im, s5_log_step, s5_b_re, s5_b_im, s5_c_re, s5_c_im,
           s5_d, s5_glu_w, s5_glu_b, w_branch_attn, w_branch_sb, w_branch_hgrn, w_branch_s5,
           w_out, norm_ffn, router_group_w, router_group_b, router_expert_w, router_expert_b,
           expert_w_gate, expert_w_up, expert_w_down):
    bsz, seq, d = x.shape
    depth = w_in.shape[0]
    n = bsz * seq
    assert w_in.shape[2] == IN_COLS and seq % 128 == 0
    tm = _pick(n, 1024)
    tq = _pick(seq, 256)
    tq_da = _pick(seq, DA_TQ)
    tb = _pick(seq, 256)

    x2 = x.reshape(n, d).astype(F32)
    cos_t, sina_t, sinb_t = _rope_tables(positions.reshape(n, 1).astype(jnp.int32), tm)

    lb_all = jnp.cumsum(jax.nn.softmax(hg_lower_bounds.astype(F32), axis=0), axis=0)
    lb_all = lb_all - lb_all[0:1]

    for l in range(depth):
        lambda_init = DA_LAMBDA_INIT_BASE - DA_LAMBDA_INIT_SCALE * math.exp(-DA_LAMBDA_INIT_RATE * l)
        w_f = w_in[l].astype(F32)
        sbq = COL_SB_Q - COL_DA_Q
        w_l = jnp.concatenate([w_f[:, REF_GATE_START:], w_f[:, :sbq],
                               w_f[:, sbq:sbq + SB_HEADS * SB_HEAD_DIM] * (SB_HEAD_DIM ** -0.5 * LOG2E),
                               w_f[:, sbq + SB_HEADS * SB_HEAD_DIM:REF_GATE_START]], axis=1)
        proj = _norm_proj(x2, norm_mix[l].astype(F32)[None, :], w_l.astype(BF16), _pick(n, NP_TM), 768)

        qk_gain = jnp.concatenate([jnp.tile(da_q_gain[l].astype(F32), 2 * DA_HEADS),
                                   jnp.tile(da_k_gain[l].astype(F32), 2 * DA_HEADS)])[None, :]
        qk = _qk_prep(proj, qk_gain, cos_t, sina_t, sinb_t, tm)
        y_a = _diff_attn(qk, proj,
                         da_lambda_q1[l].astype(F32)[None, :], da_lambda_k1[l].astype(F32)[None, :],
                         da_lambda_q2[l].astype(F32)[None, :], da_lambda_k2[l].astype(F32)[None, :],
                         da_subln_gain[l].astype(F32)[None, :], bsz, seq, tq_da, DA_GROUP, lambda_init)
        y_b = _stick_break(proj, bsz, seq, _pick(seq, SB_TQ), SB_KW, SB_GROUP)
        y_c = _hgrn(proj, lb_all[l][None, :], jnp.tile(hg_out_gain[l].astype(F32), HG_HEADS)[None, :],
                    bsz, seq, tb)
        s5p = _s5_params(s5_lambda_re[l], s5_lambda_im[l], s5_log_step[l], s5_b_re[l], s5_b_im[l],
                         s5_c_re[l], s5_c_im[l], tb)
        y_d = _s5(proj, s5p, s5_d[l].astype(F32)[None, :], s5_glu_w[l].astype(BF16),
                  s5_glu_b[l].astype(F32)[None, :], bsz, seq, tb)

        wr = jnp.concatenate([router_group_w[l], router_expert_w[l]], axis=1).astype(F32)
        wr = jnp.pad(wr, ((0, 0), (0, LANES - wr.shape[1])))
        wr_hi = wr.astype(BF16)
        wr_lo = (wr - wr_hi.astype(F32)).astype(BF16)
        br = jnp.concatenate([router_group_b[l], router_expert_b[l]]).astype(F32)
        br = jnp.pad(br, (0, LANES - br.shape[0]))[None, :]
        x2, hx = _merge(x2, y_a, y_b, y_c, y_d, proj,
                        w_branch_attn[l].astype(BF16), w_branch_sb[l].astype(BF16),
                        w_branch_hgrn[l].astype(BF16), w_branch_s5[l].astype(BF16),
                        w_out[l].astype(BF16), norm_ffn[l].astype(F32)[None, :],
                        wr_hi, wr_lo, br, _pick(n, 512))
        x2 = _moe(hx, x2, _pair_cols(expert_w_gate[l]), _pair_cols(expert_w_up[l]),
                  _pair_rows(expert_w_down[l]), tm)

    return x2.reshape(bsz, seq, d).astype(x.dtype)
```

```python
import functools
import math

import jax
import jax.numpy as jnp
import numpy as np
from jax import lax
from jax.experimental import pallas as pl
from jax.experimental.pallas import tpu as pltpu

F32 = jnp.float32
BF16 = jnp.bfloat16

RMS_EPS = 1e-6
LANES = 128
LOG2E = 1.4426950408889634

DA_HEADS = 4
DA_QK_DIM = 64
ROPE_THETA = 500000.0
ROPE_DIM = DA_QK_DIM // 4
DA_LAMBDA_INIT_BASE = 0.8
DA_LAMBDA_INIT_SCALE = 0.6
DA_LAMBDA_INIT_RATE = 0.3

SB_HEADS = 4
SB_HEAD_DIM = 64

HG_HEADS = 4
HG_DIM = 64
HG_COLS = HG_HEADS * HG_DIM
HG_SUB = 16
HG_UNROLL = 4

S5_GROUPS = 16
S5_GROUP_CH = 16
S5_STATE = 64
S5_WIDTH = S5_GROUPS * S5_GROUP_CH
S5_NSTATE = S5_GROUPS * S5_STATE
S5_EIG_CLIP = -1e-4
S5_SUB = 8

N_GROUPS = 4
EXPERTS_PER_GROUP = 4
N_EXPERTS = N_GROUPS * EXPERTS_PER_GROUP

REF_GATE_START = 3584
COL_GATE = 0
COL_DA_Q = 4096
COL_DA_K = 4608
COL_DA_V = 5120
COL_SB_Q = 5632
COL_SB_K = 5888
COL_SB_V = 6144
COL_HG_F = 6400
COL_HG_I = 6656
COL_HG_Q = 6912
COL_HG_G = 7168
COL_S5_U = 7424
IN_COLS = 7680

VMEM_LIMIT = 48 * 1024 * 1024

DA_TQ = 1024
DA_KW = 1024
DA_GROUP = 1
SB_TQ = 512
SB_KW = 256
SB_GROUP = 4
MOE_ROW_BLOCK = 512
MOE_DMA_BURST = 256
MOE_DMA_UNROLL = 8
NP_TM = 2048

_NT = (((1,), (1,)), ((), ()))
_TN = (((0,), (0,)), ((), ()))


def _cparams(sem):
    return pltpu.CompilerParams(dimension_semantics=sem, vmem_limit_bytes=VMEM_LIMIT)


def _dot(a, b):
    return jnp.dot(a, b, preferred_element_type=F32)


def _norm_proj_kernel(x_ref, g_ref, w_ref, o_ref, h_ref):
    @pl.when(pl.program_id(1) == 0)
    def _():
        x = x_ref[...]
        ms = jnp.mean(x * x, axis=-1, keepdims=True)
        h_ref[...] = (x * lax.rsqrt(ms + RMS_EPS) * g_ref[...]).astype(BF16)

    o_ref[...] = _dot(h_ref[...], w_ref[...]).astype(o_ref.dtype)


def _norm_proj(x2, gain, w, tm, tn):
    n, d = x2.shape
    cols = w.shape[1]
    return pl.pallas_call(
        _norm_proj_kernel,
        grid=(n // tm, cols // tn),
        in_specs=[pl.BlockSpec((tm, d), lambda i, j: (i, 0)),
                  pl.BlockSpec((1, d), lambda i, j: (0, 0)),
                  pl.BlockSpec((d, tn), lambda i, j: (0, j))],
        out_specs=pl.BlockSpec((tm, tn), lambda i, j: (i, j)),
        out_shape=jax.ShapeDtypeStruct((n, cols), BF16),
        scratch_shapes=[pltpu.VMEM((tm, d), BF16)],
        compiler_params=_cparams(("parallel", "arbitrary")),
        name="norm_proj",
    )(x2, gain, w)


def _rope_kernel(pos_ref, invf_ref, sa_ref, sb_ref, cos_ref, sina_ref, sinb_ref):
    ang = pos_ref[...].astype(F32) * invf_ref[...]
    c = jnp.cos(ang)
    s = jnp.sin(ang)
    cos_ref[...] = c
    sina_ref[...] = s * sa_ref[...]
    sinb_ref[...] = s * sb_ref[...]


def _rope_tables(pos_col, tm):
    n = pos_col.shape[0]
    half = ROPE_DIM // 2
    inv_freq = jnp.exp(-math.log(ROPE_THETA) * jnp.arange(half, dtype=F32) * (2.0 / ROPE_DIM))
    lane = np.arange(LANES) % DA_QK_DIM
    invf = jnp.where(lane < ROPE_DIM, inv_freq[lane % half], 0.0).astype(F32)[None, :]
    sgn_a = jnp.asarray(np.where(lane < half, -1.0, 0.0), F32)[None, :]
    sgn_b = jnp.asarray(np.where((lane >= half) & (lane < ROPE_DIM), 1.0, 0.0), F32)[None, :]
    row = pl.BlockSpec((1, LANES), lambda i: (0, 0))
    tab = pl.BlockSpec((tm, LANES), lambda i: (i, 0))
    shp = jax.ShapeDtypeStruct((n, LANES), F32)
    return pl.pallas_call(
        _rope_kernel,
        grid=(n // tm,),
        in_specs=[pl.BlockSpec((tm, 1), lambda i: (i, 0)), row, row, row],
        out_specs=[tab, tab, tab],
        out_shape=[shp, shp, shp],
        compiler_params=_cparams(("parallel",)),
        name="rope_tables",
    )(pos_col, invf, sgn_a, sgn_b)


def _qk_prep_kernel(x_ref, gain_ref, cos_ref, sina_ref, sinb_ref, bd_ref, o_ref):
    c = cos_ref[...]
    sa = sina_ref[...]
    sb = sinb_ref[...]
    bd = bd_ref[...]
    n_tiles = x_ref.shape[1] // LANES
    for j in range(n_tiles):
        sl = slice(j * LANES, (j + 1) * LANES)
        t = x_ref[:, sl].astype(F32)
        ss = _dot((t * t).astype(BF16), bd)
        y = t * lax.rsqrt(ss * (1.0 / DA_QK_DIM) + RMS_EPS) * gain_ref[:, sl]
        y = (y * c + pltpu.roll(y, LANES - ROPE_DIM // 2, 1) * sa
             + pltpu.roll(y, ROPE_DIM // 2, 1) * sb)
        if j < n_tiles // 2:
            y = y * (DA_QK_DIM ** -0.5 * LOG2E)
        o_ref[:, sl] = y.astype(BF16)


def _qk_prep(proj, gain_row, cos_t, sina_t, sinb_t, tm):
    n = proj.shape[0]
    w = 2 * DA_HEADS * 2 * DA_QK_DIM
    seg = np.arange(LANES) // DA_QK_DIM
    bd = jnp.asarray(seg[:, None] == seg[None, :], BF16)
    tab = pl.BlockSpec((tm, LANES), lambda i: (i, 0))
    return pl.pallas_call(
        _qk_prep_kernel,
        grid=(n // tm,),
        in_specs=[pl.BlockSpec((tm, w), lambda i: (i, COL_DA_Q // w)),
                  pl.BlockSpec((1, w), lambda i: (0, 0)),
                  tab, tab, tab,
                  pl.BlockSpec((LANES, LANES), lambda i: (0, 0))],
        out_specs=pl.BlockSpec((tm, w), lambda i: (i, 0)),
        out_shape=jax.ShapeDtypeStruct((n, w), BF16),
        compiler_params=_cparams(("parallel",)),
        name="qk_prep",
    )(proj, gain_row, cos_t, sina_t, sinb_t, bd)


def _diff_attn_kernel(q_ref, k_ref, v_ref, lq1_ref, lk1_ref, lq2_ref, lk2_ref, sg_ref, o_ref,
                      m_ref, l_ref, acc_ref, *, tq, kw, group, lambda_init):
    qi = pl.program_id(2)
    q = q_ref[...]
    lane = lax.broadcasted_iota(jnp.int32, q.shape, 1)
    zero = jnp.zeros_like(q)
    q2 = jnp.concatenate([jnp.where(lane < DA_QK_DIM, q, zero),
                          jnp.where(lane >= DA_QK_DIM, q, zero)], axis=0)
    m_ref[...] = jnp.full(m_ref.shape, -jnp.inf, F32)
    l_ref[...] = jnp.zeros(l_ref.shape, F32)
    acc_ref[...] = jnp.zeros(acc_ref.shape, F32)

    n_lt = kw // LANES

    def tile(start, mask_off):
        kb = k_ref[pl.ds(start, kw), :]
        vb = v_ref[pl.ds(start, kw), :]
        s = lax.dot_general(q2, kb, _NT, preferred_element_type=F32)
        if mask_off is not None:
            keep = (lax.broadcasted_iota(jnp.int32, (tq, kw), 1) + mask_off
                    <= lax.broadcasted_iota(jnp.int32, (tq, kw), 0))
            s = jnp.where(jnp.concatenate([keep, keep], axis=0), s, -jnp.inf)
        st = [s[:, j * LANES:(j + 1) * LANES] for j in range(n_lt)]
        smax = st[0]
        for j in range(1, n_lt):
            smax = jnp.maximum(smax, st[j])
        m_prev = m_ref[...]
        m_new = jnp.maximum(m_prev, jnp.max(smax, axis=-1, keepdims=True))
        alpha = jnp.exp2(m_prev - m_new)
        ps = [jnp.exp2(t - m_new) for t in st]
        lsum = ps[0]
        for j in range(1, n_lt):
            lsum = lsum + ps[j]
        p = jnp.concatenate([t.astype(BF16) for t in ps], axis=1)
        l_ref[...] = alpha * l_ref[...] + lsum
        acc_ref[...] = alpha * acc_ref[...] + _dot(p, vb)
        m_ref[...] = m_new

    n_diag = tq // kw
    n_below = qi * n_diag

    def body_group(gi, carry):
        for j in range(group):
            tile(pl.multiple_of((gi * group + j) * kw, kw), None)
        return carry

    def body_single(ki, carry):
        tile(pl.multiple_of(ki * kw, kw), None)
        return carry

    n_group = n_below // group
    lax.fori_loop(0, n_group, body_group, 0)
    lax.fori_loop(n_group * group, n_below, body_single, 0)
    for j in range(n_diag):
        tile(pl.multiple_of(qi * tq + j * kw, kw), j * kw)

    lam = (jnp.exp(jnp.sum(lq1_ref[...] * lk1_ref[...], axis=-1, keepdims=True))
           - jnp.exp(jnp.sum(lq2_ref[...] * lk2_ref[...], axis=-1, keepdims=True)) + lambda_init)
    o = acc_ref[...] / jnp.sum(l_ref[...], axis=-1, keepdims=True)
    o = o[:tq] - lam * o[tq:]
    ms = jnp.mean(o * o, axis=-1, keepdims=True)
    o = o * lax.rsqrt(ms + RMS_EPS) * sg_ref[...] * (1.0 - lambda_init)
    o_ref[...] = o.astype(o_ref.dtype)


def _diff_attn(qk, proj, lq1, lk1, lq2, lk2, subln, bsz, seq, tq, kw, group, lambda_init):
    n = qk.shape[0]
    nq = seq // tq
    kcol = (DA_HEADS * 2 * DA_QK_DIM) // LANES
    vcol = COL_DA_V // LANES
    vec = pl.BlockSpec((1, DA_QK_DIM), lambda b, h, i: (0, 0))
    return pl.pallas_call(
        functools.partial(_diff_attn_kernel, tq=tq, kw=min(kw, tq), group=group, lambda_init=lambda_init),
        grid=(bsz, DA_HEADS, nq),
        in_specs=[pl.BlockSpec((tq, LANES), lambda b, h, i: (b * nq + i, h)),
                  pl.BlockSpec((seq, LANES), lambda b, h, i: (b, kcol + h)),
                  pl.BlockSpec((seq, LANES), lambda b, h, i: (b, vcol + h)),
                  vec, vec, vec, vec,
                  pl.BlockSpec((1, LANES), lambda b, h, i: (0, 0))],
        out_specs=pl.BlockSpec((tq, LANES), lambda b, h, i: (b * nq + i, h)),
        out_shape=jax.ShapeDtypeStruct((n, DA_HEADS * LANES), BF16),
        scratch_shapes=[pltpu.VMEM((2 * tq, LANES), F32), pltpu.VMEM((2 * tq, LANES), F32),
                        pltpu.VMEM((2 * tq, LANES), F32)],
        compiler_params=_cparams(("parallel", "parallel", "arbitrary")),
        name="diff_attn",
    )(qk, qk, proj, lq1, lk1, lq2, lk2, subln)


def _stick_break_kernel(q_ref, k_ref, v_ref, o_ref, r_ref, acc_ref, *, tq, kw, group):
    qi = pl.program_id(2)
    q = q_ref[...]
    lane = lax.broadcasted_iota(jnp.int32, q.shape, 1)
    zero = jnp.zeros_like(q)
    q2 = jnp.concatenate([jnp.where(lane < SB_HEAD_DIM, q, zero),
                          jnp.where(lane >= SB_HEAD_DIM, q, zero)], axis=0)
    incl = jnp.where(lax.broadcasted_iota(jnp.int32, (kw, kw), 0)
                     >= lax.broadcasted_iota(jnp.int32, (kw, kw), 1), 1.0, 0.0).astype(BF16)
    n_lt = kw // LANES
    n_diag = tq // kw
    r_ref[...] = jnp.zeros(r_ref.shape, F32)
    acc_ref[...] = jnp.zeros(acc_ref.shape, F32)

    def run(starts, offsets):
        r = r_ref[...]
        total = None
        for start, off in zip(starts, offsets):
            u = lax.dot_general(q2, k_ref[pl.ds(start, kw), :], _NT, preferred_element_type=F32)
            neg_abs = lax.bitcast_convert_type(
                lax.bitcast_convert_type(u, jnp.int32) | jnp.int32(-2 ** 31), F32)
            sp = jnp.maximum(u, 0.0) + jnp.log(1.0 + jnp.exp2(neg_abs)) * LOG2E
            mask = None
            if off is not None:
                keep = (lax.broadcasted_iota(jnp.int32, (tq, kw), 1) + off
                        < lax.broadcasted_iota(jnp.int32, (tq, kw), 0))
                mask = jnp.concatenate([keep, keep], axis=0)
                sp = jnp.where(mask, sp, 0.0)
            cum = _dot(sp.astype(BF16), incl)
            w = jnp.exp2(u - (cum + jnp.concatenate([r] * n_lt, axis=1)))
            if mask is not None:
                w = jnp.where(mask, w, 0.0)
            part = _dot(w.astype(BF16), v_ref[pl.ds(start, kw), :])
            total = part if total is None else total + part
            r = r + cum[:, 0:1]
        acc_ref[...] += total
        r_ref[...] = r

    base = qi * tq
    run([pl.multiple_of(base + (n_diag - 1 - j) * kw, kw) for j in range(n_diag)],
        [(n_diag - 1 - j) * kw for j in range(n_diag)])

    n_below = qi * n_diag
    rem = n_below % group

    def body_single(j, carry):
        run([pl.multiple_of((n_below - 1 - j) * kw, kw)], [None])
        return carry

    def body_group(gi, carry):
        top = n_below - rem - gi * group
        run([pl.multiple_of((top - 1 - j) * kw, kw) for j in range(group)], [None] * group)
        return carry

    lax.fori_loop(0, rem, body_single, 0)
    lax.fori_loop(0, n_below // group, body_group, 0)
    lane_o = lax.broadcasted_iota(jnp.int32, (tq, LANES), 1)
    o_ref[...] = jnp.where(lane_o < SB_HEAD_DIM, acc_ref[:tq, :], acc_ref[tq:, :]).astype(o_ref.dtype)


def _stick_break(proj, bsz, seq, tq, kw, group):
    n = proj.shape[0]
    nq = seq // tq
    pairs = (SB_HEADS * SB_HEAD_DIM) // LANES
    qc, kc, vc = COL_SB_Q // LANES, COL_SB_K // LANES, COL_SB_V // LANES
    return pl.pallas_call(
        functools.partial(_stick_break_kernel, tq=tq, kw=kw, group=group),
        grid=(bsz, pairs, nq),
        in_specs=[pl.BlockSpec((tq, LANES), lambda b, p, i: (b * nq + i, qc + p)),
                  pl.BlockSpec((seq, LANES), lambda b, p, i: (b, kc + p)),
                  pl.BlockSpec((seq, LANES), lambda b, p, i: (b, vc + p))],
        out_specs=pl.BlockSpec((tq, LANES), lambda b, p, i: (b * nq + i, p)),
        out_shape=jax.ShapeDtypeStruct((n, pairs * LANES), BF16),
        scratch_shapes=[pltpu.VMEM((2 * tq, LANES), F32), pltpu.VMEM((2 * tq, LANES), F32)],
        compiler_params=_cparams(("parallel", "parallel", "arbitrary")),
        name="stick_break",
    )(proj, proj, proj)


def _split_bf16(x):
    hi = x.astype(BF16)
    lo = (x - hi.astype(F32)).astype(BF16)
    return hi, lo


def _hgrn_kernel(f_ref, i_ref, q_ref, g_ref, lb_ref, gain_ref, o_ref,
                 st_ref, b_ref, dec_ref, qq_ref, kk_ref, vv_ref, oo_ref, qe_ref, ke_ref, gm_ref, uu_ref,
                 *, tb):
    @pl.when(pl.program_id(1) == 0)
    def _():
        st_ref[...] = jnp.zeros(st_ref.shape, F32)

    nsub = tb // HG_SUB
    z = f_ref[...].astype(F32)
    lb = lb_ref[...]
    sp = jnp.maximum(z, 0.0) + jnp.log(1.0 + jnp.exp(-jnp.abs(z)))
    log_sig = z - sp
    a = jnp.log(lb)
    c = jnp.log(1.0 - lb) + log_sig
    mx = jnp.maximum(a, c)
    log_f = mx + jnp.log(jnp.exp(a - mx) + jnp.exp(c - mx))
    key = (1.0 - lb) * jax.nn.sigmoid(-z)

    row = lax.broadcasted_iota(jnp.int32, (tb, tb), 0)
    col = lax.broadcasted_iota(jnp.int32, (tb, tb), 1)
    sub_shift = HG_SUB.bit_length() - 1
    dim_shift = HG_DIM.bit_length() - 1
    same = (row >> sub_shift) == (col >> sub_shift)
    tri = jnp.where(same & (col <= row), 1.0, 0.0).astype(BF16)
    blk = jnp.where(same, 1.0, 0.0).astype(BF16)
    hi, lo = _split_bf16(log_f)
    b = _dot(tri, hi) + _dot(tri, lo)
    e = _dot(blk, hi) + _dot(blk, lo)
    qf = q_ref[...].astype(F32)
    b_ref[...] = b
    qq_ref[...] = qf
    kk_ref[...] = key
    vv_ref[...] = i_ref[...].astype(F32)
    qe_ref[...] = (qf * jnp.exp(b)).astype(BF16)
    ke_ref[...] = (key * jnp.exp(e - b)).astype(BF16)
    dec_ref[...] = jnp.exp(e)

    seg_r = lax.broadcasted_iota(jnp.int32, (HG_COLS, HG_COLS), 0) >> dim_shift
    seg_c = lax.broadcasted_iota(jnp.int32, (HG_COLS, HG_COLS), 1) >> dim_shift
    head_mask = seg_r == seg_c
    ones_bd = jnp.where(head_mask, 1.0, 0.0).astype(BF16)
    sel = jnp.where(lax.broadcasted_iota(jnp.int32, (HG_SUB, HG_SUB * HG_SUB), 0)
                    == (lax.broadcasted_iota(jnp.int32, (HG_SUB, HG_SUB * HG_SUB), 1) >> sub_shift),
                    1.0, 0.0).astype(BF16)
    srow = lax.broadcasted_iota(jnp.int32, (HG_SUB, HG_COLS), 0)

    def intra(ci, slot):
        r0 = pl.multiple_of(ci * HG_SUB, HG_SUB)
        bi = b_ref[pl.ds(r0, HG_SUB), :]
        qi = qq_ref[pl.ds(r0, HG_SUB), :]
        ki = kk_ref[pl.ds(r0, HG_SUB), :]
        vi = vv_ref[pl.ds(r0, HG_SUB), :]
        for t in range(HG_SUB):
            d = jnp.exp(jnp.minimum(bi[t:t + 1, :] - bi, 0.0))
            g = jnp.where(srow <= t, qi[t:t + 1, :] * ki * d, 0.0)
            gm_ref[slot, t * HG_SUB:(t + 1) * HG_SUB, :] = g.astype(BF16)
        sc = _dot(gm_ref[slot], ones_bd)
        vt = jnp.concatenate([vi] * HG_SUB, axis=0)
        oo_ref[pl.ds(r0, HG_SUB), :] = _dot(sel, (sc * vt).astype(BF16))
        upd = lax.dot_general(vi.astype(BF16), ke_ref[pl.ds(r0, HG_SUB), :], _TN,
                              preferred_element_type=F32)
        uu_ref[ci] = jnp.where(head_mask, upd, 0.0)

    def intra_group(gi, carry):
        for slot in range(HG_UNROLL):
            intra(gi * HG_UNROLL + slot, slot)
        return carry

    lax.fori_loop(0, nsub // HG_UNROLL, intra_group, 0)

    st = st_ref[...]
    for ci in range(nsub):
        rows = slice(ci * HG_SUB, (ci + 1) * HG_SUB)
        oo_ref[rows, :] += lax.dot_general(qe_ref[rows, :], st.astype(BF16), _NT,
                                           preferred_element_type=F32)
        st = st * dec_ref[ci * HG_SUB:ci * HG_SUB + 1, :] + uu_ref[ci]
    st_ref[...] = st

    o = oo_ref[...]
    ms = _dot((o * o).astype(BF16), ones_bd) * (1.0 / HG_DIM)
    gate = g_ref[...].astype(F32)
    gate = gate * jax.nn.sigmoid(gate)
    o_ref[...] = (o * lax.rsqrt(ms + RMS_EPS) * gain_ref[...] * gate).astype(o_ref.dtype)


def _hgrn(proj, lb_row, gain_row, bsz, seq, tb):
    n = proj.shape[0]
    nb = seq // tb
    w = HG_COLS
    cf, ci, cq, cg = COL_HG_F // w, COL_HG_I // w, COL_HG_Q // w, COL_HG_G // w

    def col(cc):
        return pl.BlockSpec((tb, w), lambda b, i: (b * nb + i, cc))

    rowspec = pl.BlockSpec((1, w), lambda b, i: (0, 0))
    big = pltpu.VMEM((tb, w), F32)
    return pl.pallas_call(
        functools.partial(_hgrn_kernel, tb=tb),
        grid=(bsz, nb),
        in_specs=[col(cf), col(ci), col(cq), col(cg), rowspec, rowspec],
        out_specs=pl.BlockSpec((tb, w), lambda b, i: (b * nb + i, 0)),
        out_shape=jax.ShapeDtypeStruct((n, w), BF16),
        scratch_shapes=[pltpu.VMEM((w, w), F32), big, big, big, big, big, big,
                        pltpu.VMEM((tb, w), BF16), pltpu.VMEM((tb, w), BF16),
                        pltpu.VMEM((HG_UNROLL, HG_SUB * HG_SUB, w), BF16),
                        pltpu.VMEM((tb // HG_SUB, w, w), F32)],
        compiler_params=_cparams(("parallel", "arbitrary")),
        name="hgrn2",
    )(proj, proj, proj, proj, lb_row, gain_row)


def _s5_kernel(u_ref, bblk_ref, lev_re_ref, lev_im_ref, pw_re_ref, pw_im_ref, cblk_ref, d_ref,
               gw_ref, gb_ref, o_ref, cr_ref, ci_ref, xr_ref, xi_ref, *, tb):
    @pl.when(pl.program_id(1) == 0)
    def _():
        cr_ref[...] = jnp.zeros(cr_ref.shape, F32)
        ci_ref[...] = jnp.zeros(ci_ref.shape, F32)

    u = u_ref[...]
    bu = _dot(u, bblk_ref[...])
    xr = bu[:, :S5_NSTATE]
    xi = bu[:, S5_NSTATE:]
    row = lax.broadcasted_iota(jnp.int32, xr.shape, 0)
    sub_lev = S5_SUB.bit_length() - 1
    for j in range(sub_lev):
        d = 1 << j
        ar = lev_re_ref[j:j + 1, :]
        ai = lev_im_ref[j:j + 1, :]
        keep = row >= d
        sr = jnp.where(keep, pltpu.roll(xr, d, 0), 0.0)
        si = jnp.where(keep, pltpu.roll(xi, d, 0), 0.0)
        xr, xi = xr + ar * sr - ai * si, xi + ar * si + ai * sr
    cr = cr_ref[...]
    ci = ci_ref[...]
    pr = pw_re_ref[...]
    pi = pw_im_ref[...]
    gr = xr[:S5_SUB] + pr * cr - pi * ci
    gi = xi[:S5_SUB] + pr * ci + pi * cr
    xr_ref[:S5_SUB, :] = gr
    xi_ref[:S5_SUB, :] = gi
    ar = lev_re_ref[sub_lev:sub_lev + 1, :]
    ai = lev_im_ref[sub_lev:sub_lev + 1, :]
    for g in range(1, tb // S5_SUB):
        rows = slice(g * S5_SUB, (g + 1) * S5_SUB)
        gr, gi = xr[rows] + ar * gr - ai * gi, xi[rows] + ar * gi + ai * gr
        xr_ref[rows, :] = gr
        xi_ref[rows, :] = gi
    cr_ref[...] = gr[S5_SUB - 1:S5_SUB, :]
    ci_ref[...] = gi[S5_SUB - 1:S5_SUB, :]
    y = (_dot(xr_ref[...].astype(BF16), cblk_ref[:S5_NSTATE, :])
         + _dot(xi_ref[...].astype(BF16), cblk_ref[S5_NSTATE:, :]) + d_ref[...] * u.astype(F32))
    y = jax.nn.gelu(y)
    zg = _dot(y.astype(BF16), gw_ref[...]) + gb_ref[...]
    o_ref[...] = (y * jax.nn.sigmoid(zg)).astype(o_ref.dtype)


def _s5_params(lam_re, lam_im, log_step, b_re, b_im, c_re, c_im, tb):
    lam_re = jnp.minimum(lam_re.astype(F32), S5_EIG_CLIP)
    lam_im = lam_im.astype(F32)
    step = jnp.exp(log_step.astype(F32))[:, None]
    mag = jnp.exp(lam_re * step)
    phase = lam_im * step
    a_re = mag * jnp.cos(phase)
    a_im = mag * jnp.sin(phase)
    denom = lam_re * lam_re + lam_im * lam_im
    num_re = a_re - 1.0
    gam_re = (num_re * lam_re + a_im * lam_im) / denom
    gam_im = (a_im * lam_re - num_re * lam_im) / denom
    b_re = b_re.astype(F32)
    b_im = b_im.astype(F32)
    bb_re = gam_re[..., None] * b_re - gam_im[..., None] * b_im
    bb_im = gam_re[..., None] * b_im + gam_im[..., None] * b_re
    eye = jnp.eye(S5_GROUPS, dtype=F32)

    def in_blk(bb):
        return jnp.einsum('gnc,gh->gchn', bb, eye).reshape(S5_WIDTH, S5_NSTATE)

    def out_blk(cc):
        return jnp.einsum('gcn,gh->gnhc', cc.astype(F32), eye).reshape(S5_NSTATE, S5_WIDTH)

    bblk = jnp.concatenate([in_blk(bb_re), in_blk(bb_im)], axis=1).astype(BF16)
    cblk = jnp.concatenate([out_blk(c_re), -out_blk(c_im)], axis=0).astype(BF16)
    ar = a_re.reshape(1, S5_NSTATE)
    ai = a_im.reshape(1, S5_NSTATE)
    sub_lev = S5_SUB.bit_length() - 1
    lev_re, lev_im = [ar], [ai]
    pw_re, pw_im = ar, ai
    for _ in range(sub_lev):
        sr, si = lev_re[-1], lev_im[-1]
        pw_re, pw_im = (jnp.concatenate([pw_re, pw_re * sr - pw_im * si], axis=0),
                        jnp.concatenate([pw_im, pw_re * si + pw_im * sr], axis=0))
        lev_re.append(sr * sr - si * si)
        lev_im.append(2.0 * sr * si)
    lev_re = jnp.concatenate(lev_re, axis=0)
    lev_im = jnp.concatenate(lev_im, axis=0)
    return bblk, cblk, lev_re, lev_im, pw_re, pw_im


def _s5(proj, params, d_row, glu_w, glu_b, bsz, seq, tb):
    n = proj.shape[0]
    nb = seq // tb
    bblk, cblk, lev_re, lev_im, pw_re, pw_im = params
    nlev = lev_re.shape[0]
    ucol = COL_S5_U // S5_WIDTH

    def full(shape):
        return pl.BlockSpec(shape, lambda b, i: (0,) * len(shape))

    return pl.pallas_call(
        functools.partial(_s5_kernel, tb=tb),
        grid=(bsz, nb),
        in_specs=[pl.BlockSpec((tb, S5_WIDTH), lambda b, i: (b * nb + i, ucol)),
                  full((S5_WIDTH, 2 * S5_NSTATE)),
                  full((nlev, S5_NSTATE)), full((nlev, S5_NSTATE)),
                  full((S5_SUB, S5_NSTATE)), full((S5_SUB, S5_NSTATE)),
                  full((2 * S5_NSTATE, S5_WIDTH)),
                  full((1, S5_WIDTH)), full((S5_WIDTH, S5_WIDTH)), full((1, S5_WIDTH))],
        out_specs=pl.BlockSpec((tb, S5_WIDTH), lambda b, i: (b * nb + i, 0)),
        out_shape=jax.ShapeDtypeStruct((n, S5_WIDTH), BF16),
        scratch_shapes=[pltpu.VMEM((1, S5_NSTATE), F32), pltpu.VMEM((1, S5_NSTATE), F32),
                        pltpu.VMEM((tb, S5_NSTATE), F32), pltpu.VMEM((tb, S5_NSTATE), F32)],
        compiler_params=_cparams(("parallel", "arbitrary")),
        name="s5",
    )(proj, bblk, lev_re, lev_im, pw_re, pw_im, cblk, d_row, glu_w, glu_b)


def _merge_kernel(x_ref, ya_ref, yb_ref, yc_ref, yd_ref, gate_ref, wa_ref, wb_ref, wc_ref, wd_ref,
                  wo_ref, gn_ref, wrh_ref, wrl_ref, br_ref, xo_ref, hx_ref):
    d = x_ref.shape[1]
    merged = None
    for i, (y_ref, w_ref) in enumerate(((ya_ref, wa_ref), (yb_ref, wb_ref),
                                        (yc_ref, wc_ref), (yd_ref, wd_ref))):
        gate = 0.5 * jnp.tanh(0.5 * gate_ref[:, i * d:(i + 1) * d].astype(F32)) + 0.5
        term = gate * _dot(y_ref[...], w_ref[...])
        merged = term if merged is None else merged + term
    xn = x_ref[...] + _dot(merged.astype(BF16), wo_ref[...])
    xo_ref[...] = xn
    ms = jnp.mean(xn * xn, axis=-1, keepdims=True)
    h = xn * lax.rsqrt(ms + RMS_EPS) * gn_ref[...]
    hi, lo = _split_bf16(h)
    hx_ref[:, :d] = hi.astype(F32)
    wrh = wrh_ref[...]
    hx_ref[:, d:] = _dot(hi, wrh) + _dot(lo, wrh) + _dot(hi, wrl_ref[...]) + br_ref[...]


def _merge(x2, ya, yb, yc, yd, proj, wa, wb, wc, wd, wo, gn_row, wr_hi, wr_lo, br_row, tm):
    n, d = x2.shape
    assert COL_GATE % (4 * d) == 0
    gcol = COL_GATE // (4 * d)

    def rows(width, cc=0):
        return pl.BlockSpec((tm, width), lambda i: (i, cc))

    def full(arr):
        return pl.BlockSpec(arr.shape, lambda i: (0, 0))

    return pl.pallas_call(
        _merge_kernel,
        grid=(n // tm,),
        in_specs=[rows(d), rows(ya.shape[1]), rows(yb.shape[1]), rows(yc.shape[1]), rows(yd.shape[1]),
                  rows(4 * d, gcol),
                  full(wa), full(wb), full(wc), full(wd), full(wo), full(gn_row),
                  full(wr_hi), full(wr_lo), full(br_row)],
        out_specs=[rows(d), rows(d + LANES)],
        out_shape=[jax.ShapeDtypeStruct((n, d), F32), jax.ShapeDtypeStruct((n, d + LANES), F32)],
        compiler_params=_cparams(("parallel",)),
        name="merge",
    )(x2, ya, yb, yc, yd, proj, wa, wb, wc, wd, wo, gn_row, wr_hi, wr_lo, br_row)


def _first_index(mask, lane):
    return jnp.min(jnp.where(mask, lane, float(LANES)), axis=-1, keepdims=True)


def _group_onehot(logits):
    lane = lax.broadcasted_iota(jnp.int32, logits.shape, 1).astype(F32)
    gl = jnp.where(lane < N_GROUPS, logits, -jnp.inf)
    g_idx = _first_index(gl == jnp.max(gl, axis=-1, keepdims=True), lane)
    return jnp.where(lane == g_idx, 1.0, 0.0)


def _route_kernel(lg_ref, tri_ref, pos_ref, meta_ref, cnt_ref, run_ref, off_ref, *, row_block):
    phase = pl.program_id(0)
    i = pl.program_id(1)
    onehot = _group_onehot(lg_ref[...])
    lane = lax.broadcasted_iota(jnp.int32, (1, LANES), 1)

    @pl.when((phase == 0) & (i == 0))
    def _():
        cnt_ref[...] = jnp.zeros(cnt_ref.shape, F32)

    @pl.when(phase == 0)
    def _():
        cnt_ref[...] += jnp.sum(onehot, axis=0, keepdims=True)
        pos_ref[...] = jnp.zeros(pos_ref.shape, jnp.int32)

    @pl.when((phase == 1) & (i == 0))
    def _():
        padded = jnp.floor((cnt_ref[...] + (row_block - 1)) * (1.0 / row_block)) * row_block
        off = jnp.zeros((1, LANES), F32)
        acc = jnp.zeros((1, 1), F32)
        for g in range(1, N_GROUPS):
            acc = acc + jnp.sum(jnp.where(lane == g - 1, padded, 0.0), axis=-1, keepdims=True)
            off = off + jnp.where(lane == g, acc, 0.0)
        off_ref[...] = off
        run_ref[...] = jnp.zeros(run_ref.shape, F32)
        meta_ref[...] = jnp.zeros(meta_ref.shape, F32)
        meta_ref[0:1, :] = cnt_ref[...]
        meta_ref[1:2, :] = off

    @pl.when(phase == 1)
    def _():
        before = _dot(tri_ref[...], onehot.astype(BF16))
        slot = jnp.sum(onehot * (before + run_ref[...] + off_ref[...]), axis=-1, keepdims=True)
        pos_ref[...] = slot.astype(jnp.int32)
        run_ref[...] += jnp.sum(onehot, axis=0, keepdims=True)


def _moe_route(hx, d, tr, row_block):
    n = hx.shape[0]
    nb = n // tr
    tri = jnp.asarray(np.tril(np.ones((tr, tr), np.float32), -1), BF16)
    pos, meta = pl.pallas_call(
        functools.partial(_route_kernel, row_block=row_block),
        grid=(2, nb),
        in_specs=[pl.BlockSpec((tr, LANES), lambda p, i: (i, d // LANES)),
                  pl.BlockSpec((tr, tr), lambda p, i: (0, 0))],
        out_specs=[pl.BlockSpec((tr, 1), lambda p, i: (p * nb + i, 0)),
                   pl.BlockSpec((8, LANES), lambda p, i: (0, 0))],
        out_shape=[jax.ShapeDtypeStruct((2 * n, 1), jnp.int32), jax.ShapeDtypeStruct((8, LANES), F32)],
        scratch_shapes=[pltpu.VMEM((1, LANES), F32)] * 3,
        compiler_params=_cparams(("arbitrary", "arbitrary")),
        name="moe_route",
    )(hx, tri)
    return pos[n:], meta


def _row_dma_wait(src_hbm, dst_ref, sem, rows):
    pltpu.make_async_copy(src_hbm.at[pl.ds(0, rows)], dst_ref.at[pl.ds(0, rows)], sem).wait()


def _burst_pipeline(n_bursts, issue_burst, wait_burst):
    for b in range(n_bursts):
        issue_burst(b, b % 2)
        if b >= 1:
            wait_burst(b - 1, (b - 1) % 2)
    wait_burst(n_bursts - 1, (n_bursts - 1) % 2)


def _scatter_kernel(pos_ref, hx_ref, hs_in_hbm, hs_hbm, sems, *, tr, burst):
    del hs_in_hbm

    def issue_burst(b, slot):
        def issue(t8, c):
            for k in range(MOE_DMA_UNROLL):
                row = b * burst + t8 * MOE_DMA_UNROLL + k
                pltpu.make_async_copy(hx_ref.at[pl.ds(row, 1)], hs_hbm.at[pl.ds(pos_ref[0, 0, row], 1)],
                                      sems.at[slot]).start(priority=k % 2)
            return c
        lax.fori_loop(0, burst // MOE_DMA_UNROLL, issue, 0)

    def wait_burst(b, slot):
        _row_dma_wait(hx_ref, hs_hbm, sems.at[slot], burst)

    _burst_pipeline(tr // burst, issue_burst, wait_burst)


def _moe_scatter(hx, pos3, n_sorted, tr):
    n, w = hx.shape
    nb = n // tr
    return pl.pallas_call(
        functools.partial(_scatter_kernel, tr=tr, burst=min(MOE_DMA_BURST, tr)),
        grid=(nb,),
        in_specs=[pl.BlockSpec((1, 1, tr), lambda i: (i, 0, 0), memory_space=pltpu.SMEM),
                  pl.BlockSpec((tr, w), lambda i: (i, 0)),
                  pl.BlockSpec(memory_space=pl.ANY)],
        out_specs=pl.BlockSpec(memory_space=pl.ANY),
        out_shape=jax.ShapeDtypeStruct((n_sorted, w), F32),
        scratch_shapes=[pltpu.SemaphoreType.DMA((2,))],
        input_output_aliases={2: 0},
        compiler_params=_cparams(("arbitrary",)),
        name="moe_scatter",
    )(pos3, hx, jnp.zeros((n_sorted, w), F32))


def _experts_kernel(gmap_ref, valid_ref, hs_ref, wg_ref, wu_ref, wd_ref, o_ref, cw_ref, acc_ref, *, d):
    i = pl.program_id(0)
    j = pl.program_id(1)
    valid = valid_ref[i] > 0

    @pl.when(j == 0)
    def _():
        logits = hs_ref[:, d:]
        lane = lax.broadcasted_iota(jnp.int32, logits.shape, 1).astype(F32)
        neg = -jnp.inf
        gl = jnp.where(lane < N_GROUPS, logits, neg)
        gmax = jnp.max(gl, axis=-1, keepdims=True)
        gsum = jnp.sum(jnp.exp(gl - gmax), axis=-1, keepdims=True)
        g_val = 1.0 / gsum
        g_idx = _first_index(gl == gmax, lane)
        lo = N_GROUPS + EXPERTS_PER_GROUP * g_idx
        el = jnp.where((lane >= lo) & (lane < lo + EXPERTS_PER_GROUP), logits, neg)
        emax = jnp.max(el, axis=-1, keepdims=True)
        esum = jnp.sum(jnp.exp(el - emax), axis=-1, keepdims=True)
        i1 = _first_index(el == emax, lane)
        el2 = jnp.where(lane == i1, neg, el)
        e2max = jnp.max(el2, axis=-1, keepdims=True)
        i2 = _first_index(el2 == e2max, lane)
        p1 = 1.0 / esum
        p2 = jnp.exp(e2max - emax) / esum
        tot = p1 + p2
        cw_ref[...] = (jnp.where(lane == i1, g_val * (p1 / tot), 0.0)
                       + jnp.where(lane == i2, g_val * (p2 / tot), 0.0))
        acc_ref[...] = jnp.zeros(acc_ref.shape, F32)

    @pl.when(valid)
    def _():
        h = hs_ref[:, :d].astype(BF16)
        a = _dot(h, jnp.concatenate([wg_ref[0, 0], wg_ref[0, 1]], axis=1))
        hid = (a * jax.nn.sigmoid(a)) * _dot(h, jnp.concatenate([wu_ref[0, 0], wu_ref[0, 1]], axis=1))
        first = N_GROUPS + EXPERTS_PER_GROUP * gmap_ref[i] + 2 * j
        lane = lax.broadcasted_iota(jnp.int32, cw_ref.shape, 1)
        cw_all = cw_ref[...]
        cw0 = jnp.sum(jnp.where(lane == first, cw_all, 0.0), axis=-1, keepdims=True)
        cw1 = jnp.sum(jnp.where(lane == first + 1, cw_all, 0.0), axis=-1, keepdims=True)
        de = hid.shape[1] // 2
        hcol = lax.broadcasted_iota(jnp.int32, hid.shape, 1)
        hid = hid * jnp.where(hcol < de, cw0, cw1)
        acc_ref[...] += _dot(hid.astype(BF16), wd_ref[0])

    @pl.when(j == pl.num_programs(1) - 1)
    def _():
        o_ref[...] = acc_ref[...]


def _moe_experts(hs, gmap, valid, wg, wu, wd, d, row_block):
    n_sorted, w = hs.shape
    _, _, _, de = wg.shape
    de2 = 2 * de
    pairs = EXPERTS_PER_GROUP // 2
    grid_spec = pltpu.PrefetchScalarGridSpec(
        num_scalar_prefetch=2,
        grid=(n_sorted // row_block, pairs),
        in_specs=[pl.BlockSpec((row_block, w), lambda i, j, gm, va: (i, 0)),
                  pl.BlockSpec((1, 2, d, de), lambda i, j, gm, va: (gm[i] * pairs + j, 0, 0, 0)),
                  pl.BlockSpec((1, 2, d, de), lambda i, j, gm, va: (gm[i] * pairs + j, 0, 0, 0)),
                  pl.BlockSpec((1, de2, d), lambda i, j, gm, va: (gm[i] * pairs + j, 0, 0))],
        out_specs=pl.BlockSpec((row_block, d), lambda i, j, gm, va: (i, 0)),
        scratch_shapes=[pltpu.VMEM((row_block, LANES), F32), pltpu.VMEM((row_block, d), F32)])
    return pl.pallas_call(
        functools.partial(_experts_kernel, d=d),
        grid_spec=grid_spec,
        out_shape=jax.ShapeDtypeStruct((n_sorted, d), F32),
        compiler_params=_cparams(("arbitrary", "arbitrary")),
        name="moe_experts",
    )(gmap, valid, hs, wg, wu, wd)


def _combine_kernel(pos_ref, ys_hbm, x_ref, o_ref, buf_ref, sems, *, tr, burst):
    def issue_burst(b, slot):
        def issue(t8, c):
            for k in range(MOE_DMA_UNROLL):
                row = b * burst + t8 * MOE_DMA_UNROLL + k
                pltpu.make_async_copy(ys_hbm.at[pl.ds(pos_ref[0, 0, row], 1)], buf_ref.at[pl.ds(row, 1)],
                                      sems.at[slot]).start(priority=k % 2)
            return c
        lax.fori_loop(0, burst // MOE_DMA_UNROLL, issue, 0)

    def wait_burst(b, slot):
        _row_dma_wait(ys_hbm, buf_ref, sems.at[slot], burst)
        rows = slice(b * burst, (b + 1) * burst)
        o_ref[rows, :] = x_ref[rows, :] + buf_ref[rows, :]

    _burst_pipeline(tr // burst, issue_burst, wait_burst)


def _moe_combine(ys, pos3, x2, tr):
    n, d = x2.shape
    return pl.pallas_call(
        functools.partial(_combine_kernel, tr=tr, burst=min(MOE_DMA_BURST, tr)),
        grid=(n // tr,),
        in_specs=[pl.BlockSpec((1, 1, tr), lambda i: (i, 0, 0), memory_space=pltpu.SMEM),
                  pl.BlockSpec(memory_space=pl.ANY),
                  pl.BlockSpec((tr, d), lambda i: (i, 0))],
        out_specs=pl.BlockSpec((tr, d), lambda i: (i, 0)),
        out_shape=jax.ShapeDtypeStruct((n, d), F32),
        scratch_shapes=[pltpu.VMEM((tr, d), F32), pltpu.SemaphoreType.DMA((2,))],
        compiler_params=_cparams(("arbitrary",)),
        name="moe_combine",
    )(pos3, ys, x2)


def _pair_cols(w):
    ne, d, de = w.shape
    return w.astype(BF16).reshape(ne // 2, 2, d, de)


def _pair_rows(w):
    ne, de, d = w.shape
    return w.astype(BF16).reshape(ne // 2, 2 * de, d)


def _moe(hx, x2, wg, wu, wd, tr):
    n, d = x2.shape
    row_block = min(MOE_ROW_BLOCK, n)
    n_blocks = n // row_block + N_GROUPS
    pos, meta = _moe_route(hx, d, tr, row_block)
    counts = meta[0, :N_GROUPS]
    ends = meta[1, :N_GROUPS] + jnp.ceil(counts / row_block) * row_block
    starts = jnp.arange(n_blocks, dtype=F32) * row_block
    grp = jnp.sum((starts[:, None] >= ends[None, :]).astype(jnp.int32), axis=1)
    valid = (grp < N_GROUPS).astype(jnp.int32)
    gmap = jnp.minimum(grp, N_GROUPS - 1)
    pos3 = pos.reshape(n // tr, 1, tr)
    hs = _moe_scatter(hx, pos3, n_blocks * row_block, tr)
    ys = _moe_experts(hs, gmap, valid, wg, wu, wd, d, row_block)
    return _moe_combine(ys, pos3, x2, tr)


def _pick(n, pref):
    t = min(n, pref)
    while n % t:
        t //= 2
    return t


def kernel(x, positions, norm_mix, w_in, da_q_gain, da_k_gain, da_lambda_q1, da_lambda_k1,
           da_lambda_q2, da_lambda_k2, da_subln_gain, hg_lower_bounds, hg_out_gain,
           s5_lambda_re, s5_lambda_im, s5_log_step, s5_b_re, s5_b_im, s5_c_re, s5_c_im,
           s5_d, s5_glu_w, s5_glu_b, w_branch_attn, w_branch_sb, w_branch_hgrn, w_branch_s5,
           w_out, norm_ffn, router_group_w, router_group_b, router_expert_w, router_expert_b,
           expert_w_gate, expert_w_up, expert_w_down):
    bsz, seq, d = x.shape
    depth = w_in.shape[0]
    n = bsz * seq
    assert w_in.shape[2] == IN_COLS and seq % 128 == 0
    tm = _pick(n, 1024)
    tq = _pick(seq, 256)
    tq_da = _pick(seq, DA_TQ)
    tb = _pick(seq, 256)

    x2 = x.reshape(n, d).astype(F32)
    cos_t, sina_t, sinb_t = _rope_tables(positions.reshape(n, 1).astype(jnp.int32), tm)

    lb_all = jnp.cumsum(jax.nn.softmax(hg_lower_bounds.astype(F32), axis=0), axis=0)
    lb_all = lb_all - lb_all[0:1]

    for l in range(depth):
        lambda_init = DA_LAMBDA_INIT_BASE - DA_LAMBDA_INIT_SCALE * math.exp(-DA_LAMBDA_INIT_RATE * l)
        w_f = w_in[l].astype(F32)
        sbq = COL_SB_Q - COL_DA_Q
        w_l = jnp.concatenate([w_f[:, REF_GATE_START:], w_f[:, :sbq],
                               w_f[:, sbq:sbq + SB_HEADS * SB_HEAD_DIM] * (SB_HEAD_DIM ** -0.5 * LOG2E),
                               w_f[:, sbq + SB_HEADS * SB_HEAD_DIM:REF_GATE_START]], axis=1)
        proj = _norm_proj(x2, norm_mix[l].astype(F32)[None, :], w_l.astype(BF16), _pick(n, NP_TM), 768)

        qk_gain = jnp.concatenate([jnp.tile(da_q_gain[l].astype(F32), 2 * DA_HEADS),
                                   jnp.tile(da_k_gain[l].astype(F32), 2 * DA_HEADS)])[None, :]
        qk = _qk_prep(proj, qk_gain, cos_t, sina_t, sinb_t, tm)
        y_a = _diff_attn(qk, proj,
                         da_lambda_q1[l].astype(F32)[None, :], da_lambda_k1[l].astype(F32)[None, :],
                         da_lambda_q2[l].astype(F32)[None, :], da_lambda_k2[l].astype(F32)[None, :],
                         da_subln_gain[l].astype(F32)[None, :], bsz, seq, tq_da, DA_KW, DA_GROUP, lambda_init)
        y_b = _stick_break(proj, bsz, seq, _pick(seq, SB_TQ), SB_KW, SB_GROUP)
        y_c = _hgrn(proj, lb_all[l][None, :], jnp.tile(hg_out_gain[l].astype(F32), HG_HEADS)[None, :],
                    bsz, seq, tb)
        s5p = _s5_params(s5_lambda_re[l], s5_lambda_im[l], s5_log_step[l], s5_b_re[l], s5_b_im[l],
                         s5_c_re[l], s5_c_im[l], tb)
        y_d = _s5(proj, s5p, s5_d[l].astype(F32)[None, :], s5_glu_w[l].astype(BF16),
                  s5_glu_b[l].astype(F32)[None, :], bsz, seq, tb)

        wr = jnp.concatenate([router_group_w[l], router_expert_w[l]], axis=1).astype(F32)
        wr = jnp.pad(wr, ((0, 0), (0, LANES - wr.shape[1])))
        wr_hi = wr.astype(BF16)
        wr_lo = (wr - wr_hi.astype(F32)).astype(BF16)
        br = jnp.concatenate([router_group_b[l], router_expert_b[l]]).astype(F32)
        br = jnp.pad(br, (0, LANES - br.shape[0]))[None, :]
        x2, hx = _merge(x2, y_a, y_b, y_c, y_d, proj,
                        w_branch_attn[l].astype(BF16), w_branch_sb[l].astype(BF16),
                        w_branch_hgrn[l].astype(BF16), w_branch_s5[l].astype(BF16),
                        w_out[l].astype(BF16), norm_ffn[l].astype(F32)[None, :],
                        wr_hi, wr_lo, br, _pick(n, 512))
        x2 = _moe(hx, x2, _pair_cols(expert_w_gate[l]), _pair_cols(expert_w_up[l]),
                  _pair_rows(expert_w_down[l]), tm)

    return x2.reshape(bsz, seq, d).astype(x.dtype)
```

```python
import functools
import math

import jax
import jax.numpy as jnp
import numpy as np
from jax import lax
from jax.experimental import pallas as pl
from jax.experimental.pallas import tpu as pltpu

F32 = jnp.float32
BF16 = jnp.bfloat16

RMS_EPS = 1e-6
LANES = 128
LOG2E = 1.4426950408889634

DA_HEADS = 4
DA_QK_DIM = 64
ROPE_THETA = 500000.0
ROPE_DIM = DA_QK_DIM // 4
DA_LAMBDA_INIT_BASE = 0.8
DA_LAMBDA_INIT_SCALE = 0.6
DA_LAMBDA_INIT_RATE = 0.3

SB_HEADS = 4
SB_HEAD_DIM = 64

HG_HEADS = 4
HG_DIM = 64
HG_COLS = HG_HEADS * HG_DIM
HG_SUB = 16
HG_UNROLL = 16

S5_GROUPS = 16
S5_GROUP_CH = 16
S5_STATE = 64
S5_WIDTH = S5_GROUPS * S5_GROUP_CH
S5_NSTATE = S5_GROUPS * S5_STATE
S5_EIG_CLIP = -1e-4
S5_SUB = 8

N_GROUPS = 4
EXPERTS_PER_GROUP = 4
N_EXPERTS = N_GROUPS * EXPERTS_PER_GROUP

REF_GATE_START = 3584
COL_GATE = 0
COL_DA_Q = 4096
COL_DA_K = 4608
COL_DA_V = 5120
COL_SB_Q = 5632
COL_SB_K = 5888
COL_SB_V = 6144
COL_HG_F = 6400
COL_HG_I = 6656
COL_HG_Q = 6912
COL_HG_G = 7168
COL_S5_U = 7424
IN_COLS = 7680

VMEM_LIMIT = 48 * 1024 * 1024

DA_TQ = 1024
DA_KW = 1024
DA_GROUP = 1
DA_DIAG_SPLIT = 2
SB_TQ = 512
SB_KW = 256
SB_GROUP = 4
MOE_ROW_BLOCK = 512
MOE_DMA_BURST = 256
MOE_DMA_UNROLL = 8
NP_TM = 2048

_NT = (((1,), (1,)), ((), ()))
_TN = (((0,), (0,)), ((), ()))


def _cparams(sem):
    return pltpu.CompilerParams(dimension_semantics=sem, vmem_limit_bytes=VMEM_LIMIT)


def _dot(a, b):
    return jnp.dot(a, b, preferred_element_type=F32)


def _norm_proj_kernel(x_ref, g_ref, w_ref, o_ref, h_ref):
    @pl.when(pl.program_id(1) == 0)
    def _():
        x = x_ref[...]
        ms = jnp.mean(x * x, axis=-1, keepdims=True)
        h_ref[...] = (x * lax.rsqrt(ms + RMS_EPS) * g_ref[...]).astype(BF16)

    o_ref[...] = _dot(h_ref[...], w_ref[...]).astype(o_ref.dtype)


def _norm_proj(x2, gain, w, tm, tn):
    n, d = x2.shape
    cols = w.shape[1]
    return pl.pallas_call(
        _norm_proj_kernel,
        grid=(n // tm, cols // tn),
        in_specs=[pl.BlockSpec((tm, d), lambda i, j: (i, 0)),
                  pl.BlockSpec((1, d), lambda i, j: (0, 0)),
                  pl.BlockSpec((d, tn), lambda i, j: (0, j))],
        out_specs=pl.BlockSpec((tm, tn), lambda i, j: (i, j)),
        out_shape=jax.ShapeDtypeStruct((n, cols), BF16),
        scratch_shapes=[pltpu.VMEM((tm, d), BF16)],
        compiler_params=_cparams(("parallel", "arbitrary")),
        name="norm_proj",
    )(x2, gain, w)


def _rope_kernel(pos_ref, invf_ref, sa_ref, sb_ref, cos_ref, sina_ref, sinb_ref):
    ang = pos_ref[...].astype(F32) * invf_ref[...]
    c = jnp.cos(ang)
    s = jnp.sin(ang)
    cos_ref[...] = c
    sina_ref[...] = s * sa_ref[...]
    sinb_ref[...] = s * sb_ref[...]


def _rope_tables(pos_col, tm):
    n = pos_col.shape[0]
    half = ROPE_DIM // 2
    inv_freq = jnp.exp(-math.log(ROPE_THETA) * jnp.arange(half, dtype=F32) * (2.0 / ROPE_DIM))
    lane = np.arange(LANES) % DA_QK_DIM
    invf = jnp.where(lane < ROPE_DIM, inv_freq[lane % half], 0.0).astype(F32)[None, :]
    sgn_a = jnp.asarray(np.where(lane < half, -1.0, 0.0), F32)[None, :]
    sgn_b = jnp.asarray(np.where((lane >= half) & (lane < ROPE_DIM), 1.0, 0.0), F32)[None, :]
    row = pl.BlockSpec((1, LANES), lambda i: (0, 0))
    tab = pl.BlockSpec((tm, LANES), lambda i: (i, 0))
    shp = jax.ShapeDtypeStruct((n, LANES), F32)
    return pl.pallas_call(
        _rope_kernel,
        grid=(n // tm,),
        in_specs=[pl.BlockSpec((tm, 1), lambda i: (i, 0)), row, row, row],
        out_specs=[tab, tab, tab],
        out_shape=[shp, shp, shp],
        compiler_params=_cparams(("parallel",)),
        name="rope_tables",
    )(pos_col, invf, sgn_a, sgn_b)


def _qk_prep_kernel(x_ref, gain_ref, cos_ref, sina_ref, sinb_ref, bd_ref, o_ref):
    c = cos_ref[...]
    sa = sina_ref[...]
    sb = sinb_ref[...]
    bd = bd_ref[...]
    n_tiles = x_ref.shape[1] // LANES
    for j in range(n_tiles):
        sl = slice(j * LANES, (j + 1) * LANES)
        t = x_ref[:, sl].astype(F32)
        ss = _dot((t * t).astype(BF16), bd)
        y = t * lax.rsqrt(ss * (1.0 / DA_QK_DIM) + RMS_EPS) * gain_ref[:, sl]
        y = (y * c + pltpu.roll(y, LANES - ROPE_DIM // 2, 1) * sa
             + pltpu.roll(y, ROPE_DIM // 2, 1) * sb)
        if j < n_tiles // 2:
            y = y * (DA_QK_DIM ** -0.5 * LOG2E)
        o_ref[:, sl] = y.astype(BF16)


def _qk_prep(proj, gain_row, cos_t, sina_t, sinb_t, tm):
    n = proj.shape[0]
    w = 2 * DA_HEADS * 2 * DA_QK_DIM
    seg = np.arange(LANES) // DA_QK_DIM
    bd = jnp.asarray(seg[:, None] == seg[None, :], BF16)
    tab = pl.BlockSpec((tm, LANES), lambda i: (i, 0))
    return pl.pallas_call(
        _qk_prep_kernel,
        grid=(n // tm,),
        in_specs=[pl.BlockSpec((tm, w), lambda i: (i, COL_DA_Q // w)),
                  pl.BlockSpec((1, w), lambda i: (0, 0)),
                  tab, tab, tab,
                  pl.BlockSpec((LANES, LANES), lambda i: (0, 0))],
        out_specs=pl.BlockSpec((tm, w), lambda i: (i, 0)),
        out_shape=jax.ShapeDtypeStruct((n, w), BF16),
        compiler_params=_cparams(("parallel",)),
        name="qk_prep",
    )(proj, gain_row, cos_t, sina_t, sinb_t, bd)


def _diff_attn_kernel(q_ref, k_ref, v_ref, lq1_ref, lk1_ref, lq2_ref, lk2_ref, sg_ref, o_ref,
                      m_ref, l_ref, acc_ref, *, tq, kw, group, lambda_init):
    qi = pl.program_id(2)
    q = q_ref[...]
    lane = lax.broadcasted_iota(jnp.int32, q.shape, 1)
    zero = jnp.zeros_like(q)
    q2 = jnp.concatenate([jnp.where(lane < DA_QK_DIM, q, zero),
                          jnp.where(lane >= DA_QK_DIM, q, zero)], axis=0)
    m_ref[...] = jnp.full(m_ref.shape, -jnp.inf, F32)
    l_ref[...] = jnp.zeros(l_ref.shape, F32)
    acc_ref[...] = jnp.zeros(acc_ref.shape, F32)

    def rows_of(ref_or_val, row_lo):
        if row_lo == 0:
            return ref_or_val[...]
        return jnp.concatenate([ref_or_val[row_lo:tq], ref_or_val[tq + row_lo:2 * tq]], axis=0)

    def put_rows(ref, row_lo, val):
        if row_lo == 0:
            ref[...] = val
        else:
            ref[row_lo:tq] = val[:tq - row_lo]
            ref[tq + row_lo:2 * tq] = val[tq - row_lo:]

    def tile(start, width, mask_off, row_lo=0):
        n_rows = tq - row_lo
        n_lt = width // LANES
        kb = k_ref[pl.ds(start, width), :]
        vb = v_ref[pl.ds(start, width), :]
        s = lax.dot_general(rows_of(q2, row_lo), kb, _NT, preferred_element_type=F32)
        if mask_off is not None:
            keep = (lax.broadcasted_iota(jnp.int32, (n_rows, width), 1) + mask_off
                    <= lax.broadcasted_iota(jnp.int32, (n_rows, width), 0) + row_lo)
            s = jnp.where(jnp.concatenate([keep, keep], axis=0), s, -jnp.inf)
        st = [s[:, j * LANES:(j + 1) * LANES] for j in range(n_lt)]
        smax = st[0]
        for j in range(1, n_lt):
            smax = jnp.maximum(smax, st[j])
        m_prev = rows_of(m_ref, row_lo)
        m_new = jnp.maximum(m_prev, jnp.max(smax, axis=-1, keepdims=True))
        alpha = jnp.exp2(m_prev - m_new)
        ps = [jnp.exp2(t - m_new) for t in st]
        lsum = ps[0]
        for j in range(1, n_lt):
            lsum = lsum + ps[j]
        p = jnp.concatenate([t.astype(BF16) for t in ps], axis=1)
        put_rows(l_ref, row_lo, alpha * rows_of(l_ref, row_lo) + lsum)
        put_rows(acc_ref, row_lo, alpha * rows_of(acc_ref, row_lo) + _dot(p, vb))
        put_rows(m_ref, row_lo, m_new)

    n_diag = DA_DIAG_SPLIT
    dw = tq // n_diag
    n_below = qi * (tq // kw)

    def body_group(gi, carry):
        for j in range(group):
            tile(pl.multiple_of((gi * group + j) * kw, kw), kw, None)
        return carry

    def body_single(ki, carry):
        tile(pl.multiple_of(ki * kw, kw), kw, None)
        return carry

    n_group = n_below // group
    lax.fori_loop(0, n_group, body_group, 0)
    lax.fori_loop(n_group * group, n_below, body_single, 0)
    for j in range(n_diag):
        tile(pl.multiple_of(qi * tq + j * dw, dw), dw, j * dw, row_lo=j * dw)

    lam = (jnp.exp(jnp.sum(lq1_ref[...] * lk1_ref[...], axis=-1, keepdims=True))
           - jnp.exp(jnp.sum(lq2_ref[...] * lk2_ref[...], axis=-1, keepdims=True)) + lambda_init)
    o = acc_ref[...] / jnp.sum(l_ref[...], axis=-1, keepdims=True)
    o = o[:tq] - lam * o[tq:]
    ms = jnp.mean(o * o, axis=-1, keepdims=True)
    o = o * lax.rsqrt(ms + RMS_EPS) * sg_ref[...] * (1.0 - lambda_init)
    o_ref[...] = o.astype(o_ref.dtype)


def _diff_attn(qk, proj, lq1, lk1, lq2, lk2, subln, bsz, seq, tq, kw, group, lambda_init):
    n = qk.shape[0]
    nq = seq // tq
    kcol = (DA_HEADS * 2 * DA_QK_DIM) // LANES
    vcol = COL_DA_V // LANES
    vec = pl.BlockSpec((1, DA_QK_DIM), lambda b, h, i: (0, 0))
    return pl.pallas_call(
        functools.partial(_diff_attn_kernel, tq=tq, kw=min(kw, tq), group=group, lambda_init=lambda_init),
        grid=(bsz, DA_HEADS, nq),
        in_specs=[pl.BlockSpec((tq, LANES), lambda b, h, i: (b * nq + i, h)),
                  pl.BlockSpec((seq, LANES), lambda b, h, i: (b, kcol + h)),
                  pl.BlockSpec((seq, LANES), lambda b, h, i: (b, vcol + h)),
                  vec, vec, vec, vec,
                  pl.BlockSpec((1, LANES), lambda b, h, i: (0, 0))],
        out_specs=pl.BlockSpec((tq, LANES), lambda b, h, i: (b * nq + i, h)),
        out_shape=jax.ShapeDtypeStruct((n, DA_HEADS * LANES), BF16),
        scratch_shapes=[pltpu.VMEM((2 * tq, LANES), F32), pltpu.VMEM((2 * tq, LANES), F32),
                        pltpu.VMEM((2 * tq, LANES), F32)],
        compiler_params=_cparams(("parallel", "parallel", "arbitrary")),
        name="diff_attn",
    )(qk, qk, proj, lq1, lk1, lq2, lk2, subln)


def _stick_break_kernel(q_ref, k_ref, v_ref, o_ref, r_ref, acc_ref, *, tq, kw, group):
    qi = pl.program_id(2)
    q = q_ref[...]
    lane = lax.broadcasted_iota(jnp.int32, q.shape, 1)
    zero = jnp.zeros_like(q)
    q2 = jnp.concatenate([jnp.where(lane < SB_HEAD_DIM, q, zero),
                          jnp.where(lane >= SB_HEAD_DIM, q, zero)], axis=0)
    incl = jnp.where(lax.broadcasted_iota(jnp.int32, (kw, kw), 0)
                     >= lax.broadcasted_iota(jnp.int32, (kw, kw), 1), 1.0, 0.0).astype(BF16)
    n_lt = kw // LANES
    n_diag = tq // kw
    r_ref[...] = jnp.zeros(r_ref.shape, F32)
    acc_ref[...] = jnp.zeros(acc_ref.shape, F32)

    def run(starts, offsets):
        r = r_ref[...]
        total = None
        for start, off in zip(starts, offsets):
            u = lax.dot_general(q2, k_ref[pl.ds(start, kw), :], _NT, preferred_element_type=F32)
            neg_abs = lax.bitcast_convert_type(
                lax.bitcast_convert_type(u, jnp.int32) | jnp.int32(-2 ** 31), F32)
            sp = jnp.maximum(u, 0.0) + jnp.log(1.0 + jnp.exp2(neg_abs)) * LOG2E
            mask = None
            if off is not None:
                keep = (lax.broadcasted_iota(jnp.int32, (tq, kw), 1) + off
                        < lax.broadcasted_iota(jnp.int32, (tq, kw), 0))
                mask = jnp.concatenate([keep, keep], axis=0)
                sp = jnp.where(mask, sp, 0.0)
            cum = _dot(sp.astype(BF16), incl)
            w = jnp.exp2(u - (cum + jnp.concatenate([r] * n_lt, axis=1)))
            if mask is not None:
                w = jnp.where(mask, w, 0.0)
            part = _dot(w.astype(BF16), v_ref[pl.ds(start, kw), :])
            total = part if total is None else total + part
            r = r + cum[:, 0:1]
        acc_ref[...] += total
        r_ref[...] = r

    base = qi * tq
    run([pl.multiple_of(base + (n_diag - 1 - j) * kw, kw) for j in range(n_diag)],
        [(n_diag - 1 - j) * kw for j in range(n_diag)])

    n_below = qi * n_diag
    rem = n_below % group

    def body_single(j, carry):
        run([pl.multiple_of((n_below - 1 - j) * kw, kw)], [None])
        return carry

    def body_group(gi, carry):
        top = n_below - rem - gi * group
        run([pl.multiple_of((top - 1 - j) * kw, kw) for j in range(group)], [None] * group)
        return carry

    lax.fori_loop(0, rem, body_single, 0)
    lax.fori_loop(0, n_below // group, body_group, 0)
    lane_o = lax.broadcasted_iota(jnp.int32, (tq, LANES), 1)
    o_ref[...] = jnp.where(lane_o < SB_HEAD_DIM, acc_ref[:tq, :], acc_ref[tq:, :]).astype(o_ref.dtype)


def _stick_break(proj, bsz, seq, tq, kw, group):
    n = proj.shape[0]
    nq = seq // tq
    pairs = (SB_HEADS * SB_HEAD_DIM) // LANES
    qc, kc, vc = COL_SB_Q // LANES, COL_SB_K // LANES, COL_SB_V // LANES
    return pl.pallas_call(
        functools.partial(_stick_break_kernel, tq=tq, kw=kw, group=group),
        grid=(bsz, pairs, nq),
        in_specs=[pl.BlockSpec((tq, LANES), lambda b, p, i: (b * nq + i, qc + p)),
                  pl.BlockSpec((seq, LANES), lambda b, p, i: (b, kc + p)),
                  pl.BlockSpec((seq, LANES), lambda b, p, i: (b, vc + p))],
        out_specs=pl.BlockSpec((tq, LANES), lambda b, p, i: (b * nq + i, p)),
        out_shape=jax.ShapeDtypeStruct((n, pairs * LANES), BF16),
        scratch_shapes=[pltpu.VMEM((2 * tq, LANES), F32), pltpu.VMEM((2 * tq, LANES), F32)],
        compiler_params=_cparams(("parallel", "parallel", "arbitrary")),
        name="stick_break",
    )(proj, proj, proj)


def _split_bf16(x):
    hi = x.astype(BF16)
    lo = (x - hi.astype(F32)).astype(BF16)
    return hi, lo


def _hgrn_kernel(f_ref, i_ref, q_ref, g_ref, lb_ref, gain_ref, o_ref,
                 st_ref, b_ref, dec_ref, qq_ref, kk_ref, vv_ref, oo_ref, qe_ref, ke_ref, gm_ref, uu_ref,
                 *, tb):
    @pl.when(pl.program_id(1) == 0)
    def _():
        st_ref[...] = jnp.zeros(st_ref.shape, F32)

    nsub = tb // HG_SUB
    z = f_ref[...].astype(F32)
    lb = lb_ref[...]
    sp = jnp.maximum(z, 0.0) + jnp.log(1.0 + jnp.exp(-jnp.abs(z)))
    log_sig = z - sp
    a = jnp.log(lb)
    c = jnp.log(1.0 - lb) + log_sig
    mx = jnp.maximum(a, c)
    log_f = mx + jnp.log(jnp.exp(a - mx) + jnp.exp(c - mx))
    key = (1.0 - lb) * jax.nn.sigmoid(-z)

    row = lax.broadcasted_iota(jnp.int32, (tb, tb), 0)
    col = lax.broadcasted_iota(jnp.int32, (tb, tb), 1)
    sub_shift = HG_SUB.bit_length() - 1
    dim_shift = HG_DIM.bit_length() - 1
    same = (row >> sub_shift) == (col >> sub_shift)
    tri = jnp.where(same & (col <= row), 1.0, 0.0).astype(BF16)
    blk = jnp.where(same, 1.0, 0.0).astype(BF16)
    hi, lo = _split_bf16(log_f)
    b = _dot(tri, hi) + _dot(tri, lo)
    e = _dot(blk, hi) + _dot(blk, lo)
    qf = q_ref[...].astype(F32)
    b_ref[...] = b
    qq_ref[...] = qf
    kk_ref[...] = key
    vv_ref[...] = i_ref[...].astype(F32)
    qe_ref[...] = (qf * jnp.exp(b)).astype(BF16)
    ke_ref[...] = (key * jnp.exp(e - b)).astype(BF16)
    dec_ref[...] = jnp.exp(e)

    seg_r = lax.broadcasted_iota(jnp.int32, (HG_COLS, HG_COLS), 0) >> dim_shift
    seg_c = lax.broadcasted_iota(jnp.int32, (HG_COLS, HG_COLS), 1) >> dim_shift
    head_mask = seg_r == seg_c
    ones_bd = jnp.where(head_mask, 1.0, 0.0).astype(BF16)
    sel = jnp.where(lax.broadcasted_iota(jnp.int32, (HG_SUB, HG_SUB * HG_SUB), 0)
                    == (lax.broadcasted_iota(jnp.int32, (HG_SUB, HG_SUB * HG_SUB), 1) >> sub_shift),
                    1.0, 0.0).astype(BF16)
    srow = lax.broadcasted_iota(jnp.int32, (HG_SUB, HG_COLS), 0)

    def intra(ci, slot):
        r0 = pl.multiple_of(ci * HG_SUB, HG_SUB)
        bi = b_ref[pl.ds(r0, HG_SUB), :]
        qi = qq_ref[pl.ds(r0, HG_SUB), :]
        ki = kk_ref[pl.ds(r0, HG_SUB), :]
        vi = vv_ref[pl.ds(r0, HG_SUB), :]
        for t in range(HG_SUB):
            d = jnp.exp(jnp.minimum(bi[t:t + 1, :] - bi, 0.0))
            g = jnp.where(srow <= t, qi[t:t + 1, :] * ki * d, 0.0)
            gm_ref[slot, t * HG_SUB:(t + 1) * HG_SUB, :] = g.astype(BF16)
        sc = _dot(gm_ref[slot], ones_bd)
        vt = jnp.concatenate([vi] * HG_SUB, axis=0)
        oo_ref[pl.ds(r0, HG_SUB), :] = _dot(sel, (sc * vt).astype(BF16))
        upd = lax.dot_general(vi.astype(BF16), ke_ref[pl.ds(r0, HG_SUB), :], _TN,
                              preferred_element_type=F32)
        uu_ref[ci] = jnp.where(head_mask, upd, 0.0)

    def intra_group(gi, carry):
        for slot in range(HG_UNROLL):
            intra(gi * HG_UNROLL + slot, slot)
        return carry

    lax.fori_loop(0, nsub // HG_UNROLL, intra_group, 0)

    st = st_ref[...]
    for ci in range(nsub):
        rows = slice(ci * HG_SUB, (ci + 1) * HG_SUB)
        oo_ref[rows, :] += lax.dot_general(qe_ref[rows, :], st.astype(BF16), _NT,
                                           preferred_element_type=F32)
        st = st * dec_ref[ci * HG_SUB:ci * HG_SUB + 1, :] + uu_ref[ci]
    st_ref[...] = st

    o = oo_ref[...]
    ms = _dot((o * o).astype(BF16), ones_bd) * (1.0 / HG_DIM)
    gate = g_ref[...].astype(F32)
    gate = gate * jax.nn.sigmoid(gate)
    o_ref[...] = (o * lax.rsqrt(ms + RMS_EPS) * gain_ref[...] * gate).astype(o_ref.dtype)


def _hgrn(proj, lb_row, gain_row, bsz, seq, tb):
    n = proj.shape[0]
    nb = seq // tb
    w = HG_COLS
    cf, ci, cq, cg = COL_HG_F // w, COL_HG_I // w, COL_HG_Q // w, COL_HG_G // w

    def col(cc):
        return pl.BlockSpec((tb, w), lambda b, i: (b * nb + i, cc))

    rowspec = pl.BlockSpec((1, w), lambda b, i: (0, 0))
    big = pltpu.VMEM((tb, w), F32)
    return pl.pallas_call(
        functools.partial(_hgrn_kernel, tb=tb),
        grid=(bsz, nb),
        in_specs=[col(cf), col(ci), col(cq), col(cg), rowspec, rowspec],
        out_specs=pl.BlockSpec((tb, w), lambda b, i: (b * nb + i, 0)),
        out_shape=jax.ShapeDtypeStruct((n, w), BF16),
        scratch_shapes=[pltpu.VMEM((w, w), F32), big, big, big, big, big, big,
                        pltpu.VMEM((tb, w), BF16), pltpu.VMEM((tb, w), BF16),
                        pltpu.VMEM((HG_UNROLL, HG_SUB * HG_SUB, w), BF16),
                        pltpu.VMEM((tb // HG_SUB, w, w), F32)],
        compiler_params=_cparams(("parallel", "arbitrary")),
        name="hgrn2",
    )(proj, proj, proj, proj, lb_row, gain_row)


def _s5_kernel(u_ref, bblk_ref, lev_re_ref, lev_im_ref, pw_re_ref, pw_im_ref, cblk_ref, d_ref,
               gw_ref, gb_ref, o_ref, cr_ref, ci_ref, xr_ref, xi_ref, *, tb):
    @pl.when(pl.program_id(1) == 0)
    def _():
        cr_ref[...] = jnp.zeros(cr_ref.shape, F32)
        ci_ref[...] = jnp.zeros(ci_ref.shape, F32)

    u = u_ref[...]
    bu = _dot(u, bblk_ref[...])
    xr = bu[:, :S5_NSTATE]
    xi = bu[:, S5_NSTATE:]
    row = lax.broadcasted_iota(jnp.int32, xr.shape, 0)
    sub_lev = S5_SUB.bit_length() - 1
    for j in range(sub_lev):
        d = 1 << j
        ar = lev_re_ref[j:j + 1, :]
        ai = lev_im_ref[j:j + 1, :]
        keep = row >= d
        sr = jnp.where(keep, pltpu.roll(xr, d, 0), 0.0)
        si = jnp.where(keep, pltpu.roll(xi, d, 0), 0.0)
        xr, xi = xr + ar * sr - ai * si, xi + ar * si + ai * sr
    cr = cr_ref[...]
    ci = ci_ref[...]
    pr = pw_re_ref[...]
    pi = pw_im_ref[...]
    gr = xr[:S5_SUB] + pr * cr - pi * ci
    gi = xi[:S5_SUB] + pr * ci + pi * cr
    xr_ref[:S5_SUB, :] = gr
    xi_ref[:S5_SUB, :] = gi
    ar = lev_re_ref[sub_lev:sub_lev + 1, :]
    ai = lev_im_ref[sub_lev:sub_lev + 1, :]
    for g in range(1, tb // S5_SUB):
        rows = slice(g * S5_SUB, (g + 1) * S5_SUB)
        gr, gi = xr[rows] + ar * gr - ai * gi, xi[rows] + ar * gi + ai * gr
        xr_ref[rows, :] = gr
        xi_ref[rows, :] = gi
    cr_ref[...] = gr[S5_SUB - 1:S5_SUB, :]
    ci_ref[...] = gi[S5_SUB - 1:S5_SUB, :]
    y = (_dot(xr_ref[...].astype(BF16), cblk_ref[:S5_NSTATE, :])
         + _dot(xi_ref[...].astype(BF16), cblk_ref[S5_NSTATE:, :]) + d_ref[...] * u.astype(F32))
    y = jax.nn.gelu(y)
    zg = _dot(y.astype(BF16), gw_ref[...]) + gb_ref[...]
    o_ref[...] = (y * jax.nn.sigmoid(zg)).astype(o_ref.dtype)


def _s5_params(lam_re, lam_im, log_step, b_re, b_im, c_re, c_im, tb):
    lam_re = jnp.minimum(lam_re.astype(F32), S5_EIG_CLIP)
    lam_im = lam_im.astype(F32)
    step = jnp.exp(log_step.astype(F32))[:, None]
    mag = jnp.exp(lam_re * step)
    phase = lam_im * step
    a_re = mag * jnp.cos(phase)
    a_im = mag * jnp.sin(phase)
    denom = lam_re * lam_re + lam_im * lam_im
    num_re = a_re - 1.0
    gam_re = (num_re * lam_re + a_im * lam_im) / denom
    gam_im = (a_im * lam_re - num_re * lam_im) / denom
    b_re = b_re.astype(F32)
    b_im = b_im.astype(F32)
    bb_re = gam_re[..., None] * b_re - gam_im[..., None] * b_im
    bb_im = gam_re[..., None] * b_im + gam_im[..., None] * b_re
    eye = jnp.eye(S5_GROUPS, dtype=F32)

    def in_blk(bb):
        return jnp.einsum('gnc,gh->gchn', bb, eye).reshape(S5_WIDTH, S5_NSTATE)

    def out_blk(cc):
        return jnp.einsum('gcn,gh->gnhc', cc.astype(F32), eye).reshape(S5_NSTATE, S5_WIDTH)

    bblk = jnp.concatenate([in_blk(bb_re), in_blk(bb_im)], axis=1).astype(BF16)
    cblk = jnp.concatenate([out_blk(c_re), -out_blk(c_im)], axis=0).astype(BF16)
    ar = a_re.reshape(1, S5_NSTATE)
    ai = a_im.reshape(1, S5_NSTATE)
    sub_lev = S5_SUB.bit_length() - 1
    lev_re, lev_im = [ar], [ai]
    pw_re, pw_im = ar, ai
    for _ in range(sub_lev):
        sr, si = lev_re[-1], lev_im[-1]
        pw_re, pw_im = (jnp.concatenate([pw_re, pw_re * sr - pw_im * si], axis=0),
                        jnp.concatenate([pw_im, pw_re * si + pw_im * sr], axis=0))
        lev_re.append(sr * sr - si * si)
        lev_im.append(2.0 * sr * si)
    lev_re = jnp.concatenate(lev_re, axis=0)
    lev_im = jnp.concatenate(lev_im, axis=0)
    return bblk, cblk, lev_re, lev_im, pw_re, pw_im


def _s5(proj, params, d_row, glu_w, glu_b, bsz, seq, tb):
    n = proj.shape[0]
    nb = seq // tb
    bblk, cblk, lev_re, lev_im, pw_re, pw_im = params
    nlev = lev_re.shape[0]
    ucol = COL_S5_U // S5_WIDTH

    def full(shape):
        return pl.BlockSpec(shape, lambda b, i: (0,) * len(shape))

    return pl.pallas_call(
        functools.partial(_s5_kernel, tb=tb),
        grid=(bsz, nb),
        in_specs=[pl.BlockSpec((tb, S5_WIDTH), lambda b, i: (b * nb + i, ucol)),
                  full((S5_WIDTH, 2 * S5_NSTATE)),
                  full((nlev, S5_NSTATE)), full((nlev, S5_NSTATE)),
                  full((S5_SUB, S5_NSTATE)), full((S5_SUB, S5_NSTATE)),
                  full((2 * S5_NSTATE, S5_WIDTH)),
                  full((1, S5_WIDTH)), full((S5_WIDTH, S5_WIDTH)), full((1, S5_WIDTH))],
        out_specs=pl.BlockSpec((tb, S5_WIDTH), lambda b, i: (b * nb + i, 0)),
        out_shape=jax.ShapeDtypeStruct((n, S5_WIDTH), BF16),
        scratch_shapes=[pltpu.VMEM((1, S5_NSTATE), F32), pltpu.VMEM((1, S5_NSTATE), F32),
                        pltpu.VMEM((tb, S5_NSTATE), F32), pltpu.VMEM((tb, S5_NSTATE), F32)],
        compiler_params=_cparams(("parallel", "arbitrary")),
        name="s5",
    )(proj, bblk, lev_re, lev_im, pw_re, pw_im, cblk, d_row, glu_w, glu_b)


def _merge_kernel(x_ref, ya_ref, yb_ref, yc_ref, yd_ref, gate_ref, wa_ref, wb_ref, wc_ref, wd_ref,
                  wo_ref, gn_ref, wrh_ref, wrl_ref, br_ref, xo_ref, hx_ref):
    d = x_ref.shape[1]
    merged = None
    for i, (y_ref, w_ref) in enumerate(((ya_ref, wa_ref), (yb_ref, wb_ref),
                                        (yc_ref, wc_ref), (yd_ref, wd_ref))):
        gate = 0.5 * jnp.tanh(0.5 * gate_ref[:, i * d:(i + 1) * d].astype(F32)) + 0.5
        term = gate * _dot(y_ref[...], w_ref[...])
        merged = term if merged is None else merged + term
    xn = x_ref[...] + _dot(merged.astype(BF16), wo_ref[...])
    xo_ref[...] = xn
    ms = jnp.mean(xn * xn, axis=-1, keepdims=True)
    h = xn * lax.rsqrt(ms + RMS_EPS) * gn_ref[...]
    hi, lo = _split_bf16(h)
    hx_ref[:, :d] = hi.astype(F32)
    wrh = wrh_ref[...]
    hx_ref[:, d:] = _dot(hi, wrh) + _dot(lo, wrh) + _dot(hi, wrl_ref[...]) + br_ref[...]


def _merge(x2, ya, yb, yc, yd, proj, wa, wb, wc, wd, wo, gn_row, wr_hi, wr_lo, br_row, tm):
    n, d = x2.shape
    assert COL_GATE % (4 * d) == 0
    gcol = COL_GATE // (4 * d)

    def rows(width, cc=0):
        return pl.BlockSpec((tm, width), lambda i: (i, cc))

    def full(arr):
        return pl.BlockSpec(arr.shape, lambda i: (0, 0))

    return pl.pallas_call(
        _merge_kernel,
        grid=(n // tm,),
        in_specs=[rows(d), rows(ya.shape[1]), rows(yb.shape[1]), rows(yc.shape[1]), rows(yd.shape[1]),
                  rows(4 * d, gcol),
                  full(wa), full(wb), full(wc), full(wd), full(wo), full(gn_row),
                  full(wr_hi), full(wr_lo), full(br_row)],
        out_specs=[rows(d), rows(d + LANES)],
        out_shape=[jax.ShapeDtypeStruct((n, d), F32), jax.ShapeDtypeStruct((n, d + LANES), F32)],
        compiler_params=_cparams(("parallel",)),
        name="merge",
    )(x2, ya, yb, yc, yd, proj, wa, wb, wc, wd, wo, gn_row, wr_hi, wr_lo, br_row)


def _first_index(mask, lane):
    return jnp.min(jnp.where(mask, lane, float(LANES)), axis=-1, keepdims=True)


def _group_onehot(logits):
    lane = lax.broadcasted_iota(jnp.int32, logits.shape, 1).astype(F32)
    gl = jnp.where(lane < N_GROUPS, logits, -jnp.inf)
    g_idx = _first_index(gl == jnp.max(gl, axis=-1, keepdims=True), lane)
    return jnp.where(lane == g_idx, 1.0, 0.0)


def _route_kernel(lg_ref, tri_ref, pos_ref, meta_ref, cnt_ref, run_ref, off_ref, *, row_block):
    phase = pl.program_id(0)
    i = pl.program_id(1)
    onehot = _group_onehot(lg_ref[...])
    lane = lax.broadcasted_iota(jnp.int32, (1, LANES), 1)

    @pl.when((phase == 0) & (i == 0))
    def _():
        cnt_ref[...] = jnp.zeros(cnt_ref.shape, F32)

    @pl.when(phase == 0)
    def _():
        cnt_ref[...] += jnp.sum(onehot, axis=0, keepdims=True)
        pos_ref[...] = jnp.zeros(pos_ref.shape, jnp.int32)

    @pl.when((phase == 1) & (i == 0))
    def _():
        padded = jnp.floor((cnt_ref[...] + (row_block - 1)) * (1.0 / row_block)) * row_block
        off = jnp.zeros((1, LANES), F32)
        acc = jnp.zeros((1, 1), F32)
        for g in range(1, N_GROUPS):
            acc = acc + jnp.sum(jnp.where(lane == g - 1, padded, 0.0), axis=-1, keepdims=True)
            off = off + jnp.where(lane == g, acc, 0.0)
        off_ref[...] = off
        run_ref[...] = jnp.zeros(run_ref.shape, F32)
        meta_ref[...] = jnp.zeros(meta_ref.shape, F32)
        meta_ref[0:1, :] = cnt_ref[...]
        meta_ref[1:2, :] = off

    @pl.when(phase == 1)
    def _():
        before = _dot(tri_ref[...], onehot.astype(BF16))
        slot = jnp.sum(onehot * (before + run_ref[...] + off_ref[...]), axis=-1, keepdims=True)
        pos_ref[...] = slot.astype(jnp.int32)
        run_ref[...] += jnp.sum(onehot, axis=0, keepdims=True)


def _moe_route(hx, d, tr, row_block):
    n = hx.shape[0]
    nb = n // tr
    tri = jnp.asarray(np.tril(np.ones((tr, tr), np.float32), -1), BF16)
    pos, meta = pl.pallas_call(
        functools.partial(_route_kernel, row_block=row_block),
        grid=(2, nb),
        in_specs=[pl.BlockSpec((tr, LANES), lambda p, i: (i, d // LANES)),
                  pl.BlockSpec((tr, tr), lambda p, i: (0, 0))],
        out_specs=[pl.BlockSpec((tr, 1), lambda p, i: (p * nb + i, 0)),
                   pl.BlockSpec((8, LANES), lambda p, i: (0, 0))],
        out_shape=[jax.ShapeDtypeStruct((2 * n, 1), jnp.int32), jax.ShapeDtypeStruct((8, LANES), F32)],
        scratch_shapes=[pltpu.VMEM((1, LANES), F32)] * 3,
        compiler_params=_cparams(("arbitrary", "arbitrary")),
        name="moe_route",
    )(hx, tri)
    return pos[n:], meta


def _row_dma_wait(src_hbm, dst_ref, sem, rows):
    pltpu.make_async_copy(src_hbm.at[pl.ds(0, rows)], dst_ref.at[pl.ds(0, rows)], sem).wait()


def _burst_pipeline(n_bursts, issue_burst, wait_burst):
    for b in range(n_bursts):
        issue_burst(b, b % 2)
        if b >= 1:
            wait_burst(b - 1, (b - 1) % 2)
    wait_burst(n_bursts - 1, (n_bursts - 1) % 2)


def _scatter_kernel(pos_ref, hx_ref, hs_in_hbm, hs_hbm, sems, *, tr, burst):
    del hs_in_hbm

    def issue_burst(b, slot):
        def issue(t8, c):
            for k in range(MOE_DMA_UNROLL):
                row = b * burst + t8 * MOE_DMA_UNROLL + k
                pltpu.make_async_copy(hx_ref.at[pl.ds(row, 1)], hs_hbm.at[pl.ds(pos_ref[0, 0, row], 1)],
                                      sems.at[slot]).start(priority=k % 2)
            return c
        lax.fori_loop(0, burst // MOE_DMA_UNROLL, issue, 0)

    def wait_burst(b, slot):
        _row_dma_wait(hx_ref, hs_hbm, sems.at[slot], burst)

    _burst_pipeline(tr // burst, issue_burst, wait_burst)


def _moe_scatter(hx, pos3, n_sorted, tr):
    n, w = hx.shape
    nb = n // tr
    return pl.pallas_call(
        functools.partial(_scatter_kernel, tr=tr, burst=min(MOE_DMA_BURST, tr)),
        grid=(nb,),
        in_specs=[pl.BlockSpec((1, 1, tr), lambda i: (i, 0, 0), memory_space=pltpu.SMEM),
                  pl.BlockSpec((tr, w), lambda i: (i, 0)),
                  pl.BlockSpec(memory_space=pl.ANY)],
        out_specs=pl.BlockSpec(memory_space=pl.ANY),
        out_shape=jax.ShapeDtypeStruct((n_sorted, w), F32),
        scratch_shapes=[pltpu.SemaphoreType.DMA((2,))],
        input_output_aliases={2: 0},
        compiler_params=_cparams(("arbitrary",)),
        name="moe_scatter",
    )(pos3, hx, jnp.zeros((n_sorted, w), F32))


def _experts_kernel(gmap_ref, valid_ref, hs_ref, wg_ref, wu_ref, wd_ref, o_ref, cw_ref, acc_ref, *, d):
    i = pl.program_id(0)
    j = pl.program_id(1)
    valid = valid_ref[i] > 0

    @pl.when(j == 0)
    def _():
        logits = hs_ref[:, d:]
        lane = lax.broadcasted_iota(jnp.int32, logits.shape, 1).astype(F32)
        neg = -jnp.inf
        gl = jnp.where(lane < N_GROUPS, logits, neg)
        gmax = jnp.max(gl, axis=-1, keepdims=True)
        gsum = jnp.sum(jnp.exp(gl - gmax), axis=-1, keepdims=True)
        g_val = 1.0 / gsum
        g_idx = _first_index(gl == gmax, lane)
        lo = N_GROUPS + EXPERTS_PER_GROUP * g_idx
        el = jnp.where((lane >= lo) & (lane < lo + EXPERTS_PER_GROUP), logits, neg)
        emax = jnp.max(el, axis=-1, keepdims=True)
        esum = jnp.sum(jnp.exp(el - emax), axis=-1, keepdims=True)
        i1 = _first_index(el == emax, lane)
        el2 = jnp.where(lane == i1, neg, el)
        e2max = jnp.max(el2, axis=-1, keepdims=True)
        i2 = _first_index(el2 == e2max, lane)
        p1 = 1.0 / esum
        p2 = jnp.exp(e2max - emax) / esum
        tot = p1 + p2
        cw_ref[...] = (jnp.where(lane == i1, g_val * (p1 / tot), 0.0)
                       + jnp.where(lane == i2, g_val * (p2 / tot), 0.0))
        acc_ref[...] = jnp.zeros(acc_ref.shape, F32)

    @pl.when(valid)
    def _():
        h = hs_ref[:, :d].astype(BF16)
        a = _dot(h, jnp.concatenate([wg_ref[0, 0], wg_ref[0, 1]], axis=1))
        hid = (a * jax.nn.sigmoid(a)) * _dot(h, jnp.concatenate([wu_ref[0, 0], wu_ref[0, 1]], axis=1))
        first = N_GROUPS + EXPERTS_PER_GROUP * gmap_ref[i] + 2 * j
        lane = lax.broadcasted_iota(jnp.int32, cw_ref.shape, 1)
        cw_all = cw_ref[...]
        cw0 = jnp.sum(jnp.where(lane == first, cw_all, 0.0), axis=-1, keepdims=True)
        cw1 = jnp.sum(jnp.where(lane == first + 1, cw_all, 0.0), axis=-1, keepdims=True)
        de = hid.shape[1] // 2
        hcol = lax.broadcasted_iota(jnp.int32, hid.shape, 1)
        hid = hid * jnp.where(hcol < de, cw0, cw1)
        acc_ref[...] += _dot(hid.astype(BF16), wd_ref[0])

    @pl.when(j == pl.num_programs(1) - 1)
    def _():
        o_ref[...] = acc_ref[...]


def _moe_experts(hs, gmap, valid, wg, wu, wd, d, row_block):
    n_sorted, w = hs.shape
    _, _, _, de = wg.shape
    de2 = 2 * de
    pairs = EXPERTS_PER_GROUP // 2
    grid_spec = pltpu.PrefetchScalarGridSpec(
        num_scalar_prefetch=2,
        grid=(n_sorted // row_block, pairs),
        in_specs=[pl.BlockSpec((row_block, w), lambda i, j, gm, va: (i, 0)),
                  pl.BlockSpec((1, 2, d, de), lambda i, j, gm, va: (gm[i] * pairs + j, 0, 0, 0)),
                  pl.BlockSpec((1, 2, d, de), lambda i, j, gm, va: (gm[i] * pairs + j, 0, 0, 0)),
                  pl.BlockSpec((1, de2, d), lambda i, j, gm, va: (gm[i] * pairs + j, 0, 0))],
        out_specs=pl.BlockSpec((row_block, d), lambda i, j, gm, va: (i, 0)),
        scratch_shapes=[pltpu.VMEM((row_block, LANES), F32), pltpu.VMEM((row_block, d), F32)])
    return pl.pallas_call(
        functools.partial(_experts_kernel, d=d),
        grid_spec=grid_spec,
        out_shape=jax.ShapeDtypeStruct((n_sorted, d), F32),
        compiler_params=_cparams(("arbitrary", "arbitrary")),
        name="moe_experts",
    )(gmap, valid, hs, wg, wu, wd)


def _combine_kernel(pos_ref, ys_hbm, x_ref, o_ref, buf_ref, sems, *, tr, burst):
    def issue_burst(b, slot):
        def issue(t8, c):
            for k in range(MOE_DMA_UNROLL):
                row = b * burst + t8 * MOE_DMA_UNROLL + k
                pltpu.make_async_copy(ys_hbm.at[pl.ds(pos_ref[0, 0, row], 1)], buf_ref.at[pl.ds(row, 1)],
                                      sems.at[slot]).start(priority=k % 2)
            return c
        lax.fori_loop(0, burst // MOE_DMA_UNROLL, issue, 0)

    def wait_burst(b, slot):
        _row_dma_wait(ys_hbm, buf_ref, sems.at[slot], burst)
        rows = slice(b * burst, (b + 1) * burst)
        o_ref[rows, :] = x_ref[rows, :] + buf_ref[rows, :]

    _burst_pipeline(tr // burst, issue_burst, wait_burst)


def _moe_combine(ys, pos3, x2, tr):
    n, d = x2.shape
    return pl.pallas_call(
        functools.partial(_combine_kernel, tr=tr, burst=min(MOE_DMA_BURST, tr)),
        grid=(n // tr,),
        in_specs=[pl.BlockSpec((1, 1, tr), lambda i: (i, 0, 0), memory_space=pltpu.SMEM),
                  pl.BlockSpec(memory_space=pl.ANY),
                  pl.BlockSpec((tr, d), lambda i: (i, 0))],
        out_specs=pl.BlockSpec((tr, d), lambda i: (i, 0)),
        out_shape=jax.ShapeDtypeStruct((n, d), F32),
        scratch_shapes=[pltpu.VMEM((tr, d), F32), pltpu.SemaphoreType.DMA((2,))],
        compiler_params=_cparams(("arbitrary",)),
        name="moe_combine",
    )(pos3, ys, x2)


def _pair_cols(w):
    ne, d, de = w.shape
    return w.astype(BF16).reshape(ne // 2, 2, d, de)


def _pair_rows(w):
    ne, de, d = w.shape
    return w.astype(BF16).reshape(ne // 2, 2 * de, d)


def _moe(hx, x2, wg, wu, wd, tr):
    n, d = x2.shape
    row_block = min(MOE_ROW_BLOCK, n)
    n_blocks = n // row_block + N_GROUPS
    pos, meta = _moe_route(hx, d, tr, row_block)
    counts = meta[0, :N_GROUPS]
    ends = meta[1, :N_GROUPS] + jnp.ceil(counts / row_block) * row_block
    starts = jnp.arange(n_blocks, dtype=F32) * row_block
    grp = jnp.sum((starts[:, None] >= ends[None, :]).astype(jnp.int32), axis=1)
    valid = (grp < N_GROUPS).astype(jnp.int32)
    gmap = jnp.minimum(grp, N_GROUPS - 1)
    pos3 = pos.reshape(n // tr, 1, tr)
    hs = _moe_scatter(hx, pos3, n_blocks * row_block, tr)
    ys = _moe_experts(hs, gmap, valid, wg, wu, wd, d, row_block)
    return _moe_combine(ys, pos3, x2, tr)


def _pick(n, pref):
    t = min(n, pref)
    while n % t:
        t //= 2
    return t


def kernel(x, positions, norm_mix, w_in, da_q_gain, da_k_gain, da_lambda_q1, da_lambda_k1,
           da_lambda_q2, da_lambda_k2, da_subln_gain, hg_lower_bounds, hg_out_gain,
           s5_lambda_re, s5_lambda_im, s5_log_step, s5_b_re, s5_b_im, s5_c_re, s5_c_im,
           s5_d, s5_glu_w, s5_glu_b, w_branch_attn, w_branch_sb, w_branch_hgrn, w_branch_s5,
           w_out, norm_ffn, router_group_w, router_group_b, router_expert_w, router_expert_b,
           expert_w_gate, expert_w_up, expert_w_down):
    bsz, seq, d = x.shape
    depth = w_in.shape[0]
    n = bsz * seq
    assert w_in.shape[2] == IN_COLS and seq % 128 == 0
    tm = _pick(n, 1024)
    tq = _pick(seq, 256)
    tq_da = _pick(seq, DA_TQ)
    tb = _pick(seq, 512)

    x2 = x.reshape(n, d).astype(F32)
    cos_t, sina_t, sinb_t = _rope_tables(positions.reshape(n, 1).astype(jnp.int32), tm)

    lb_all = jnp.cumsum(jax.nn.softmax(hg_lower_bounds.astype(F32), axis=0), axis=0)
    lb_all = lb_all - lb_all[0:1]

    for l in range(depth):
        lambda_init = DA_LAMBDA_INIT_BASE - DA_LAMBDA_INIT_SCALE * math.exp(-DA_LAMBDA_INIT_RATE * l)
        w_f = w_in[l].astype(F32)
        sbq = COL_SB_Q - COL_DA_Q
        w_l = jnp.concatenate([w_f[:, REF_GATE_START:], w_f[:, :sbq],
                               w_f[:, sbq:sbq + SB_HEADS * SB_HEAD_DIM] * (SB_HEAD_DIM ** -0.5 * LOG2E),
                               w_f[:, sbq + SB_HEADS * SB_HEAD_DIM:REF_GATE_START]], axis=1)
        proj = _norm_proj(x2, norm_mix[l].astype(F32)[None, :], w_l.astype(BF16), _pick(n, NP_TM), 1536)

        qk_gain = jnp.concatenate([jnp.tile(da_q_gain[l].astype(F32), 2 * DA_HEADS),
                                   jnp.tile(da_k_gain[l].astype(F32), 2 * DA_HEADS)])[None, :]
        qk = _qk_prep(proj, qk_gain, cos_t, sina_t, sinb_t, tm)
        y_a = _diff_attn(qk, proj,
                         da_lambda_q1[l].astype(F32)[None, :], da_lambda_k1[l].astype(F32)[None, :],
                         da_lambda_q2[l].astype(F32)[None, :], da_lambda_k2[l].astype(F32)[None, :],
                         da_subln_gain[l].astype(F32)[None, :], bsz, seq, tq_da, DA_KW, DA_GROUP, lambda_init)
        y_b = _stick_break(proj, bsz, seq, _pick(seq, SB_TQ), SB_KW, SB_GROUP)
        y_c = _hgrn(proj, lb_all[l][None, :], jnp.tile(hg_out_gain[l].astype(F32), HG_HEADS)[None, :],
                    bsz, seq, tb)
        s5p = _s5_params(s5_lambda_re[l], s5_lambda_im[l], s5_log_step[l], s5_b_re[l], s5_b_im[l],
                         s5_c_re[l], s5_c_im[l], tb)
        y_d = _s5(proj, s5p, s5_d[l].astype(F32)[None, :], s5_glu_w[l].astype(BF16),
                  s5_glu_b[l].astype(F32)[None, :], bsz, seq, tb)

        wr = jnp.concatenate([router_group_w[l], router_expert_w[l]], axis=1).astype(F32)
        wr = jnp.pad(wr, ((0, 0), (0, LANES - wr.shape[1])))
        wr_hi = wr.astype(BF16)
        wr_lo = (wr - wr_hi.astype(F32)).astype(BF16)
        br = jnp.concatenate([router_group_b[l], router_expert_b[l]]).astype(F32)
        br = jnp.pad(br, (0, LANES - br.shape[0]))[None, :]
        x2, hx = _merge(x2, y_a, y_b, y_c, y_d, proj,
                        w_branch_attn[l].astype(BF16), w_branch_sb[l].astype(BF16),
                        w_branch_hgrn[l].astype(BF16), w_branch_s5[l].astype(BF16),
                        w_out[l].astype(BF16), norm_ffn[l].astype(F32)[None, :],
                        wr_hi, wr_lo, br, _pick(n, 512))
        x2 = _moe(hx, x2, _pair_cols(expert_w_gate[l]), _pair_cols(expert_w_up[l]),
                  _pair_rows(expert_w_down[l]), tm)

    return x2.reshape(bsz, seq, d).astype(x.dtype)
```

```python
import functools
import math

import jax
import jax.numpy as jnp
import numpy as np
from jax import lax
from jax.experimental import pallas as pl
from jax.experimental.pallas import tpu as pltpu

F32 = jnp.float32
BF16 = jnp.bfloat16

RMS_EPS = 1e-6
LANES = 128
LOG2E = 1.4426950408889634

DA_HEADS = 4
DA_QK_DIM = 64
ROPE_THETA = 500000.0
ROPE_DIM = DA_QK_DIM // 4
DA_LAMBDA_INIT_BASE = 0.8
DA_LAMBDA_INIT_SCALE = 0.6
DA_LAMBDA_INIT_RATE = 0.3

SB_HEADS = 4
SB_HEAD_DIM = 64

HG_HEADS = 4
HG_DIM = 64
HG_COLS = HG_HEADS * HG_DIM
HG_SUB = 16
HG_UNROLL = 16

S5_GROUPS = 16
S5_GROUP_CH = 16
S5_STATE = 64
S5_WIDTH = S5_GROUPS * S5_GROUP_CH
S5_NSTATE = S5_GROUPS * S5_STATE
S5_EIG_CLIP = -1e-4
S5_SUB = 8

N_GROUPS = 4
EXPERTS_PER_GROUP = 4
N_EXPERTS = N_GROUPS * EXPERTS_PER_GROUP

REF_GATE_START = 3584
COL_GATE = 0
COL_DA_Q = 4096
COL_DA_K = 4608
COL_DA_V = 5120
COL_SB_Q = 5632
COL_SB_K = 5888
COL_SB_V = 6144
COL_HG_F = 6400
COL_HG_I = 6656
COL_HG_Q = 6912
COL_HG_G = 7168
COL_S5_U = 7424
IN_COLS = 7680

VMEM_LIMIT = 48 * 1024 * 1024

DA_TQ = 1024
DA_KW = 1024
DA_GROUP = 1
DA_DIAG_SPLIT = 4
SB_TQ = 512
SB_KW = 256
SB_GROUP = 4
MOE_ROW_BLOCK = 512
MOE_DMA_BURST = 128
MOE_DMA_UNROLL = 8
NP_TM = 2048

_NT = (((1,), (1,)), ((), ()))
_TN = (((0,), (0,)), ((), ()))


def _cparams(sem):
    return pltpu.CompilerParams(dimension_semantics=sem, vmem_limit_bytes=VMEM_LIMIT)


def _dot(a, b):
    return jnp.dot(a, b, preferred_element_type=F32)


def _norm_proj_kernel(x_ref, g_ref, w_ref, o_ref, h_ref):
    @pl.when(pl.program_id(1) == 0)
    def _():
        x = x_ref[...]
        ms = jnp.mean(x * x, axis=-1, keepdims=True)
        h_ref[...] = (x * lax.rsqrt(ms + RMS_EPS) * g_ref[...]).astype(BF16)

    o_ref[...] = _dot(h_ref[...], w_ref[...]).astype(o_ref.dtype)


def _norm_proj(x2, gain, w, tm, tn):
    n, d = x2.shape
    cols = w.shape[1]
    return pl.pallas_call(
        _norm_proj_kernel,
        grid=(n // tm, cols // tn),
        in_specs=[pl.BlockSpec((tm, d), lambda i, j: (i, 0)),
                  pl.BlockSpec((1, d), lambda i, j: (0, 0)),
                  pl.BlockSpec((d, tn), lambda i, j: (0, j))],
        out_specs=pl.BlockSpec((tm, tn), lambda i, j: (i, j)),
        out_shape=jax.ShapeDtypeStruct((n, cols), BF16),
        scratch_shapes=[pltpu.VMEM((tm, d), BF16)],
        compiler_params=_cparams(("parallel", "arbitrary")),
        name="norm_proj",
    )(x2, gain, w)


def _rope_kernel(pos_ref, invf_ref, sa_ref, sb_ref, cos_ref, sina_ref, sinb_ref):
    ang = pos_ref[...].astype(F32) * invf_ref[...]
    c = jnp.cos(ang)
    s = jnp.sin(ang)
    cos_ref[...] = c
    sina_ref[...] = s * sa_ref[...]
    sinb_ref[...] = s * sb_ref[...]


def _rope_tables(pos_col, tm):
    n = pos_col.shape[0]
    half = ROPE_DIM // 2
    inv_freq = jnp.exp(-math.log(ROPE_THETA) * jnp.arange(half, dtype=F32) * (2.0 / ROPE_DIM))
    lane = np.arange(LANES) % DA_QK_DIM
    invf = jnp.where(lane < ROPE_DIM, inv_freq[lane % half], 0.0).astype(F32)[None, :]
    sgn_a = jnp.asarray(np.where(lane < half, -1.0, 0.0), F32)[None, :]
    sgn_b = jnp.asarray(np.where((lane >= half) & (lane < ROPE_DIM), 1.0, 0.0), F32)[None, :]
    row = pl.BlockSpec((1, LANES), lambda i: (0, 0))
    tab = pl.BlockSpec((tm, LANES), lambda i: (i, 0))
    shp = jax.ShapeDtypeStruct((n, LANES), F32)
    return pl.pallas_call(
        _rope_kernel,
        grid=(n // tm,),
        in_specs=[pl.BlockSpec((tm, 1), lambda i: (i, 0)), row, row, row],
        out_specs=[tab, tab, tab],
        out_shape=[shp, shp, shp],
        compiler_params=_cparams(("parallel",)),
        name="rope_tables",
    )(pos_col, invf, sgn_a, sgn_b)


def _qk_prep_kernel(x_ref, gain_ref, cos_ref, sina_ref, sinb_ref, bd_ref, o_ref):
    c = cos_ref[...]
    sa = sina_ref[...]
    sb = sinb_ref[...]
    bd = bd_ref[...]
    n_tiles = x_ref.shape[1] // LANES
    for j in range(n_tiles):
        sl = slice(j * LANES, (j + 1) * LANES)
        t = x_ref[:, sl].astype(F32)
        ss = _dot((t * t).astype(BF16), bd)
        y = t * lax.rsqrt(ss * (1.0 / DA_QK_DIM) + RMS_EPS) * gain_ref[:, sl]
        y = (y * c + pltpu.roll(y, LANES - ROPE_DIM // 2, 1) * sa
             + pltpu.roll(y, ROPE_DIM // 2, 1) * sb)
        if j < n_tiles // 2:
            y = y * (DA_QK_DIM ** -0.5 * LOG2E)
        o_ref[:, sl] = y.astype(BF16)


def _qk_prep(proj, gain_row, cos_t, sina_t, sinb_t, tm):
    n = proj.shape[0]
    w = 2 * DA_HEADS * 2 * DA_QK_DIM
    seg = np.arange(LANES) // DA_QK_DIM
    bd = jnp.asarray(seg[:, None] == seg[None, :], BF16)
    tab = pl.BlockSpec((tm, LANES), lambda i: (i, 0))
    return pl.pallas_call(
        _qk_prep_kernel,
        grid=(n // tm,),
        in_specs=[pl.BlockSpec((tm, w), lambda i: (i, COL_DA_Q // w)),
                  pl.BlockSpec((1, w), lambda i: (0, 0)),
                  tab, tab, tab,
                  pl.BlockSpec((LANES, LANES), lambda i: (0, 0))],
        out_specs=pl.BlockSpec((tm, w), lambda i: (i, 0)),
        out_shape=jax.ShapeDtypeStruct((n, w), BF16),
        compiler_params=_cparams(("parallel",)),
        name="qk_prep",
    )(proj, gain_row, cos_t, sina_t, sinb_t, bd)


def _diff_attn_kernel(q_ref, k_ref, v_ref, lq1_ref, lk1_ref, lq2_ref, lk2_ref, sg_ref, o_ref,
                      m_ref, l_ref, acc_ref, *, tq, kw, group, lambda_init):
    qi = pl.program_id(2)
    q = q_ref[...]
    lane = lax.broadcasted_iota(jnp.int32, q.shape, 1)
    zero = jnp.zeros_like(q)
    q2 = jnp.concatenate([jnp.where(lane < DA_QK_DIM, q, zero),
                          jnp.where(lane >= DA_QK_DIM, q, zero)], axis=0)
    m_ref[...] = jnp.full(m_ref.shape, -jnp.inf, F32)
    l_ref[...] = jnp.zeros(l_ref.shape, F32)
    acc_ref[...] = jnp.zeros(acc_ref.shape, F32)

    def rows_of(ref_or_val, row_lo):
        if row_lo == 0:
            return ref_or_val[...]
        return jnp.concatenate([ref_or_val[row_lo:tq], ref_or_val[tq + row_lo:2 * tq]], axis=0)

    def put_rows(ref, row_lo, val):
        if row_lo == 0:
            ref[...] = val
        else:
            ref[row_lo:tq] = val[:tq - row_lo]
            ref[tq + row_lo:2 * tq] = val[tq - row_lo:]

    def tile(start, width, mask_off, row_lo=0):
        n_rows = tq - row_lo
        n_lt = width // LANES
        kb = k_ref[pl.ds(start, width), :]
        vb = v_ref[pl.ds(start, width), :]
        s = lax.dot_general(rows_of(q2, row_lo), kb, _NT, preferred_element_type=F32)
        if mask_off is not None:
            keep = (lax.broadcasted_iota(jnp.int32, (n_rows, width), 1) + mask_off
                    <= lax.broadcasted_iota(jnp.int32, (n_rows, width), 0) + row_lo)
            s = jnp.where(jnp.concatenate([keep, keep], axis=0), s, -jnp.inf)
        st = [s[:, j * LANES:(j + 1) * LANES] for j in range(n_lt)]
        smax = st[0]
        for j in range(1, n_lt):
            smax = jnp.maximum(smax, st[j])
        m_prev = rows_of(m_ref, row_lo)
        m_new = jnp.maximum(m_prev, jnp.max(smax, axis=-1, keepdims=True))
        alpha = jnp.exp2(m_prev - m_new)
        ps = [jnp.exp2(t - m_new) for t in st]
        lsum = ps[0]
        for j in range(1, n_lt):
            lsum = lsum + ps[j]
        p = jnp.concatenate([t.astype(BF16) for t in ps], axis=1)
        put_rows(l_ref, row_lo, alpha * rows_of(l_ref, row_lo) + lsum)
        put_rows(acc_ref, row_lo, alpha * rows_of(acc_ref, row_lo) + _dot(p, vb))
        put_rows(m_ref, row_lo, m_new)

    n_diag = DA_DIAG_SPLIT
    dw = tq // n_diag
    n_below = qi * (tq // kw)

    def body_group(gi, carry):
        for j in range(group):
            tile(pl.multiple_of((gi * group + j) * kw, kw), kw, None)
        return carry

    def body_single(ki, carry):
        tile(pl.multiple_of(ki * kw, kw), kw, None)
        return carry

    n_group = n_below // group
    lax.fori_loop(0, n_group, body_group, 0)
    lax.fori_loop(n_group * group, n_below, body_single, 0)
    for j in range(n_diag):
        tile(pl.multiple_of(qi * tq + j * dw, dw), dw, j * dw, row_lo=j * dw)

    lam = (jnp.exp(jnp.sum(lq1_ref[...] * lk1_ref[...], axis=-1, keepdims=True))
           - jnp.exp(jnp.sum(lq2_ref[...] * lk2_ref[...], axis=-1, keepdims=True)) + lambda_init)
    o = acc_ref[...] / jnp.sum(l_ref[...], axis=-1, keepdims=True)
    o = o[:tq] - lam * o[tq:]
    ms = jnp.mean(o * o, axis=-1, keepdims=True)
    o = o * lax.rsqrt(ms + RMS_EPS) * sg_ref[...] * (1.0 - lambda_init)
    o_ref[...] = o.astype(o_ref.dtype)


def _diff_attn(qk, proj, lq1, lk1, lq2, lk2, subln, bsz, seq, tq, kw, group, lambda_init):
    n = qk.shape[0]
    nq = seq // tq
    kcol = (DA_HEADS * 2 * DA_QK_DIM) // LANES
    vcol = COL_DA_V // LANES
    vec = pl.BlockSpec((1, DA_QK_DIM), lambda b, h, i: (0, 0))
    return pl.pallas_call(
        functools.partial(_diff_attn_kernel, tq=tq, kw=min(kw, tq), group=group, lambda_init=lambda_init),
        grid=(bsz, DA_HEADS, nq),
        in_specs=[pl.BlockSpec((tq, LANES), lambda b, h, i: (b * nq + i, h)),
                  pl.BlockSpec((seq, LANES), lambda b, h, i: (b, kcol + h)),
                  pl.BlockSpec((seq, LANES), lambda b, h, i: (b, vcol + h)),
                  vec, vec, vec, vec,
                  pl.BlockSpec((1, LANES), lambda b, h, i: (0, 0))],
        out_specs=pl.BlockSpec((tq, LANES), lambda b, h, i: (b * nq + i, h)),
        out_shape=jax.ShapeDtypeStruct((n, DA_HEADS * LANES), BF16),
        scratch_shapes=[pltpu.VMEM((2 * tq, LANES), F32), pltpu.VMEM((2 * tq, LANES), F32),
                        pltpu.VMEM((2 * tq, LANES), F32)],
        compiler_params=_cparams(("parallel", "parallel", "arbitrary")),
        name="diff_attn",
    )(qk, qk, proj, lq1, lk1, lq2, lk2, subln)


def _stick_break_kernel(q_ref, k_ref, v_ref, o_ref, r_ref, acc_ref, *, tq, kw, group):
    qi = pl.program_id(2)
    q = q_ref[...]
    lane = lax.broadcasted_iota(jnp.int32, q.shape, 1)
    zero = jnp.zeros_like(q)
    q2 = jnp.concatenate([jnp.where(lane < SB_HEAD_DIM, q, zero),
                          jnp.where(lane >= SB_HEAD_DIM, q, zero)], axis=0)
    incl = jnp.where(lax.broadcasted_iota(jnp.int32, (kw, kw), 0)
                     >= lax.broadcasted_iota(jnp.int32, (kw, kw), 1), 1.0, 0.0).astype(BF16)
    n_lt = kw // LANES
    n_diag = tq // kw
    r_ref[...] = jnp.zeros(r_ref.shape, F32)
    acc_ref[...] = jnp.zeros(acc_ref.shape, F32)

    def run(starts, offsets):
        r = r_ref[...]
        total = None
        for start, off in zip(starts, offsets):
            u = lax.dot_general(q2, k_ref[pl.ds(start, kw), :], _NT, preferred_element_type=F32)
            neg_abs = lax.bitcast_convert_type(
                lax.bitcast_convert_type(u, jnp.int32) | jnp.int32(-2 ** 31), F32)
            sp = jnp.maximum(u, 0.0) + jnp.log(1.0 + jnp.exp2(neg_abs)) * LOG2E
            mask = None
            if off is not None:
                keep = (lax.broadcasted_iota(jnp.int32, (tq, kw), 1) + off
                        < lax.broadcasted_iota(jnp.int32, (tq, kw), 0))
                mask = jnp.concatenate([keep, keep], axis=0)
                sp = jnp.where(mask, sp, 0.0)
            cum = _dot(sp.astype(BF16), incl)
            w = jnp.exp2(u - (cum + jnp.concatenate([r] * n_lt, axis=1)))
            if mask is not None:
                w = jnp.where(mask, w, 0.0)
            part = _dot(w.astype(BF16), v_ref[pl.ds(start, kw), :])
            total = part if total is None else total + part
            r = r + cum[:, 0:1]
        acc_ref[...] += total
        r_ref[...] = r

    base = qi * tq
    run([pl.multiple_of(base + (n_diag - 1 - j) * kw, kw) for j in range(n_diag)],
        [(n_diag - 1 - j) * kw for j in range(n_diag)])

    n_below = qi * n_diag
    rem = n_below % group

    def body_single(j, carry):
        run([pl.multiple_of((n_below - 1 - j) * kw, kw)], [None])
        return carry

    def body_group(gi, carry):
        top = n_below - rem - gi * group
        run([pl.multiple_of((top - 1 - j) * kw, kw) for j in range(group)], [None] * group)
        return carry

    lax.fori_loop(0, rem, body_single, 0)
    lax.fori_loop(0, n_below // group, body_group, 0)
    lane_o = lax.broadcasted_iota(jnp.int32, (tq, LANES), 1)
    o_ref[...] = jnp.where(lane_o < SB_HEAD_DIM, acc_ref[:tq, :], acc_ref[tq:, :]).astype(o_ref.dtype)


def _stick_break(proj, bsz, seq, tq, kw, group):
    n = proj.shape[0]
    nq = seq // tq
    pairs = (SB_HEADS * SB_HEAD_DIM) // LANES
    qc, kc, vc = COL_SB_Q // LANES, COL_SB_K // LANES, COL_SB_V // LANES
    return pl.pallas_call(
        functools.partial(_stick_break_kernel, tq=tq, kw=kw, group=group),
        grid=(bsz, pairs, nq),
        in_specs=[pl.BlockSpec((tq, LANES), lambda b, p, i: (b * nq + i, qc + p)),
                  pl.BlockSpec((seq, LANES), lambda b, p, i: (b, kc + p)),
                  pl.BlockSpec((seq, LANES), lambda b, p, i: (b, vc + p))],
        out_specs=pl.BlockSpec((tq, LANES), lambda b, p, i: (b * nq + i, p)),
        out_shape=jax.ShapeDtypeStruct((n, pairs * LANES), BF16),
        scratch_shapes=[pltpu.VMEM((2 * tq, LANES), F32), pltpu.VMEM((2 * tq, LANES), F32)],
        compiler_params=_cparams(("parallel", "parallel", "arbitrary")),
        name="stick_break",
    )(proj, proj, proj)


def _split_bf16(x):
    hi = x.astype(BF16)
    lo = (x - hi.astype(F32)).astype(BF16)
    return hi, lo


def _hgrn_kernel(f_ref, i_ref, q_ref, g_ref, lb_ref, gain_ref, o_ref,
                 st_ref, b_ref, dec_ref, qq_ref, kk_ref, vv_ref, oo_ref, qe_ref, ke_ref, gm_ref, uu_ref,
                 *, tb):
    @pl.when(pl.program_id(1) == 0)
    def _():
        st_ref[...] = jnp.zeros(st_ref.shape, F32)

    nsub = tb // HG_SUB
    z = f_ref[...].astype(F32)
    lb = lb_ref[...]
    sp = jnp.maximum(z, 0.0) + jnp.log(1.0 + jnp.exp(-jnp.abs(z)))
    log_sig = z - sp
    a = jnp.log(lb)
    c = jnp.log(1.0 - lb) + log_sig
    mx = jnp.maximum(a, c)
    log_f = mx + jnp.log(jnp.exp(a - mx) + jnp.exp(c - mx))
    key = (1.0 - lb) * jax.nn.sigmoid(-z)

    row = lax.broadcasted_iota(jnp.int32, (tb, tb), 0)
    col = lax.broadcasted_iota(jnp.int32, (tb, tb), 1)
    sub_shift = HG_SUB.bit_length() - 1
    dim_shift = HG_DIM.bit_length() - 1
    same = (row >> sub_shift) == (col >> sub_shift)
    tri = jnp.where(same & (col <= row), 1.0, 0.0).astype(BF16)
    blk = jnp.where(same, 1.0, 0.0).astype(BF16)
    hi, lo = _split_bf16(log_f)
    b = _dot(tri, hi) + _dot(tri, lo)
    e = _dot(blk, hi) + _dot(blk, lo)
    qf = q_ref[...].astype(F32)
    b_ref[...] = b
    qq_ref[...] = qf
    kk_ref[...] = key
    vv_ref[...] = i_ref[...].astype(F32)
    qe_ref[...] = (qf * jnp.exp(b)).astype(BF16)
    ke_ref[...] = (key * jnp.exp(e - b)).astype(BF16)
    dec_ref[...] = jnp.exp(e)

    seg_r = lax.broadcasted_iota(jnp.int32, (HG_COLS, HG_COLS), 0) >> dim_shift
    seg_c = lax.broadcasted_iota(jnp.int32, (HG_COLS, HG_COLS), 1) >> dim_shift
    head_mask = seg_r == seg_c
    ones_bd = jnp.where(head_mask, 1.0, 0.0).astype(BF16)
    sel = jnp.where(lax.broadcasted_iota(jnp.int32, (HG_SUB, HG_SUB * HG_SUB), 0)
                    == (lax.broadcasted_iota(jnp.int32, (HG_SUB, HG_SUB * HG_SUB), 1) >> sub_shift),
                    1.0, 0.0).astype(BF16)
    srow = lax.broadcasted_iota(jnp.int32, (HG_SUB, HG_COLS), 0)

    def intra(ci, slot):
        r0 = pl.multiple_of(ci * HG_SUB, HG_SUB)
        bi = b_ref[pl.ds(r0, HG_SUB), :]
        qi = qq_ref[pl.ds(r0, HG_SUB), :]
        ki = kk_ref[pl.ds(r0, HG_SUB), :]
        vi = vv_ref[pl.ds(r0, HG_SUB), :]
        for t in range(HG_SUB):
            d = jnp.exp(jnp.minimum(bi[t:t + 1, :] - bi, 0.0))
            g = jnp.where(srow <= t, qi[t:t + 1, :] * ki * d, 0.0)
            gm_ref[slot, t * HG_SUB:(t + 1) * HG_SUB, :] = g.astype(BF16)
        sc = _dot(gm_ref[slot], ones_bd)
        vt = jnp.concatenate([vi] * HG_SUB, axis=0)
        oo_ref[pl.ds(r0, HG_SUB), :] = _dot(sel, (sc * vt).astype(BF16))
        upd = lax.dot_general(vi.astype(BF16), ke_ref[pl.ds(r0, HG_SUB), :], _TN,
                              preferred_element_type=F32)
        uu_ref[ci] = jnp.where(head_mask, upd, 0.0)

    def intra_group(gi, carry):
        for slot in range(HG_UNROLL):
            intra(gi * HG_UNROLL + slot, slot)
        return carry

    lax.fori_loop(0, nsub // HG_UNROLL, intra_group, 0)

    st = st_ref[...]
    for ci in range(nsub):
        rows = slice(ci * HG_SUB, (ci + 1) * HG_SUB)
        oo_ref[rows, :] += lax.dot_general(qe_ref[rows, :], st.astype(BF16), _NT,
                                           preferred_element_type=F32)
        st = st * dec_ref[ci * HG_SUB:ci * HG_SUB + 1, :] + uu_ref[ci]
    st_ref[...] = st

    o = oo_ref[...]
    ms = _dot((o * o).astype(BF16), ones_bd) * (1.0 / HG_DIM)
    gate = g_ref[...].astype(F32)
    gate = gate * jax.nn.sigmoid(gate)
    o_ref[...] = (o * lax.rsqrt(ms + RMS_EPS) * gain_ref[...] * gate).astype(o_ref.dtype)


def _hgrn(proj, lb_row, gain_row, bsz, seq, tb):
    n = proj.shape[0]
    nb = seq // tb
    w = HG_COLS
    cf, ci, cq, cg = COL_HG_F // w, COL_HG_I // w, COL_HG_Q // w, COL_HG_G // w

    def col(cc):
        return pl.BlockSpec((tb, w), lambda b, i: (b * nb + i, cc))

    rowspec = pl.BlockSpec((1, w), lambda b, i: (0, 0))
    big = pltpu.VMEM((tb, w), F32)
    return pl.pallas_call(
        functools.partial(_hgrn_kernel, tb=tb),
        grid=(bsz, nb),
        in_specs=[col(cf), col(ci), col(cq), col(cg), rowspec, rowspec],
        out_specs=pl.BlockSpec((tb, w), lambda b, i: (b * nb + i, 0)),
        out_shape=jax.ShapeDtypeStruct((n, w), BF16),
        scratch_shapes=[pltpu.VMEM((w, w), F32), big, big, big, big, big, big,
                        pltpu.VMEM((tb, w), BF16), pltpu.VMEM((tb, w), BF16),
                        pltpu.VMEM((HG_UNROLL, HG_SUB * HG_SUB, w), BF16),
                        pltpu.VMEM((tb // HG_SUB, w, w), F32)],
        compiler_params=_cparams(("parallel", "arbitrary")),
        name="hgrn2",
    )(proj, proj, proj, proj, lb_row, gain_row)


def _s5_kernel(u_ref, bblk_ref, lev_re_ref, lev_im_ref, pw_re_ref, pw_im_ref, cblk_ref, d_ref,
               gw_ref, gb_ref, o_ref, cr_ref, ci_ref, xr_ref, xi_ref, *, tb):
    @pl.when(pl.program_id(1) == 0)
    def _():
        cr_ref[...] = jnp.zeros(cr_ref.shape, F32)
        ci_ref[...] = jnp.zeros(ci_ref.shape, F32)

    u = u_ref[...]
    bu = _dot(u, bblk_ref[...])
    xr = bu[:, :S5_NSTATE]
    xi = bu[:, S5_NSTATE:]
    row = lax.broadcasted_iota(jnp.int32, xr.shape, 0)
    sub_lev = S5_SUB.bit_length() - 1
    for j in range(sub_lev):
        d = 1 << j
        ar = lev_re_ref[j:j + 1, :]
        ai = lev_im_ref[j:j + 1, :]
        keep = row >= d
        sr = jnp.where(keep, pltpu.roll(xr, d, 0), 0.0)
        si = jnp.where(keep, pltpu.roll(xi, d, 0), 0.0)
        xr, xi = xr + ar * sr - ai * si, xi + ar * si + ai * sr
    cr = cr_ref[...]
    ci = ci_ref[...]
    pr = pw_re_ref[...]
    pi = pw_im_ref[...]
    gr = xr[:S5_SUB] + pr * cr - pi * ci
    gi = xi[:S5_SUB] + pr * ci + pi * cr
    xr_ref[:S5_SUB, :] = gr
    xi_ref[:S5_SUB, :] = gi
    ar = lev_re_ref[sub_lev:sub_lev + 1, :]
    ai = lev_im_ref[sub_lev:sub_lev + 1, :]
    for g in range(1, tb // S5_SUB):
        rows = slice(g * S5_SUB, (g + 1) * S5_SUB)
        gr, gi = xr[rows] + ar * gr - ai * gi, xi[rows] + ar * gi + ai * gr
        xr_ref[rows, :] = gr
        xi_ref[rows, :] = gi
    cr_ref[...] = gr[S5_SUB - 1:S5_SUB, :]
    ci_ref[...] = gi[S5_SUB - 1:S5_SUB, :]
    y = (_dot(xr_ref[...].astype(BF16), cblk_ref[:S5_NSTATE, :])
         + _dot(xi_ref[...].astype(BF16), cblk_ref[S5_NSTATE:, :]) + d_ref[...] * u.astype(F32))
    y = jax.nn.gelu(y)
    zg = _dot(y.astype(BF16), gw_ref[...]) + gb_ref[...]
    o_ref[...] = (y * jax.nn.sigmoid(zg)).astype(o_ref.dtype)


def _s5_params(lam_re, lam_im, log_step, b_re, b_im, c_re, c_im, tb):
    lam_re = jnp.minimum(lam_re.astype(F32), S5_EIG_CLIP)
    lam_im = lam_im.astype(F32)
    step = jnp.exp(log_step.astype(F32))[:, None]
    mag = jnp.exp(lam_re * step)
    phase = lam_im * step
    a_re = mag * jnp.cos(phase)
    a_im = mag * jnp.sin(phase)
    denom = lam_re * lam_re + lam_im * lam_im
    num_re = a_re - 1.0
    gam_re = (num_re * lam_re + a_im * lam_im) / denom
    gam_im = (a_im * lam_re - num_re * lam_im) / denom
    b_re = b_re.astype(F32)
    b_im = b_im.astype(F32)
    bb_re = gam_re[..., None] * b_re - gam_im[..., None] * b_im
    bb_im = gam_re[..., None] * b_im + gam_im[..., None] * b_re
    eye = jnp.eye(S5_GROUPS, dtype=F32)

    def in_blk(bb):
        return jnp.einsum('gnc,gh->gchn', bb, eye).reshape(S5_WIDTH, S5_NSTATE)

    def out_blk(cc):
        return jnp.einsum('gcn,gh->gnhc', cc.astype(F32), eye).reshape(S5_NSTATE, S5_WIDTH)

    bblk = jnp.concatenate([in_blk(bb_re), in_blk(bb_im)], axis=1).astype(BF16)
    cblk = jnp.concatenate([out_blk(c_re), -out_blk(c_im)], axis=0).astype(BF16)
    ar = a_re.reshape(1, S5_NSTATE)
    ai = a_im.reshape(1, S5_NSTATE)
    sub_lev = S5_SUB.bit_length() - 1
    lev_re, lev_im = [ar], [ai]
    pw_re, pw_im = ar, ai
    for _ in range(sub_lev):
        sr, si = lev_re[-1], lev_im[-1]
        pw_re, pw_im = (jnp.concatenate([pw_re, pw_re * sr - pw_im * si], axis=0),
                        jnp.concatenate([pw_im, pw_re * si + pw_im * sr], axis=0))
        lev_re.append(sr * sr - si * si)
        lev_im.append(2.0 * sr * si)
    lev_re = jnp.concatenate(lev_re, axis=0)
    lev_im = jnp.concatenate(lev_im, axis=0)
    return bblk, cblk, lev_re, lev_im, pw_re, pw_im


def _s5(proj, params, d_row, glu_w, glu_b, bsz, seq, tb):
    n = proj.shape[0]
    nb = seq // tb
    bblk, cblk, lev_re, lev_im, pw_re, pw_im = params
    nlev = lev_re.shape[0]
    ucol = COL_S5_U // S5_WIDTH

    def full(shape):
        return pl.BlockSpec(shape, lambda b, i: (0,) * len(shape))

    return pl.pallas_call(
        functools.partial(_s5_kernel, tb=tb),
        grid=(bsz, nb),
        in_specs=[pl.BlockSpec((tb, S5_WIDTH), lambda b, i: (b * nb + i, ucol)),
                  full((S5_WIDTH, 2 * S5_NSTATE)),
                  full((nlev, S5_NSTATE)), full((nlev, S5_NSTATE)),
                  full((S5_SUB, S5_NSTATE)), full((S5_SUB, S5_NSTATE)),
                  full((2 * S5_NSTATE, S5_WIDTH)),
                  full((1, S5_WIDTH)), full((S5_WIDTH, S5_WIDTH)), full((1, S5_WIDTH))],
        out_specs=pl.BlockSpec((tb, S5_WIDTH), lambda b, i: (b * nb + i, 0)),
        out_shape=jax.ShapeDtypeStruct((n, S5_WIDTH), BF16),
        scratch_shapes=[pltpu.VMEM((1, S5_NSTATE), F32), pltpu.VMEM((1, S5_NSTATE), F32),
                        pltpu.VMEM((tb, S5_NSTATE), F32), pltpu.VMEM((tb, S5_NSTATE), F32)],
        compiler_params=_cparams(("parallel", "arbitrary")),
        name="s5",
    )(proj, bblk, lev_re, lev_im, pw_re, pw_im, cblk, d_row, glu_w, glu_b)


def _merge_kernel(x_ref, ya_ref, yb_ref, yc_ref, yd_ref, gate_ref, wa_ref, wb_ref, wc_ref, wd_ref,
                  wo_ref, gn_ref, wrh_ref, wrl_ref, br_ref, xo_ref, hx_ref):
    d = x_ref.shape[1]
    merged = None
    for i, (y_ref, w_ref) in enumerate(((ya_ref, wa_ref), (yb_ref, wb_ref),
                                        (yc_ref, wc_ref), (yd_ref, wd_ref))):
        gate = 0.5 * jnp.tanh(0.5 * gate_ref[:, i * d:(i + 1) * d].astype(F32)) + 0.5
        term = gate * _dot(y_ref[...], w_ref[...])
        merged = term if merged is None else merged + term
    xn = x_ref[...] + _dot(merged.astype(BF16), wo_ref[...])
    xo_ref[...] = xn
    ms = jnp.mean(xn * xn, axis=-1, keepdims=True)
    h = xn * lax.rsqrt(ms + RMS_EPS) * gn_ref[...]
    hi, lo = _split_bf16(h)
    hx_ref[:, :d] = hi.astype(F32)
    wrh = wrh_ref[...]
    hx_ref[:, d:] = _dot(hi, wrh) + _dot(lo, wrh) + _dot(hi, wrl_ref[...]) + br_ref[...]


def _merge(x2, ya, yb, yc, yd, proj, wa, wb, wc, wd, wo, gn_row, wr_hi, wr_lo, br_row, tm):
    n, d = x2.shape
    assert COL_GATE % (4 * d) == 0
    gcol = COL_GATE // (4 * d)

    def rows(width, cc=0):
        return pl.BlockSpec((tm, width), lambda i: (i, cc))

    def full(arr):
        return pl.BlockSpec(arr.shape, lambda i: (0, 0))

    return pl.pallas_call(
        _merge_kernel,
        grid=(n // tm,),
        in_specs=[rows(d), rows(ya.shape[1]), rows(yb.shape[1]), rows(yc.shape[1]), rows(yd.shape[1]),
                  rows(4 * d, gcol),
                  full(wa), full(wb), full(wc), full(wd), full(wo), full(gn_row),
                  full(wr_hi), full(wr_lo), full(br_row)],
        out_specs=[rows(d), rows(d + LANES)],
        out_shape=[jax.ShapeDtypeStruct((n, d), F32), jax.ShapeDtypeStruct((n, d + LANES), F32)],
        compiler_params=_cparams(("parallel",)),
        name="merge",
    )(x2, ya, yb, yc, yd, proj, wa, wb, wc, wd, wo, gn_row, wr_hi, wr_lo, br_row)


def _first_index(mask, lane):
    return jnp.min(jnp.where(mask, lane, float(LANES)), axis=-1, keepdims=True)


def _group_onehot(logits):
    lane = lax.broadcasted_iota(jnp.int32, logits.shape, 1).astype(F32)
    gl = jnp.where(lane < N_GROUPS, logits, -jnp.inf)
    g_idx = _first_index(gl == jnp.max(gl, axis=-1, keepdims=True), lane)
    return jnp.where(lane == g_idx, 1.0, 0.0)


def _route_kernel(lg_ref, tri_ref, pos_ref, meta_ref, cnt_ref, run_ref, off_ref, *, row_block):
    phase = pl.program_id(0)
    i = pl.program_id(1)
    onehot = _group_onehot(lg_ref[...])
    lane = lax.broadcasted_iota(jnp.int32, (1, LANES), 1)

    @pl.when((phase == 0) & (i == 0))
    def _():
        cnt_ref[...] = jnp.zeros(cnt_ref.shape, F32)

    @pl.when(phase == 0)
    def _():
        cnt_ref[...] += jnp.sum(onehot, axis=0, keepdims=True)
        pos_ref[...] = jnp.zeros(pos_ref.shape, jnp.int32)

    @pl.when((phase == 1) & (i == 0))
    def _():
        padded = jnp.floor((cnt_ref[...] + (row_block - 1)) * (1.0 / row_block)) * row_block
        off = jnp.zeros((1, LANES), F32)
        acc = jnp.zeros((1, 1), F32)
        for g in range(1, N_GROUPS):
            acc = acc + jnp.sum(jnp.where(lane == g - 1, padded, 0.0), axis=-1, keepdims=True)
            off = off + jnp.where(lane == g, acc, 0.0)
        off_ref[...] = off
        run_ref[...] = jnp.zeros(run_ref.shape, F32)
        meta_ref[...] = jnp.zeros(meta_ref.shape, F32)
        meta_ref[0:1, :] = cnt_ref[...]
        meta_ref[1:2, :] = off

    @pl.when(phase == 1)
    def _():
        before = _dot(tri_ref[...], onehot.astype(BF16))
        slot = jnp.sum(onehot * (before + run_ref[...] + off_ref[...]), axis=-1, keepdims=True)
        pos_ref[...] = slot.astype(jnp.int32)
        run_ref[...] += jnp.sum(onehot, axis=0, keepdims=True)


def _moe_route(hx, d, tr, row_block):
    n = hx.shape[0]
    nb = n // tr
    tri = jnp.asarray(np.tril(np.ones((tr, tr), np.float32), -1), BF16)
    pos, meta = pl.pallas_call(
        functools.partial(_route_kernel, row_block=row_block),
        grid=(2, nb),
        in_specs=[pl.BlockSpec((tr, LANES), lambda p, i: (i, d // LANES)),
                  pl.BlockSpec((tr, tr), lambda p, i: (0, 0))],
        out_specs=[pl.BlockSpec((tr, 1), lambda p, i: (p * nb + i, 0)),
                   pl.BlockSpec((8, LANES), lambda p, i: (0, 0))],
        out_shape=[jax.ShapeDtypeStruct((2 * n, 1), jnp.int32), jax.ShapeDtypeStruct((8, LANES), F32)],
        scratch_shapes=[pltpu.VMEM((1, LANES), F32)] * 3,
        compiler_params=_cparams(("arbitrary", "arbitrary")),
        name="moe_route",
    )(hx, tri)
    return pos[n:], meta


def _row_dma_wait(src_hbm, dst_ref, sem, rows):
    pltpu.make_async_copy(src_hbm.at[pl.ds(0, rows)], dst_ref.at[pl.ds(0, rows)], sem).wait()


def _burst_pipeline(n_bursts, issue_burst, wait_burst):
    for b in range(n_bursts):
        issue_burst(b, b % 2)
        if b >= 1:
            wait_burst(b - 1, (b - 1) % 2)
    wait_burst(n_bursts - 1, (n_bursts - 1) % 2)


def _scatter_kernel(pos_ref, ends_ref, hx_ref, hs_hbm, zero_ref, sems, *, tr, burst, row_block):
    @pl.when(pl.program_id(0) == 0)
    def _():
        zero_ref[...] = jnp.zeros(zero_ref.shape, F32)
        run = zero_ref.shape[0]
        for g in range(N_GROUPS):
            start = pl.multiple_of((ends_ref[g] >> 3) << 3, 8)
            pltpu.make_async_copy(zero_ref, hs_hbm.at[pl.ds(start, run)], sems.at[0]).start()
        for g in range(N_GROUPS):
            pltpu.make_async_copy(zero_ref, hs_hbm.at[pl.ds(0, run)], sems.at[0]).wait()
        used = ends_ref[N_GROUPS]
        n_tail = (hs_hbm.shape[0] - used) // row_block
        for k in range(N_GROUPS + 1):
            @pl.when(k < n_tail)
            def _():
                start = pl.multiple_of(used + k * row_block, 8)
                cp = pltpu.make_async_copy(zero_ref.at[pl.ds(0, row_block)],
                                           hs_hbm.at[pl.ds(start, row_block)], sems.at[0])
                cp.start()
                cp.wait()

    def issue_burst(b, slot):
        def issue(t8, c):
            for k in range(MOE_DMA_UNROLL):
                row = b * burst + t8 * MOE_DMA_UNROLL + k
                pltpu.make_async_copy(hx_ref.at[pl.ds(row, 1)], hs_hbm.at[pl.ds(pos_ref[0, 0, row], 1)],
                                      sems.at[slot]).start(priority=k % 2)
            return c
        lax.fori_loop(0, burst // MOE_DMA_UNROLL, issue, 0)

    def wait_burst(b, slot):
        _row_dma_wait(hx_ref, hs_hbm, sems.at[slot], burst)

    _burst_pipeline(tr // burst, issue_burst, wait_burst)


def _moe_scatter(hx, pos3, real_ends, n_sorted, tr, row_block):
    n, w = hx.shape
    nb = n // tr
    return pl.pallas_call(
        functools.partial(_scatter_kernel, tr=tr, burst=min(MOE_DMA_BURST, tr), row_block=row_block),
        grid=(nb,),
        in_specs=[pl.BlockSpec((1, 1, tr), lambda i: (i, 0, 0), memory_space=pltpu.SMEM),
                  pl.BlockSpec(memory_space=pltpu.SMEM),
                  pl.BlockSpec((tr, w), lambda i: (i, 0))],
        out_specs=pl.BlockSpec(memory_space=pl.ANY),
        out_shape=jax.ShapeDtypeStruct((n_sorted, w), F32),
        scratch_shapes=[pltpu.VMEM((row_block + 8, w), F32), pltpu.SemaphoreType.DMA((2,))],
        compiler_params=_cparams(("arbitrary",)),
        name="moe_scatter",
    )(pos3, real_ends, hx)


def _experts_kernel(gmap_ref, valid_ref, hs_ref, wg_ref, wu_ref, wd_ref, o_ref, cw_ref, acc_ref, *, d):
    i = pl.program_id(0)
    j = pl.program_id(1)
    valid = valid_ref[i] > 0

    @pl.when(j == 0)
    def _():
        acc_ref[...] = jnp.zeros(acc_ref.shape, F32)

    @pl.when((j == 0) & valid)
    def _():
        logits = hs_ref[:, d:]
        lane = lax.broadcasted_iota(jnp.int32, logits.shape, 1).astype(F32)
        neg = -jnp.inf
        gl = jnp.where(lane < N_GROUPS, logits, neg)
        gmax = jnp.max(gl, axis=-1, keepdims=True)
        gsum = jnp.sum(jnp.exp(gl - gmax), axis=-1, keepdims=True)
        g_val = 1.0 / gsum
        g_idx = _first_index(gl == gmax, lane)
        lo = N_GROUPS + EXPERTS_PER_GROUP * g_idx
        el = jnp.where((lane >= lo) & (lane < lo + EXPERTS_PER_GROUP), logits, neg)
        emax = jnp.max(el, axis=-1, keepdims=True)
        esum = jnp.sum(jnp.exp(el - emax), axis=-1, keepdims=True)
        i1 = _first_index(el == emax, lane)
        el2 = jnp.where(lane == i1, neg, el)
        e2max = jnp.max(el2, axis=-1, keepdims=True)
        i2 = _first_index(el2 == e2max, lane)
        p1 = 1.0 / esum
        p2 = jnp.exp(e2max - emax) / esum
        tot = p1 + p2
        cw_ref[...] = (jnp.where(lane == i1, g_val * (p1 / tot), 0.0)
                       + jnp.where(lane == i2, g_val * (p2 / tot), 0.0))

    @pl.when(valid)
    def _():
        h = hs_ref[:, :d].astype(BF16)
        a = _dot(h, jnp.concatenate([wg_ref[0, 0], wg_ref[0, 1]], axis=1))
        hid = (a * jax.nn.sigmoid(a)) * _dot(h, jnp.concatenate([wu_ref[0, 0], wu_ref[0, 1]], axis=1))
        first = N_GROUPS + EXPERTS_PER_GROUP * gmap_ref[i] + 2 * j
        lane = lax.broadcasted_iota(jnp.int32, cw_ref.shape, 1)
        cw_all = cw_ref[...]
        cw0 = jnp.sum(jnp.where(lane == first, cw_all, 0.0), axis=-1, keepdims=True)
        cw1 = jnp.sum(jnp.where(lane == first + 1, cw_all, 0.0), axis=-1, keepdims=True)
        de = hid.shape[1] // 2
        hcol = lax.broadcasted_iota(jnp.int32, hid.shape, 1)
        hid = hid * jnp.where(hcol < de, cw0, cw1)
        acc_ref[...] += _dot(hid.astype(BF16), wd_ref[0])

    @pl.when(j == pl.num_programs(1) - 1)
    def _():
        o_ref[...] = acc_ref[...]


def _moe_experts(hs, gmap, valid, wg, wu, wd, d, row_block):
    n_sorted, w = hs.shape
    _, _, _, de = wg.shape
    de2 = 2 * de
    pairs = EXPERTS_PER_GROUP // 2
    grid_spec = pltpu.PrefetchScalarGridSpec(
        num_scalar_prefetch=2,
        grid=(n_sorted // row_block, pairs),
        in_specs=[pl.BlockSpec((row_block, w), lambda i, j, gm, va: (i * va[i], 0)),
                  pl.BlockSpec((1, 2, d, de), lambda i, j, gm, va: (gm[i] * pairs + j, 0, 0, 0)),
                  pl.BlockSpec((1, 2, d, de), lambda i, j, gm, va: (gm[i] * pairs + j, 0, 0, 0)),
                  pl.BlockSpec((1, de2, d), lambda i, j, gm, va: (gm[i] * pairs + j, 0, 0))],
        out_specs=pl.BlockSpec((row_block, d), lambda i, j, gm, va: (i, 0)),
        scratch_shapes=[pltpu.VMEM((row_block, LANES), F32), pltpu.VMEM((row_block, d), F32)])
    return pl.pallas_call(
        functools.partial(_experts_kernel, d=d),
        grid_spec=grid_spec,
        out_shape=jax.ShapeDtypeStruct((n_sorted, d), F32),
        compiler_params=_cparams(("arbitrary", "arbitrary")),
        name="moe_experts",
    )(gmap, valid, hs, wg, wu, wd)


def _combine_kernel(pos_ref, ys_hbm, x_ref, o_ref, buf_ref, sems, *, tr, burst):
    def issue_burst(b, slot):
        def issue(t8, c):
            for k in range(MOE_DMA_UNROLL):
                row = b * burst + t8 * MOE_DMA_UNROLL + k
                pltpu.make_async_copy(ys_hbm.at[pl.ds(pos_ref[0, 0, row], 1)], buf_ref.at[pl.ds(row, 1)],
                                      sems.at[slot]).start(priority=k % 2)
            return c
        lax.fori_loop(0, burst // MOE_DMA_UNROLL, issue, 0)

    def wait_burst(b, slot):
        _row_dma_wait(ys_hbm, buf_ref, sems.at[slot], burst)
        rows = slice(b * burst, (b + 1) * burst)
        o_ref[rows, :] = x_ref[rows, :] + buf_ref[rows, :]

    _burst_pipeline(tr // burst, issue_burst, wait_burst)


def _moe_combine(ys, pos3, x2, tr):
    n, d = x2.shape
    return pl.pallas_call(
        functools.partial(_combine_kernel, tr=tr, burst=min(MOE_DMA_BURST, tr)),
        grid=(n // tr,),
        in_specs=[pl.BlockSpec((1, 1, tr), lambda i: (i, 0, 0), memory_space=pltpu.SMEM),
                  pl.BlockSpec(memory_space=pl.ANY),
                  pl.BlockSpec((tr, d), lambda i: (i, 0))],
        out_specs=pl.BlockSpec((tr, d), lambda i: (i, 0)),
        out_shape=jax.ShapeDtypeStruct((n, d), F32),
        scratch_shapes=[pltpu.VMEM((tr, d), F32), pltpu.SemaphoreType.DMA((2,))],
        compiler_params=_cparams(("arbitrary",)),
        name="moe_combine",
    )(pos3, ys, x2)


def _pair_cols(w):
    ne, d, de = w.shape
    return w.astype(BF16).reshape(ne // 2, 2, d, de)


def _pair_rows(w):
    ne, de, d = w.shape
    return w.astype(BF16).reshape(ne // 2, 2 * de, d)


def _moe(hx, x2, wg, wu, wd, tr):
    n, d = x2.shape
    row_block = min(MOE_ROW_BLOCK, n)
    n_blocks = n // row_block + N_GROUPS + 1
    pos, meta = _moe_route(hx, d, tr, row_block)
    counts = meta[0, :N_GROUPS]
    ends = meta[1, :N_GROUPS] + jnp.ceil(counts / row_block) * row_block
    starts = jnp.arange(n_blocks, dtype=F32) * row_block
    grp = jnp.sum((starts[:, None] >= ends[None, :]).astype(jnp.int32), axis=1)
    valid = (grp < N_GROUPS).astype(jnp.int32)
    gmap = jnp.minimum(grp, N_GROUPS - 1)
    real_ends = jnp.concatenate([meta[1, :N_GROUPS] + counts, ends[N_GROUPS - 1:]]).astype(jnp.int32)
    pos3 = pos.reshape(n // tr, 1, tr)
    hs = _moe_scatter(hx, pos3, real_ends, n_blocks * row_block, tr, row_block)
    ys = _moe_experts(hs, gmap, valid, wg, wu, wd, d, row_block)
    return _moe_combine(ys, pos3, x2, tr)


def _pick(n, pref):
    t = min(n, pref)
    while n % t:
        t //= 2
    return t


def kernel(x, positions, norm_mix, w_in, da_q_gain, da_k_gain, da_lambda_q1, da_lambda_k1,
           da_lambda_q2, da_lambda_k2, da_subln_gain, hg_lower_bounds, hg_out_gain,
           s5_lambda_re, s5_lambda_im, s5_log_step, s5_b_re, s5_b_im, s5_c_re, s5_c_im,
           s5_d, s5_glu_w, s5_glu_b, w_branch_attn, w_branch_sb, w_branch_hgrn, w_branch_s5,
           w_out, norm_ffn, router_group_w, router_group_b, router_expert_w, router_expert_b,
           expert_w_gate, expert_w_up, expert_w_down):
    bsz, seq, d = x.shape
    depth = w_in.shape[0]
    n = bsz * seq
    assert w_in.shape[2] == IN_COLS and seq % 128 == 0
    tm = _pick(n, 1024)
    tq = _pick(seq, 256)
    tq_da = _pick(seq, DA_TQ)
    tb = _pick(seq, 512)

    x2 = x.reshape(n, d).astype(F32)
    cos_t, sina_t, sinb_t = _rope_tables(positions.reshape(n, 1).astype(jnp.int32), tm)

    lb_all = jnp.cumsum(jax.nn.softmax(hg_lower_bounds.astype(F32), axis=0), axis=0)
    lb_all = lb_all - lb_all[0:1]

    for l in range(depth):
        lambda_init = DA_LAMBDA_INIT_BASE - DA_LAMBDA_INIT_SCALE * math.exp(-DA_LAMBDA_INIT_RATE * l)
        w_f = w_in[l].astype(F32)
        sbq = COL_SB_Q - COL_DA_Q
        w_l = jnp.concatenate([w_f[:, REF_GATE_START:], w_f[:, :sbq],
                               w_f[:, sbq:sbq + SB_HEADS * SB_HEAD_DIM] * (SB_HEAD_DIM ** -0.5 * LOG2E),
                               w_f[:, sbq + SB_HEADS * SB_HEAD_DIM:REF_GATE_START]], axis=1)
        proj = _norm_proj(x2, norm_mix[l].astype(F32)[None, :], w_l.astype(BF16), _pick(n, NP_TM), 1536)

        qk_gain = jnp.concatenate([jnp.tile(da_q_gain[l].astype(F32), 2 * DA_HEADS),
                                   jnp.tile(da_k_gain[l].astype(F32), 2 * DA_HEADS)])[None, :]
        qk = _qk_prep(proj, qk_gain, cos_t, sina_t, sinb_t, tm)
        y_a = _diff_attn(qk, proj,
                         da_lambda_q1[l].astype(F32)[None, :], da_lambda_k1[l].astype(F32)[None, :],
                         da_lambda_q2[l].astype(F32)[None, :], da_lambda_k2[l].astype(F32)[None, :],
                         da_subln_gain[l].astype(F32)[None, :], bsz, seq, tq_da, DA_KW, DA_GROUP, lambda_init)
        y_b = _stick_break(proj, bsz, seq, _pick(seq, SB_TQ), SB_KW, SB_GROUP)
        y_c = _hgrn(proj, lb_all[l][None, :], jnp.tile(hg_out_gain[l].astype(F32), HG_HEADS)[None, :],
                    bsz, seq, tb)
        s5p = _s5_params(s5_lambda_re[l], s5_lambda_im[l], s5_log_step[l], s5_b_re[l], s5_b_im[l],
                         s5_c_re[l], s5_c_im[l], tb)
        y_d = _s5(proj, s5p, s5_d[l].astype(F32)[None, :], s5_glu_w[l].astype(BF16),
                  s5_glu_b[l].astype(F32)[None, :], bsz, seq, tb)

        wr = jnp.concatenate([router_group_w[l], router_expert_w[l]], axis=1).astype(F32)
        wr = jnp.pad(wr, ((0, 0), (0, LANES - wr.shape[1])))
        wr_hi = wr.astype(BF16)
        wr_lo = (wr - wr_hi.astype(F32)).astype(BF16)
        br = jnp.concatenate([router_group_b[l], router_expert_b[l]]).astype(F32)
        br = jnp.pad(br, (0, LANES - br.shape[0]))[None, :]
        x2, hx = _merge(x2, y_a, y_b, y_c, y_d, proj,
                        w_branch_attn[l].astype(BF16), w_branch_sb[l].astype(BF16),
                        w_branch_hgrn[l].astype(BF16), w_branch_s5[l].astype(BF16),
                        w_out[l].astype(BF16), norm_ffn[l].astype(F32)[None, :],
                        wr_hi, wr_lo, br, _pick(n, 512))
        x2 = _moe(hx, x2, _pair_cols(expert_w_gate[l]), _pair_cols(expert_w_up[l]),
                  _pair_rows(expert_w_down[l]), tm)

    return x2.reshape(bsz, seq, d).astype(x.dtype)
```

```python
import functools
import math

import jax
import jax.numpy as jnp
import numpy as np
from jax import lax
from jax.experimental import pallas as pl
from jax.experimental.pallas import tpu as pltpu

F32 = jnp.float32
BF16 = jnp.bfloat16

RMS_EPS = 1e-6
LANES = 128
LOG2E = 1.4426950408889634

DA_HEADS = 4
DA_QK_DIM = 64
ROPE_THETA = 500000.0
ROPE_DIM = DA_QK_DIM // 4
DA_LAMBDA_INIT_BASE = 0.8
DA_LAMBDA_INIT_SCALE = 0.6
DA_LAMBDA_INIT_RATE = 0.3

SB_HEADS = 4
SB_HEAD_DIM = 64

HG_HEADS = 4
HG_DIM = 64
HG_COLS = HG_HEADS * HG_DIM
HG_SUB = 16
HG_UNROLL = 16

S5_GROUPS = 16
S5_GROUP_CH = 16
S5_STATE = 64
S5_WIDTH = S5_GROUPS * S5_GROUP_CH
S5_NSTATE = S5_GROUPS * S5_STATE
S5_EIG_CLIP = -1e-4
S5_SUB = 8

N_GROUPS = 4
EXPERTS_PER_GROUP = 4
N_EXPERTS = N_GROUPS * EXPERTS_PER_GROUP

REF_GATE_START = 3584
COL_GATE = 0
COL_DA_Q = 4096
COL_DA_K = 4608
COL_DA_V = 5120
COL_SB_Q = 5632
COL_SB_K = 5888
COL_SB_V = 6144
COL_HG_F = 6400
COL_HG_I = 6656
COL_HG_Q = 6912
COL_HG_G = 7168
COL_S5_U = 7424
IN_COLS = 7680

VMEM_LIMIT = 48 * 1024 * 1024

DA_TQ = 1024
DA_KW = 1024
DA_GROUP = 1
DA_DIAG_SPLIT = 4
SB_TQ = 512
SB_KW = 256
SB_GROUP = 4
MOE_ROW_BLOCK = 512
MOE_DMA_BURST = 256
MOE_DMA_UNROLL = 8
NP_TM = 2048

_NT = (((1,), (1,)), ((), ()))
_TN = (((0,), (0,)), ((), ()))


def _cparams(sem):
    return pltpu.CompilerParams(dimension_semantics=sem, vmem_limit_bytes=VMEM_LIMIT)


def _dot(a, b):
    return jnp.dot(a, b, preferred_element_type=F32)


def _norm_proj_kernel(x_ref, g_ref, w_ref, o_ref, h_ref):
    @pl.when(pl.program_id(1) == 0)
    def _():
        x = x_ref[...]
        ms = jnp.mean(x * x, axis=-1, keepdims=True)
        h_ref[...] = (x * lax.rsqrt(ms + RMS_EPS) * g_ref[...]).astype(BF16)

    o_ref[...] = _dot(h_ref[...], w_ref[...]).astype(o_ref.dtype)


def _norm_proj(x2, gain, w, tm, tn):
    n, d = x2.shape
    cols = w.shape[1]
    return pl.pallas_call(
        _norm_proj_kernel,
        grid=(n // tm, cols // tn),
        in_specs=[pl.BlockSpec((tm, d), lambda i, j: (i, 0)),
                  pl.BlockSpec((1, d), lambda i, j: (0, 0)),
                  pl.BlockSpec((d, tn), lambda i, j: (0, j))],
        out_specs=pl.BlockSpec((tm, tn), lambda i, j: (i, j)),
        out_shape=jax.ShapeDtypeStruct((n, cols), BF16),
        scratch_shapes=[pltpu.VMEM((tm, d), BF16)],
        compiler_params=_cparams(("parallel", "arbitrary")),
        name="norm_proj",
    )(x2, gain, w)


def _rope_kernel(pos_ref, invf_ref, sa_ref, sb_ref, cos_ref, sina_ref, sinb_ref):
    ang = pos_ref[...].astype(F32) * invf_ref[...]
    c = jnp.cos(ang)
    s = jnp.sin(ang)
    cos_ref[...] = c
    sina_ref[...] = s * sa_ref[...]
    sinb_ref[...] = s * sb_ref[...]


def _rope_tables(pos_col, tm):
    n = pos_col.shape[0]
    half = ROPE_DIM // 2
    inv_freq = jnp.exp(-math.log(ROPE_THETA) * jnp.arange(half, dtype=F32) * (2.0 / ROPE_DIM))
    lane = np.arange(LANES) % DA_QK_DIM
    invf = jnp.where(lane < ROPE_DIM, inv_freq[lane % half], 0.0).astype(F32)[None, :]
    sgn_a = jnp.asarray(np.where(lane < half, -1.0, 0.0), F32)[None, :]
    sgn_b = jnp.asarray(np.where((lane >= half) & (lane < ROPE_DIM), 1.0, 0.0), F32)[None, :]
    row = pl.BlockSpec((1, LANES), lambda i: (0, 0))
    tab = pl.BlockSpec((tm, LANES), lambda i: (i, 0))
    shp = jax.ShapeDtypeStruct((n, LANES), F32)
    return pl.pallas_call(
        _rope_kernel,
        grid=(n // tm,),
        in_specs=[pl.BlockSpec((tm, 1), lambda i: (i, 0)), row, row, row],
        out_specs=[tab, tab, tab],
        out_shape=[shp, shp, shp],
        compiler_params=_cparams(("parallel",)),
        name="rope_tables",
    )(pos_col, invf, sgn_a, sgn_b)


def _qk_prep_kernel(x_ref, gain_ref, cos_ref, sina_ref, sinb_ref, bd_ref, o_ref):
    c = cos_ref[...]
    sa = sina_ref[...]
    sb = sinb_ref[...]
    bd = bd_ref[...]
    n_tiles = x_ref.shape[1] // LANES
    for j in range(n_tiles):
        sl = slice(j * LANES, (j + 1) * LANES)
        t = x_ref[:, sl].astype(F32)
        ss = _dot((t * t).astype(BF16), bd)
        y = t * lax.rsqrt(ss * (1.0 / DA_QK_DIM) + RMS_EPS) * gain_ref[:, sl]
        y = (y * c + pltpu.roll(y, LANES - ROPE_DIM // 2, 1) * sa
             + pltpu.roll(y, ROPE_DIM // 2, 1) * sb)
        if j < n_tiles // 2:
            y = y * (DA_QK_DIM ** -0.5 * LOG2E)
        o_ref[:, sl] = y.astype(BF16)


def _qk_prep(proj, gain_row, cos_t, sina_t, sinb_t, tm):
    n = proj.shape[0]
    w = 2 * DA_HEADS * 2 * DA_QK_DIM
    seg = np.arange(LANES) // DA_QK_DIM
    bd = jnp.asarray(seg[:, None] == seg[None, :], BF16)
    tab = pl.BlockSpec((tm, LANES), lambda i: (i, 0))
    return pl.pallas_call(
        _qk_prep_kernel,
        grid=(n // tm,),
        in_specs=[pl.BlockSpec((tm, w), lambda i: (i, COL_DA_Q // w)),
                  pl.BlockSpec((1, w), lambda i: (0, 0)),
                  tab, tab, tab,
                  pl.BlockSpec((LANES, LANES), lambda i: (0, 0))],
        out_specs=pl.BlockSpec((tm, w), lambda i: (i, 0)),
        out_shape=jax.ShapeDtypeStruct((n, w), BF16),
        compiler_params=_cparams(("parallel",)),
        name="qk_prep",
    )(proj, gain_row, cos_t, sina_t, sinb_t, bd)


def _diff_attn_kernel(q_ref, k_ref, v_ref, lq1_ref, lk1_ref, lq2_ref, lk2_ref, sg_ref, o_ref,
                      m_ref, l_ref, acc_ref, *, tq, kw, group, lambda_init):
    qi = pl.program_id(2)
    q = q_ref[...]
    lane = lax.broadcasted_iota(jnp.int32, q.shape, 1)
    zero = jnp.zeros_like(q)
    q2 = jnp.concatenate([jnp.where(lane < DA_QK_DIM, q, zero),
                          jnp.where(lane >= DA_QK_DIM, q, zero)], axis=0)
    m_ref[...] = jnp.full(m_ref.shape, -jnp.inf, F32)
    l_ref[...] = jnp.zeros(l_ref.shape, F32)
    acc_ref[...] = jnp.zeros(acc_ref.shape, F32)

    def rows_of(ref_or_val, row_lo):
        if row_lo == 0:
            return ref_or_val[...]
        return jnp.concatenate([ref_or_val[row_lo:tq], ref_or_val[tq + row_lo:2 * tq]], axis=0)

    def put_rows(ref, row_lo, val):
        if row_lo == 0:
            ref[...] = val
        else:
            ref[row_lo:tq] = val[:tq - row_lo]
            ref[tq + row_lo:2 * tq] = val[tq - row_lo:]

    def tile(start, width, mask_off, row_lo=0):
        n_rows = tq - row_lo
        n_lt = width // LANES
        kb = k_ref[pl.ds(start, width), :]
        vb = v_ref[pl.ds(start, width), :]
        s = lax.dot_general(rows_of(q2, row_lo), kb, _NT, preferred_element_type=F32)
        if mask_off is not None:
            keep = (lax.broadcasted_iota(jnp.int32, (n_rows, width), 1) + mask_off
                    <= lax.broadcasted_iota(jnp.int32, (n_rows, width), 0) + row_lo)
            s = jnp.where(jnp.concatenate([keep, keep], axis=0), s, -jnp.inf)
        st = [s[:, j * LANES:(j + 1) * LANES] for j in range(n_lt)]
        smax = st[0]
        for j in range(1, n_lt):
            smax = jnp.maximum(smax, st[j])
        m_prev = rows_of(m_ref, row_lo)
        m_new = jnp.maximum(m_prev, jnp.max(smax, axis=-1, keepdims=True))
        alpha = jnp.exp2(m_prev - m_new)
        ps = [jnp.exp2(t - m_new) for t in st]
        lsum = ps[0]
        for j in range(1, n_lt):
            lsum = lsum + ps[j]
        p = jnp.concatenate([t.astype(BF16) for t in ps], axis=1)
        put_rows(l_ref, row_lo, alpha * rows_of(l_ref, row_lo) + lsum)
        put_rows(acc_ref, row_lo, alpha * rows_of(acc_ref, row_lo) + _dot(p, vb))
        put_rows(m_ref, row_lo, m_new)

    n_diag = DA_DIAG_SPLIT
    dw = tq // n_diag
    n_below = qi * (tq // kw)

    def body_group(gi, carry):
        for j in range(group):
            tile(pl.multiple_of((gi * group + j) * kw, kw), kw, None)
        return carry

    def body_single(ki, carry):
        tile(pl.multiple_of(ki * kw, kw), kw, None)
        return carry

    n_group = n_below // group
    lax.fori_loop(0, n_group, body_group, 0)
    lax.fori_loop(n_group * group, n_below, body_single, 0)
    for j in range(n_diag):
        tile(pl.multiple_of(qi * tq + j * dw, dw), dw, j * dw, row_lo=j * dw)

    lam = (jnp.exp(jnp.sum(lq1_ref[...] * lk1_ref[...], axis=-1, keepdims=True))
           - jnp.exp(jnp.sum(lq2_ref[...] * lk2_ref[...], axis=-1, keepdims=True)) + lambda_init)
    o = acc_ref[...] / jnp.sum(l_ref[...], axis=-1, keepdims=True)
    o = o[:tq] - lam * o[tq:]
    ms = jnp.mean(o * o, axis=-1, keepdims=True)
    o = o * lax.rsqrt(ms + RMS_EPS) * sg_ref[...] * (1.0 - lambda_init)
    o_ref[...] = o.astype(o_ref.dtype)


def _diff_attn(qk, proj, lq1, lk1, lq2, lk2, subln, bsz, seq, tq, kw, group, lambda_init):
    n = qk.shape[0]
    nq = seq // tq
    kcol = (DA_HEADS * 2 * DA_QK_DIM) // LANES
    vcol = COL_DA_V // LANES
    vec = pl.BlockSpec((1, DA_QK_DIM), lambda b, h, i: (0, 0))
    return pl.pallas_call(
        functools.partial(_diff_attn_kernel, tq=tq, kw=min(kw, tq), group=group, lambda_init=lambda_init),
        grid=(bsz, DA_HEADS, nq),
        in_specs=[pl.BlockSpec((tq, LANES), lambda b, h, i: (b * nq + i, h)),
                  pl.BlockSpec((seq, LANES), lambda b, h, i: (b, kcol + h)),
                  pl.BlockSpec((seq, LANES), lambda b, h, i: (b, vcol + h)),
                  vec, vec, vec, vec,
                  pl.BlockSpec((1, LANES), lambda b, h, i: (0, 0))],
        out_specs=pl.BlockSpec((tq, LANES), lambda b, h, i: (b * nq + i, h)),
        out_shape=jax.ShapeDtypeStruct((n, DA_HEADS * LANES), BF16),
        scratch_shapes=[pltpu.VMEM((2 * tq, LANES), F32), pltpu.VMEM((2 * tq, LANES), F32),
                        pltpu.VMEM((2 * tq, LANES), F32)],
        compiler_params=_cparams(("parallel", "parallel", "arbitrary")),
        name="diff_attn",
    )(qk, qk, proj, lq1, lk1, lq2, lk2, subln)


def _stick_break_kernel(q_ref, k_ref, v_ref, o_ref, r_ref, acc_ref, *, tq, kw, group):
    qi = pl.program_id(2)
    q = q_ref[...]
    lane = lax.broadcasted_iota(jnp.int32, q.shape, 1)
    zero = jnp.zeros_like(q)
    q2 = jnp.concatenate([jnp.where(lane < SB_HEAD_DIM, q, zero),
                          jnp.where(lane >= SB_HEAD_DIM, q, zero)], axis=0)
    incl = jnp.where(lax.broadcasted_iota(jnp.int32, (kw, kw), 0)
                     >= lax.broadcasted_iota(jnp.int32, (kw, kw), 1), 1.0, 0.0).astype(BF16)
    n_lt = kw // LANES
    n_diag = tq // kw
    r_ref[...] = jnp.zeros(r_ref.shape, F32)
    acc_ref[...] = jnp.zeros(acc_ref.shape, F32)

    def run(starts, offsets):
        r = r_ref[...]
        total = None
        for start, off in zip(starts, offsets):
            u = lax.dot_general(q2, k_ref[pl.ds(start, kw), :], _NT, preferred_element_type=F32)
            neg_abs = lax.bitcast_convert_type(
                lax.bitcast_convert_type(u, jnp.int32) | jnp.int32(-2 ** 31), F32)
            sp = jnp.maximum(u, 0.0) + jnp.log(1.0 + jnp.exp2(neg_abs)) * LOG2E
            mask = None
            if off is not None:
                keep = (lax.broadcasted_iota(jnp.int32, (tq, kw), 1) + off
                        < lax.broadcasted_iota(jnp.int32, (tq, kw), 0))
                mask = jnp.concatenate([keep, keep], axis=0)
                sp = jnp.where(mask, sp, 0.0)
            cum = _dot(sp.astype(BF16), incl)
            w = jnp.exp2(u - (cum + jnp.concatenate([r] * n_lt, axis=1)))
            if mask is not None:
                w = jnp.where(mask, w, 0.0)
            part = _dot(w.astype(BF16), v_ref[pl.ds(start, kw), :])
            total = part if total is None else total + part
            r = r + cum[:, 0:1]
        acc_ref[...] += total
        r_ref[...] = r

    base = qi * tq
    run([pl.multiple_of(base + (n_diag - 1 - j) * kw, kw) for j in range(n_diag)],
        [(n_diag - 1 - j) * kw for j in range(n_diag)])

    n_below = qi * n_diag
    rem = n_below % group

    def body_single(j, carry):
        run([pl.multiple_of((n_below - 1 - j) * kw, kw)], [None])
        return carry

    def body_group(gi, carry):
        top = n_below - rem - gi * group
        run([pl.multiple_of((top - 1 - j) * kw, kw) for j in range(group)], [None] * group)
        return carry

    lax.fori_loop(0, rem, body_single, 0)
    lax.fori_loop(0, n_below // group, body_group, 0)
    lane_o = lax.broadcasted_iota(jnp.int32, (tq, LANES), 1)
    o_ref[...] = jnp.where(lane_o < SB_HEAD_DIM, acc_ref[:tq, :], acc_ref[tq:, :]).astype(o_ref.dtype)


def _stick_break(proj, bsz, seq, tq, kw, group):
    n = proj.shape[0]
    nq = seq // tq
    pairs = (SB_HEADS * SB_HEAD_DIM) // LANES
    qc, kc, vc = COL_SB_Q // LANES, COL_SB_K // LANES, COL_SB_V // LANES
    return pl.pallas_call(
        functools.partial(_stick_break_kernel, tq=tq, kw=kw, group=group),
        grid=(bsz, pairs, nq),
        in_specs=[pl.BlockSpec((tq, LANES), lambda b, p, i: (b * nq + i, qc + p)),
                  pl.BlockSpec((seq, LANES), lambda b, p, i: (b, kc + p)),
                  pl.BlockSpec((seq, LANES), lambda b, p, i: (b, vc + p))],
        out_specs=pl.BlockSpec((tq, LANES), lambda b, p, i: (b * nq + i, p)),
        out_shape=jax.ShapeDtypeStruct((n, pairs * LANES), BF16),
        scratch_shapes=[pltpu.VMEM((2 * tq, LANES), F32), pltpu.VMEM((2 * tq, LANES), F32)],
        compiler_params=_cparams(("parallel", "parallel", "arbitrary")),
        name="stick_break",
    )(proj, proj, proj)


def _split_bf16(x):
    hi = x.astype(BF16)
    lo = (x - hi.astype(F32)).astype(BF16)
    return hi, lo


def _hgrn_kernel(f_ref, i_ref, q_ref, g_ref, lb_ref, gain_ref, o_ref,
                 st_ref, b_ref, dec_ref, qq_ref, kk_ref, vv_ref, oo_ref, qe_ref, ke_ref, gm_ref, uu_ref,
                 *, tb):
    @pl.when(pl.program_id(1) == 0)
    def _():
        st_ref[...] = jnp.zeros(st_ref.shape, F32)

    nsub = tb // HG_SUB
    z = f_ref[...].astype(F32)
    lb = lb_ref[...]
    sp = jnp.maximum(z, 0.0) + jnp.log(1.0 + jnp.exp(-jnp.abs(z)))
    log_sig = z - sp
    a = jnp.log(lb)
    c = jnp.log(1.0 - lb) + log_sig
    mx = jnp.maximum(a, c)
    log_f = mx + jnp.log(jnp.exp(a - mx) + jnp.exp(c - mx))
    key = (1.0 - lb) * jax.nn.sigmoid(-z)

    row = lax.broadcasted_iota(jnp.int32, (tb, tb), 0)
    col = lax.broadcasted_iota(jnp.int32, (tb, tb), 1)
    sub_shift = HG_SUB.bit_length() - 1
    dim_shift = HG_DIM.bit_length() - 1
    same = (row >> sub_shift) == (col >> sub_shift)
    tri = jnp.where(same & (col <= row), 1.0, 0.0).astype(BF16)
    blk = jnp.where(same, 1.0, 0.0).astype(BF16)
    hi, lo = _split_bf16(log_f)
    b = _dot(tri, hi) + _dot(tri, lo)
    e = _dot(blk, hi) + _dot(blk, lo)
    qf = q_ref[...].astype(F32)
    b_ref[...] = b
    qq_ref[...] = qf
    kk_ref[...] = key
    vv_ref[...] = i_ref[...].astype(F32)
    qe_ref[...] = (qf * jnp.exp(b)).astype(BF16)
    ke_ref[...] = (key * jnp.exp(e - b)).astype(BF16)
    dec_ref[...] = jnp.exp(e)

    seg_r = lax.broadcasted_iota(jnp.int32, (HG_COLS, HG_COLS), 0) >> dim_shift
    seg_c = lax.broadcasted_iota(jnp.int32, (HG_COLS, HG_COLS), 1) >> dim_shift
    head_mask = seg_r == seg_c
    ones_bd = jnp.where(head_mask, 1.0, 0.0).astype(BF16)
    sel = jnp.where(lax.broadcasted_iota(jnp.int32, (HG_SUB, HG_SUB * HG_SUB), 0)
                    == (lax.broadcasted_iota(jnp.int32, (HG_SUB, HG_SUB * HG_SUB), 1) >> sub_shift),
                    1.0, 0.0).astype(BF16)
    srow = lax.broadcasted_iota(jnp.int32, (HG_SUB, HG_COLS), 0)

    def intra(ci, slot):
        r0 = pl.multiple_of(ci * HG_SUB, HG_SUB)
        bi = b_ref[pl.ds(r0, HG_SUB), :]
        qi = qq_ref[pl.ds(r0, HG_SUB), :]
        ki = kk_ref[pl.ds(r0, HG_SUB), :]
        vi = vv_ref[pl.ds(r0, HG_SUB), :]
        for t in range(HG_SUB):
            d = jnp.exp(jnp.minimum(bi[t:t + 1, :] - bi, 0.0))
            g = jnp.where(srow <= t, qi[t:t + 1, :] * ki * d, 0.0)
            gm_ref[slot, t * HG_SUB:(t + 1) * HG_SUB, :] = g.astype(BF16)
        sc = _dot(gm_ref[slot], ones_bd)
        vt = jnp.concatenate([vi] * HG_SUB, axis=0)
        oo_ref[pl.ds(r0, HG_SUB), :] = _dot(sel, (sc * vt).astype(BF16))
        upd = lax.dot_general(vi.astype(BF16), ke_ref[pl.ds(r0, HG_SUB), :], _TN,
                              preferred_element_type=F32)
        uu_ref[ci] = jnp.where(head_mask, upd, 0.0)

    def intra_group(gi, carry):
        for slot in range(HG_UNROLL):
            intra(gi * HG_UNROLL + slot, slot)
        return carry

    lax.fori_loop(0, nsub // HG_UNROLL, intra_group, 0)

    st = st_ref[...]
    for ci in range(nsub):
        rows = slice(ci * HG_SUB, (ci + 1) * HG_SUB)
        oo_ref[rows, :] += lax.dot_general(qe_ref[rows, :], st.astype(BF16), _NT,
                                           preferred_element_type=F32)
        st = st * dec_ref[ci * HG_SUB:ci * HG_SUB + 1, :] + uu_ref[ci]
    st_ref[...] = st

    o = oo_ref[...]
    ms = _dot((o * o).astype(BF16), ones_bd) * (1.0 / HG_DIM)
    gate = g_ref[...].astype(F32)
    gate = gate * jax.nn.sigmoid(gate)
    o_ref[...] = (o * lax.rsqrt(ms + RMS_EPS) * gain_ref[...] * gate).astype(o_ref.dtype)


def _hgrn(proj, lb_row, gain_row, bsz, seq, tb):
    n = proj.shape[0]
    nb = seq // tb
    w = HG_COLS
    cf, ci, cq, cg = COL_HG_F // w, COL_HG_I // w, COL_HG_Q // w, COL_HG_G // w

    def col(cc):
        return pl.BlockSpec((tb, w), lambda b, i: (b * nb + i, cc))

    rowspec = pl.BlockSpec((1, w), lambda b, i: (0, 0))
    big = pltpu.VMEM((tb, w), F32)
    return pl.pallas_call(
        functools.partial(_hgrn_kernel, tb=tb),
        grid=(bsz, nb),
        in_specs=[col(cf), col(ci), col(cq), col(cg), rowspec, rowspec],
        out_specs=pl.BlockSpec((tb, w), lambda b, i: (b * nb + i, 0)),
        out_shape=jax.ShapeDtypeStruct((n, w), BF16),
        scratch_shapes=[pltpu.VMEM((w, w), F32), big, big, big, big, big, big,
                        pltpu.VMEM((tb, w), BF16), pltpu.VMEM((tb, w), BF16),
                        pltpu.VMEM((HG_UNROLL, HG_SUB * HG_SUB, w), BF16),
                        pltpu.VMEM((tb // HG_SUB, w, w), F32)],
        compiler_params=_cparams(("parallel", "arbitrary")),
        name="hgrn2",
    )(proj, proj, proj, proj, lb_row, gain_row)


def _s5_kernel(u_ref, bblk_ref, lev_re_ref, lev_im_ref, pw_re_ref, pw_im_ref, cblk_ref, d_ref,
               gw_ref, gb_ref, o_ref, cr_ref, ci_ref, xr_ref, xi_ref, *, tb):
    @pl.when(pl.program_id(1) == 0)
    def _():
        cr_ref[...] = jnp.zeros(cr_ref.shape, F32)
        ci_ref[...] = jnp.zeros(ci_ref.shape, F32)

    u = u_ref[...]
    bu = _dot(u, bblk_ref[...])
    xr = bu[:, :S5_NSTATE]
    xi = bu[:, S5_NSTATE:]
    row = lax.broadcasted_iota(jnp.int32, xr.shape, 0)
    sub_lev = S5_SUB.bit_length() - 1
    for j in range(sub_lev):
        d = 1 << j
        ar = lev_re_ref[j:j + 1, :]
        ai = lev_im_ref[j:j + 1, :]
        keep = row >= d
        sr = jnp.where(keep, pltpu.roll(xr, d, 0), 0.0)
        si = jnp.where(keep, pltpu.roll(xi, d, 0), 0.0)
        xr, xi = xr + ar * sr - ai * si, xi + ar * si + ai * sr
    cr = cr_ref[...]
    ci = ci_ref[...]
    pr = pw_re_ref[...]
    pi = pw_im_ref[...]
    gr = xr[:S5_SUB] + pr * cr - pi * ci
    gi = xi[:S5_SUB] + pr * ci + pi * cr
    xr_ref[:S5_SUB, :] = gr
    xi_ref[:S5_SUB, :] = gi
    ar = lev_re_ref[sub_lev:sub_lev + 1, :]
    ai = lev_im_ref[sub_lev:sub_lev + 1, :]
    for g in range(1, tb // S5_SUB):
        rows = slice(g * S5_SUB, (g + 1) * S5_SUB)
        gr, gi = xr[rows] + ar * gr - ai * gi, xi[rows] + ar * gi + ai * gr
        xr_ref[rows, :] = gr
        xi_ref[rows, :] = gi
    cr_ref[...] = gr[S5_SUB - 1:S5_SUB, :]
    ci_ref[...] = gi[S5_SUB - 1:S5_SUB, :]
    y = (_dot(xr_ref[...].astype(BF16), cblk_ref[:S5_NSTATE, :])
         + _dot(xi_ref[...].astype(BF16), cblk_ref[S5_NSTATE:, :]) + d_ref[...] * u.astype(F32))
    y = jax.nn.gelu(y)
    zg = _dot(y.astype(BF16), gw_ref[...]) + gb_ref[...]
    o_ref[...] = (y * jax.nn.sigmoid(zg)).astype(o_ref.dtype)


def _s5_params(lam_re, lam_im, log_step, b_re, b_im, c_re, c_im, tb):
    lam_re = jnp.minimum(lam_re.astype(F32), S5_EIG_CLIP)
    lam_im = lam_im.astype(F32)
    step = jnp.exp(log_step.astype(F32))[:, None]
    mag = jnp.exp(lam_re * step)
    phase = lam_im * step
    a_re = mag * jnp.cos(phase)
    a_im = mag * jnp.sin(phase)
    denom = lam_re * lam_re + lam_im * lam_im
    num_re = a_re - 1.0
    gam_re = (num_re * lam_re + a_im * lam_im) / denom
    gam_im = (a_im * lam_re - num_re * lam_im) / denom
    b_re = b_re.astype(F32)
    b_im = b_im.astype(F32)
    bb_re = gam_re[..., None] * b_re - gam_im[..., None] * b_im
    bb_im = gam_re[..., None] * b_im + gam_im[..., None] * b_re
    eye = jnp.eye(S5_GROUPS, dtype=F32)

    def in_blk(bb):
        return jnp.einsum('gnc,gh->gchn', bb, eye).reshape(S5_WIDTH, S5_NSTATE)

    def out_blk(cc):
        return jnp.einsum('gcn,gh->gnhc', cc.astype(F32), eye).reshape(S5_NSTATE, S5_WIDTH)

    bblk = jnp.concatenate([in_blk(bb_re), in_blk(bb_im)], axis=1).astype(BF16)
    cblk = jnp.concatenate([out_blk(c_re), -out_blk(c_im)], axis=0).astype(BF16)
    ar = a_re.reshape(1, S5_NSTATE)
    ai = a_im.reshape(1, S5_NSTATE)
    sub_lev = S5_SUB.bit_length() - 1
    lev_re, lev_im = [ar], [ai]
    pw_re, pw_im = ar, ai
    for _ in range(sub_lev):
        sr, si = lev_re[-1], lev_im[-1]
        pw_re, pw_im = (jnp.concatenate([pw_re, pw_re * sr - pw_im * si], axis=0),
                        jnp.concatenate([pw_im, pw_re * si + pw_im * sr], axis=0))
        lev_re.append(sr * sr - si * si)
        lev_im.append(2.0 * sr * si)
    lev_re = jnp.concatenate(lev_re, axis=0)
    lev_im = jnp.concatenate(lev_im, axis=0)
    return bblk, cblk, lev_re, lev_im, pw_re, pw_im


def _s5(proj, params, d_row, glu_w, glu_b, bsz, seq, tb):
    n = proj.shape[0]
    nb = seq // tb
    bblk, cblk, lev_re, lev_im, pw_re, pw_im = params
    nlev = lev_re.shape[0]
    ucol = COL_S5_U // S5_WIDTH

    def full(shape):
        return pl.BlockSpec(shape, lambda b, i: (0,) * len(shape))

    return pl.pallas_call(
        functools.partial(_s5_kernel, tb=tb),
        grid=(bsz, nb),
        in_specs=[pl.BlockSpec((tb, S5_WIDTH), lambda b, i: (b * nb + i, ucol)),
                  full((S5_WIDTH, 2 * S5_NSTATE)),
                  full((nlev, S5_NSTATE)), full((nlev, S5_NSTATE)),
                  full((S5_SUB, S5_NSTATE)), full((S5_SUB, S5_NSTATE)),
                  full((2 * S5_NSTATE, S5_WIDTH)),
                  full((1, S5_WIDTH)), full((S5_WIDTH, S5_WIDTH)), full((1, S5_WIDTH))],
        out_specs=pl.BlockSpec((tb, S5_WIDTH), lambda b, i: (b * nb + i, 0)),
        out_shape=jax.ShapeDtypeStruct((n, S5_WIDTH), BF16),
        scratch_shapes=[pltpu.VMEM((1, S5_NSTATE), F32), pltpu.VMEM((1, S5_NSTATE), F32),
                        pltpu.VMEM((tb, S5_NSTATE), F32), pltpu.VMEM((tb, S5_NSTATE), F32)],
        compiler_params=_cparams(("parallel", "arbitrary")),
        name="s5",
    )(proj, bblk, lev_re, lev_im, pw_re, pw_im, cblk, d_row, glu_w, glu_b)


def _merge_kernel(x_ref, ya_ref, yb_ref, yc_ref, yd_ref, gate_ref, wa_ref, wb_ref, wc_ref, wd_ref,
                  wo_ref, gn_ref, wrh_ref, wrl_ref, br_ref, xo_ref, hx_ref):
    d = x_ref.shape[1]
    merged = None
    for i, (y_ref, w_ref) in enumerate(((ya_ref, wa_ref), (yb_ref, wb_ref),
                                        (yc_ref, wc_ref), (yd_ref, wd_ref))):
        gate = 0.5 * jnp.tanh(0.5 * gate_ref[:, i * d:(i + 1) * d].astype(F32)) + 0.5
        term = gate * _dot(y_ref[...], w_ref[...])
        merged = term if merged is None else merged + term
    xn = x_ref[...] + _dot(merged.astype(BF16), wo_ref[...])
    xo_ref[...] = xn
    ms = jnp.mean(xn * xn, axis=-1, keepdims=True)
    h = xn * lax.rsqrt(ms + RMS_EPS) * gn_ref[...]
    hi, lo = _split_bf16(h)
    hx_ref[:, :d] = hi.astype(F32)
    wrh = wrh_ref[...]
    hx_ref[:, d:] = _dot(hi, wrh) + _dot(lo, wrh) + _dot(hi, wrl_ref[...]) + br_ref[...]


def _merge(x2, ya, yb, yc, yd, proj, wa, wb, wc, wd, wo, gn_row, wr_hi, wr_lo, br_row, tm):
    n, d = x2.shape
    assert COL_GATE % (4 * d) == 0
    gcol = COL_GATE // (4 * d)

    def rows(width, cc=0):
        return pl.BlockSpec((tm, width), lambda i: (i, cc))

    def full(arr):
        return pl.BlockSpec(arr.shape, lambda i: (0, 0))

    return pl.pallas_call(
        _merge_kernel,
        grid=(n // tm,),
        in_specs=[rows(d), rows(ya.shape[1]), rows(yb.shape[1]), rows(yc.shape[1]), rows(yd.shape[1]),
                  rows(4 * d, gcol),
                  full(wa), full(wb), full(wc), full(wd), full(wo), full(gn_row),
                  full(wr_hi), full(wr_lo), full(br_row)],
        out_specs=[rows(d), rows(d + LANES)],
        out_shape=[jax.ShapeDtypeStruct((n, d), F32), jax.ShapeDtypeStruct((n, d + LANES), F32)],
        compiler_params=_cparams(("parallel",)),
        name="merge",
    )(x2, ya, yb, yc, yd, proj, wa, wb, wc, wd, wo, gn_row, wr_hi, wr_lo, br_row)


def _first_index(mask, lane):
    return jnp.min(jnp.where(mask, lane, float(LANES)), axis=-1, keepdims=True)


def _group_onehot(logits):
    lane = lax.broadcasted_iota(jnp.int32, logits.shape, 1).astype(F32)
    gl = jnp.where(lane < N_GROUPS, logits, -jnp.inf)
    g_idx = _first_index(gl == jnp.max(gl, axis=-1, keepdims=True), lane)
    return jnp.where(lane == g_idx, 1.0, 0.0)


def _route_kernel(lg_ref, tri_ref, pos_ref, meta_ref, cnt_ref, run_ref, off_ref, *, row_block):
    phase = pl.program_id(0)
    i = pl.program_id(1)
    onehot = _group_onehot(lg_ref[...])
    lane = lax.broadcasted_iota(jnp.int32, (1, LANES), 1)

    @pl.when((phase == 0) & (i == 0))
    def _():
        cnt_ref[...] = jnp.zeros(cnt_ref.shape, F32)

    @pl.when(phase == 0)
    def _():
        cnt_ref[...] += jnp.sum(onehot, axis=0, keepdims=True)
        pos_ref[...] = jnp.zeros(pos_ref.shape, jnp.int32)

    @pl.when((phase == 1) & (i == 0))
    def _():
        padded = jnp.floor((cnt_ref[...] + (row_block - 1)) * (1.0 / row_block)) * row_block
        off = jnp.zeros((1, LANES), F32)
        acc = jnp.zeros((1, 1), F32)
        for g in range(1, N_GROUPS):
            acc = acc + jnp.sum(jnp.where(lane == g - 1, padded, 0.0), axis=-1, keepdims=True)
            off = off + jnp.where(lane == g, acc, 0.0)
        off_ref[...] = off
        run_ref[...] = jnp.zeros(run_ref.shape, F32)
        meta_ref[...] = jnp.zeros(meta_ref.shape, F32)
        meta_ref[0:1, :] = cnt_ref[...]
        meta_ref[1:2, :] = off

    @pl.when(phase == 1)
    def _():
        before = _dot(tri_ref[...], onehot.astype(BF16))
        slot = jnp.sum(onehot * (before + run_ref[...] + off_ref[...]), axis=-1, keepdims=True)
        pos_ref[...] = slot.astype(jnp.int32)
        run_ref[...] += jnp.sum(onehot, axis=0, keepdims=True)


def _moe_route(hx, d, tr, row_block):
    n = hx.shape[0]
    nb = n // tr
    tri = jnp.asarray(np.tril(np.ones((tr, tr), np.float32), -1), BF16)
    pos, meta = pl.pallas_call(
        functools.partial(_route_kernel, row_block=row_block),
        grid=(2, nb),
        in_specs=[pl.BlockSpec((tr, LANES), lambda p, i: (i, d // LANES)),
                  pl.BlockSpec((tr, tr), lambda p, i: (0, 0))],
        out_specs=[pl.BlockSpec((tr, 1), lambda p, i: (p * nb + i, 0)),
                   pl.BlockSpec((8, LANES), lambda p, i: (0, 0))],
        out_shape=[jax.ShapeDtypeStruct((2 * n, 1), jnp.int32), jax.ShapeDtypeStruct((8, LANES), F32)],
        scratch_shapes=[pltpu.VMEM((1, LANES), F32)] * 3,
        compiler_params=_cparams(("arbitrary", "arbitrary")),
        name="moe_route",
    )(hx, tri)
    return pos[n:], meta


def _row_dma_wait(src_hbm, dst_ref, sem, rows):
    pltpu.make_async_copy(src_hbm.at[pl.ds(0, rows)], dst_ref.at[pl.ds(0, rows)], sem).wait()


def _burst_pipeline(n_bursts, issue_burst, wait_burst):
    for b in range(n_bursts):
        issue_burst(b, b % 2)
        if b >= 1:
            wait_burst(b - 1, (b - 1) % 2)
    wait_burst(n_bursts - 1, (n_bursts - 1) % 2)


def _scatter_kernel(pos_ref, ends_ref, hx_ref, hs_hbm, zero_ref, sems, *, tr, burst, row_block):
    @pl.when(pl.program_id(0) == 0)
    def _():
        zero_ref[...] = jnp.zeros(zero_ref.shape, F32)
        run = zero_ref.shape[0]
        for g in range(N_GROUPS):
            start = pl.multiple_of((ends_ref[g] >> 3) << 3, 8)
            pltpu.make_async_copy(zero_ref, hs_hbm.at[pl.ds(start, run)], sems.at[0]).start()
        for g in range(N_GROUPS):
            pltpu.make_async_copy(zero_ref, hs_hbm.at[pl.ds(0, run)], sems.at[0]).wait()
        used = ends_ref[N_GROUPS]
        n_tail = (hs_hbm.shape[0] - used) // row_block
        for k in range(N_GROUPS + 1):
            @pl.when(k < n_tail)
            def _():
                start = pl.multiple_of(used + k * row_block, 8)
                cp = pltpu.make_async_copy(zero_ref.at[pl.ds(0, row_block)],
                                           hs_hbm.at[pl.ds(start, row_block)], sems.at[0])
                cp.start()
                cp.wait()

    def issue_burst(b, slot):
        def issue(t8, c):
            for k in range(MOE_DMA_UNROLL):
                row = b * burst + t8 * MOE_DMA_UNROLL + k
                pltpu.make_async_copy(hx_ref.at[pl.ds(row, 1)], hs_hbm.at[pl.ds(pos_ref[0, 0, row], 1)],
                                      sems.at[slot]).start(priority=k % 2)
            return c
        lax.fori_loop(0, burst // MOE_DMA_UNROLL, issue, 0)

    def wait_burst(b, slot):
        _row_dma_wait(hx_ref, hs_hbm, sems.at[slot], burst)

    _burst_pipeline(tr // burst, issue_burst, wait_burst)


def _moe_scatter(hx, pos3, real_ends, n_sorted, tr, row_block):
    n, w = hx.shape
    nb = n // tr
    return pl.pallas_call(
        functools.partial(_scatter_kernel, tr=tr, burst=min(MOE_DMA_BURST, tr), row_block=row_block),
        grid=(nb,),
        in_specs=[pl.BlockSpec((1, 1, tr), lambda i: (i, 0, 0), memory_space=pltpu.SMEM),
                  pl.BlockSpec(memory_space=pltpu.SMEM),
                  pl.BlockSpec((tr, w), lambda i: (i, 0))],
        out_specs=pl.BlockSpec(memory_space=pl.ANY),
        out_shape=jax.ShapeDtypeStruct((n_sorted, w), F32),
        scratch_shapes=[pltpu.VMEM((row_block + 8, w), F32), pltpu.SemaphoreType.DMA((2,))],
        compiler_params=_cparams(("arbitrary",)),
        name="moe_scatter",
    )(pos3, real_ends, hx)


def _experts_kernel(gmap_ref, valid_ref, hs_ref, wg_ref, wu_ref, wd_ref, o_ref, cw_ref, acc_ref, *, d):
    i = pl.program_id(0)
    j = pl.program_id(1)
    valid = valid_ref[i] > 0

    @pl.when(j == 0)
    def _():
        acc_ref[...] = jnp.zeros(acc_ref.shape, F32)

    @pl.when((j == 0) & valid)
    def _():
        logits = hs_ref[:, d:]
        lane = lax.broadcasted_iota(jnp.int32, logits.shape, 1).astype(F32)
        neg = -jnp.inf
        gl = jnp.where(lane < N_GROUPS, logits, neg)
        gmax = jnp.max(gl, axis=-1, keepdims=True)
        gsum = jnp.sum(jnp.exp(gl - gmax), axis=-1, keepdims=True)
        g_val = 1.0 / gsum
        g_idx = _first_index(gl == gmax, lane)
        lo = N_GROUPS + EXPERTS_PER_GROUP * g_idx
        el = jnp.where((lane >= lo) & (lane < lo + EXPERTS_PER_GROUP), logits, neg)
        emax = jnp.max(el, axis=-1, keepdims=True)
        esum = jnp.sum(jnp.exp(el - emax), axis=-1, keepdims=True)
        i1 = _first_index(el == emax, lane)
        el2 = jnp.where(lane == i1, neg, el)
        e2max = jnp.max(el2, axis=-1, keepdims=True)
        i2 = _first_index(el2 == e2max, lane)
        p1 = 1.0 / esum
        p2 = jnp.exp(e2max - emax) / esum
        tot = p1 + p2
        cw_ref[...] = (jnp.where(lane == i1, g_val * (p1 / tot), 0.0)
                       + jnp.where(lane == i2, g_val * (p2 / tot), 0.0))

    @pl.when(valid)
    def _():
        h = hs_ref[:, :d].astype(BF16)
        a = _dot(h, jnp.concatenate([wg_ref[0, 0], wg_ref[0, 1]], axis=1))
        hid = (a * jax.nn.sigmoid(a)) * _dot(h, jnp.concatenate([wu_ref[0, 0], wu_ref[0, 1]], axis=1))
        first = N_GROUPS + EXPERTS_PER_GROUP * gmap_ref[i] + 2 * j
        lane = lax.broadcasted_iota(jnp.int32, cw_ref.shape, 1)
        cw_all = cw_ref[...]
        cw0 = jnp.sum(jnp.where(lane == first, cw_all, 0.0), axis=-1, keepdims=True)
        cw1 = jnp.sum(jnp.where(lane == first + 1, cw_all, 0.0), axis=-1, keepdims=True)
        de = hid.shape[1] // 2
        hcol = lax.broadcasted_iota(jnp.int32, hid.shape, 1)
        hid = hid * jnp.where(hcol < de, cw0, cw1)
        acc_ref[...] += _dot(hid.astype(BF16), wd_ref[0])

    @pl.when(j == pl.num_programs(1) - 1)
    def _():
        o_ref[...] = acc_ref[...]


def _moe_experts(hs, gmap, valid, wg, wu, wd, d, row_block):
    n_sorted, w = hs.shape
    _, _, _, de = wg.shape
    de2 = 2 * de
    pairs = EXPERTS_PER_GROUP // 2
    grid_spec = pltpu.PrefetchScalarGridSpec(
        num_scalar_prefetch=2,
        grid=(n_sorted // row_block, pairs),
        in_specs=[pl.BlockSpec((row_block, w), lambda i, j, gm, va: (i * va[i], 0)),
                  pl.BlockSpec((1, 2, d, de), lambda i, j, gm, va: (gm[i] * pairs + j, 0, 0, 0)),
                  pl.BlockSpec((1, 2, d, de), lambda i, j, gm, va: (gm[i] * pairs + j, 0, 0, 0)),
                  pl.BlockSpec((1, de2, d), lambda i, j, gm, va: (gm[i] * pairs + j, 0, 0))],
        out_specs=pl.BlockSpec((row_block, d), lambda i, j, gm, va: (i, 0)),
        scratch_shapes=[pltpu.VMEM((row_block, LANES), F32), pltpu.VMEM((row_block, d), F32)])
    return pl.pallas_call(
        functools.partial(_experts_kernel, d=d),
        grid_spec=grid_spec,
        out_shape=jax.ShapeDtypeStruct((n_sorted, d), F32),
        compiler_params=_cparams(("arbitrary", "arbitrary")),
        name="moe_experts",
    )(gmap, valid, hs, wg, wu, wd)


def _combine_kernel(pos_ref, ys_hbm, x_ref, o_ref, buf_ref, sems, *, tr, burst):
    def issue_burst(b, slot):
        def issue(t8, c):
            for k in range(MOE_DMA_UNROLL):
                row = b * burst + t8 * MOE_DMA_UNROLL + k
                pltpu.make_async_copy(ys_hbm.at[pl.ds(pos_ref[0, 0, row], 1)], buf_ref.at[pl.ds(row, 1)],
                                      sems.at[slot]).start(priority=k % 2)
            return c
        lax.fori_loop(0, burst // MOE_DMA_UNROLL, issue, 0)

    def wait_burst(b, slot):
        _row_dma_wait(ys_hbm, buf_ref, sems.at[slot], burst)
        rows = slice(b * burst, (b + 1) * burst)
        o_ref[rows, :] = x_ref[rows, :] + buf_ref[rows, :]

    _burst_pipeline(tr // burst, issue_burst, wait_burst)


def _moe_combine(ys, pos3, x2, tr):
    n, d = x2.shape
    return pl.pallas_call(
        functools.partial(_combine_kernel, tr=tr, burst=min(MOE_DMA_BURST, tr)),
        grid=(n // tr,),
        in_specs=[pl.BlockSpec((1, 1, tr), lambda i: (i, 0, 0), memory_space=pltpu.SMEM),
                  pl.BlockSpec(memory_space=pl.ANY),
                  pl.BlockSpec((tr, d), lambda i: (i, 0))],
        out_specs=pl.BlockSpec((tr, d), lambda i: (i, 0)),
        out_shape=jax.ShapeDtypeStruct((n, d), F32),
        scratch_shapes=[pltpu.VMEM((tr, d), F32), pltpu.SemaphoreType.DMA((2,))],
        compiler_params=_cparams(("arbitrary",)),
        name="moe_combine",
    )(pos3, ys, x2)


def _pair_cols(w):
    ne, d, de = w.shape
    return w.astype(BF16).reshape(ne // 2, 2, d, de)


def _pair_rows(w):
    ne, de, d = w.shape
    return w.astype(BF16).reshape(ne // 2, 2 * de, d)


def _moe(hx, x2, wg, wu, wd, tr):
    n, d = x2.shape
    row_block = min(MOE_ROW_BLOCK, n)
    n_blocks = n // row_block + N_GROUPS + 1
    pos, meta = _moe_route(hx, d, tr, row_block)
    counts = meta[0, :N_GROUPS]
    ends = meta[1, :N_GROUPS] + jnp.ceil(counts / row_block) * row_block
    starts = jnp.arange(n_blocks, dtype=F32) * row_block
    grp = jnp.sum((starts[:, None] >= ends[None, :]).astype(jnp.int32), axis=1)
    valid = (grp < N_GROUPS).astype(jnp.int32)
    gmap = jnp.minimum(grp, N_GROUPS - 1)
    real_ends = jnp.concatenate([meta[1, :N_GROUPS] + counts, ends[N_GROUPS - 1:]]).astype(jnp.int32)
    pos3 = pos.reshape(n // tr, 1, tr)
    hs = _moe_scatter(hx, pos3, real_ends, n_blocks * row_block, tr, row_block)
    ys = _moe_experts(hs, gmap, valid, wg, wu, wd, d, row_block)
    return _moe_combine(ys, pos3, x2, tr)


def _pick(n, pref):
    t = min(n, pref)
    while n % t:
        t //= 2
    return t


def kernel(x, positions, norm_mix, w_in, da_q_gain, da_k_gain, da_lambda_q1, da_lambda_k1,
           da_lambda_q2, da_lambda_k2, da_subln_gain, hg_lower_bounds, hg_out_gain,
           s5_lambda_re, s5_lambda_im, s5_log_step, s5_b_re, s5_b_im, s5_c_re, s5_c_im,
           s5_d, s5_glu_w, s5_glu_b, w_branch_attn, w_branch_sb, w_branch_hgrn, w_branch_s5,
           w_out, norm_ffn, router_group_w, router_group_b, router_expert_w, router_expert_b,
           expert_w_gate, expert_w_up, expert_w_down):
    bsz, seq, d = x.shape
    depth = w_in.shape[0]
    n = bsz * seq
    assert w_in.shape[2] == IN_COLS and seq % 128 == 0
    tm = _pick(n, 1024)
    tq = _pick(seq, 256)
    tq_da = _pick(seq, DA_TQ)
    tb = _pick(seq, 512)

    x2 = x.reshape(n, d).astype(F32)
    cos_t, sina_t, sinb_t = _rope_tables(positions.reshape(n, 1).astype(jnp.int32), tm)

    lb_all = jnp.cumsum(jax.nn.softmax(hg_lower_bounds.astype(F32), axis=0), axis=0)
    lb_all = lb_all - lb_all[0:1]

    for l in range(depth):
        lambda_init = DA_LAMBDA_INIT_BASE - DA_LAMBDA_INIT_SCALE * math.exp(-DA_LAMBDA_INIT_RATE * l)
        w_f = w_in[l].astype(F32)
        sbq = COL_SB_Q - COL_DA_Q
        w_l = jnp.concatenate([w_f[:, REF_GATE_START:], w_f[:, :sbq],
                               w_f[:, sbq:sbq + SB_HEADS * SB_HEAD_DIM] * (SB_HEAD_DIM ** -0.5 * LOG2E),
                               w_f[:, sbq + SB_HEADS * SB_HEAD_DIM:REF_GATE_START]], axis=1)
        proj = _norm_proj(x2, norm_mix[l].astype(F32)[None, :], w_l.astype(BF16), _pick(n, NP_TM), 1536)

        qk_gain = jnp.concatenate([jnp.tile(da_q_gain[l].astype(F32), 2 * DA_HEADS),
                                   jnp.tile(da_k_gain[l].astype(F32), 2 * DA_HEADS)])[None, :]
        qk = _qk_prep(proj, qk_gain, cos_t, sina_t, sinb_t, tm)
        y_a = _diff_attn(qk, proj,
                         da_lambda_q1[l].astype(F32)[None, :], da_lambda_k1[l].astype(F32)[None, :],
                         da_lambda_q2[l].astype(F32)[None, :], da_lambda_k2[l].astype(F32)[None, :],
                         da_subln_gain[l].astype(F32)[None, :], bsz, seq, tq_da, DA_KW, DA_GROUP, lambda_init)
        y_b = _stick_break(proj, bsz, seq, _pick(seq, SB_TQ), SB_KW, SB_GROUP)
        y_c = _hgrn(proj, lb_all[l][None, :], jnp.tile(hg_out_gain[l].astype(F32), HG_HEADS)[None, :],
                    bsz, seq, tb)
        s5p = _s5_params(s5_lambda_re[l], s5_lambda_im[l], s5_log_step[l], s5_b_re[l], s5_b_im[l],
                         s5_c_re[l], s5_c_im[l], tb)
        y_d = _s5(proj, s5p, s5_d[l].astype(F32)[None, :], s5_glu_w[l].astype(BF16),
                  s5_glu_b[l].astype(F32)[None, :], bsz, seq, tb)

        wr = jnp.concatenate([router_group_w[l], router_expert_w[l]], axis=1).astype(F32)
        wr = jnp.pad(wr, ((0, 0), (0, LANES - wr.shape[1])))
        wr_hi = wr.astype(BF16)
        wr_lo = (wr - wr_hi.astype(F32)).astype(BF16)
        br = jnp.concatenate([router_group_b[l], router_expert_b[l]]).astype(F32)
        br = jnp.pad(br, (0, LANES - br.shape[0]))[None, :]
        x2, hx = _merge(x2, y_a, y_b, y_c, y_d, proj,
                        w_branch_attn[l].astype(BF16), w_branch_sb[l].astype(BF16),
                        w_branch_hgrn[l].astype(BF16), w_branch_s5[l].astype(BF16),
                        w_out[l].astype(BF16), norm_ffn[l].astype(F32)[None, :],
                        wr_hi, wr_lo, br, _pick(n, 512))
        x2 = _moe(hx, x2, _pair_cols(expert_w_gate[l]), _pair_cols(expert_w_up[l]),
                  _pair_rows(expert_w_down[l]), tm)

    return x2.reshape(bsz, seq, d).astype(x.dtype)
```

```python
import functools
import math

import jax
import jax.numpy as jnp
import numpy as np
from jax import lax
from jax.experimental import pallas as pl
from jax.experimental.pallas import tpu as pltpu

F32 = jnp.float32
BF16 = jnp.bfloat16

RMS_EPS = 1e-6
LANES = 128
LOG2E = 1.4426950408889634

DA_HEADS = 4
DA_QK_DIM = 64
ROPE_THETA = 500000.0
ROPE_DIM = DA_QK_DIM // 4
DA_LAMBDA_INIT_BASE = 0.8
DA_LAMBDA_INIT_SCALE = 0.6
DA_LAMBDA_INIT_RATE = 0.3

SB_HEADS = 4
SB_HEAD_DIM = 64

HG_HEADS = 4
HG_DIM = 64
HG_COLS = HG_HEADS * HG_DIM
HG_SUB = 16
HG_UNROLL = 16

S5_GROUPS = 16
S5_GROUP_CH = 16
S5_STATE = 64
S5_WIDTH = S5_GROUPS * S5_GROUP_CH
S5_NSTATE = S5_GROUPS * S5_STATE
S5_EIG_CLIP = -1e-4
S5_SUB = 8

N_GROUPS = 4
EXPERTS_PER_GROUP = 4

REF_GATE_START = 3584
COL_GATE = 0
COL_DA_Q = 4096
COL_DA_V = 5120
COL_SB_Q = 5632
COL_SB_K = 5888
COL_SB_V = 6144
COL_HG_F = 6400
COL_HG_I = 6656
COL_HG_Q = 6912
COL_HG_G = 7168
COL_S5_U = 7424
IN_COLS = 7680

VMEM_LIMIT = 48 * 1024 * 1024

ROW_TM = 1024
MERGE_TM = 512
SCAN_TB = 512
NP_TN = 1536
DA_TQ = 1024
DA_KW = 1024
DA_GROUP = 1
DA_DIAG_SPLIT = 4
SB_TQ = 512
SB_KW = 256
SB_GROUP = 4
MOE_ROW_BLOCK = 512
MOE_DMA_BURST = 256
MOE_DMA_UNROLL = 8
NP_TM = 2048

_NT = (((1,), (1,)), ((), ()))
_TN = (((0,), (0,)), ((), ()))


def _cparams(sem):
    return pltpu.CompilerParams(dimension_semantics=sem, vmem_limit_bytes=VMEM_LIMIT)


def _dot(a, b):
    return jnp.dot(a, b, preferred_element_type=F32)


def _norm_proj_kernel(x_ref, g_ref, w_ref, o_ref, h_ref):
    @pl.when(pl.program_id(1) == 0)
    def _():
        x = x_ref[...]
        ms = jnp.mean(x * x, axis=-1, keepdims=True)
        h_ref[...] = (x * lax.rsqrt(ms + RMS_EPS) * g_ref[...]).astype(BF16)

    o_ref[...] = _dot(h_ref[...], w_ref[...]).astype(o_ref.dtype)


def _norm_proj(x2, gain, w, tm, tn):
    n, d = x2.shape
    cols = w.shape[1]
    return pl.pallas_call(
        _norm_proj_kernel,
        grid=(n // tm, cols // tn),
        in_specs=[pl.BlockSpec((tm, d), lambda i, j: (i, 0)),
                  pl.BlockSpec((1, d), lambda i, j: (0, 0)),
                  pl.BlockSpec((d, tn), lambda i, j: (0, j))],
        out_specs=pl.BlockSpec((tm, tn), lambda i, j: (i, j)),
        out_shape=jax.ShapeDtypeStruct((n, cols), BF16),
        scratch_shapes=[pltpu.VMEM((tm, d), BF16)],
        compiler_params=_cparams(("parallel", "arbitrary")),
        name="norm_proj",
    )(x2, gain, w)


def _rope_kernel(pos_ref, invf_ref, sa_ref, sb_ref, cos_ref, sina_ref, sinb_ref):
    ang = pos_ref[...].astype(F32) * invf_ref[...]
    c = jnp.cos(ang)
    s = jnp.sin(ang)
    cos_ref[...] = c
    sina_ref[...] = s * sa_ref[...]
    sinb_ref[...] = s * sb_ref[...]


def _rope_tables(pos_col, tm):
    n = pos_col.shape[0]
    half = ROPE_DIM // 2
    inv_freq = jnp.exp(-math.log(ROPE_THETA) * jnp.arange(half, dtype=F32) * (2.0 / ROPE_DIM))
    lane = np.arange(LANES) % DA_QK_DIM
    invf = jnp.where(lane < ROPE_DIM, inv_freq[lane % half], 0.0).astype(F32)[None, :]
    sgn_a = jnp.asarray(np.where(lane < half, -1.0, 0.0), F32)[None, :]
    sgn_b = jnp.asarray(np.where((lane >= half) & (lane < ROPE_DIM), 1.0, 0.0), F32)[None, :]
    row = pl.BlockSpec((1, LANES), lambda i: (0, 0))
    tab = pl.BlockSpec((tm, LANES), lambda i: (i, 0))
    shp = jax.ShapeDtypeStruct((n, LANES), F32)
    return pl.pallas_call(
        _rope_kernel,
        grid=(n // tm,),
        in_specs=[pl.BlockSpec((tm, 1), lambda i: (i, 0)), row, row, row],
        out_specs=[tab, tab, tab],
        out_shape=[shp, shp, shp],
        compiler_params=_cparams(("parallel",)),
        name="rope_tables",
    )(pos_col, invf, sgn_a, sgn_b)


def _qk_prep_kernel(x_ref, gain_ref, cos_ref, sina_ref, sinb_ref, bd_ref, o_ref):
    c = cos_ref[...]
    sa = sina_ref[...]
    sb = sinb_ref[...]
    bd = bd_ref[...]
    n_tiles = x_ref.shape[1] // LANES
    for j in range(n_tiles):
        sl = slice(j * LANES, (j + 1) * LANES)
        t = x_ref[:, sl].astype(F32)
        ss = _dot((t * t).astype(BF16), bd)
        y = t * lax.rsqrt(ss * (1.0 / DA_QK_DIM) + RMS_EPS) * gain_ref[:, sl]
        y = (y * c + pltpu.roll(y, LANES - ROPE_DIM // 2, 1) * sa
             + pltpu.roll(y, ROPE_DIM // 2, 1) * sb)
        if j < n_tiles // 2:
            y = y * (DA_QK_DIM ** -0.5 * LOG2E)
        o_ref[:, sl] = y.astype(BF16)


def _qk_prep(proj, gain_row, cos_t, sina_t, sinb_t, tm):
    n = proj.shape[0]
    w = 2 * DA_HEADS * 2 * DA_QK_DIM
    seg = np.arange(LANES) // DA_QK_DIM
    bd = jnp.asarray(seg[:, None] == seg[None, :], BF16)
    tab = pl.BlockSpec((tm, LANES), lambda i: (i, 0))
    return pl.pallas_call(
        _qk_prep_kernel,
        grid=(n // tm,),
        in_specs=[pl.BlockSpec((tm, w), lambda i: (i, COL_DA_Q // w)),
                  pl.BlockSpec((1, w), lambda i: (0, 0)),
                  tab, tab, tab,
                  pl.BlockSpec((LANES, LANES), lambda i: (0, 0))],
        out_specs=pl.BlockSpec((tm, w), lambda i: (i, 0)),
        out_shape=jax.ShapeDtypeStruct((n, w), BF16),
        compiler_params=_cparams(("parallel",)),
        name="qk_prep",
    )(proj, gain_row, cos_t, sina_t, sinb_t, bd)


def _diff_attn_kernel(q_ref, k_ref, v_ref, lq1_ref, lk1_ref, lq2_ref, lk2_ref, sg_ref, o_ref,
                      m_ref, l_ref, acc_ref, *, tq, kw, group, lambda_init):
    qi = pl.program_id(2)
    q = q_ref[...]
    lane = lax.broadcasted_iota(jnp.int32, q.shape, 1)
    zero = jnp.zeros_like(q)
    q2 = jnp.concatenate([jnp.where(lane < DA_QK_DIM, q, zero),
                          jnp.where(lane >= DA_QK_DIM, q, zero)], axis=0)
    m_ref[...] = jnp.full(m_ref.shape, -jnp.inf, F32)
    l_ref[...] = jnp.zeros(l_ref.shape, F32)
    acc_ref[...] = jnp.zeros(acc_ref.shape, F32)

    def rows_of(ref_or_val, row_lo):
        if row_lo == 0:
            return ref_or_val[...]
        return jnp.concatenate([ref_or_val[row_lo:tq], ref_or_val[tq + row_lo:2 * tq]], axis=0)

    def put_rows(ref, row_lo, val):
        if row_lo == 0:
            ref[...] = val
        else:
            ref[row_lo:tq] = val[:tq - row_lo]
            ref[tq + row_lo:2 * tq] = val[tq - row_lo:]

    def tile(start, width, mask_off, row_lo=0):
        n_rows = tq - row_lo
        n_lt = width // LANES
        kb = k_ref[pl.ds(start, width), :]
        vb = v_ref[pl.ds(start, width), :]
        s = lax.dot_general(rows_of(q2, row_lo), kb, _NT, preferred_element_type=F32)
        if mask_off is not None:
            keep = (lax.broadcasted_iota(jnp.int32, (n_rows, width), 1) + mask_off
                    <= lax.broadcasted_iota(jnp.int32, (n_rows, width), 0) + row_lo)
            s = jnp.where(jnp.concatenate([keep, keep], axis=0), s, -jnp.inf)
        st = [s[:, j * LANES:(j + 1) * LANES] for j in range(n_lt)]
        smax = st[0]
        for j in range(1, n_lt):
            smax = jnp.maximum(smax, st[j])
        m_prev = rows_of(m_ref, row_lo)
        m_new = jnp.maximum(m_prev, jnp.max(smax, axis=-1, keepdims=True))
        alpha = jnp.exp2(m_prev - m_new)
        ps = [jnp.exp2(t - m_new) for t in st]
        lsum = ps[0]
        for j in range(1, n_lt):
            lsum = lsum + ps[j]
        p = jnp.concatenate([t.astype(BF16) for t in ps], axis=1)
        put_rows(l_ref, row_lo, alpha * rows_of(l_ref, row_lo) + lsum)
        put_rows(acc_ref, row_lo, alpha * rows_of(acc_ref, row_lo) + _dot(p, vb))
        put_rows(m_ref, row_lo, m_new)

    n_diag = DA_DIAG_SPLIT
    dw = tq // n_diag
    n_below = qi * (tq // kw)

    def body_group(gi, carry):
        for j in range(group):
            tile(pl.multiple_of((gi * group + j) * kw, kw), kw, None)
        return carry

    def body_single(ki, carry):
        tile(pl.multiple_of(ki * kw, kw), kw, None)
        return carry

    n_group = n_below // group
    lax.fori_loop(0, n_group, body_group, 0)
    lax.fori_loop(n_group * group, n_below, body_single, 0)
    for j in range(n_diag):
        tile(pl.multiple_of(qi * tq + j * dw, dw), dw, j * dw, row_lo=j * dw)

    lam = (jnp.exp(jnp.sum(lq1_ref[...] * lk1_ref[...], axis=-1, keepdims=True))
           - jnp.exp(jnp.sum(lq2_ref[...] * lk2_ref[...], axis=-1, keepdims=True)) + lambda_init)
    o = acc_ref[...] / jnp.sum(l_ref[...], axis=-1, keepdims=True)
    o = o[:tq] - lam * o[tq:]
    ms = jnp.mean(o * o, axis=-1, keepdims=True)
    o = o * lax.rsqrt(ms + RMS_EPS) * sg_ref[...] * (1.0 - lambda_init)
    o_ref[...] = o.astype(o_ref.dtype)


def _diff_attn(qk, proj, lq1, lk1, lq2, lk2, subln, bsz, seq, tq, kw, group, lambda_init):
    n = qk.shape[0]
    nq = seq // tq
    kcol = (DA_HEADS * 2 * DA_QK_DIM) // LANES
    vcol = COL_DA_V // LANES
    vec = pl.BlockSpec((1, DA_QK_DIM), lambda b, h, i: (0, 0))
    return pl.pallas_call(
        functools.partial(_diff_attn_kernel, tq=tq, kw=min(kw, tq), group=group, lambda_init=lambda_init),
        grid=(bsz, DA_HEADS, nq),
        in_specs=[pl.BlockSpec((tq, LANES), lambda b, h, i: (b * nq + i, h)),
                  pl.BlockSpec((seq, LANES), lambda b, h, i: (b, kcol + h)),
                  pl.BlockSpec((seq, LANES), lambda b, h, i: (b, vcol + h)),
                  vec, vec, vec, vec,
                  pl.BlockSpec((1, LANES), lambda b, h, i: (0, 0))],
        out_specs=pl.BlockSpec((tq, LANES), lambda b, h, i: (b * nq + i, h)),
        out_shape=jax.ShapeDtypeStruct((n, DA_HEADS * LANES), BF16),
        scratch_shapes=[pltpu.VMEM((2 * tq, LANES), F32), pltpu.VMEM((2 * tq, LANES), F32),
                        pltpu.VMEM((2 * tq, LANES), F32)],
        compiler_params=_cparams(("parallel", "parallel", "arbitrary")),
        name="diff_attn",
    )(qk, qk, proj, lq1, lk1, lq2, lk2, subln)


def _stick_break_kernel(q_ref, k_ref, v_ref, o_ref, r_ref, acc_ref, *, tq, kw, group):
    qi = pl.program_id(2)
    q = q_ref[...]
    lane = lax.broadcasted_iota(jnp.int32, q.shape, 1)
    zero = jnp.zeros_like(q)
    q2 = jnp.concatenate([jnp.where(lane < SB_HEAD_DIM, q, zero),
                          jnp.where(lane >= SB_HEAD_DIM, q, zero)], axis=0)
    incl = jnp.where(lax.broadcasted_iota(jnp.int32, (kw, kw), 0)
                     >= lax.broadcasted_iota(jnp.int32, (kw, kw), 1), 1.0, 0.0).astype(BF16)
    n_lt = kw // LANES
    n_diag = tq // kw
    r_ref[...] = jnp.zeros(r_ref.shape, F32)
    acc_ref[...] = jnp.zeros(acc_ref.shape, F32)

    def run(starts, offsets):
        r = r_ref[...]
        total = None
        for start, off in zip(starts, offsets):
            u = lax.dot_general(q2, k_ref[pl.ds(start, kw), :], _NT, preferred_element_type=F32)
            neg_abs = lax.bitcast_convert_type(
                lax.bitcast_convert_type(u, jnp.int32) | jnp.int32(-2 ** 31), F32)
            sp = jnp.maximum(u, 0.0) + jnp.log(1.0 + jnp.exp2(neg_abs)) * LOG2E
            mask = None
            if off is not None:
                keep = (lax.broadcasted_iota(jnp.int32, (tq, kw), 1) + off
                        < lax.broadcasted_iota(jnp.int32, (tq, kw), 0))
                mask = jnp.concatenate([keep, keep], axis=0)
                sp = jnp.where(mask, sp, 0.0)
            cum = _dot(sp.astype(BF16), incl)
            w = jnp.exp2(u - (cum + jnp.concatenate([r] * n_lt, axis=1)))
            if mask is not None:
                w = jnp.where(mask, w, 0.0)
            part = _dot(w.astype(BF16), v_ref[pl.ds(start, kw), :])
            total = part if total is None else total + part
            r = r + cum[:, 0:1]
        acc_ref[...] += total
        r_ref[...] = r

    base = qi * tq
    run([pl.multiple_of(base + (n_diag - 1 - j) * kw, kw) for j in range(n_diag)],
        [(n_diag - 1 - j) * kw for j in range(n_diag)])

    n_below = qi * n_diag
    rem = n_below % group

    def body_rem(pi, carry):
        top = n_below - pi * n_diag
        run([pl.multiple_of((top - 1 - j) * kw, kw) for j in range(n_diag)], [None] * n_diag)
        return carry

    def body_group(gi, carry):
        top = n_below - rem - gi * group
        run([pl.multiple_of((top - 1 - j) * kw, kw) for j in range(group)], [None] * group)
        return carry

    lax.fori_loop(0, rem // n_diag, body_rem, 0)
    lax.fori_loop(0, n_below // group, body_group, 0)
    lane_o = lax.broadcasted_iota(jnp.int32, (tq, LANES), 1)
    o_ref[...] = jnp.where(lane_o < SB_HEAD_DIM, acc_ref[:tq, :], acc_ref[tq:, :]).astype(o_ref.dtype)


def _stick_break(proj, bsz, seq, tq, kw, group):
    assert tq % kw == 0 and group % (tq // kw) == 0
    n = proj.shape[0]
    nq = seq // tq
    pairs = (SB_HEADS * SB_HEAD_DIM) // LANES
    qc, kc, vc = COL_SB_Q // LANES, COL_SB_K // LANES, COL_SB_V // LANES
    return pl.pallas_call(
        functools.partial(_stick_break_kernel, tq=tq, kw=kw, group=group),
        grid=(bsz, pairs, nq),
        in_specs=[pl.BlockSpec((tq, LANES), lambda b, p, i: (b * nq + i, qc + p)),
                  pl.BlockSpec((seq, LANES), lambda b, p, i: (b, kc + p)),
                  pl.BlockSpec((seq, LANES), lambda b, p, i: (b, vc + p))],
        out_specs=pl.BlockSpec((tq, LANES), lambda b, p, i: (b * nq + i, p)),
        out_shape=jax.ShapeDtypeStruct((n, pairs * LANES), BF16),
        scratch_shapes=[pltpu.VMEM((2 * tq, LANES), F32), pltpu.VMEM((2 * tq, LANES), F32)],
        compiler_params=_cparams(("parallel", "parallel", "arbitrary")),
        name="stick_break",
    )(proj, proj, proj)


def _split_bf16(x):
    hi = x.astype(BF16)
    lo = (x - hi.astype(F32)).astype(BF16)
    return hi, lo


def _hgrn_kernel(f_ref, i_ref, q_ref, g_ref, lb_ref, gain_ref, o_ref,
                 st_ref, b_ref, dec_ref, qq_ref, kk_ref, vv_ref, oo_ref, qe_ref, ke_ref, gm_ref, uu_ref,
                 *, tb):
    @pl.when(pl.program_id(1) == 0)
    def _():
        st_ref[...] = jnp.zeros(st_ref.shape, F32)

    nsub = tb // HG_SUB
    z = f_ref[...].astype(F32)
    lb = lb_ref[...]
    sp = jnp.maximum(z, 0.0) + jnp.log(1.0 + jnp.exp(-jnp.abs(z)))
    log_sig = z - sp
    a = jnp.log(lb)
    c = jnp.log(1.0 - lb) + log_sig
    mx = jnp.maximum(a, c)
    log_f = mx + jnp.log(jnp.exp(a - mx) + jnp.exp(c - mx))
    key = (1.0 - lb) * jax.nn.sigmoid(-z)

    row = lax.broadcasted_iota(jnp.int32, (tb, tb), 0)
    col = lax.broadcasted_iota(jnp.int32, (tb, tb), 1)
    sub_shift = HG_SUB.bit_length() - 1
    dim_shift = HG_DIM.bit_length() - 1
    same = (row >> sub_shift) == (col >> sub_shift)
    tri = jnp.where(same & (col <= row), 1.0, 0.0).astype(BF16)
    blk = jnp.where(same, 1.0, 0.0).astype(BF16)
    hi, lo = _split_bf16(log_f)
    b = _dot(tri, hi) + _dot(tri, lo)
    e = _dot(blk, hi) + _dot(blk, lo)
    qf = q_ref[...].astype(F32)
    b_ref[...] = b
    qq_ref[...] = qf
    kk_ref[...] = key
    vv_ref[...] = i_ref[...].astype(F32)
    qe_ref[...] = (qf * jnp.exp(b)).astype(BF16)
    ke_ref[...] = (key * jnp.exp(e - b)).astype(BF16)
    dec_ref[...] = jnp.exp(e)

    seg_r = lax.broadcasted_iota(jnp.int32, (HG_COLS, HG_COLS), 0) >> dim_shift
    seg_c = lax.broadcasted_iota(jnp.int32, (HG_COLS, HG_COLS), 1) >> dim_shift
    head_mask = seg_r == seg_c
    ones_bd = jnp.where(head_mask, 1.0, 0.0).astype(BF16)
    sel = jnp.where(lax.broadcasted_iota(jnp.int32, (HG_SUB, HG_SUB * HG_SUB), 0)
                    == (lax.broadcasted_iota(jnp.int32, (HG_SUB, HG_SUB * HG_SUB), 1) >> sub_shift),
                    1.0, 0.0).astype(BF16)
    srow = lax.broadcasted_iota(jnp.int32, (HG_SUB, HG_COLS), 0)

    def intra(ci, slot):
        r0 = pl.multiple_of(ci * HG_SUB, HG_SUB)
        bi = b_ref[pl.ds(r0, HG_SUB), :]
        qi = qq_ref[pl.ds(r0, HG_SUB), :]
        ki = kk_ref[pl.ds(r0, HG_SUB), :]
        vi = vv_ref[pl.ds(r0, HG_SUB), :]
        for t in range(HG_SUB):
            d = jnp.exp(jnp.minimum(bi[t:t + 1, :] - bi, 0.0))
            g = jnp.where(srow <= t, qi[t:t + 1, :] * ki * d, 0.0)
            gm_ref[slot, t * HG_SUB:(t + 1) * HG_SUB, :] = g.astype(BF16)
        sc = _dot(gm_ref[slot], ones_bd)
        vt = jnp.concatenate([vi] * HG_SUB, axis=0)
        oo_ref[pl.ds(r0, HG_SUB), :] = _dot(sel, (sc * vt).astype(BF16))
        upd = lax.dot_general(vi.astype(BF16), ke_ref[pl.ds(r0, HG_SUB), :], _TN,
                              preferred_element_type=F32)
        uu_ref[ci] = jnp.where(head_mask, upd, 0.0)

    def intra_group(gi, carry):
        for slot in range(HG_UNROLL):
            intra(gi * HG_UNROLL + slot, slot)
        return carry

    lax.fori_loop(0, nsub // HG_UNROLL, intra_group, 0)

    st = st_ref[...]
    for ci in range(nsub):
        rows = slice(ci * HG_SUB, (ci + 1) * HG_SUB)
        oo_ref[rows, :] += lax.dot_general(qe_ref[rows, :], st.astype(BF16), _NT,
                                           preferred_element_type=F32)
        st = st * dec_ref[ci * HG_SUB:ci * HG_SUB + 1, :] + uu_ref[ci]
    st_ref[...] = st

    o = oo_ref[...]
    ms = _dot((o * o).astype(BF16), ones_bd) * (1.0 / HG_DIM)
    gate = g_ref[...].astype(F32)
    gate = gate * jax.nn.sigmoid(gate)
    o_ref[...] = (o * lax.rsqrt(ms + RMS_EPS) * gain_ref[...] * gate).astype(o_ref.dtype)


def _hgrn(proj, lb_row, gain_row, bsz, seq, tb):
    n = proj.shape[0]
    nb = seq // tb
    w = HG_COLS
    cf, ci, cq, cg = COL_HG_F // w, COL_HG_I // w, COL_HG_Q // w, COL_HG_G // w

    def col(cc):
        return pl.BlockSpec((tb, w), lambda b, i: (b * nb + i, cc))

    rowspec = pl.BlockSpec((1, w), lambda b, i: (0, 0))
    big = pltpu.VMEM((tb, w), F32)
    return pl.pallas_call(
        functools.partial(_hgrn_kernel, tb=tb),
        grid=(bsz, nb),
        in_specs=[col(cf), col(ci), col(cq), col(cg), rowspec, rowspec],
        out_specs=pl.BlockSpec((tb, w), lambda b, i: (b * nb + i, 0)),
        out_shape=jax.ShapeDtypeStruct((n, w), BF16),
        scratch_shapes=[pltpu.VMEM((w, w), F32), big, big, big, big, big, big,
                        pltpu.VMEM((tb, w), BF16), pltpu.VMEM((tb, w), BF16),
                        pltpu.VMEM((HG_UNROLL, HG_SUB * HG_SUB, w), BF16),
                        pltpu.VMEM((tb // HG_SUB, w, w), F32)],
        compiler_params=_cparams(("parallel", "arbitrary")),
        name="hgrn2",
    )(proj, proj, proj, proj, lb_row, gain_row)


def _s5_kernel(u_ref, bblk_ref, lev_re_ref, lev_im_ref, pw_re_ref, pw_im_ref, cblk_ref, d_ref,
               gw_ref, gb_ref, o_ref, cr_ref, ci_ref, xr_ref, xi_ref, *, tb):
    @pl.when(pl.program_id(1) == 0)
    def _():
        cr_ref[...] = jnp.zeros(cr_ref.shape, F32)
        ci_ref[...] = jnp.zeros(ci_ref.shape, F32)

    u = u_ref[...]
    bu = _dot(u, bblk_ref[...])
    xr = bu[:, :S5_NSTATE]
    xi = bu[:, S5_NSTATE:]
    row = lax.broadcasted_iota(jnp.int32, xr.shape, 0)
    sub_lev = S5_SUB.bit_length() - 1
    for j in range(sub_lev):
        d = 1 << j
        ar = lev_re_ref[j:j + 1, :]
        ai = lev_im_ref[j:j + 1, :]
        keep = row >= d
        sr = jnp.where(keep, pltpu.roll(xr, d, 0), 0.0)
        si = jnp.where(keep, pltpu.roll(xi, d, 0), 0.0)
        xr, xi = xr + ar * sr - ai * si, xi + ar * si + ai * sr
    cr = cr_ref[...]
    ci = ci_ref[...]
    pr = pw_re_ref[...]
    pi = pw_im_ref[...]
    gr = xr[:S5_SUB] + pr * cr - pi * ci
    gi = xi[:S5_SUB] + pr * ci + pi * cr
    xr_ref[:S5_SUB, :] = gr
    xi_ref[:S5_SUB, :] = gi
    ar = lev_re_ref[sub_lev:sub_lev + 1, :]
    ai = lev_im_ref[sub_lev:sub_lev + 1, :]
    for g in range(1, tb // S5_SUB):
        rows = slice(g * S5_SUB, (g + 1) * S5_SUB)
        gr, gi = xr[rows] + ar * gr - ai * gi, xi[rows] + ar * gi + ai * gr
        xr_ref[rows, :] = gr
        xi_ref[rows, :] = gi
    cr_ref[...] = gr[S5_SUB - 1:S5_SUB, :]
    ci_ref[...] = gi[S5_SUB - 1:S5_SUB, :]
    y = (_dot(xr_ref[...].astype(BF16), cblk_ref[:S5_NSTATE, :])
         + _dot(xi_ref[...].astype(BF16), cblk_ref[S5_NSTATE:, :]) + d_ref[...] * u.astype(F32))
    y = jax.nn.gelu(y)
    zg = _dot(y.astype(BF16), gw_ref[...]) + gb_ref[...]
    o_ref[...] = (y * jax.nn.sigmoid(zg)).astype(o_ref.dtype)


def _s5_params(lam_re, lam_im, log_step, b_re, b_im, c_re, c_im, tb):
    lam_re = jnp.minimum(lam_re.astype(F32), S5_EIG_CLIP)
    lam_im = lam_im.astype(F32)
    step = jnp.exp(log_step.astype(F32))[:, None]
    mag = jnp.exp(lam_re * step)
    phase = lam_im * step
    a_re = mag * jnp.cos(phase)
    a_im = mag * jnp.sin(phase)
    denom = lam_re * lam_re + lam_im * lam_im
    num_re = a_re - 1.0
    gam_re = (num_re * lam_re + a_im * lam_im) / denom
    gam_im = (a_im * lam_re - num_re * lam_im) / denom
    b_re = b_re.astype(F32)
    b_im = b_im.astype(F32)
    bb_re = gam_re[..., None] * b_re - gam_im[..., None] * b_im
    bb_im = gam_re[..., None] * b_im + gam_im[..., None] * b_re
    eye = jnp.eye(S5_GROUPS, dtype=F32)

    def in_blk(bb):
        return jnp.einsum('gnc,gh->gchn', bb, eye).reshape(S5_WIDTH, S5_NSTATE)

    def out_blk(cc):
        return jnp.einsum('gcn,gh->gnhc', cc.astype(F32), eye).reshape(S5_NSTATE, S5_WIDTH)

    bblk = jnp.concatenate([in_blk(bb_re), in_blk(bb_im)], axis=1).astype(BF16)
    cblk = jnp.concatenate([out_blk(c_re), -out_blk(c_im)], axis=0).astype(BF16)
    ar = a_re.reshape(1, S5_NSTATE)
    ai = a_im.reshape(1, S5_NSTATE)
    sub_lev = S5_SUB.bit_length() - 1
    lev_re, lev_im = [ar], [ai]
    pw_re, pw_im = ar, ai
    for _ in range(sub_lev):
        sr, si = lev_re[-1], lev_im[-1]
        pw_re, pw_im = (jnp.concatenate([pw_re, pw_re * sr - pw_im * si], axis=0),
                        jnp.concatenate([pw_im, pw_re * si + pw_im * sr], axis=0))
        lev_re.append(sr * sr - si * si)
        lev_im.append(2.0 * sr * si)
    lev_re = jnp.concatenate(lev_re, axis=0)
    lev_im = jnp.concatenate(lev_im, axis=0)
    return bblk, cblk, lev_re, lev_im, pw_re, pw_im


def _s5(proj, params, d_row, glu_w, glu_b, bsz, seq, tb):
    n = proj.shape[0]
    nb = seq // tb
    bblk, cblk, lev_re, lev_im, pw_re, pw_im = params
    nlev = lev_re.shape[0]
    ucol = COL_S5_U // S5_WIDTH

    def full(shape):
        return pl.BlockSpec(shape, lambda b, i: (0,) * len(shape))

    return pl.pallas_call(
        functools.partial(_s5_kernel, tb=tb),
        grid=(bsz, nb),
        in_specs=[pl.BlockSpec((tb, S5_WIDTH), lambda b, i: (b * nb + i, ucol)),
                  full((S5_WIDTH, 2 * S5_NSTATE)),
                  full((nlev, S5_NSTATE)), full((nlev, S5_NSTATE)),
                  full((S5_SUB, S5_NSTATE)), full((S5_SUB, S5_NSTATE)),
                  full((2 * S5_NSTATE, S5_WIDTH)),
                  full((1, S5_WIDTH)), full((S5_WIDTH, S5_WIDTH)), full((1, S5_WIDTH))],
        out_specs=pl.BlockSpec((tb, S5_WIDTH), lambda b, i: (b * nb + i, 0)),
        out_shape=jax.ShapeDtypeStruct((n, S5_WIDTH), BF16),
        scratch_shapes=[pltpu.VMEM((1, S5_NSTATE), F32), pltpu.VMEM((1, S5_NSTATE), F32),
                        pltpu.VMEM((tb, S5_NSTATE), F32), pltpu.VMEM((tb, S5_NSTATE), F32)],
        compiler_params=_cparams(("parallel", "arbitrary")),
        name="s5",
    )(proj, bblk, lev_re, lev_im, pw_re, pw_im, cblk, d_row, glu_w, glu_b)


def _merge_kernel(x_ref, ya_ref, yb_ref, yc_ref, yd_ref, gate_ref, wa_ref, wb_ref, wc_ref, wd_ref,
                  wo_ref, gn_ref, wrh_ref, wrl_ref, br_ref, xo_ref, hx_ref):
    d = x_ref.shape[1]
    merged = None
    for i, (y_ref, w_ref) in enumerate(((ya_ref, wa_ref), (yb_ref, wb_ref),
                                        (yc_ref, wc_ref), (yd_ref, wd_ref))):
        gate = 0.5 * jnp.tanh(0.5 * gate_ref[:, i * d:(i + 1) * d].astype(F32)) + 0.5
        term = gate * _dot(y_ref[...], w_ref[...])
        merged = term if merged is None else merged + term
    xn = x_ref[...] + _dot(merged.astype(BF16), wo_ref[...])
    xo_ref[...] = xn
    ms = jnp.mean(xn * xn, axis=-1, keepdims=True)
    h = xn * lax.rsqrt(ms + RMS_EPS) * gn_ref[...]
    hi, lo = _split_bf16(h)
    hx_ref[:, :d] = hi.astype(F32)
    wrh = wrh_ref[...]
    hx_ref[:, d:] = _dot(hi, wrh) + _dot(lo, wrh) + _dot(hi, wrl_ref[...]) + br_ref[...]


def _merge(x2, ya, yb, yc, yd, proj, wa, wb, wc, wd, wo, gn_row, wr_hi, wr_lo, br_row, tm):
    n, d = x2.shape
    assert COL_GATE % (4 * d) == 0
    gcol = COL_GATE // (4 * d)

    def rows(width, cc=0):
        return pl.BlockSpec((tm, width), lambda i: (i, cc))

    def full(arr):
        return pl.BlockSpec(arr.shape, lambda i: (0, 0))

    return pl.pallas_call(
        _merge_kernel,
        grid=(n // tm,),
        in_specs=[rows(d), rows(ya.shape[1]), rows(yb.shape[1]), rows(yc.shape[1]), rows(yd.shape[1]),
                  rows(4 * d, gcol),
                  full(wa), full(wb), full(wc), full(wd), full(wo), full(gn_row),
                  full(wr_hi), full(wr_lo), full(br_row)],
        out_specs=[rows(d), rows(d + LANES)],
        out_shape=[jax.ShapeDtypeStruct((n, d), F32), jax.ShapeDtypeStruct((n, d + LANES), F32)],
        compiler_params=_cparams(("parallel",)),
        name="merge",
    )(x2, ya, yb, yc, yd, proj, wa, wb, wc, wd, wo, gn_row, wr_hi, wr_lo, br_row)


def _first_index(mask, lane):
    return jnp.min(jnp.where(mask, lane, float(LANES)), axis=-1, keepdims=True)


def _group_onehot(logits):
    lane = lax.broadcasted_iota(jnp.int32, logits.shape, 1).astype(F32)
    gl = jnp.where(lane < N_GROUPS, logits, -jnp.inf)
    g_idx = _first_index(gl == jnp.max(gl, axis=-1, keepdims=True), lane)
    return jnp.where(lane == g_idx, 1.0, 0.0)


def _route_kernel(lg_ref, tri_ref, pos_ref, meta_ref, cnt_ref, run_ref, off_ref, *, row_block):
    phase = pl.program_id(0)
    i = pl.program_id(1)
    onehot = _group_onehot(lg_ref[...])
    lane = lax.broadcasted_iota(jnp.int32, (1, LANES), 1)

    @pl.when((phase == 0) & (i == 0))
    def _():
        cnt_ref[...] = jnp.zeros(cnt_ref.shape, F32)

    @pl.when(phase == 0)
    def _():
        cnt_ref[...] += jnp.sum(onehot, axis=0, keepdims=True)
        pos_ref[...] = jnp.zeros(pos_ref.shape, jnp.int32)

    @pl.when((phase == 1) & (i == 0))
    def _():
        padded = jnp.floor((cnt_ref[...] + (row_block - 1)) * (1.0 / row_block)) * row_block
        off = jnp.zeros((1, LANES), F32)
        acc = jnp.zeros((1, 1), F32)
        for g in range(1, N_GROUPS):
            acc = acc + jnp.sum(jnp.where(lane == g - 1, padded, 0.0), axis=-1, keepdims=True)
            off = off + jnp.where(lane == g, acc, 0.0)
        off_ref[...] = off
        run_ref[...] = jnp.zeros(run_ref.shape, F32)
        meta_ref[...] = jnp.zeros(meta_ref.shape, F32)
        meta_ref[0:1, :] = cnt_ref[...]
        meta_ref[1:2, :] = off

    @pl.when(phase == 1)
    def _():
        before = _dot(tri_ref[...], onehot.astype(BF16))
        slot = jnp.sum(onehot * (before + run_ref[...] + off_ref[...]), axis=-1, keepdims=True)
        pos_ref[...] = slot.astype(jnp.int32)
        run_ref[...] += jnp.sum(onehot, axis=0, keepdims=True)


def _moe_route(hx, d, tr, row_block):
    n = hx.shape[0]
    nb = n // tr
    tri = jnp.asarray(np.tril(np.ones((tr, tr), np.float32), -1), BF16)
    pos, meta = pl.pallas_call(
        functools.partial(_route_kernel, row_block=row_block),
        grid=(2, nb),
        in_specs=[pl.BlockSpec((tr, LANES), lambda p, i: (i, d // LANES)),
                  pl.BlockSpec((tr, tr), lambda p, i: (0, 0))],
        out_specs=[pl.BlockSpec((tr, 1), lambda p, i: (p * nb + i, 0)),
                   pl.BlockSpec((8, LANES), lambda p, i: (0, 0))],
        out_shape=[jax.ShapeDtypeStruct((2 * n, 1), jnp.int32), jax.ShapeDtypeStruct((8, LANES), F32)],
        scratch_shapes=[pltpu.VMEM((1, LANES), F32)] * 3,
        compiler_params=_cparams(("arbitrary", "arbitrary")),
        name="moe_route",
    )(hx, tri)
    return pos[n:], meta


def _row_dma_wait(src_hbm, dst_ref, sem, rows):
    pltpu.make_async_copy(src_hbm.at[pl.ds(0, rows)], dst_ref.at[pl.ds(0, rows)], sem).wait()


def _burst_pipeline(n_bursts, issue_burst, wait_burst):
    for b in range(n_bursts):
        issue_burst(b, b % 2)
        if b >= 1:
            wait_burst(b - 1, (b - 1) % 2)
    wait_burst(n_bursts - 1, (n_bursts - 1) % 2)


def _scatter_kernel(pos_ref, ends_ref, hx_ref, hs_hbm, zero_ref, sems, *, tr, burst, row_block):
    @pl.when(pl.program_id(0) == 0)
    def _():
        zero_ref[...] = jnp.zeros(zero_ref.shape, F32)
        run = zero_ref.shape[0]
        for g in range(N_GROUPS):
            start = pl.multiple_of((ends_ref[g] >> 3) << 3, 8)
            pltpu.make_async_copy(zero_ref, hs_hbm.at[pl.ds(start, run)], sems.at[0]).start()
        for g in range(N_GROUPS):
            pltpu.make_async_copy(zero_ref, hs_hbm.at[pl.ds(0, run)], sems.at[0]).wait()
        used = ends_ref[N_GROUPS]
        n_tail = (hs_hbm.shape[0] - used) // row_block
        for k in range(N_GROUPS + 1):
            @pl.when(k < n_tail)
            def _():
                start = pl.multiple_of(used + k * row_block, 8)
                cp = pltpu.make_async_copy(zero_ref.at[pl.ds(0, row_block)],
                                           hs_hbm.at[pl.ds(start, row_block)], sems.at[0])
                cp.start()
                cp.wait()

    def issue_burst(b, slot):
        def issue(t8, c):
            for k in range(MOE_DMA_UNROLL):
                row = b * burst + t8 * MOE_DMA_UNROLL + k
                pltpu.make_async_copy(hx_ref.at[pl.ds(row, 1)], hs_hbm.at[pl.ds(pos_ref[0, 0, row], 1)],
                                      sems.at[slot]).start(priority=k % 2)
            return c
        lax.fori_loop(0, burst // MOE_DMA_UNROLL, issue, 0)

    def wait_burst(b, slot):
        _row_dma_wait(hx_ref, hs_hbm, sems.at[slot], burst)

    _burst_pipeline(tr // burst, issue_burst, wait_burst)


def _moe_scatter(hx, pos3, real_ends, n_sorted, tr, row_block):
    n, w = hx.shape
    nb = n // tr
    return pl.pallas_call(
        functools.partial(_scatter_kernel, tr=tr, burst=min(MOE_DMA_BURST, tr), row_block=row_block),
        grid=(nb,),
        in_specs=[pl.BlockSpec((1, 1, tr), lambda i: (i, 0, 0), memory_space=pltpu.SMEM),
                  pl.BlockSpec(memory_space=pltpu.SMEM),
                  pl.BlockSpec((tr, w), lambda i: (i, 0))],
        out_specs=pl.BlockSpec(memory_space=pl.ANY),
        out_shape=jax.ShapeDtypeStruct((n_sorted, w), F32),
        scratch_shapes=[pltpu.VMEM((row_block + 8, w), F32), pltpu.SemaphoreType.DMA((2,))],
        compiler_params=_cparams(("arbitrary",)),
        name="moe_scatter",
    )(pos3, real_ends, hx)


def _experts_kernel(gmap_ref, valid_ref, hs_ref, wg_ref, wu_ref, wd_ref, o_ref, cw_ref, acc_ref, *, d):
    i = pl.program_id(0)
    j = pl.program_id(1)
    valid = valid_ref[i] > 0

    @pl.when(j == 0)
    def _():
        acc_ref[...] = jnp.zeros(acc_ref.shape, F32)

    @pl.when((j == 0) & valid)
    def _():
        logits = hs_ref[:, d:]
        lane = lax.broadcasted_iota(jnp.int32, logits.shape, 1).astype(F32)
        neg = -jnp.inf
        gl = jnp.where(lane < N_GROUPS, logits, neg)
        gmax = jnp.max(gl, axis=-1, keepdims=True)
        gsum = jnp.sum(jnp.exp(gl - gmax), axis=-1, keepdims=True)
        g_val = 1.0 / gsum
        g_idx = _first_index(gl == gmax, lane)
        lo = N_GROUPS + EXPERTS_PER_GROUP * g_idx
        el = jnp.where((lane >= lo) & (lane < lo + EXPERTS_PER_GROUP), logits, neg)
        emax = jnp.max(el, axis=-1, keepdims=True)
        esum = jnp.sum(jnp.exp(el - emax), axis=-1, keepdims=True)
        i1 = _first_index(el == emax, lane)
        el2 = jnp.where(lane == i1, neg, el)
        e2max = jnp.max(el2, axis=-1, keepdims=True)
        i2 = _first_index(el2 == e2max, lane)
        p1 = 1.0 / esum
        p2 = jnp.exp(e2max - emax) / esum
        tot = p1 + p2
        cw_ref[...] = (jnp.where(lane == i1, g_val * (p1 / tot), 0.0)
                       + jnp.where(lane == i2, g_val * (p2 / tot), 0.0))

    @pl.when(valid)
    def _():
        h = hs_ref[:, :d].astype(BF16)
        a = _dot(h, jnp.concatenate([wg_ref[0, 0], wg_ref[0, 1]], axis=1))
        hid = (a * jax.nn.sigmoid(a)) * _dot(h, jnp.concatenate([wu_ref[0, 0], wu_ref[0, 1]], axis=1))
        first = N_GROUPS + EXPERTS_PER_GROUP * gmap_ref[i] + 2 * j
        lane = lax.broadcasted_iota(jnp.int32, cw_ref.shape, 1)
        cw_all = cw_ref[...]
        cw0 = jnp.sum(jnp.where(lane == first, cw_all, 0.0), axis=-1, keepdims=True)
        cw1 = jnp.sum(jnp.where(lane == first + 1, cw_all, 0.0), axis=-1, keepdims=True)
        de = hid.shape[1] // 2
        hcol = lax.broadcasted_iota(jnp.int32, hid.shape, 1)
        hid = hid * jnp.where(hcol < de, cw0, cw1)
        acc_ref[...] += _dot(hid.astype(BF16), wd_ref[0])

    @pl.when(j == pl.num_programs(1) - 1)
    def _():
        o_ref[...] = acc_ref[...]


def _moe_experts(hs, gmap, valid, wg, wu, wd, d, row_block):
    n_sorted, w = hs.shape
    _, _, _, de = wg.shape
    de2 = 2 * de
    pairs = EXPERTS_PER_GROUP // 2
    grid_spec = pltpu.PrefetchScalarGridSpec(
        num_scalar_prefetch=2,
        grid=(n_sorted // row_block, pairs),
        in_specs=[pl.BlockSpec((row_block, w), lambda i, j, gm, va: (i * va[i], 0)),
                  pl.BlockSpec((1, 2, d, de), lambda i, j, gm, va: (gm[i] * pairs + j, 0, 0, 0)),
                  pl.BlockSpec((1, 2, d, de), lambda i, j, gm, va: (gm[i] * pairs + j, 0, 0, 0)),
                  pl.BlockSpec((1, de2, d), lambda i, j, gm, va: (gm[i] * pairs + j, 0, 0))],
        out_specs=pl.BlockSpec((row_block, d), lambda i, j, gm, va: (i, 0)),
        scratch_shapes=[pltpu.VMEM((row_block, LANES), F32), pltpu.VMEM((row_block, d), F32)])
    return pl.pallas_call(
        functools.partial(_experts_kernel, d=d),
        grid_spec=grid_spec,
        out_shape=jax.ShapeDtypeStruct((n_sorted, d), F32),
        compiler_params=_cparams(("arbitrary", "arbitrary")),
        name="moe_experts",
    )(gmap, valid, hs, wg, wu, wd)


def _combine_kernel(pos_ref, ys_hbm, x_ref, o_ref, buf_ref, sems, *, tr, burst):
    def issue_burst(b, slot):
        def issue(t8, c):
            for k in range(MOE_DMA_UNROLL):
                row = b * burst + t8 * MOE_DMA_UNROLL + k
                pltpu.make_async_copy(ys_hbm.at[pl.ds(pos_ref[0, 0, row], 1)], buf_ref.at[pl.ds(row, 1)],
                                      sems.at[slot]).start(priority=k % 2)
            return c
        lax.fori_loop(0, burst // MOE_DMA_UNROLL, issue, 0)

    def wait_burst(b, slot):
        _row_dma_wait(ys_hbm, buf_ref, sems.at[slot], burst)
        rows = slice(b * burst, (b + 1) * burst)
        o_ref[rows, :] = x_ref[rows, :] + buf_ref[rows, :]

    _burst_pipeline(tr // burst, issue_burst, wait_burst)


def _moe_combine(ys, pos3, x2, tr):
    n, d = x2.shape
    return pl.pallas_call(
        functools.partial(_combine_kernel, tr=tr, burst=min(MOE_DMA_BURST, tr)),
        grid=(n // tr,),
        in_specs=[pl.BlockSpec((1, 1, tr), lambda i: (i, 0, 0), memory_space=pltpu.SMEM),
                  pl.BlockSpec(memory_space=pl.ANY),
                  pl.BlockSpec((tr, d), lambda i: (i, 0))],
        out_specs=pl.BlockSpec((tr, d), lambda i: (i, 0)),
        out_shape=jax.ShapeDtypeStruct((n, d), F32),
        scratch_shapes=[pltpu.VMEM((tr, d), F32), pltpu.SemaphoreType.DMA((2,))],
        compiler_params=_cparams(("arbitrary",)),
        name="moe_combine",
    )(pos3, ys, x2)


def _pair_cols(w):
    ne, d, de = w.shape
    return w.astype(BF16).reshape(ne // 2, 2, d, de)


def _pair_rows(w):
    ne, de, d = w.shape
    return w.astype(BF16).reshape(ne // 2, 2 * de, d)


def _moe(hx, x2, wg, wu, wd, tr):
    n, d = x2.shape
    row_block = min(MOE_ROW_BLOCK, n)
    n_blocks = n // row_block + N_GROUPS + 1
    pos, meta = _moe_route(hx, d, tr, row_block)
    counts = meta[0, :N_GROUPS]
    ends = meta[1, :N_GROUPS] + jnp.ceil(counts / row_block) * row_block
    starts = jnp.arange(n_blocks, dtype=F32) * row_block
    grp = jnp.sum((starts[:, None] >= ends[None, :]).astype(jnp.int32), axis=1)
    valid = (grp < N_GROUPS).astype(jnp.int32)
    gmap = jnp.minimum(grp, N_GROUPS - 1)
    real_ends = jnp.concatenate([meta[1, :N_GROUPS] + counts, ends[N_GROUPS - 1:]]).astype(jnp.int32)
    pos3 = pos.reshape(n // tr, 1, tr)
    hs = _moe_scatter(hx, pos3, real_ends, n_blocks * row_block, tr, row_block)
    ys = _moe_experts(hs, gmap, valid, wg, wu, wd, d, row_block)
    return _moe_combine(ys, pos3, x2, tr)


def _pick(n, pref):
    t = min(n, pref)
    while n % t:
        t //= 2
    return t


def kernel(x, positions, norm_mix, w_in, da_q_gain, da_k_gain, da_lambda_q1, da_lambda_k1,
           da_lambda_q2, da_lambda_k2, da_subln_gain, hg_lower_bounds, hg_out_gain,
           s5_lambda_re, s5_lambda_im, s5_log_step, s5_b_re, s5_b_im, s5_c_re, s5_c_im,
           s5_d, s5_glu_w, s5_glu_b, w_branch_attn, w_branch_sb, w_branch_hgrn, w_branch_s5,
           w_out, norm_ffn, router_group_w, router_group_b, router_expert_w, router_expert_b,
           expert_w_gate, expert_w_up, expert_w_down):
    bsz, seq, d = x.shape
    depth = w_in.shape[0]
    n = bsz * seq
    assert w_in.shape[2] == IN_COLS and seq % 128 == 0
    tm = _pick(n, ROW_TM)
    tq_da = _pick(seq, DA_TQ)
    tb = _pick(seq, SCAN_TB)

    x2 = x.reshape(n, d).astype(F32)
    cos_t, sina_t, sinb_t = _rope_tables(positions.reshape(n, 1).astype(jnp.int32), tm)

    lb_all = jnp.cumsum(jax.nn.softmax(hg_lower_bounds.astype(F32), axis=0), axis=0)
    lb_all = lb_all - lb_all[0:1]

    for l in range(depth):
        lambda_init = DA_LAMBDA_INIT_BASE - DA_LAMBDA_INIT_SCALE * math.exp(-DA_LAMBDA_INIT_RATE * l)
        w_f = w_in[l].astype(F32)
        sbq = COL_SB_Q - COL_DA_Q
        w_l = jnp.concatenate([w_f[:, REF_GATE_START:], w_f[:, :sbq],
                               w_f[:, sbq:sbq + SB_HEADS * SB_HEAD_DIM] * (SB_HEAD_DIM ** -0.5 * LOG2E),
                               w_f[:, sbq + SB_HEADS * SB_HEAD_DIM:REF_GATE_START]], axis=1)
        proj = _norm_proj(x2, norm_mix[l].astype(F32)[None, :], w_l.astype(BF16), _pick(n, NP_TM), NP_TN)

        qk_gain = jnp.concatenate([jnp.tile(da_q_gain[l].astype(F32), 2 * DA_HEADS),
                                   jnp.tile(da_k_gain[l].astype(F32), 2 * DA_HEADS)])[None, :]
        qk = _qk_prep(proj, qk_gain, cos_t, sina_t, sinb_t, tm)
        y_a = _diff_attn(qk, proj,
                         da_lambda_q1[l].astype(F32)[None, :], da_lambda_k1[l].astype(F32)[None, :],
                         da_lambda_q2[l].astype(F32)[None, :], da_lambda_k2[l].astype(F32)[None, :],
                         da_subln_gain[l].astype(F32)[None, :], bsz, seq, tq_da, DA_KW, DA_GROUP, lambda_init)
        y_b = _stick_break(proj, bsz, seq, _pick(seq, SB_TQ), SB_KW, SB_GROUP)
        y_c = _hgrn(proj, lb_all[l][None, :], jnp.tile(hg_out_gain[l].astype(F32), HG_HEADS)[None, :],
                    bsz, seq, tb)
        s5p = _s5_params(s5_lambda_re[l], s5_lambda_im[l], s5_log_step[l], s5_b_re[l], s5_b_im[l],
                         s5_c_re[l], s5_c_im[l], tb)
        y_d = _s5(proj, s5p, s5_d[l].astype(F32)[None, :], s5_glu_w[l].astype(BF16),
                  s5_glu_b[l].astype(F32)[None, :], bsz, seq, tb)

        wr = jnp.concatenate([router_group_w[l], router_expert_w[l]], axis=1).astype(F32)
        wr = jnp.pad(wr, ((0, 0), (0, LANES - wr.shape[1])))
        wr_hi = wr.astype(BF16)
        wr_lo = (wr - wr_hi.astype(F32)).astype(BF16)
        br = jnp.concatenate([router_group_b[l], router_expert_b[l]]).astype(F32)
        br = jnp.pad(br, (0, LANES - br.shape[0]))[None, :]
        x2, hx = _merge(x2, y_a, y_b, y_c, y_d, proj,
                        w_branch_attn[l].astype(BF16), w_branch_sb[l].astype(BF16),
                        w_branch_hgrn[l].astype(BF16), w_branch_s5[l].astype(BF16),
                        w_out[l].astype(BF16), norm_ffn[l].astype(F32)[None, :],
                        wr_hi, wr_lo, br, _pick(n, MERGE_TM))
        x2 = _moe(hx, x2, _pair_cols(expert_w_gate[l]), _pair_cols(expert_w_up[l]),
                  _pair_rows(expert_w_down[l]), tm)

    return x2.reshape(bsz, seq, d).astype(x.dtype)
```

```python
import functools
import math

import jax
import jax.numpy as jnp
import numpy as np
from jax import lax
from jax.experimental import pallas as pl
from jax.experimental.pallas import tpu as pltpu

F32 = jnp.float32
BF16 = jnp.bfloat16

RMS_EPS = 1e-6
LANES = 128
LOG2E = 1.4426950408889634

DA_HEADS = 4
DA_QK_DIM = 64
ROPE_THETA = 500000.0
ROPE_DIM = DA_QK_DIM // 4
DA_LAMBDA_INIT_BASE = 0.8
DA_LAMBDA_INIT_SCALE = 0.6
DA_LAMBDA_INIT_RATE = 0.3

SB_HEADS = 4
SB_HEAD_DIM = 64

HG_HEADS = 4
HG_DIM = 64
HG_COLS = HG_HEADS * HG_DIM
HG_SUB = 16
HG_UNROLL = 16

S5_GROUPS = 16
S5_GROUP_CH = 16
S5_STATE = 64
S5_WIDTH = S5_GROUPS * S5_GROUP_CH
S5_NSTATE = S5_GROUPS * S5_STATE
S5_EIG_CLIP = -1e-4
S5_SUB = 8

N_GROUPS = 4
EXPERTS_PER_GROUP = 4

REF_GATE_START = 3584
COL_GATE = 0
COL_DA_Q = 4096
COL_DA_V = 5120
COL_SB_Q = 5632
COL_SB_K = 5888
COL_SB_V = 6144
COL_HG_F = 6400
COL_HG_I = 6656
COL_HG_Q = 6912
COL_HG_G = 7168
COL_S5_U = 7424
IN_COLS = 7680

VMEM_LIMIT = 48 * 1024 * 1024

ROW_TM = 1024
MERGE_TM = 512
SCAN_TB = 512
NP_TN = 1536
DA_TQ = 1024
DA_KW = 1024
DA_GROUP = 1
DA_DIAG_SPLIT = 4
SB_TQ = 512
SB_KW = 256
SB_GROUP = 4
MOE_ROW_BLOCK = 512
MOE_DMA_BURST = 256
MOE_DMA_UNROLL = 8
NP_TM = 2048

_NT = (((1,), (1,)), ((), ()))
_TN = (((0,), (0,)), ((), ()))


def _cparams(sem):
    return pltpu.CompilerParams(dimension_semantics=sem, vmem_limit_bytes=VMEM_LIMIT)


def _dot(a, b):
    return jnp.dot(a, b, preferred_element_type=F32)


def _norm_proj_kernel(x_ref, g_ref, w_ref, o_ref, h_ref):
    @pl.when(pl.program_id(1) == 0)
    def _():
        x = x_ref[...]
        ms = jnp.mean(x * x, axis=-1, keepdims=True)
        h_ref[...] = (x * lax.rsqrt(ms + RMS_EPS) * g_ref[...]).astype(BF16)

    o_ref[...] = _dot(h_ref[...], w_ref[...]).astype(o_ref.dtype)


def _norm_proj(x2, gain, w, tm, tn):
    n, d = x2.shape
    cols = w.shape[1]
    return pl.pallas_call(
        _norm_proj_kernel,
        grid=(n // tm, cols // tn),
        in_specs=[pl.BlockSpec((tm, d), lambda i, j: (i, 0)),
                  pl.BlockSpec((1, d), lambda i, j: (0, 0)),
                  pl.BlockSpec((d, tn), lambda i, j: (0, j))],
        out_specs=pl.BlockSpec((tm, tn), lambda i, j: (i, j)),
        out_shape=jax.ShapeDtypeStruct((n, cols), BF16),
        scratch_shapes=[pltpu.VMEM((tm, d), BF16)],
        compiler_params=_cparams(("parallel", "arbitrary")),
        name="norm_proj",
    )(x2, gain, w)


def _rope_kernel(pos_ref, invf_ref, sa_ref, sb_ref, cos_ref, sina_ref, sinb_ref):
    ang = pos_ref[...].astype(F32) * invf_ref[...]
    c = jnp.cos(ang)
    s = jnp.sin(ang)
    cos_ref[...] = c
    sina_ref[...] = s * sa_ref[...]
    sinb_ref[...] = s * sb_ref[...]


def _rope_tables(pos_col, tm):
    n = pos_col.shape[0]
    half = ROPE_DIM // 2
    inv_freq = jnp.exp(-math.log(ROPE_THETA) * jnp.arange(half, dtype=F32) * (2.0 / ROPE_DIM))
    lane = np.arange(LANES) % DA_QK_DIM
    invf = jnp.where(lane < ROPE_DIM, inv_freq[lane % half], 0.0).astype(F32)[None, :]
    sgn_a = jnp.asarray(np.where(lane < half, -1.0, 0.0), F32)[None, :]
    sgn_b = jnp.asarray(np.where((lane >= half) & (lane < ROPE_DIM), 1.0, 0.0), F32)[None, :]
    row = pl.BlockSpec((1, LANES), lambda i: (0, 0))
    tab = pl.BlockSpec((tm, LANES), lambda i: (i, 0))
    shp = jax.ShapeDtypeStruct((n, LANES), F32)
    return pl.pallas_call(
        _rope_kernel,
        grid=(n // tm,),
        in_specs=[pl.BlockSpec((tm, 1), lambda i: (i, 0)), row, row, row],
        out_specs=[tab, tab, tab],
        out_shape=[shp, shp, shp],
        compiler_params=_cparams(("parallel",)),
        name="rope_tables",
    )(pos_col, invf, sgn_a, sgn_b)


def _qk_prep_kernel(x_ref, gain_ref, cos_ref, sina_ref, sinb_ref, bd_ref, o_ref):
    c = cos_ref[...]
    sa = sina_ref[...]
    sb = sinb_ref[...]
    bd = bd_ref[...]
    n_tiles = x_ref.shape[1] // LANES
    for j in range(n_tiles):
        sl = slice(j * LANES, (j + 1) * LANES)
        t = x_ref[:, sl].astype(F32)
        ss = _dot((t * t).astype(BF16), bd)
        y = t * lax.rsqrt(ss * (1.0 / DA_QK_DIM) + RMS_EPS) * gain_ref[:, sl]
        y = (y * c + pltpu.roll(y, LANES - ROPE_DIM // 2, 1) * sa
             + pltpu.roll(y, ROPE_DIM // 2, 1) * sb)
        if j < n_tiles // 2:
            y = y * (DA_QK_DIM ** -0.5 * LOG2E)
        o_ref[:, sl] = y.astype(BF16)


def _qk_prep(proj, gain_row, cos_t, sina_t, sinb_t, tm):
    n = proj.shape[0]
    w = 2 * DA_HEADS * 2 * DA_QK_DIM
    seg = np.arange(LANES) // DA_QK_DIM
    bd = jnp.asarray(seg[:, None] == seg[None, :], BF16)
    tab = pl.BlockSpec((tm, LANES), lambda i: (i, 0))
    return pl.pallas_call(
        _qk_prep_kernel,
        grid=(n // tm,),
        in_specs=[pl.BlockSpec((tm, w), lambda i: (i, COL_DA_Q // w)),
                  pl.BlockSpec((1, w), lambda i: (0, 0)),
                  tab, tab, tab,
                  pl.BlockSpec((LANES, LANES), lambda i: (0, 0))],
        out_specs=pl.BlockSpec((tm, w), lambda i: (i, 0)),
        out_shape=jax.ShapeDtypeStruct((n, w), BF16),
        compiler_params=_cparams(("parallel",)),
        name="qk_prep",
    )(proj, gain_row, cos_t, sina_t, sinb_t, bd)


def _diff_attn_kernel(q_ref, k_ref, v_ref, lq1_ref, lk1_ref, lq2_ref, lk2_ref, sg_ref, o_ref,
                      m_ref, l_ref, acc_ref, *, tq, kw, group, lambda_init):
    qi = pl.program_id(2)
    q = q_ref[...]
    lane = lax.broadcasted_iota(jnp.int32, q.shape, 1)
    zero = jnp.zeros_like(q)
    q2 = jnp.concatenate([jnp.where(lane < DA_QK_DIM, q, zero),
                          jnp.where(lane >= DA_QK_DIM, q, zero)], axis=0)
    m_ref[...] = jnp.full(m_ref.shape, -jnp.inf, F32)
    l_ref[...] = jnp.zeros(l_ref.shape, F32)
    acc_ref[...] = jnp.zeros(acc_ref.shape, F32)

    def rows_of(ref_or_val, row_lo):
        if row_lo == 0:
            return ref_or_val[...]
        return jnp.concatenate([ref_or_val[row_lo:tq], ref_or_val[tq + row_lo:2 * tq]], axis=0)

    def put_rows(ref, row_lo, val):
        if row_lo == 0:
            ref[...] = val
        else:
            ref[row_lo:tq] = val[:tq - row_lo]
            ref[tq + row_lo:2 * tq] = val[tq - row_lo:]

    def tile(start, width, mask_off, row_lo=0):
        n_rows = tq - row_lo
        n_lt = width // LANES
        kb = k_ref[pl.ds(start, width), :]
        vb = v_ref[pl.ds(start, width), :]
        s = lax.dot_general(rows_of(q2, row_lo), kb, _NT, preferred_element_type=F32)
        if mask_off is not None:
            keep = (lax.broadcasted_iota(jnp.int32, (n_rows, width), 1) + mask_off
                    <= lax.broadcasted_iota(jnp.int32, (n_rows, width), 0) + row_lo)
            s = jnp.where(jnp.concatenate([keep, keep], axis=0), s, -jnp.inf)
        st = [s[:, j * LANES:(j + 1) * LANES] for j in range(n_lt)]
        smax = st[0]
        for j in range(1, n_lt):
            smax = jnp.maximum(smax, st[j])
        m_prev = rows_of(m_ref, row_lo)
        m_new = jnp.maximum(m_prev, jnp.max(smax, axis=-1, keepdims=True))
        alpha = jnp.exp2(m_prev - m_new)
        ps = [jnp.exp2(t - m_new) for t in st]
        lsum = ps[0]
        for j in range(1, n_lt):
            lsum = lsum + ps[j]
        p = jnp.concatenate([t.astype(BF16) for t in ps], axis=1)
        put_rows(l_ref, row_lo, alpha * rows_of(l_ref, row_lo) + lsum)
        put_rows(acc_ref, row_lo, alpha * rows_of(acc_ref, row_lo) + _dot(p, vb))
        put_rows(m_ref, row_lo, m_new)

    n_diag = DA_DIAG_SPLIT
    dw = tq // n_diag
    n_below = qi * (tq // kw)

    def body_group(gi, carry):
        for j in range(group):
            tile(pl.multiple_of((gi * group + j) * kw, kw), kw, None)
        return carry

    def body_single(ki, carry):
        tile(pl.multiple_of(ki * kw, kw), kw, None)
        return carry

    n_group = n_below // group
    lax.fori_loop(0, n_group, body_group, 0)
    lax.fori_loop(n_group * group, n_below, body_single, 0)
    for j in range(n_diag):
        tile(pl.multiple_of(qi * tq + j * dw, dw), dw, j * dw, row_lo=j * dw)

    lam = (jnp.exp(jnp.sum(lq1_ref[...] * lk1_ref[...], axis=-1, keepdims=True))
           - jnp.exp(jnp.sum(lq2_ref[...] * lk2_ref[...], axis=-1, keepdims=True)) + lambda_init)
    o = acc_ref[...] / jnp.sum(l_ref[...], axis=-1, keepdims=True)
    o = o[:tq] - lam * o[tq:]
    ms = jnp.mean(o * o, axis=-1, keepdims=True)
    o = o * lax.rsqrt(ms + RMS_EPS) * sg_ref[...] * (1.0 - lambda_init)
    o_ref[...] = o.astype(o_ref.dtype)


def _diff_attn(qk, proj, lq1, lk1, lq2, lk2, subln, bsz, seq, tq, kw, group, lambda_init):
    n = qk.shape[0]
    nq = seq // tq
    kcol = (DA_HEADS * 2 * DA_QK_DIM) // LANES
    vcol = COL_DA_V // LANES
    vec = pl.BlockSpec((1, DA_QK_DIM), lambda b, h, i: (0, 0))
    return pl.pallas_call(
        functools.partial(_diff_attn_kernel, tq=tq, kw=min(kw, tq), group=group, lambda_init=lambda_init),
        grid=(bsz, DA_HEADS, nq),
        in_specs=[pl.BlockSpec((tq, LANES), lambda b, h, i: (b * nq + i, h)),
                  pl.BlockSpec((seq, LANES), lambda b, h, i: (b, kcol + h)),
                  pl.BlockSpec((seq, LANES), lambda b, h, i: (b, vcol + h)),
                  vec, vec, vec, vec,
                  pl.BlockSpec((1, LANES), lambda b, h, i: (0, 0))],
        out_specs=pl.BlockSpec((tq, LANES), lambda b, h, i: (b * nq + i, h)),
        out_shape=jax.ShapeDtypeStruct((n, DA_HEADS * LANES), BF16),
        scratch_shapes=[pltpu.VMEM((2 * tq, LANES), F32), pltpu.VMEM((2 * tq, LANES), F32),
                        pltpu.VMEM((2 * tq, LANES), F32)],
        compiler_params=_cparams(("parallel", "parallel", "arbitrary")),
        name="diff_attn",
    )(qk, qk, proj, lq1, lk1, lq2, lk2, subln)


def _stick_break_kernel(q_ref, k_ref, v_ref, o_ref, r_ref, acc_ref, *, tq, kw, group):
    qi = pl.program_id(2)
    q = q_ref[...]
    lane = lax.broadcasted_iota(jnp.int32, q.shape, 1)
    zero = jnp.zeros_like(q)
    q2 = jnp.concatenate([jnp.where(lane < SB_HEAD_DIM, q, zero),
                          jnp.where(lane >= SB_HEAD_DIM, q, zero)], axis=0)
    incl = jnp.where(lax.broadcasted_iota(jnp.int32, (kw, kw), 0)
                     >= lax.broadcasted_iota(jnp.int32, (kw, kw), 1), 1.0, 0.0).astype(BF16)
    n_lt = kw // LANES
    n_diag = tq // kw
    r_ref[...] = jnp.zeros(r_ref.shape, F32)
    acc_ref[...] = jnp.zeros(acc_ref.shape, F32)

    def run(starts, offsets):
        def both(x, lo):
            return x if lo == 0 else jnp.concatenate([x[lo:tq], x[tq + lo:]], axis=0)

        def spread(x, lo, fill):
            if lo == 0:
                return x
            return jnp.concatenate([fill[:lo], x[:tq - lo], fill[tq:tq + lo], x[tq - lo:]], axis=0)

        r = r_ref[...]
        total = None
        for start, off in zip(starts, offsets):
            lo = off or 0
            u = lax.dot_general(both(q2, lo), k_ref[pl.ds(start, kw), :], _NT, preferred_element_type=F32)
            neg_abs = lax.bitcast_convert_type(
                lax.bitcast_convert_type(u, jnp.int32) | jnp.int32(-2 ** 31), F32)
            sp = jnp.maximum(u, 0.0) + jnp.log(1.0 + jnp.exp2(neg_abs)) * LOG2E
            mask = None
            if off is not None:
                keep = (lax.broadcasted_iota(jnp.int32, (tq - lo, kw), 1) + off
                        < lax.broadcasted_iota(jnp.int32, (tq - lo, kw), 0) + lo)
                mask = jnp.concatenate([keep, keep], axis=0)
                sp = jnp.where(mask, sp, 0.0)
            cum = _dot(sp.astype(BF16), incl)
            r_blk = both(r, lo)
            w = jnp.exp2(u - (cum + jnp.concatenate([r_blk] * n_lt, axis=1)))
            if mask is not None:
                w = jnp.where(mask, w, 0.0)
            part = _dot(w.astype(BF16), v_ref[pl.ds(start, kw), :])
            part = spread(part, lo, jnp.zeros((2 * tq, LANES), F32))
            total = part if total is None else total + part
            r = spread(r_blk + cum[:, 0:1], lo, r)
        acc_ref[...] += total
        r_ref[...] = r

    base = qi * tq
    run([pl.multiple_of(base + (n_diag - 1 - j) * kw, kw) for j in range(n_diag)],
        [(n_diag - 1 - j) * kw for j in range(n_diag)])

    n_below = qi * n_diag
    rem = n_below % group

    def body_rem(pi, carry):
        top = n_below - pi * n_diag
        run([pl.multiple_of((top - 1 - j) * kw, kw) for j in range(n_diag)], [None] * n_diag)
        return carry

    def body_group(gi, carry):
        top = n_below - rem - gi * group
        run([pl.multiple_of((top - 1 - j) * kw, kw) for j in range(group)], [None] * group)
        return carry

    lax.fori_loop(0, rem // n_diag, body_rem, 0)
    lax.fori_loop(0, n_below // group, body_group, 0)
    lane_o = lax.broadcasted_iota(jnp.int32, (tq, LANES), 1)
    o_ref[...] = jnp.where(lane_o < SB_HEAD_DIM, acc_ref[:tq, :], acc_ref[tq:, :]).astype(o_ref.dtype)


def _stick_break(proj, bsz, seq, tq, kw, group):
    assert tq % kw == 0 and group % (tq // kw) == 0
    n = proj.shape[0]
    nq = seq // tq
    pairs = (SB_HEADS * SB_HEAD_DIM) // LANES
    qc, kc, vc = COL_SB_Q // LANES, COL_SB_K // LANES, COL_SB_V // LANES
    return pl.pallas_call(
        functools.partial(_stick_break_kernel, tq=tq, kw=kw, group=group),
        grid=(bsz, pairs, nq),
        in_specs=[pl.BlockSpec((tq, LANES), lambda b, p, i: (b * nq + i, qc + p)),
                  pl.BlockSpec((seq, LANES), lambda b, p, i: (b, kc + p)),
                  pl.BlockSpec((seq, LANES), lambda b, p, i: (b, vc + p))],
        out_specs=pl.BlockSpec((tq, LANES), lambda b, p, i: (b * nq + i, p)),
        out_shape=jax.ShapeDtypeStruct((n, pairs * LANES), BF16),
        scratch_shapes=[pltpu.VMEM((2 * tq, LANES), F32), pltpu.VMEM((2 * tq, LANES), F32)],
        compiler_params=_cparams(("parallel", "parallel", "arbitrary")),
        name="stick_break",
    )(proj, proj, proj)


def _split_bf16(x):
    hi = x.astype(BF16)
    lo = (x - hi.astype(F32)).astype(BF16)
    return hi, lo


def _hgrn_kernel(f_ref, i_ref, q_ref, g_ref, lb_ref, gain_ref, o_ref,
                 st_ref, b_ref, dec_ref, qq_ref, kk_ref, vv_ref, oo_ref, qe_ref, ke_ref, gm_ref, uu_ref,
                 *, tb):
    @pl.when(pl.program_id(1) == 0)
    def _():
        st_ref[...] = jnp.zeros(st_ref.shape, F32)

    nsub = tb // HG_SUB
    z = f_ref[...].astype(F32)
    lb = lb_ref[...]
    sp = jnp.maximum(z, 0.0) + jnp.log(1.0 + jnp.exp(-jnp.abs(z)))
    log_sig = z - sp
    a = jnp.log(lb)
    c = jnp.log(1.0 - lb) + log_sig
    mx = jnp.maximum(a, c)
    log_f = mx + jnp.log(jnp.exp(a - mx) + jnp.exp(c - mx))
    key = (1.0 - lb) * jax.nn.sigmoid(-z)

    row = lax.broadcasted_iota(jnp.int32, (tb, tb), 0)
    col = lax.broadcasted_iota(jnp.int32, (tb, tb), 1)
    sub_shift = HG_SUB.bit_length() - 1
    dim_shift = HG_DIM.bit_length() - 1
    same = (row >> sub_shift) == (col >> sub_shift)
    tri = jnp.where(same & (col <= row), 1.0, 0.0).astype(BF16)
    blk = jnp.where(same, 1.0, 0.0).astype(BF16)
    hi, lo = _split_bf16(log_f)
    b = _dot(tri, hi) + _dot(tri, lo)
    e = _dot(blk, hi) + _dot(blk, lo)
    qf = q_ref[...].astype(F32)
    b_ref[...] = b
    qq_ref[...] = qf
    kk_ref[...] = key
    vv_ref[...] = i_ref[...].astype(F32)
    qe_ref[...] = (qf * jnp.exp(b)).astype(BF16)
    ke_ref[...] = (key * jnp.exp(e - b)).astype(BF16)
    dec_ref[...] = jnp.exp(e)

    seg_r = lax.broadcasted_iota(jnp.int32, (HG_COLS, HG_COLS), 0) >> dim_shift
    seg_c = lax.broadcasted_iota(jnp.int32, (HG_COLS, HG_COLS), 1) >> dim_shift
    head_mask = seg_r == seg_c
    ones_bd = jnp.where(head_mask, 1.0, 0.0).astype(BF16)
    sel = jnp.where(lax.broadcasted_iota(jnp.int32, (HG_SUB, HG_SUB * HG_SUB), 0)
                    == (lax.broadcasted_iota(jnp.int32, (HG_SUB, HG_SUB * HG_SUB), 1) >> sub_shift),
                    1.0, 0.0).astype(BF16)
    srow = lax.broadcasted_iota(jnp.int32, (HG_SUB, HG_COLS), 0)

    def intra(ci, slot):
        r0 = pl.multiple_of(ci * HG_SUB, HG_SUB)
        bi = b_ref[pl.ds(r0, HG_SUB), :]
        qi = qq_ref[pl.ds(r0, HG_SUB), :]
        ki = kk_ref[pl.ds(r0, HG_SUB), :]
        vi = vv_ref[pl.ds(r0, HG_SUB), :]
        for t in range(HG_SUB):
            d = jnp.exp(jnp.minimum(bi[t:t + 1, :] - bi, 0.0))
            g = jnp.where(srow <= t, qi[t:t + 1, :] * ki * d, 0.0)
            gm_ref[slot, t * HG_SUB:(t + 1) * HG_SUB, :] = g.astype(BF16)
        sc = _dot(gm_ref[slot], ones_bd)
        vt = jnp.concatenate([vi] * HG_SUB, axis=0)
        oo_ref[pl.ds(r0, HG_SUB), :] = _dot(sel, (sc * vt).astype(BF16))
        upd = lax.dot_general(vi.astype(BF16), ke_ref[pl.ds(r0, HG_SUB), :], _TN,
                              preferred_element_type=F32)
        uu_ref[ci] = jnp.where(head_mask, upd, 0.0)

    def intra_group(gi, carry):
        for slot in range(HG_UNROLL):
            intra(gi * HG_UNROLL + slot, slot)
        return carry

    lax.fori_loop(0, nsub // HG_UNROLL, intra_group, 0)

    st = st_ref[...]
    for ci in range(nsub):
        rows = slice(ci * HG_SUB, (ci + 1) * HG_SUB)
        oo_ref[rows, :] += lax.dot_general(qe_ref[rows, :], st.astype(BF16), _NT,
                                           preferred_element_type=F32)
        st = st * dec_ref[ci * HG_SUB:ci * HG_SUB + 1, :] + uu_ref[ci]
    st_ref[...] = st

    o = oo_ref[...]
    ms = _dot((o * o).astype(BF16), ones_bd) * (1.0 / HG_DIM)
    gate = g_ref[...].astype(F32)
    gate = gate * jax.nn.sigmoid(gate)
    o_ref[...] = (o * lax.rsqrt(ms + RMS_EPS) * gain_ref[...] * gate).astype(o_ref.dtype)


def _hgrn(proj, lb_row, gain_row, bsz, seq, tb):
    n = proj.shape[0]
    nb = seq // tb
    w = HG_COLS
    cf, ci, cq, cg = COL_HG_F // w, COL_HG_I // w, COL_HG_Q // w, COL_HG_G // w

    def col(cc):
        return pl.BlockSpec((tb, w), lambda b, i: (b * nb + i, cc))

    rowspec = pl.BlockSpec((1, w), lambda b, i: (0, 0))
    big = pltpu.VMEM((tb, w), F32)
    return pl.pallas_call(
        functools.partial(_hgrn_kernel, tb=tb),
        grid=(bsz, nb),
        in_specs=[col(cf), col(ci), col(cq), col(cg), rowspec, rowspec],
        out_specs=pl.BlockSpec((tb, w), lambda b, i: (b * nb + i, 0)),
        out_shape=jax.ShapeDtypeStruct((n, w), BF16),
        scratch_shapes=[pltpu.VMEM((w, w), F32), big, big, big, big, big, big,
                        pltpu.VMEM((tb, w), BF16), pltpu.VMEM((tb, w), BF16),
                        pltpu.VMEM((HG_UNROLL, HG_SUB * HG_SUB, w), BF16),
                        pltpu.VMEM((tb // HG_SUB, w, w), F32)],
        compiler_params=_cparams(("parallel", "arbitrary")),
        name="hgrn2",
    )(proj, proj, proj, proj, lb_row, gain_row)


def _s5_kernel(u_ref, bblk_ref, lev_re_ref, lev_im_ref, pw_re_ref, pw_im_ref, cblk_ref, d_ref,
               gw_ref, gb_ref, o_ref, cr_ref, ci_ref, xr_ref, xi_ref, *, tb):
    @pl.when(pl.program_id(1) == 0)
    def _():
        cr_ref[...] = jnp.zeros(cr_ref.shape, F32)
        ci_ref[...] = jnp.zeros(ci_ref.shape, F32)

    u = u_ref[...]
    bu = _dot(u, bblk_ref[...])
    xr = bu[:, :S5_NSTATE]
    xi = bu[:, S5_NSTATE:]
    row = lax.broadcasted_iota(jnp.int32, xr.shape, 0)
    sub_lev = S5_SUB.bit_length() - 1
    for j in range(sub_lev):
        d = 1 << j
        ar = lev_re_ref[j:j + 1, :]
        ai = lev_im_ref[j:j + 1, :]
        keep = row >= d
        sr = jnp.where(keep, pltpu.roll(xr, d, 0), 0.0)
        si = jnp.where(keep, pltpu.roll(xi, d, 0), 0.0)
        xr, xi = xr + ar * sr - ai * si, xi + ar * si + ai * sr
    cr = cr_ref[...]
    ci = ci_ref[...]
    pr = pw_re_ref[...]
    pi = pw_im_ref[...]
    gr = xr[:S5_SUB] + pr * cr - pi * ci
    gi = xi[:S5_SUB] + pr * ci + pi * cr
    xr_ref[:S5_SUB, :] = gr
    xi_ref[:S5_SUB, :] = gi
    ar = lev_re_ref[sub_lev:sub_lev + 1, :]
    ai = lev_im_ref[sub_lev:sub_lev + 1, :]
    for g in range(1, tb // S5_SUB):
        rows = slice(g * S5_SUB, (g + 1) * S5_SUB)
        gr, gi = xr[rows] + ar * gr - ai * gi, xi[rows] + ar * gi + ai * gr
        xr_ref[rows, :] = gr
        xi_ref[rows, :] = gi
    cr_ref[...] = gr[S5_SUB - 1:S5_SUB, :]
    ci_ref[...] = gi[S5_SUB - 1:S5_SUB, :]
    y = (_dot(xr_ref[...].astype(BF16), cblk_ref[:S5_NSTATE, :])
         + _dot(xi_ref[...].astype(BF16), cblk_ref[S5_NSTATE:, :]) + d_ref[...] * u.astype(F32))
    y = jax.nn.gelu(y)
    zg = _dot(y.astype(BF16), gw_ref[...]) + gb_ref[...]
    o_ref[...] = (y * jax.nn.sigmoid(zg)).astype(o_ref.dtype)


def _s5_params(lam_re, lam_im, log_step, b_re, b_im, c_re, c_im, tb):
    lam_re = jnp.minimum(lam_re.astype(F32), S5_EIG_CLIP)
    lam_im = lam_im.astype(F32)
    step = jnp.exp(log_step.astype(F32))[:, None]
    mag = jnp.exp(lam_re * step)
    phase = lam_im * step
    a_re = mag * jnp.cos(phase)
    a_im = mag * jnp.sin(phase)
    denom = lam_re * lam_re + lam_im * lam_im
    num_re = a_re - 1.0
    gam_re = (num_re * lam_re + a_im * lam_im) / denom
    gam_im = (a_im * lam_re - num_re * lam_im) / denom
    b_re = b_re.astype(F32)
    b_im = b_im.astype(F32)
    bb_re = gam_re[..., None] * b_re - gam_im[..., None] * b_im
    bb_im = gam_re[..., None] * b_im + gam_im[..., None] * b_re
    eye = jnp.eye(S5_GROUPS, dtype=F32)

    def in_blk(bb):
        return jnp.einsum('gnc,gh->gchn', bb, eye).reshape(S5_WIDTH, S5_NSTATE)

    def out_blk(cc):
        return jnp.einsum('gcn,gh->gnhc', cc.astype(F32), eye).reshape(S5_NSTATE, S5_WIDTH)

    bblk = jnp.concatenate([in_blk(bb_re), in_blk(bb_im)], axis=1).astype(BF16)
    cblk = jnp.concatenate([out_blk(c_re), -out_blk(c_im)], axis=0).astype(BF16)
    ar = a_re.reshape(1, S5_NSTATE)
    ai = a_im.reshape(1, S5_NSTATE)
    sub_lev = S5_SUB.bit_length() - 1
    lev_re, lev_im = [ar], [ai]
    pw_re, pw_im = ar, ai
    for _ in range(sub_lev):
        sr, si = lev_re[-1], lev_im[-1]
        pw_re, pw_im = (jnp.concatenate([pw_re, pw_re * sr - pw_im * si], axis=0),
                        jnp.concatenate([pw_im, pw_re * si + pw_im * sr], axis=0))
        lev_re.append(sr * sr - si * si)
        lev_im.append(2.0 * sr * si)
    lev_re = jnp.concatenate(lev_re, axis=0)
    lev_im = jnp.concatenate(lev_im, axis=0)
    return bblk, cblk, lev_re, lev_im, pw_re, pw_im


def _s5(proj, params, d_row, glu_w, glu_b, bsz, seq, tb):
    n = proj.shape[0]
    nb = seq // tb
    bblk, cblk, lev_re, lev_im, pw_re, pw_im = params
    nlev = lev_re.shape[0]
    ucol = COL_S5_U // S5_WIDTH

    def full(shape):
        return pl.BlockSpec(shape, lambda b, i: (0,) * len(shape))

    return pl.pallas_call(
        functools.partial(_s5_kernel, tb=tb),
        grid=(bsz, nb),
        in_specs=[pl.BlockSpec((tb, S5_WIDTH), lambda b, i: (b * nb + i, ucol)),
                  full((S5_WIDTH, 2 * S5_NSTATE)),
                  full((nlev, S5_NSTATE)), full((nlev, S5_NSTATE)),
                  full((S5_SUB, S5_NSTATE)), full((S5_SUB, S5_NSTATE)),
                  full((2 * S5_NSTATE, S5_WIDTH)),
                  full((1, S5_WIDTH)), full((S5_WIDTH, S5_WIDTH)), full((1, S5_WIDTH))],
        out_specs=pl.BlockSpec((tb, S5_WIDTH), lambda b, i: (b * nb + i, 0)),
        out_shape=jax.ShapeDtypeStruct((n, S5_WIDTH), BF16),
        scratch_shapes=[pltpu.VMEM((1, S5_NSTATE), F32), pltpu.VMEM((1, S5_NSTATE), F32),
                        pltpu.VMEM((tb, S5_NSTATE), F32), pltpu.VMEM((tb, S5_NSTATE), F32)],
        compiler_params=_cparams(("parallel", "arbitrary")),
        name="s5",
    )(proj, bblk, lev_re, lev_im, pw_re, pw_im, cblk, d_row, glu_w, glu_b)


def _first_index(mask, lane):
    return jnp.min(jnp.where(mask, lane, float(LANES)), axis=-1, keepdims=True)


def _combine_weights(logits):
    lane = lax.broadcasted_iota(jnp.int32, logits.shape, 1).astype(F32)
    neg = -jnp.inf
    gl = jnp.where(lane < N_GROUPS, logits, neg)
    gmax = jnp.max(gl, axis=-1, keepdims=True)
    gsum = jnp.sum(jnp.exp(gl - gmax), axis=-1, keepdims=True)
    g_val = 1.0 / gsum
    g_idx = _first_index(gl == gmax, lane)
    lo = N_GROUPS + EXPERTS_PER_GROUP * g_idx
    el = jnp.where((lane >= lo) & (lane < lo + EXPERTS_PER_GROUP), logits, neg)
    emax = jnp.max(el, axis=-1, keepdims=True)
    esum = jnp.sum(jnp.exp(el - emax), axis=-1, keepdims=True)
    i1 = _first_index(el == emax, lane)
    el2 = jnp.where(lane == i1, neg, el)
    e2max = jnp.max(el2, axis=-1, keepdims=True)
    i2 = _first_index(el2 == e2max, lane)
    p1 = 1.0 / esum
    p2 = jnp.exp(e2max - emax) / esum
    tot = p1 + p2
    return (jnp.where(lane == i1, g_val * (p1 / tot), 0.0)
            + jnp.where(lane == i2, g_val * (p2 / tot), 0.0))


def _merge_kernel(x_ref, ya_ref, yb_ref, yc_ref, yd_ref, gate_ref, wa_ref, wb_ref, wc_ref, wd_ref,
                  wo_ref, gn_ref, wrh_ref, wrl_ref, br_ref, xo_ref, hx_ref):
    d = x_ref.shape[1]
    merged = None
    for i, (y_ref, w_ref) in enumerate(((ya_ref, wa_ref), (yb_ref, wb_ref),
                                        (yc_ref, wc_ref), (yd_ref, wd_ref))):
        gate = 0.5 * jnp.tanh(0.5 * gate_ref[:, i * d:(i + 1) * d].astype(F32)) + 0.5
        term = gate * _dot(y_ref[...], w_ref[...])
        merged = term if merged is None else merged + term
    xn = x_ref[...] + _dot(merged.astype(BF16), wo_ref[...])
    xo_ref[...] = xn
    ms = jnp.mean(xn * xn, axis=-1, keepdims=True)
    h = xn * lax.rsqrt(ms + RMS_EPS) * gn_ref[...]
    hi, lo = _split_bf16(h)
    hx_ref[:, :d] = hi.astype(F32)
    wrh = wrh_ref[...]
    logits = _dot(hi, wrh) + _dot(lo, wrh) + _dot(hi, wrl_ref[...]) + br_ref[...]
    lane = lax.broadcasted_iota(jnp.int32, logits.shape, 1)
    hx_ref[:, d:] = jnp.where(lane < N_GROUPS, logits, _combine_weights(logits))


def _merge(x2, ya, yb, yc, yd, proj, wa, wb, wc, wd, wo, gn_row, wr_hi, wr_lo, br_row, tm):
    n, d = x2.shape
    assert COL_GATE % (4 * d) == 0
    gcol = COL_GATE // (4 * d)

    def rows(width, cc=0):
        return pl.BlockSpec((tm, width), lambda i: (i, cc))

    def full(arr):
        return pl.BlockSpec(arr.shape, lambda i: (0, 0))

    return pl.pallas_call(
        _merge_kernel,
        grid=(n // tm,),
        in_specs=[rows(d), rows(ya.shape[1]), rows(yb.shape[1]), rows(yc.shape[1]), rows(yd.shape[1]),
                  rows(4 * d, gcol),
                  full(wa), full(wb), full(wc), full(wd), full(wo), full(gn_row),
                  full(wr_hi), full(wr_lo), full(br_row)],
        out_specs=[rows(d), rows(d + LANES)],
        out_shape=[jax.ShapeDtypeStruct((n, d), F32), jax.ShapeDtypeStruct((n, d + LANES), F32)],
        compiler_params=_cparams(("parallel",)),
        name="merge",
    )(x2, ya, yb, yc, yd, proj, wa, wb, wc, wd, wo, gn_row, wr_hi, wr_lo, br_row)


def _group_onehot(logits):
    lane = lax.broadcasted_iota(jnp.int32, logits.shape, 1).astype(F32)
    gl = jnp.where(lane < N_GROUPS, logits, -jnp.inf)
    g_idx = _first_index(gl == jnp.max(gl, axis=-1, keepdims=True), lane)
    return jnp.where(lane == g_idx, 1.0, 0.0)


def _route_kernel(lg_ref, tri_ref, pos_ref, meta_ref, cnt_ref, run_ref, off_ref, *, row_block):
    phase = pl.program_id(0)
    i = pl.program_id(1)
    onehot = _group_onehot(lg_ref[...])
    lane = lax.broadcasted_iota(jnp.int32, (1, LANES), 1)

    @pl.when((phase == 0) & (i == 0))
    def _():
        cnt_ref[...] = jnp.zeros(cnt_ref.shape, F32)

    @pl.when(phase == 0)
    def _():
        cnt_ref[...] += jnp.sum(onehot, axis=0, keepdims=True)
        pos_ref[...] = jnp.zeros(pos_ref.shape, jnp.int32)

    @pl.when((phase == 1) & (i == 0))
    def _():
        padded = jnp.floor((cnt_ref[...] + (row_block - 1)) * (1.0 / row_block)) * row_block
        off = jnp.zeros((1, LANES), F32)
        acc = jnp.zeros((1, 1), F32)
        for g in range(1, N_GROUPS):
            acc = acc + jnp.sum(jnp.where(lane == g - 1, padded, 0.0), axis=-1, keepdims=True)
            off = off + jnp.where(lane == g, acc, 0.0)
        off_ref[...] = off
        run_ref[...] = jnp.zeros(run_ref.shape, F32)
        meta_ref[...] = jnp.zeros(meta_ref.shape, F32)
        meta_ref[0:1, :] = cnt_ref[...]
        meta_ref[1:2, :] = off

    @pl.when(phase == 1)
    def _():
        before = _dot(tri_ref[...], onehot.astype(BF16))
        slot = jnp.sum(onehot * (before + run_ref[...] + off_ref[...]), axis=-1, keepdims=True)
        pos_ref[...] = slot.astype(jnp.int32)
        run_ref[...] += jnp.sum(onehot, axis=0, keepdims=True)


def _moe_route(hx, d, tr, row_block):
    n = hx.shape[0]
    nb = n // tr
    tri = jnp.asarray(np.tril(np.ones((tr, tr), np.float32), -1), BF16)
    pos, meta = pl.pallas_call(
        functools.partial(_route_kernel, row_block=row_block),
        grid=(2, nb),
        in_specs=[pl.BlockSpec((tr, LANES), lambda p, i: (i, d // LANES)),
                  pl.BlockSpec((tr, tr), lambda p, i: (0, 0))],
        out_specs=[pl.BlockSpec((tr, 1), lambda p, i: (p * nb + i, 0)),
                   pl.BlockSpec((8, LANES), lambda p, i: (0, 0))],
        out_shape=[jax.ShapeDtypeStruct((2 * n, 1), jnp.int32), jax.ShapeDtypeStruct((8, LANES), F32)],
        scratch_shapes=[pltpu.VMEM((1, LANES), F32)] * 3,
        compiler_params=_cparams(("arbitrary", "arbitrary")),
        name="moe_route",
    )(hx, tri)
    return pos[n:], meta


def _row_dma_wait(src_hbm, dst_ref, sem, rows):
    pltpu.make_async_copy(src_hbm.at[pl.ds(0, rows)], dst_ref.at[pl.ds(0, rows)], sem).wait()


def _burst_pipeline(n_bursts, issue_burst, wait_burst):
    for b in range(n_bursts):
        issue_burst(b, b % 2)
        if b >= 1:
            wait_burst(b - 1, (b - 1) % 2)
    wait_burst(n_bursts - 1, (n_bursts - 1) % 2)


def _scatter_kernel(pos_ref, ends_ref, hx_ref, hs_hbm, zero_ref, sems, *, tr, burst, row_block):
    @pl.when(pl.program_id(0) == 0)
    def _():
        zero_ref[...] = jnp.zeros(zero_ref.shape, F32)
        run = zero_ref.shape[0]
        for g in range(N_GROUPS):
            start = pl.multiple_of((ends_ref[g] >> 3) << 3, 8)
            pltpu.make_async_copy(zero_ref, hs_hbm.at[pl.ds(start, run)], sems.at[0]).start()
        for g in range(N_GROUPS):
            pltpu.make_async_copy(zero_ref, hs_hbm.at[pl.ds(0, run)], sems.at[0]).wait()
        used = ends_ref[N_GROUPS]
        n_tail = (hs_hbm.shape[0] - used) // row_block
        for k in range(N_GROUPS + 1):
            @pl.when(k < n_tail)
            def _():
                start = pl.multiple_of(used + k * row_block, 8)
                cp = pltpu.make_async_copy(zero_ref.at[pl.ds(0, row_block)],
                                           hs_hbm.at[pl.ds(start, row_block)], sems.at[0])
                cp.start()
                cp.wait()

    def issue_burst(b, slot):
        def issue(t8, c):
            for k in range(MOE_DMA_UNROLL):
                row = b * burst + t8 * MOE_DMA_UNROLL + k
                pltpu.make_async_copy(hx_ref.at[pl.ds(row, 1)], hs_hbm.at[pl.ds(pos_ref[0, 0, row], 1)],
                                      sems.at[slot]).start(priority=k % 2)
            return c
        lax.fori_loop(0, burst // MOE_DMA_UNROLL, issue, 0)

    def wait_burst(b, slot):
        _row_dma_wait(hx_ref, hs_hbm, sems.at[slot], burst)

    _burst_pipeline(tr // burst, issue_burst, wait_burst)


def _moe_scatter(hx, pos3, real_ends, n_sorted, tr, row_block):
    n, w = hx.shape
    nb = n // tr
    return pl.pallas_call(
        functools.partial(_scatter_kernel, tr=tr, burst=min(MOE_DMA_BURST, tr), row_block=row_block),
        grid=(nb,),
        in_specs=[pl.BlockSpec((1, 1, tr), lambda i: (i, 0, 0), memory_space=pltpu.SMEM),
                  pl.BlockSpec(memory_space=pltpu.SMEM),
                  pl.BlockSpec((tr, w), lambda i: (i, 0))],
        out_specs=pl.BlockSpec(memory_space=pl.ANY),
        out_shape=jax.ShapeDtypeStruct((n_sorted, w), F32),
        scratch_shapes=[pltpu.VMEM((row_block + 8, w), F32), pltpu.SemaphoreType.DMA((2,))],
        compiler_params=_cparams(("arbitrary",)),
        name="moe_scatter",
    )(pos3, real_ends, hx)


def _experts_kernel(gmap_ref, valid_ref, hs_ref, wg_ref, wu_ref, wd_ref, o_ref, acc_ref, *, d):
    i = pl.program_id(0)
    j = pl.program_id(1)
    valid = valid_ref[i] > 0

    @pl.when(j == 0)
    def _():
        acc_ref[...] = jnp.zeros(acc_ref.shape, F32)

    @pl.when(valid)
    def _():
        h = hs_ref[:, :d].astype(BF16)
        a = _dot(h, jnp.concatenate([wg_ref[0, 0], wg_ref[0, 1]], axis=1))
        hid = (a * jax.nn.sigmoid(a)) * _dot(h, jnp.concatenate([wu_ref[0, 0], wu_ref[0, 1]], axis=1))
        first = N_GROUPS + EXPERTS_PER_GROUP * gmap_ref[i] + 2 * j
        cw_all = hs_ref[:, d:]
        lane = lax.broadcasted_iota(jnp.int32, cw_all.shape, 1)
        cw0 = jnp.sum(jnp.where(lane == first, cw_all, 0.0), axis=-1, keepdims=True)
        cw1 = jnp.sum(jnp.where(lane == first + 1, cw_all, 0.0), axis=-1, keepdims=True)
        de = hid.shape[1] // 2
        hcol = lax.broadcasted_iota(jnp.int32, hid.shape, 1)
        hid = hid * jnp.where(hcol < de, cw0, cw1)
        acc_ref[...] += _dot(hid.astype(BF16), wd_ref[0])

    @pl.when(j == pl.num_programs(1) - 1)
    def _():
        o_ref[...] = acc_ref[...]


def _moe_experts(hs, gmap, valid, wg, wu, wd, d, row_block):
    n_sorted, w = hs.shape
    _, _, _, de = wg.shape
    de2 = 2 * de
    pairs = EXPERTS_PER_GROUP // 2
    grid_spec = pltpu.PrefetchScalarGridSpec(
        num_scalar_prefetch=2,
        grid=(n_sorted // row_block, pairs),
        in_specs=[pl.BlockSpec((row_block, w), lambda i, j, gm, va: (i * va[i], 0)),
                  pl.BlockSpec((1, 2, d, de), lambda i, j, gm, va: (gm[i] * pairs + j, 0, 0, 0)),
                  pl.BlockSpec((1, 2, d, de), lambda i, j, gm, va: (gm[i] * pairs + j, 0, 0, 0)),
                  pl.BlockSpec((1, de2, d), lambda i, j, gm, va: (gm[i] * pairs + j, 0, 0))],
        out_specs=pl.BlockSpec((row_block, d), lambda i, j, gm, va: (i, 0)),
        scratch_shapes=[pltpu.VMEM((row_block, d), F32)])
    return pl.pallas_call(
        functools.partial(_experts_kernel, d=d),
        grid_spec=grid_spec,
        out_shape=jax.ShapeDtypeStruct((n_sorted, d), F32),
        compiler_params=_cparams(("arbitrary", "arbitrary")),
        name="moe_experts",
    )(gmap, valid, hs, wg, wu, wd)


def _combine_kernel(pos_ref, ys_hbm, x_ref, o_ref, buf_ref, sems, *, tr, burst):
    def issue_burst(b, slot):
        def issue(t8, c):
            for k in range(MOE_DMA_UNROLL):
                row = b * burst + t8 * MOE_DMA_UNROLL + k
                pltpu.make_async_copy(ys_hbm.at[pl.ds(pos_ref[0, 0, row], 1)], buf_ref.at[pl.ds(row, 1)],
                                      sems.at[slot]).start(priority=k % 2)
            return c
        lax.fori_loop(0, burst // MOE_DMA_UNROLL, issue, 0)

    def wait_burst(b, slot):
        _row_dma_wait(ys_hbm, buf_ref, sems.at[slot], burst)
        rows = slice(b * burst, (b + 1) * burst)
        o_ref[rows, :] = x_ref[rows, :] + buf_ref[rows, :]

    _burst_pipeline(tr // burst, issue_burst, wait_burst)


def _moe_combine(ys, pos3, x2, tr):
    n, d = x2.shape
    return pl.pallas_call(
        functools.partial(_combine_kernel, tr=tr, burst=min(MOE_DMA_BURST, tr)),
        grid=(n // tr,),
        in_specs=[pl.BlockSpec((1, 1, tr), lambda i: (i, 0, 0), memory_space=pltpu.SMEM),
                  pl.BlockSpec(memory_space=pl.ANY),
                  pl.BlockSpec((tr, d), lambda i: (i, 0))],
        out_specs=pl.BlockSpec((tr, d), lambda i: (i, 0)),
        out_shape=jax.ShapeDtypeStruct((n, d), F32),
        scratch_shapes=[pltpu.VMEM((tr, d), F32), pltpu.SemaphoreType.DMA((2,))],
        compiler_params=_cparams(("arbitrary",)),
        name="moe_combine",
    )(pos3, ys, x2)


def _pair_cols(w):
    ne, d, de = w.shape
    return w.astype(BF16).reshape(ne // 2, 2, d, de)


def _pair_rows(w):
    ne, de, d = w.shape
    return w.astype(BF16).reshape(ne // 2, 2 * de, d)


def _moe(hx, x2, wg, wu, wd, tr):
    n, d = x2.shape
    row_block = min(MOE_ROW_BLOCK, n)
    n_blocks = n // row_block + N_GROUPS + 1
    pos, meta = _moe_route(hx, d, tr, row_block)
    counts = meta[0, :N_GROUPS]
    ends = meta[1, :N_GROUPS] + jnp.ceil(counts / row_block) * row_block
    starts = jnp.arange(n_blocks, dtype=F32) * row_block
    grp = jnp.sum((starts[:, None] >= ends[None, :]).astype(jnp.int32), axis=1)
    valid = (grp < N_GROUPS).astype(jnp.int32)
    gmap = jnp.minimum(grp, N_GROUPS - 1)
    real_ends = jnp.concatenate([meta[1, :N_GROUPS] + counts, ends[N_GROUPS - 1:]]).astype(jnp.int32)
    pos3 = pos.reshape(n // tr, 1, tr)
    hs = _moe_scatter(hx, pos3, real_ends, n_blocks * row_block, tr, row_block)
    ys = _moe_experts(hs, gmap, valid, wg, wu, wd, d, row_block)
    return _moe_combine(ys, pos3, x2, tr)


def _pick(n, pref):
    t = min(n, pref)
    while n % t:
        t //= 2
    return t


def kernel(x, positions, norm_mix, w_in, da_q_gain, da_k_gain, da_lambda_q1, da_lambda_k1,
           da_lambda_q2, da_lambda_k2, da_subln_gain, hg_lower_bounds, hg_out_gain,
           s5_lambda_re, s5_lambda_im, s5_log_step, s5_b_re, s5_b_im, s5_c_re, s5_c_im,
           s5_d, s5_glu_w, s5_glu_b, w_branch_attn, w_branch_sb, w_branch_hgrn, w_branch_s5,
           w_out, norm_ffn, router_group_w, router_group_b, router_expert_w, router_expert_b,
           expert_w_gate, expert_w_up, expert_w_down):
    bsz, seq, d = x.shape
    depth = w_in.shape[0]
    n = bsz * seq
    assert w_in.shape[2] == IN_COLS and seq % 128 == 0
    tm = _pick(n, ROW_TM)
    tq_da = _pick(seq, DA_TQ)
    tb = _pick(seq, SCAN_TB)

    x2 = x.reshape(n, d).astype(F32)
    cos_t, sina_t, sinb_t = _rope_tables(positions.reshape(n, 1).astype(jnp.int32), tm)

    lb_all = jnp.cumsum(jax.nn.softmax(hg_lower_bounds.astype(F32), axis=0), axis=0)
    lb_all = lb_all - lb_all[0:1]

    for l in range(depth):
        lambda_init = DA_LAMBDA_INIT_BASE - DA_LAMBDA_INIT_SCALE * math.exp(-DA_LAMBDA_INIT_RATE * l)
        w_f = w_in[l].astype(F32)
        sbq = COL_SB_Q - COL_DA_Q
        w_l = jnp.concatenate([w_f[:, REF_GATE_START:], w_f[:, :sbq],
                               w_f[:, sbq:sbq + SB_HEADS * SB_HEAD_DIM] * (SB_HEAD_DIM ** -0.5 * LOG2E),
                               w_f[:, sbq + SB_HEADS * SB_HEAD_DIM:REF_GATE_START]], axis=1)
        proj = _norm_proj(x2, norm_mix[l].astype(F32)[None, :], w_l.astype(BF16), _pick(n, NP_TM), NP_TN)

        qk_gain = jnp.concatenate([jnp.tile(da_q_gain[l].astype(F32), 2 * DA_HEADS),
                                   jnp.tile(da_k_gain[l].astype(F32), 2 * DA_HEADS)])[None, :]
        qk = _qk_prep(proj, qk_gain, cos_t, sina_t, sinb_t, tm)
        y_a = _diff_attn(qk, proj,
                         da_lambda_q1[l].astype(F32)[None, :], da_lambda_k1[l].astype(F32)[None, :],
                         da_lambda_q2[l].astype(F32)[None, :], da_lambda_k2[l].astype(F32)[None, :],
                         da_subln_gain[l].astype(F32)[None, :], bsz, seq, tq_da, DA_KW, DA_GROUP, lambda_init)
        y_b = _stick_break(proj, bsz, seq, _pick(seq, SB_TQ), SB_KW, SB_GROUP)
        y_c = _hgrn(proj, lb_all[l][None, :], jnp.tile(hg_out_gain[l].astype(F32), HG_HEADS)[None, :],
                    bsz, seq, tb)
        s5p = _s5_params(s5_lambda_re[l], s5_lambda_im[l], s5_log_step[l], s5_b_re[l], s5_b_im[l],
                         s5_c_re[l], s5_c_im[l], tb)
        y_d = _s5(proj, s5p, s5_d[l].astype(F32)[None, :], s5_glu_w[l].astype(BF16),
                  s5_glu_b[l].astype(F32)[None, :], bsz, seq, tb)

        wr = jnp.concatenate([router_group_w[l], router_expert_w[l]], axis=1).astype(F32)
        wr = jnp.pad(wr, ((0, 0), (0, LANES - wr.shape[1])))
        wr_hi = wr.astype(BF16)
        wr_lo = (wr - wr_hi.astype(F32)).astype(BF16)
        br = jnp.concatenate([router_group_b[l], router_expert_b[l]]).astype(F32)
        br = jnp.pad(br, (0, LANES - br.shape[0]))[None, :]
        x2, hx = _merge(x2, y_a, y_b, y_c, y_d, proj,
                        w_branch_attn[l].astype(BF16), w_branch_sb[l].astype(BF16),
                        w_branch_hgrn[l].astype(BF16), w_branch_s5[l].astype(BF16),
                        w_out[l].astype(BF16), norm_ffn[l].astype(F32)[None, :],
                        wr_hi, wr_lo, br, _pick(n, MERGE_TM))
        x2 = _moe(hx, x2, _pair_cols(expert_w_gate[l]), _pair_cols(expert_w_up[l]),
                  _pair_rows(expert_w_down[l]), tm)

    return x2.reshape(bsz, seq, d).astype(x.dtype)
```

```python
import functools
import math

import jax
import jax.numpy as jnp
import numpy as np
from jax import lax
from jax.experimental import pallas as pl
from jax.experimental.pallas import tpu as pltpu

F32 = jnp.float32
BF16 = jnp.bfloat16

RMS_EPS = 1e-6
LANES = 128
LOG2E = 1.4426950408889634

DA_HEADS = 4
DA_QK_DIM = 64
ROPE_THETA = 500000.0
ROPE_DIM = DA_QK_DIM // 4
DA_LAMBDA_INIT_BASE = 0.8
DA_LAMBDA_INIT_SCALE = 0.6
DA_LAMBDA_INIT_RATE = 0.3

SB_HEADS = 4
SB_HEAD_DIM = 64

HG_HEADS = 4
HG_DIM = 64
HG_COLS = HG_HEADS * HG_DIM
HG_SUB = 16
HG_UNROLL = 16

S5_GROUPS = 16
S5_GROUP_CH = 16
S5_STATE = 64
S5_WIDTH = S5_GROUPS * S5_GROUP_CH
S5_NSTATE = S5_GROUPS * S5_STATE
S5_EIG_CLIP = -1e-4
S5_SUB = 8

N_GROUPS = 4
EXPERTS_PER_GROUP = 4

REF_GATE_START = 3584
COL_GATE = 0
COL_DA_Q = 4096
COL_DA_V = 5120
COL_SB_Q = 5632
COL_SB_K = 5888
COL_SB_V = 6144
COL_HG_F = 6400
COL_HG_I = 6656
COL_HG_Q = 6912
COL_HG_G = 7168
COL_S5_U = 7424
IN_COLS = 7680

VMEM_LIMIT = 48 * 1024 * 1024

ROW_TM = 1024
MERGE_TM = 512
SCAN_TB = 512
NP_TN = 1536
DA_TQ = 1024
DA_KW = 1024
DA_GROUP = 1
DA_DIAG_SPLIT = 4
SB_TQ = 512
SB_KW = 256
SB_GROUP = 4
MOE_ROW_BLOCK = 512
MOE_DMA_BURST = 256
MOE_DMA_UNROLL = 8
NP_TM = 2048

_NT = (((1,), (1,)), ((), ()))
_TN = (((0,), (0,)), ((), ()))


def _cparams(sem):
    return pltpu.CompilerParams(dimension_semantics=sem, vmem_limit_bytes=VMEM_LIMIT)


def _dot(a, b):
    return jnp.dot(a, b, preferred_element_type=F32)


def _norm_proj_kernel(x_ref, g_ref, w_ref, o_ref, h_ref):
    @pl.when(pl.program_id(1) == 0)
    def _():
        x = x_ref[...]
        ms = jnp.mean(x * x, axis=-1, keepdims=True)
        h_ref[...] = (x * lax.rsqrt(ms + RMS_EPS) * g_ref[...]).astype(BF16)

    o_ref[...] = _dot(h_ref[...], w_ref[...]).astype(o_ref.dtype)


def _norm_proj(x2, gain, w, tm, tn):
    n, d = x2.shape
    cols = w.shape[1]
    return pl.pallas_call(
        _norm_proj_kernel,
        grid=(n // tm, cols // tn),
        in_specs=[pl.BlockSpec((tm, d), lambda i, j: (i, 0)),
                  pl.BlockSpec((1, d), lambda i, j: (0, 0)),
                  pl.BlockSpec((d, tn), lambda i, j: (0, j))],
        out_specs=pl.BlockSpec((tm, tn), lambda i, j: (i, j)),
        out_shape=jax.ShapeDtypeStruct((n, cols), BF16),
        scratch_shapes=[pltpu.VMEM((tm, d), BF16)],
        compiler_params=_cparams(("parallel", "arbitrary")),
        name="norm_proj",
    )(x2, gain, w)


def _rope_kernel(pos_ref, invf_ref, sa_ref, sb_ref, cos_ref, sina_ref, sinb_ref):
    ang = pos_ref[...].astype(F32) * invf_ref[...]
    c = jnp.cos(ang)
    s = jnp.sin(ang)
    cos_ref[...] = c
    sina_ref[...] = s * sa_ref[...]
    sinb_ref[...] = s * sb_ref[...]


def _rope_tables(pos_col, tm):
    n = pos_col.shape[0]
    half = ROPE_DIM // 2
    inv_freq = jnp.exp(-math.log(ROPE_THETA) * jnp.arange(half, dtype=F32) * (2.0 / ROPE_DIM))
    lane = np.arange(LANES) % DA_QK_DIM
    invf = jnp.where(lane < ROPE_DIM, inv_freq[lane % half], 0.0).astype(F32)[None, :]
    sgn_a = jnp.asarray(np.where(lane < half, -1.0, 0.0), F32)[None, :]
    sgn_b = jnp.asarray(np.where((lane >= half) & (lane < ROPE_DIM), 1.0, 0.0), F32)[None, :]
    row = pl.BlockSpec((1, LANES), lambda i: (0, 0))
    tab = pl.BlockSpec((tm, LANES), lambda i: (i, 0))
    shp = jax.ShapeDtypeStruct((n, LANES), F32)
    return pl.pallas_call(
        _rope_kernel,
        grid=(n // tm,),
        in_specs=[pl.BlockSpec((tm, 1), lambda i: (i, 0)), row, row, row],
        out_specs=[tab, tab, tab],
        out_shape=[shp, shp, shp],
        compiler_params=_cparams(("parallel",)),
        name="rope_tables",
    )(pos_col, invf, sgn_a, sgn_b)


def _qk_prep_kernel(x_ref, gain_ref, cos_ref, sina_ref, sinb_ref, bd_ref, rot_ref, o_ref):
    c = cos_ref[...]
    s = sinb_ref[...] - sina_ref[...]
    bd = bd_ref[...]
    rot = rot_ref[...]
    n_tiles = x_ref.shape[1] // LANES
    for j in range(n_tiles):
        sl = slice(j * LANES, (j + 1) * LANES)
        t = x_ref[:, sl].astype(F32)
        ss = _dot((t * t).astype(BF16), bd)
        y = t * lax.rsqrt(ss * (1.0 / DA_QK_DIM) + RMS_EPS) * gain_ref[:, sl]
        y = y * c + _dot(y.astype(BF16), rot) * s
        if j < n_tiles // 2:
            y = y * (DA_QK_DIM ** -0.5 * LOG2E)
        o_ref[:, sl] = y.astype(BF16)


def _qk_prep(proj, gain_row, cos_t, sina_t, sinb_t, tm):
    n = proj.shape[0]
    w = 2 * DA_HEADS * 2 * DA_QK_DIM
    seg = np.arange(LANES) // DA_QK_DIM
    bd = jnp.asarray(seg[:, None] == seg[None, :], BF16)
    half = ROPE_DIM // 2
    src = np.arange(LANES)[:, None]
    dst = np.arange(LANES)[None, :]
    first = dst % DA_QK_DIM < half
    second = (dst % DA_QK_DIM >= half) & (dst % DA_QK_DIM < ROPE_DIM)
    rot = jnp.asarray(np.where(first & (src == dst + half), -1.0, 0.0)
                      + np.where(second & (src == dst - half), 1.0, 0.0), BF16)
    tab = pl.BlockSpec((tm, LANES), lambda i: (i, 0))
    sq = pl.BlockSpec((LANES, LANES), lambda i: (0, 0))
    return pl.pallas_call(
        _qk_prep_kernel,
        grid=(n // tm,),
        in_specs=[pl.BlockSpec((tm, w), lambda i: (i, COL_DA_Q // w)),
                  pl.BlockSpec((1, w), lambda i: (0, 0)),
                  tab, tab, tab, sq, sq],
        out_specs=pl.BlockSpec((tm, w), lambda i: (i, 0)),
        out_shape=jax.ShapeDtypeStruct((n, w), BF16),
        compiler_params=_cparams(("parallel",)),
        name="qk_prep",
    )(proj, gain_row, cos_t, sina_t, sinb_t, bd, rot)


def _diff_attn_kernel(q_ref, k_ref, v_ref, lq1_ref, lk1_ref, lq2_ref, lk2_ref, sg_ref, o_ref,
                      m_ref, l_ref, acc_ref, *, tq, kw, group, lambda_init):
    qi = pl.program_id(2)
    q = q_ref[...]
    lane = lax.broadcasted_iota(jnp.int32, q.shape, 1)
    zero = jnp.zeros_like(q)
    q2 = jnp.concatenate([jnp.where(lane < DA_QK_DIM, q, zero),
                          jnp.where(lane >= DA_QK_DIM, q, zero)], axis=0)
    m_ref[...] = jnp.full(m_ref.shape, -jnp.inf, F32)
    l_ref[...] = jnp.zeros(l_ref.shape, F32)
    acc_ref[...] = jnp.zeros(acc_ref.shape, F32)

    def rows_of(ref_or_val, row_lo):
        if row_lo == 0:
            return ref_or_val[...]
        return jnp.concatenate([ref_or_val[row_lo:tq], ref_or_val[tq + row_lo:2 * tq]], axis=0)

    def put_rows(ref, row_lo, val):
        if row_lo == 0:
            ref[...] = val
        else:
            ref[row_lo:tq] = val[:tq - row_lo]
            ref[tq + row_lo:2 * tq] = val[tq - row_lo:]

    def tile(start, width, mask_off, row_lo=0):
        n_rows = tq - row_lo
        n_lt = width // LANES
        kb = k_ref[pl.ds(start, width), :]
        vb = v_ref[pl.ds(start, width), :]
        s = lax.dot_general(rows_of(q2, row_lo), kb, _NT, preferred_element_type=F32)
        if mask_off is not None:
            keep = (lax.broadcasted_iota(jnp.int32, (n_rows, width), 1) + mask_off
                    <= lax.broadcasted_iota(jnp.int32, (n_rows, width), 0) + row_lo)
            s = jnp.where(jnp.concatenate([keep, keep], axis=0), s, -jnp.inf)
        st = [s[:, j * LANES:(j + 1) * LANES] for j in range(n_lt)]
        smax = st[0]
        for j in range(1, n_lt):
            smax = jnp.maximum(smax, st[j])
        m_prev = rows_of(m_ref, row_lo)
        m_new = jnp.maximum(m_prev, jnp.max(smax, axis=-1, keepdims=True))
        alpha = jnp.exp2(m_prev - m_new)
        ps = [jnp.exp2(t - m_new) for t in st]
        lsum = ps[0]
        for j in range(1, n_lt):
            lsum = lsum + ps[j]
        p = jnp.concatenate([t.astype(BF16) for t in ps], axis=1)
        put_rows(l_ref, row_lo, alpha * rows_of(l_ref, row_lo) + lsum)
        put_rows(acc_ref, row_lo, alpha * rows_of(acc_ref, row_lo) + _dot(p, vb))
        put_rows(m_ref, row_lo, m_new)

    n_diag = DA_DIAG_SPLIT
    dw = tq // n_diag
    n_below = qi * (tq // kw)

    def body_group(gi, carry):
        for j in range(group):
            tile(pl.multiple_of((gi * group + j) * kw, kw), kw, None)
        return carry

    def body_single(ki, carry):
        tile(pl.multiple_of(ki * kw, kw), kw, None)
        return carry

    n_group = n_below // group
    lax.fori_loop(0, n_group, body_group, 0)
    lax.fori_loop(n_group * group, n_below, body_single, 0)
    for j in range(n_diag):
        tile(pl.multiple_of(qi * tq + j * dw, dw), dw, j * dw, row_lo=j * dw)

    lam = (jnp.exp(jnp.sum(lq1_ref[...] * lk1_ref[...], axis=-1, keepdims=True))
           - jnp.exp(jnp.sum(lq2_ref[...] * lk2_ref[...], axis=-1, keepdims=True)) + lambda_init)
    o = acc_ref[...] / jnp.sum(l_ref[...], axis=-1, keepdims=True)
    o = o[:tq] - lam * o[tq:]
    ms = jnp.mean(o * o, axis=-1, keepdims=True)
    o = o * lax.rsqrt(ms + RMS_EPS) * sg_ref[...] * (1.0 - lambda_init)
    o_ref[...] = o.astype(o_ref.dtype)


def _diff_attn(qk, proj, lq1, lk1, lq2, lk2, subln, bsz, seq, tq, kw, group, lambda_init):
    n = qk.shape[0]
    nq = seq // tq
    kcol = (DA_HEADS * 2 * DA_QK_DIM) // LANES
    vcol = COL_DA_V // LANES
    vec = pl.BlockSpec((1, DA_QK_DIM), lambda b, h, i: (0, 0))
    return pl.pallas_call(
        functools.partial(_diff_attn_kernel, tq=tq, kw=min(kw, tq), group=group, lambda_init=lambda_init),
        grid=(bsz, DA_HEADS, nq),
        in_specs=[pl.BlockSpec((tq, LANES), lambda b, h, i: (b * nq + i, h)),
                  pl.BlockSpec((seq, LANES), lambda b, h, i: (b, kcol + h)),
                  pl.BlockSpec((seq, LANES), lambda b, h, i: (b, vcol + h)),
                  vec, vec, vec, vec,
                  pl.BlockSpec((1, LANES), lambda b, h, i: (0, 0))],
        out_specs=pl.BlockSpec((tq, LANES), lambda b, h, i: (b * nq + i, h)),
        out_shape=jax.ShapeDtypeStruct((n, DA_HEADS * LANES), BF16),
        scratch_shapes=[pltpu.VMEM((2 * tq, LANES), F32), pltpu.VMEM((2 * tq, LANES), F32),
                        pltpu.VMEM((2 * tq, LANES), F32)],
        compiler_params=_cparams(("parallel", "parallel", "arbitrary")),
        name="diff_attn",
    )(qk, qk, proj, lq1, lk1, lq2, lk2, subln)


def _stick_break_kernel(q_ref, k_ref, v_ref, o_ref, r_ref, acc_ref, *, tq, kw, group):
    qi = pl.program_id(2)
    q = q_ref[...]
    lane = lax.broadcasted_iota(jnp.int32, q.shape, 1)
    zero = jnp.zeros_like(q)
    q2 = jnp.concatenate([jnp.where(lane < SB_HEAD_DIM, q, zero),
                          jnp.where(lane >= SB_HEAD_DIM, q, zero)], axis=0)
    incl = jnp.where(lax.broadcasted_iota(jnp.int32, (kw, kw), 0)
                     >= lax.broadcasted_iota(jnp.int32, (kw, kw), 1), 1.0, 0.0).astype(BF16)
    n_lt = kw // LANES
    n_diag = tq // kw
    r_ref[...] = jnp.zeros(r_ref.shape, F32)
    acc_ref[...] = jnp.zeros(acc_ref.shape, F32)

    def run(starts, offsets):
        def both(x, lo):
            return x if lo == 0 else jnp.concatenate([x[lo:tq], x[tq + lo:]], axis=0)

        def spread(x, lo, fill):
            if lo == 0:
                return x
            return jnp.concatenate([fill[:lo], x[:tq - lo], fill[tq:tq + lo], x[tq - lo:]], axis=0)

        r = r_ref[...]
        total = None
        for start, off in zip(starts, offsets):
            lo = off or 0
            u = lax.dot_general(both(q2, lo), k_ref[pl.ds(start, kw), :], _NT, preferred_element_type=F32)
            neg_abs = lax.bitcast_convert_type(
                lax.bitcast_convert_type(u, jnp.int32) | jnp.int32(-2 ** 31), F32)
            sp = jnp.maximum(u, 0.0) + jnp.log(1.0 + jnp.exp2(neg_abs)) * LOG2E
            mask = None
            if off is not None:
                keep = (lax.broadcasted_iota(jnp.int32, (tq - lo, kw), 1) + off
                        < lax.broadcasted_iota(jnp.int32, (tq - lo, kw), 0) + lo)
                mask = jnp.concatenate([keep, keep], axis=0)
                sp = jnp.where(mask, sp, 0.0)
            cum = _dot(sp.astype(BF16), incl)
            r_blk = both(r, lo)
            w = jnp.exp2(u - (cum + jnp.concatenate([r_blk] * n_lt, axis=1)))
            if mask is not None:
                w = jnp.where(mask, w, 0.0)
            part = _dot(w.astype(BF16), v_ref[pl.ds(start, kw), :])
            part = spread(part, lo, jnp.zeros((2 * tq, LANES), F32))
            total = part if total is None else total + part
            r = spread(r_blk + cum[:, 0:1], lo, r)
        acc_ref[...] += total
        r_ref[...] = r

    base = qi * tq
    run([pl.multiple_of(base + (n_diag - 1 - j) * kw, kw) for j in range(n_diag)],
        [(n_diag - 1 - j) * kw for j in range(n_diag)])

    n_below = qi * n_diag
    rem = n_below % group

    def body_rem(pi, carry):
        top = n_below - pi * n_diag
        run([pl.multiple_of((top - 1 - j) * kw, kw) for j in range(n_diag)], [None] * n_diag)
        return carry

    def body_group(gi, carry):
        top = n_below - rem - gi * group
        run([pl.multiple_of((top - 1 - j) * kw, kw) for j in range(group)], [None] * group)
        return carry

    lax.fori_loop(0, rem // n_diag, body_rem, 0)
    lax.fori_loop(0, n_below // group, body_group, 0)
    lane_o = lax.broadcasted_iota(jnp.int32, (tq, LANES), 1)
    o_ref[...] = jnp.where(lane_o < SB_HEAD_DIM, acc_ref[:tq, :], acc_ref[tq:, :]).astype(o_ref.dtype)


def _stick_break(proj, bsz, seq, tq, kw, group):
    assert tq % kw == 0 and group % (tq // kw) == 0
    n = proj.shape[0]
    nq = seq // tq
    pairs = (SB_HEADS * SB_HEAD_DIM) // LANES
    qc, kc, vc = COL_SB_Q // LANES, COL_SB_K // LANES, COL_SB_V // LANES
    return pl.pallas_call(
        functools.partial(_stick_break_kernel, tq=tq, kw=kw, group=group),
        grid=(bsz, pairs, nq),
        in_specs=[pl.BlockSpec((tq, LANES), lambda b, p, i: (b * nq + i, qc + p)),
                  pl.BlockSpec((seq, LANES), lambda b, p, i: (b, kc + p)),
                  pl.BlockSpec((seq, LANES), lambda b, p, i: (b, vc + p))],
        out_specs=pl.BlockSpec((tq, LANES), lambda b, p, i: (b * nq + i, p)),
        out_shape=jax.ShapeDtypeStruct((n, pairs * LANES), BF16),
        scratch_shapes=[pltpu.VMEM((2 * tq, LANES), F32), pltpu.VMEM((2 * tq, LANES), F32)],
        compiler_params=_cparams(("parallel", "parallel", "arbitrary")),
        name="stick_break",
    )(proj, proj, proj)


def _split_bf16(x):
    hi = x.astype(BF16)
    lo = (x - hi.astype(F32)).astype(BF16)
    return hi, lo


def _hgrn_kernel(f_ref, i_ref, q_ref, g_ref, lb_ref, gain_ref, o_ref,
                 st_ref, b_ref, dec_ref, qq_ref, kk_ref, vv_ref, oo_ref, qe_ref, ke_ref, gm_ref, uu_ref,
                 *, tb):
    @pl.when(pl.program_id(1) == 0)
    def _():
        st_ref[...] = jnp.zeros(st_ref.shape, F32)

    nsub = tb // HG_SUB
    z = f_ref[...].astype(F32)
    lb = lb_ref[...]
    sp = jnp.maximum(z, 0.0) + jnp.log(1.0 + jnp.exp(-jnp.abs(z)))
    log_sig = z - sp
    a = jnp.log(lb)
    c = jnp.log(1.0 - lb) + log_sig
    mx = jnp.maximum(a, c)
    log_f = mx + jnp.log(jnp.exp(a - mx) + jnp.exp(c - mx))
    key = (1.0 - lb) * jax.nn.sigmoid(-z)

    row = lax.broadcasted_iota(jnp.int32, (tb, tb), 0)
    col = lax.broadcasted_iota(jnp.int32, (tb, tb), 1)
    sub_shift = HG_SUB.bit_length() - 1
    dim_shift = HG_DIM.bit_length() - 1
    same = (row >> sub_shift) == (col >> sub_shift)
    tri = jnp.where(same & (col <= row), 1.0, 0.0).astype(BF16)
    blk = jnp.where(same, 1.0, 0.0).astype(BF16)
    hi, lo = _split_bf16(log_f)
    b = _dot(tri, hi) + _dot(tri, lo)
    e = _dot(blk, hi) + _dot(blk, lo)
    qf = q_ref[...].astype(F32)
    b_ref[...] = b
    qq_ref[...] = qf
    kk_ref[...] = key
    vv_ref[...] = i_ref[...].astype(F32)
    qe_ref[...] = (qf * jnp.exp(b)).astype(BF16)
    ke_ref[...] = (key * jnp.exp(e - b)).astype(BF16)
    dec_ref[...] = jnp.exp(e)

    seg_r = lax.broadcasted_iota(jnp.int32, (HG_COLS, HG_COLS), 0) >> dim_shift
    seg_c = lax.broadcasted_iota(jnp.int32, (HG_COLS, HG_COLS), 1) >> dim_shift
    head_mask = seg_r == seg_c
    ones_bd = jnp.where(head_mask, 1.0, 0.0).astype(BF16)
    sel = jnp.where(lax.broadcasted_iota(jnp.int32, (HG_SUB, HG_SUB * HG_SUB), 0)
                    == (lax.broadcasted_iota(jnp.int32, (HG_SUB, HG_SUB * HG_SUB), 1) >> sub_shift),
                    1.0, 0.0).astype(BF16)
    srow = lax.broadcasted_iota(jnp.int32, (HG_SUB, HG_COLS), 0)

    def intra(ci, slot):
        r0 = pl.multiple_of(ci * HG_SUB, HG_SUB)
        bi = b_ref[pl.ds(r0, HG_SUB), :]
        qi = qq_ref[pl.ds(r0, HG_SUB), :]
        ki = kk_ref[pl.ds(r0, HG_SUB), :]
        vi = vv_ref[pl.ds(r0, HG_SUB), :]
        for t in range(HG_SUB):
            d = jnp.exp(jnp.minimum(bi[t:t + 1, :] - bi, 0.0))
            g = jnp.where(srow <= t, qi[t:t + 1, :] * ki * d, 0.0)
            gm_ref[slot, t * HG_SUB:(t + 1) * HG_SUB, :] = g.astype(BF16)
        sc = _dot(gm_ref[slot], ones_bd)
        vt = jnp.concatenate([vi] * HG_SUB, axis=0)
        oo_ref[pl.ds(r0, HG_SUB), :] = _dot(sel, (sc * vt).astype(BF16))
        upd = lax.dot_general(vi.astype(BF16), ke_ref[pl.ds(r0, HG_SUB), :], _TN,
                              preferred_element_type=F32)
        uu_ref[ci] = jnp.where(head_mask, upd, 0.0)

    def intra_group(gi, carry):
        for slot in range(HG_UNROLL):
            intra(gi * HG_UNROLL + slot, slot)
        return carry

    lax.fori_loop(0, nsub // HG_UNROLL, intra_group, 0)

    st = st_ref[...]
    for ci in range(nsub):
        rows = slice(ci * HG_SUB, (ci + 1) * HG_SUB)
        oo_ref[rows, :] += lax.dot_general(qe_ref[rows, :], st.astype(BF16), _NT,
                                           preferred_element_type=F32)
        st = st * dec_ref[ci * HG_SUB:ci * HG_SUB + 1, :] + uu_ref[ci]
    st_ref[...] = st

    o = oo_ref[...]
    ms = _dot((o * o).astype(BF16), ones_bd) * (1.0 / HG_DIM)
    gate = g_ref[...].astype(F32)
    gate = gate * jax.nn.sigmoid(gate)
    o_ref[...] = (o * lax.rsqrt(ms + RMS_EPS) * gain_ref[...] * gate).astype(o_ref.dtype)


def _hgrn(proj, lb_row, gain_row, bsz, seq, tb):
    n = proj.shape[0]
    nb = seq // tb
    w = HG_COLS
    cf, ci, cq, cg = COL_HG_F // w, COL_HG_I // w, COL_HG_Q // w, COL_HG_G // w

    def col(cc):
        return pl.BlockSpec((tb, w), lambda b, i: (b * nb + i, cc))

    rowspec = pl.BlockSpec((1, w), lambda b, i: (0, 0))
    big = pltpu.VMEM((tb, w), F32)
    return pl.pallas_call(
        functools.partial(_hgrn_kernel, tb=tb),
        grid=(bsz, nb),
        in_specs=[col(cf), col(ci), col(cq), col(cg), rowspec, rowspec],
        out_specs=pl.BlockSpec((tb, w), lambda b, i: (b * nb + i, 0)),
        out_shape=jax.ShapeDtypeStruct((n, w), BF16),
        scratch_shapes=[pltpu.VMEM((w, w), F32), big, big, big, big, big, big,
                        pltpu.VMEM((tb, w), BF16), pltpu.VMEM((tb, w), BF16),
                        pltpu.VMEM((HG_UNROLL, HG_SUB * HG_SUB, w), BF16),
                        pltpu.VMEM((tb // HG_SUB, w, w), F32)],
        compiler_params=_cparams(("parallel", "arbitrary")),
        name="hgrn2",
    )(proj, proj, proj, proj, lb_row, gain_row)


def _s5_kernel(u_ref, bblk_ref, lev_re_ref, lev_im_ref, pw_re_ref, pw_im_ref, cblk_ref, d_ref,
               gw_ref, gb_ref, o_ref, cr_ref, ci_ref, xr_ref, xi_ref, *, tb):
    @pl.when(pl.program_id(1) == 0)
    def _():
        cr_ref[...] = jnp.zeros(cr_ref.shape, F32)
        ci_ref[...] = jnp.zeros(ci_ref.shape, F32)

    u = u_ref[...]
    bu = _dot(u, bblk_ref[...])
    xr = bu[:, :S5_NSTATE]
    xi = bu[:, S5_NSTATE:]
    row = lax.broadcasted_iota(jnp.int32, xr.shape, 0)
    sub_lev = S5_SUB.bit_length() - 1
    for j in range(sub_lev):
        d = 1 << j
        ar = lev_re_ref[j:j + 1, :]
        ai = lev_im_ref[j:j + 1, :]
        keep = row >= d
        sr = jnp.where(keep, pltpu.roll(xr, d, 0), 0.0)
        si = jnp.where(keep, pltpu.roll(xi, d, 0), 0.0)
        xr, xi = xr + ar * sr - ai * si, xi + ar * si + ai * sr
    cr = cr_ref[...]
    ci = ci_ref[...]
    pr = pw_re_ref[...]
    pi = pw_im_ref[...]
    gr = xr[:S5_SUB] + pr * cr - pi * ci
    gi = xi[:S5_SUB] + pr * ci + pi * cr
    xr_ref[:S5_SUB, :] = gr
    xi_ref[:S5_SUB, :] = gi
    ar = lev_re_ref[sub_lev:sub_lev + 1, :]
    ai = lev_im_ref[sub_lev:sub_lev + 1, :]
    for g in range(1, tb // S5_SUB):
        rows = slice(g * S5_SUB, (g + 1) * S5_SUB)
        gr, gi = xr[rows] + ar * gr - ai * gi, xi[rows] + ar * gi + ai * gr
        xr_ref[rows, :] = gr
        xi_ref[rows, :] = gi
    cr_ref[...] = gr[S5_SUB - 1:S5_SUB, :]
    ci_ref[...] = gi[S5_SUB - 1:S5_SUB, :]
    y = (_dot(xr_ref[...].astype(BF16), cblk_ref[:S5_NSTATE, :])
         + _dot(xi_ref[...].astype(BF16), cblk_ref[S5_NSTATE:, :]) + d_ref[...] * u.astype(F32))
    y = jax.nn.gelu(y)
    zg = _dot(y.astype(BF16), gw_ref[...]) + gb_ref[...]
    o_ref[...] = (y * jax.nn.sigmoid(zg)).astype(o_ref.dtype)


def _s5_params(lam_re, lam_im, log_step, b_re, b_im, c_re, c_im, tb):
    lam_re = jnp.minimum(lam_re.astype(F32), S5_EIG_CLIP)
    lam_im = lam_im.astype(F32)
    step = jnp.exp(log_step.astype(F32))[:, None]
    mag = jnp.exp(lam_re * step)
    phase = lam_im * step
    a_re = mag * jnp.cos(phase)
    a_im = mag * jnp.sin(phase)
    denom = lam_re * lam_re + lam_im * lam_im
    num_re = a_re - 1.0
    gam_re = (num_re * lam_re + a_im * lam_im) / denom
    gam_im = (a_im * lam_re - num_re * lam_im) / denom
    b_re = b_re.astype(F32)
    b_im = b_im.astype(F32)
    bb_re = gam_re[..., None] * b_re - gam_im[..., None] * b_im
    bb_im = gam_re[..., None] * b_im + gam_im[..., None] * b_re
    eye = jnp.eye(S5_GROUPS, dtype=F32)

    def in_blk(bb):
        return jnp.einsum('gnc,gh->gchn', bb, eye).reshape(S5_WIDTH, S5_NSTATE)

    def out_blk(cc):
        return jnp.einsum('gcn,gh->gnhc', cc.astype(F32), eye).reshape(S5_NSTATE, S5_WIDTH)

    bblk = jnp.concatenate([in_blk(bb_re), in_blk(bb_im)], axis=1).astype(BF16)
    cblk = jnp.concatenate([out_blk(c_re), -out_blk(c_im)], axis=0).astype(BF16)
    ar = a_re.reshape(1, S5_NSTATE)
    ai = a_im.reshape(1, S5_NSTATE)
    sub_lev = S5_SUB.bit_length() - 1
    lev_re, lev_im = [ar], [ai]
    pw_re, pw_im = ar, ai
    for _ in range(sub_lev):
        sr, si = lev_re[-1], lev_im[-1]
        pw_re, pw_im = (jnp.concatenate([pw_re, pw_re * sr - pw_im * si], axis=0),
                        jnp.concatenate([pw_im, pw_re * si + pw_im * sr], axis=0))
        lev_re.append(sr * sr - si * si)
        lev_im.append(2.0 * sr * si)
    lev_re = jnp.concatenate(lev_re, axis=0)
    lev_im = jnp.concatenate(lev_im, axis=0)
    return bblk, cblk, lev_re, lev_im, pw_re, pw_im


def _s5(proj, params, d_row, glu_w, glu_b, bsz, seq, tb):
    n = proj.shape[0]
    nb = seq // tb
    bblk, cblk, lev_re, lev_im, pw_re, pw_im = params
    nlev = lev_re.shape[0]
    ucol = COL_S5_U // S5_WIDTH

    def full(shape):
        return pl.BlockSpec(shape, lambda b, i: (0,) * len(shape))

    return pl.pallas_call(
        functools.partial(_s5_kernel, tb=tb),
        grid=(bsz, nb),
        in_specs=[pl.BlockSpec((tb, S5_WIDTH), lambda b, i: (b * nb + i, ucol)),
                  full((S5_WIDTH, 2 * S5_NSTATE)),
                  full((nlev, S5_NSTATE)), full((nlev, S5_NSTATE)),
                  full((S5_SUB, S5_NSTATE)), full((S5_SUB, S5_NSTATE)),
                  full((2 * S5_NSTATE, S5_WIDTH)),
                  full((1, S5_WIDTH)), full((S5_WIDTH, S5_WIDTH)), full((1, S5_WIDTH))],
        out_specs=pl.BlockSpec((tb, S5_WIDTH), lambda b, i: (b * nb + i, 0)),
        out_shape=jax.ShapeDtypeStruct((n, S5_WIDTH), BF16),
        scratch_shapes=[pltpu.VMEM((1, S5_NSTATE), F32), pltpu.VMEM((1, S5_NSTATE), F32),
                        pltpu.VMEM((tb, S5_NSTATE), F32), pltpu.VMEM((tb, S5_NSTATE), F32)],
        compiler_params=_cparams(("parallel", "arbitrary")),
        name="s5",
    )(proj, bblk, lev_re, lev_im, pw_re, pw_im, cblk, d_row, glu_w, glu_b)


def _first_index(mask, lane):
    return jnp.min(jnp.where(mask, lane, float(LANES)), axis=-1, keepdims=True)


def _combine_weights(logits):
    lane = lax.broadcasted_iota(jnp.int32, logits.shape, 1).astype(F32)
    neg = -jnp.inf
    gl = jnp.where(lane < N_GROUPS, logits, neg)
    gmax = jnp.max(gl, axis=-1, keepdims=True)
    gsum = jnp.sum(jnp.exp(gl - gmax), axis=-1, keepdims=True)
    g_val = 1.0 / gsum
    g_idx = _first_index(gl == gmax, lane)
    lo = N_GROUPS + EXPERTS_PER_GROUP * g_idx
    el = jnp.where((lane >= lo) & (lane < lo + EXPERTS_PER_GROUP), logits, neg)
    emax = jnp.max(el, axis=-1, keepdims=True)
    esum = jnp.sum(jnp.exp(el - emax), axis=-1, keepdims=True)
    i1 = _first_index(el == emax, lane)
    el2 = jnp.where(lane == i1, neg, el)
    e2max = jnp.max(el2, axis=-1, keepdims=True)
    i2 = _first_index(el2 == e2max, lane)
    p1 = 1.0 / esum
    p2 = jnp.exp(e2max - emax) / esum
    tot = p1 + p2
    return (jnp.where(lane == i1, g_val * (p1 / tot), 0.0)
            + jnp.where(lane == i2, g_val * (p2 / tot), 0.0))


def _merge_kernel(x_ref, ya_ref, yb_ref, yc_ref, yd_ref, gate_ref, wa_ref, wb_ref, wc_ref, wd_ref,
                  wo_ref, gn_ref, wrh_ref, wrl_ref, br_ref, xo_ref, hx_ref):
    d = x_ref.shape[1]
    merged = None
    for i, (y_ref, w_ref) in enumerate(((ya_ref, wa_ref), (yb_ref, wb_ref),
                                        (yc_ref, wc_ref), (yd_ref, wd_ref))):
        gate = 0.5 * jnp.tanh(0.5 * gate_ref[:, i * d:(i + 1) * d].astype(F32)) + 0.5
        term = gate * _dot(y_ref[...], w_ref[...])
        merged = term if merged is None else merged + term
    xn = x_ref[...] + _dot(merged.astype(BF16), wo_ref[...])
    xo_ref[...] = xn
    ms = jnp.mean(xn * xn, axis=-1, keepdims=True)
    h = xn * lax.rsqrt(ms + RMS_EPS) * gn_ref[...]
    hi, lo = _split_bf16(h)
    hx_ref[:, :d] = hi.astype(F32)
    wrh = wrh_ref[...]
    logits = _dot(hi, wrh) + _dot(lo, wrh) + _dot(hi, wrl_ref[...]) + br_ref[...]
    lane = lax.broadcasted_iota(jnp.int32, logits.shape, 1)
    hx_ref[:, d:] = jnp.where(lane < N_GROUPS, logits, _combine_weights(logits))


def _merge(x2, ya, yb, yc, yd, proj, wa, wb, wc, wd, wo, gn_row, wr_hi, wr_lo, br_row, tm):
    n, d = x2.shape
    assert COL_GATE % (4 * d) == 0
    gcol = COL_GATE // (4 * d)

    def rows(width, cc=0):
        return pl.BlockSpec((tm, width), lambda i: (i, cc))

    def full(arr):
        return pl.BlockSpec(arr.shape, lambda i: (0, 0))

    return pl.pallas_call(
        _merge_kernel,
        grid=(n // tm,),
        in_specs=[rows(d), rows(ya.shape[1]), rows(yb.shape[1]), rows(yc.shape[1]), rows(yd.shape[1]),
                  rows(4 * d, gcol),
                  full(wa), full(wb), full(wc), full(wd), full(wo), full(gn_row),
                  full(wr_hi), full(wr_lo), full(br_row)],
        out_specs=[rows(d), rows(d + LANES)],
        out_shape=[jax.ShapeDtypeStruct((n, d), F32), jax.ShapeDtypeStruct((n, d + LANES), F32)],
        compiler_params=_cparams(("parallel",)),
        name="merge",
    )(x2, ya, yb, yc, yd, proj, wa, wb, wc, wd, wo, gn_row, wr_hi, wr_lo, br_row)


def _group_onehot(logits):
    lane = lax.broadcasted_iota(jnp.int32, logits.shape, 1).astype(F32)
    gl = jnp.where(lane < N_GROUPS, logits, -jnp.inf)
    g_idx = _first_index(gl == jnp.max(gl, axis=-1, keepdims=True), lane)
    return jnp.where(lane == g_idx, 1.0, 0.0)


def _route_kernel(lg_ref, tri_ref, pos_ref, meta_ref, cnt_ref, run_ref, off_ref, *, row_block):
    phase = pl.program_id(0)
    i = pl.program_id(1)
    onehot = _group_onehot(lg_ref[...])
    lane = lax.broadcasted_iota(jnp.int32, (1, LANES), 1)

    @pl.when((phase == 0) & (i == 0))
    def _():
        cnt_ref[...] = jnp.zeros(cnt_ref.shape, F32)

    @pl.when(phase == 0)
    def _():
        cnt_ref[...] += jnp.sum(onehot, axis=0, keepdims=True)
        pos_ref[...] = jnp.zeros(pos_ref.shape, jnp.int32)

    @pl.when((phase == 1) & (i == 0))
    def _():
        padded = jnp.floor((cnt_ref[...] + (row_block - 1)) * (1.0 / row_block)) * row_block
        off = jnp.zeros((1, LANES), F32)
        acc = jnp.zeros((1, 1), F32)
        for g in range(1, N_GROUPS):
            acc = acc + jnp.sum(jnp.where(lane == g - 1, padded, 0.0), axis=-1, keepdims=True)
            off = off + jnp.where(lane == g, acc, 0.0)
        off_ref[...] = off
        run_ref[...] = jnp.zeros(run_ref.shape, F32)
        meta_ref[...] = jnp.zeros(meta_ref.shape, F32)
        meta_ref[0:1, :] = cnt_ref[...]
        meta_ref[1:2, :] = off

    @pl.when(phase == 1)
    def _():
        before = _dot(tri_ref[...], onehot.astype(BF16))
        slot = jnp.sum(onehot * (before + run_ref[...] + off_ref[...]), axis=-1, keepdims=True)
        pos_ref[...] = slot.astype(jnp.int32)
        run_ref[...] += jnp.sum(onehot, axis=0, keepdims=True)


def _moe_route(hx, d, tr, row_block):
    n = hx.shape[0]
    nb = n // tr
    tri = jnp.asarray(np.tril(np.ones((tr, tr), np.float32), -1), BF16)
    pos, meta = pl.pallas_call(
        functools.partial(_route_kernel, row_block=row_block),
        grid=(2, nb),
        in_specs=[pl.BlockSpec((tr, LANES), lambda p, i: (i, d // LANES)),
                  pl.BlockSpec((tr, tr), lambda p, i: (0, 0))],
        out_specs=[pl.BlockSpec((tr, 1), lambda p, i: (p * nb + i, 0)),
                   pl.BlockSpec((8, LANES), lambda p, i: (0, 0))],
        out_shape=[jax.ShapeDtypeStruct((2 * n, 1), jnp.int32), jax.ShapeDtypeStruct((8, LANES), F32)],
        scratch_shapes=[pltpu.VMEM((1, LANES), F32)] * 3,
        compiler_params=_cparams(("arbitrary", "arbitrary")),
        name="moe_route",
    )(hx, tri)
    return pos[n:], meta


def _row_dma_wait(src_hbm, dst_ref, sem, rows):
    pltpu.make_async_copy(src_hbm.at[pl.ds(0, rows)], dst_ref.at[pl.ds(0, rows)], sem).wait()


def _burst_pipeline(n_bursts, issue_burst, wait_burst):
    for b in range(n_bursts):
        issue_burst(b, b % 2)
        if b >= 1:
            wait_burst(b - 1, (b - 1) % 2)
    wait_burst(n_bursts - 1, (n_bursts - 1) % 2)


def _scatter_kernel(pos_ref, ends_ref, hx_ref, hs_hbm, zero_ref, sems, *, tr, burst, row_block):
    @pl.when(pl.program_id(0) == 0)
    def _():
        zero_ref[...] = jnp.zeros(zero_ref.shape, F32)
        run = zero_ref.shape[0]
        for g in range(N_GROUPS):
            start = pl.multiple_of((ends_ref[g] >> 3) << 3, 8)
            pltpu.make_async_copy(zero_ref, hs_hbm.at[pl.ds(start, run)], sems.at[0]).start()
        for g in range(N_GROUPS):
            pltpu.make_async_copy(zero_ref, hs_hbm.at[pl.ds(0, run)], sems.at[0]).wait()
        used = ends_ref[N_GROUPS]
        n_tail = (hs_hbm.shape[0] - used) // row_block
        for k in range(N_GROUPS + 1):
            @pl.when(k < n_tail)
            def _():
                start = pl.multiple_of(used + k * row_block, 8)
                cp = pltpu.make_async_copy(zero_ref.at[pl.ds(0, row_block)],
                                           hs_hbm.at[pl.ds(start, row_block)], sems.at[0])
                cp.start()
                cp.wait()

    def issue_burst(b, slot):
        def issue(t8, c):
            for k in range(MOE_DMA_UNROLL):
                row = b * burst + t8 * MOE_DMA_UNROLL + k
                pltpu.make_async_copy(hx_ref.at[pl.ds(row, 1)], hs_hbm.at[pl.ds(pos_ref[0, 0, row], 1)],
                                      sems.at[slot]).start(priority=k % 2)
            return c
        lax.fori_loop(0, burst // MOE_DMA_UNROLL, issue, 0)

    def wait_burst(b, slot):
        _row_dma_wait(hx_ref, hs_hbm, sems.at[slot], burst)

    _burst_pipeline(tr // burst, issue_burst, wait_burst)


def _moe_scatter(hx, pos3, real_ends, n_sorted, tr, row_block):
    n, w = hx.shape
    nb = n // tr
    return pl.pallas_call(
        functools.partial(_scatter_kernel, tr=tr, burst=min(MOE_DMA_BURST, tr), row_block=row_block),
        grid=(nb,),
        in_specs=[pl.BlockSpec((1, 1, tr), lambda i: (i, 0, 0), memory_space=pltpu.SMEM),
                  pl.BlockSpec(memory_space=pltpu.SMEM),
                  pl.BlockSpec((tr, w), lambda i: (i, 0))],
        out_specs=pl.BlockSpec(memory_space=pl.ANY),
        out_shape=jax.ShapeDtypeStruct((n_sorted, w), F32),
        scratch_shapes=[pltpu.VMEM((row_block + 8, w), F32), pltpu.SemaphoreType.DMA((2,))],
        compiler_params=_cparams(("arbitrary",)),
        name="moe_scatter",
    )(pos3, real_ends, hx)


def _experts_kernel(gmap_ref, valid_ref, hs_ref, wg_ref, wu_ref, wd_ref, o_ref, acc_ref, *, d):
    i = pl.program_id(0)
    j = pl.program_id(1)
    valid = valid_ref[i] > 0

    @pl.when(j == 0)
    def _():
        acc_ref[...] = jnp.zeros(acc_ref.shape, F32)

    @pl.when(valid)
    def _():
        h = hs_ref[:, :d].astype(BF16)
        a = _dot(h, jnp.concatenate([wg_ref[0, 0], wg_ref[0, 1]], axis=1))
        hid = (a * jax.nn.sigmoid(a)) * _dot(h, jnp.concatenate([wu_ref[0, 0], wu_ref[0, 1]], axis=1))
        first = N_GROUPS + EXPERTS_PER_GROUP * gmap_ref[i] + 2 * j
        cw_all = hs_ref[:, d:]
        lane = lax.broadcasted_iota(jnp.int32, cw_all.shape, 1)
        cw0 = jnp.sum(jnp.where(lane == first, cw_all, 0.0), axis=-1, keepdims=True)
        cw1 = jnp.sum(jnp.where(lane == first + 1, cw_all, 0.0), axis=-1, keepdims=True)
        de = hid.shape[1] // 2
        hcol = lax.broadcasted_iota(jnp.int32, hid.shape, 1)
        hid = hid * jnp.where(hcol < de, cw0, cw1)
        acc_ref[...] += _dot(hid.astype(BF16), wd_ref[0])

    @pl.when(j == pl.num_programs(1) - 1)
    def _():
        o_ref[...] = acc_ref[...]


def _moe_experts(hs, gmap, valid, wg, wu, wd, d, row_block):
    n_sorted, w = hs.shape
    _, _, _, de = wg.shape
    de2 = 2 * de
    pairs = EXPERTS_PER_GROUP // 2
    grid_spec = pltpu.PrefetchScalarGridSpec(
        num_scalar_prefetch=2,
        grid=(n_sorted // row_block, pairs),
        in_specs=[pl.BlockSpec((row_block, w), lambda i, j, gm, va: (i * va[i], 0)),
                  pl.BlockSpec((1, 2, d, de), lambda i, j, gm, va: (gm[i] * pairs + j, 0, 0, 0)),
                  pl.BlockSpec((1, 2, d, de), lambda i, j, gm, va: (gm[i] * pairs + j, 0, 0, 0)),
                  pl.BlockSpec((1, de2, d), lambda i, j, gm, va: (gm[i] * pairs + j, 0, 0))],
        out_specs=pl.BlockSpec((row_block, d), lambda i, j, gm, va: (i, 0)),
        scratch_shapes=[pltpu.VMEM((row_block, d), F32)])
    return pl.pallas_call(
        functools.partial(_experts_kernel, d=d),
        grid_spec=grid_spec,
        out_shape=jax.ShapeDtypeStruct((n_sorted, d), F32),
        compiler_params=_cparams(("arbitrary", "arbitrary")),
        name="moe_experts",
    )(gmap, valid, hs, wg, wu, wd)


def _combine_kernel(pos_ref, ys_hbm, x_ref, o_ref, buf_ref, sems, *, tr, burst):
    def issue_burst(b, slot):
        def issue(t8, c):
            for k in range(MOE_DMA_UNROLL):
                row = b * burst + t8 * MOE_DMA_UNROLL + k
                pltpu.make_async_copy(ys_hbm.at[pl.ds(pos_ref[0, 0, row], 1)], buf_ref.at[pl.ds(row, 1)],
                                      sems.at[slot]).start(priority=k % 2)
            return c
        lax.fori_loop(0, burst // MOE_DMA_UNROLL, issue, 0)

    def wait_burst(b, slot):
        _row_dma_wait(ys_hbm, buf_ref, sems.at[slot], burst)
        rows = slice(b * burst, (b + 1) * burst)
        o_ref[rows, :] = x_ref[rows, :] + buf_ref[rows, :]

    _burst_pipeline(tr // burst, issue_burst, wait_burst)


def _moe_combine(ys, pos3, x2, tr):
    n, d = x2.shape
    return pl.pallas_call(
        functools.partial(_combine_kernel, tr=tr, burst=min(MOE_DMA_BURST, tr)),
        grid=(n // tr,),
        in_specs=[pl.BlockSpec((1, 1, tr), lambda i: (i, 0, 0), memory_space=pltpu.SMEM),
                  pl.BlockSpec(memory_space=pl.ANY),
                  pl.BlockSpec((tr, d), lambda i: (i, 0))],
        out_specs=pl.BlockSpec((tr, d), lambda i: (i, 0)),
        out_shape=jax.ShapeDtypeStruct((n, d), F32),
        scratch_shapes=[pltpu.VMEM((tr, d), F32), pltpu.SemaphoreType.DMA((2,))],
        compiler_params=_cparams(("arbitrary",)),
        name="moe_combine",
    )(pos3, ys, x2)


def _pair_cols(w):
    ne, d, de = w.shape
    return w.astype(BF16).reshape(ne // 2, 2, d, de)


def _pair_rows(w):
    ne, de, d = w.shape
    return w.astype(BF16).reshape(ne // 2, 2 * de, d)


def _moe(hx, x2, wg, wu, wd, tr):
    n, d = x2.shape
    row_block = min(MOE_ROW_BLOCK, n)
    n_blocks = n // row_block + N_GROUPS + 1
    pos, meta = _moe_route(hx, d, tr, row_block)
    counts = meta[0, :N_GROUPS]
    ends = meta[1, :N_GROUPS] + jnp.ceil(counts / row_block) * row_block
    starts = jnp.arange(n_blocks, dtype=F32) * row_block
    grp = jnp.sum((starts[:, None] >= ends[None, :]).astype(jnp.int32), axis=1)
    valid = (grp < N_GROUPS).astype(jnp.int32)
    gmap = jnp.minimum(grp, N_GROUPS - 1)
    real_ends = jnp.concatenate([meta[1, :N_GROUPS] + counts, ends[N_GROUPS - 1:]]).astype(jnp.int32)
    pos3 = pos.reshape(n // tr, 1, tr)
    hs = _moe_scatter(hx, pos3, real_ends, n_blocks * row_block, tr, row_block)
    ys = _moe_experts(hs, gmap, valid, wg, wu, wd, d, row_block)
    return _moe_combine(ys, pos3, x2, tr)


def _pick(n, pref):
    t = min(n, pref)
    while n % t:
        t //= 2
    return t


def kernel(x, positions, norm_mix, w_in, da_q_gain, da_k_gain, da_lambda_q1, da_lambda_k1,
           da_lambda_q2, da_lambda_k2, da_subln_gain, hg_lower_bounds, hg_out_gain,
           s5_lambda_re, s5_lambda_im, s5_log_step, s5_b_re, s5_b_im, s5_c_re, s5_c_im,
           s5_d, s5_glu_w, s5_glu_b, w_branch_attn, w_branch_sb, w_branch_hgrn, w_branch_s5,
           w_out, norm_ffn, router_group_w, router_group_b, router_expert_w, router_expert_b,
           expert_w_gate, expert_w_up, expert_w_down):
    bsz, seq, d = x.shape
    depth = w_in.shape[0]
    n = bsz * seq
    assert w_in.shape[2] == IN_COLS and seq % 128 == 0
    tm = _pick(n, ROW_TM)
    tq_da = _pick(seq, DA_TQ)
    tb = _pick(seq, SCAN_TB)

    x2 = x.reshape(n, d).astype(F32)
    cos_t, sina_t, sinb_t = _rope_tables(positions.reshape(n, 1).astype(jnp.int32), tm)

    lb_all = jnp.cumsum(jax.nn.softmax(hg_lower_bounds.astype(F32), axis=0), axis=0)
    lb_all = lb_all - lb_all[0:1]

    for l in range(depth):
        lambda_init = DA_LAMBDA_INIT_BASE - DA_LAMBDA_INIT_SCALE * math.exp(-DA_LAMBDA_INIT_RATE * l)
        w_f = w_in[l].astype(F32)
        sbq = COL_SB_Q - COL_DA_Q
        w_l = jnp.concatenate([w_f[:, REF_GATE_START:], w_f[:, :sbq],
                               w_f[:, sbq:sbq + SB_HEADS * SB_HEAD_DIM] * (SB_HEAD_DIM ** -0.5 * LOG2E),
                               w_f[:, sbq + SB_HEADS * SB_HEAD_DIM:REF_GATE_START]], axis=1)
        proj = _norm_proj(x2, norm_mix[l].astype(F32)[None, :], w_l.astype(BF16), _pick(n, NP_TM), NP_TN)

        qk_gain = jnp.concatenate([jnp.tile(da_q_gain[l].astype(F32), 2 * DA_HEADS),
                                   jnp.tile(da_k_gain[l].astype(F32), 2 * DA_HEADS)])[None, :]
        qk = _qk_prep(proj, qk_gain, cos_t, sina_t, sinb_t, tm)
        y_a = _diff_attn(qk, proj,
                         da_lambda_q1[l].astype(F32)[None, :], da_lambda_k1[l].astype(F32)[None, :],
                         da_lambda_q2[l].astype(F32)[None, :], da_lambda_k2[l].astype(F32)[None, :],
                         da_subln_gain[l].astype(F32)[None, :], bsz, seq, tq_da, DA_KW, DA_GROUP, lambda_init)
        y_b = _stick_break(proj, bsz, seq, _pick(seq, SB_TQ), SB_KW, SB_GROUP)
        y_c = _hgrn(proj, lb_all[l][None, :], jnp.tile(hg_out_gain[l].astype(F32), HG_HEADS)[None, :],
                    bsz, seq, tb)
        s5p = _s5_params(s5_lambda_re[l], s5_lambda_im[l], s5_log_step[l], s5_b_re[l], s5_b_im[l],
                         s5_c_re[l], s5_c_im[l], tb)
        y_d = _s5(proj, s5p, s5_d[l].astype(F32)[None, :], s5_glu_w[l].astype(BF16),
                  s5_glu_b[l].astype(F32)[None, :], bsz, seq, tb)

        wr = jnp.concatenate([router_group_w[l], router_expert_w[l]], axis=1).astype(F32)
        wr = jnp.pad(wr, ((0, 0), (0, LANES - wr.shape[1])))
        wr_hi = wr.astype(BF16)
        wr_lo = (wr - wr_hi.astype(F32)).astype(BF16)
        br = jnp.concatenate([router_group_b[l], router_expert_b[l]]).astype(F32)
        br = jnp.pad(br, (0, LANES - br.shape[0]))[None, :]
        x2, hx = _merge(x2, y_a, y_b, y_c, y_d, proj,
                        w_branch_attn[l].astype(BF16), w_branch_sb[l].astype(BF16),
                        w_branch_hgrn[l].astype(BF16), w_branch_s5[l].astype(BF16),
                        w_out[l].astype(BF16), norm_ffn[l].astype(F32)[None, :],
                        wr_hi, wr_lo, br, _pick(n, MERGE_TM))
        x2 = _moe(hx, x2, _pair_cols(expert_w_gate[l]), _pair_cols(expert_w_up[l]),
                  _pair_rows(expert_w_down[l]), tm)

    return x2.reshape(bsz, seq, d).astype(x.dtype)
```
